```python
import math
import jax
import jax.numpy as jnp
from jax import lax
import numpy as np

D_MODEL = 1024
BATCH = 4
SEQ = 4096
DEPTH = 2

GRID_W = 64
CTX_LEN = 256
N_EVEN = (DEPTH + 1) // 2
N_ODD = DEPTH // 2
N_MOD = 6
ATT_WIDTH = D_MODEL // 2
HY_WIDTH = D_MODEL - ATT_WIDTH
ATT_V_DIM = 128
ATT_HEADS = ATT_WIDTH // ATT_V_DIM
ATT_QK_DIM = ATT_V_DIM // 2
Q_COLS = ATT_HEADS * 2 * ATT_QK_DIM
K_COLS = ATT_HEADS * 2 * ATT_QK_DIM
V_COLS = ATT_HEADS * ATT_V_DIM
KV_START = Q_COLS
HY_START = Q_COLS + K_COLS + V_COLS
HY_COLS = 3 * HY_WIDTH
IN_COLS = HY_START + HY_COLS
Q_BLOCK = 128
ROPE_BASE = 10000.0
HY_POS_EMB = 33
HY_FILTER_HIDDEN = 64
HY_SHORT_CONV = 3
HY_DECAY_TARGET = 1e-2
HY_FAST_DECAY_PCT = 0.3
HY_SLOW_DECAY_PCT = 1.5
HY_FILTER_STD = 0.005
POOL_WINDOWS = (2, 4, 8, 16)
POOL_GROUPS = len(POOL_WINDOWS)
POOL_GROUP_DIM = D_MODEL // POOL_GROUPS
D_FF = 4 * D_MODEL
NORM_EPS = 1e-6
SUBLN_EPS = 1e-5

kernel_name = "hybrid_diffattn_hyena_pool_dit_block"


def _rmsnorm(x, g, eps=NORM_EPS):
    xf = x.astype(jnp.float32)
    y = xf * lax.rsqrt(jnp.mean(xf * xf, axis=-1, keepdims=True) + eps)
    return (y * g.astype(jnp.float32)).astype(x.dtype)


def _modulate(h, shift, scale):
    return h * (1 + scale) + shift


def _grid_positions(seq_len):
    rows = seq_len // GRID_W
    t = jnp.arange(rows * GRID_W, dtype=jnp.int32)
    return t // GRID_W, t % GRID_W


def _axial_rope(x, row, col):
    axis_dim = ATT_QK_DIM // 2
    n_freq = axis_dim // 2
    inv = ROPE_BASE ** (-jnp.arange(n_freq, dtype=jnp.float32) / n_freq)

    def rot(xa, pos):
        ang = pos.astype(jnp.float32)[:, None] * inv[None, :]
        cos = jnp.cos(ang)[None, :, None, None, :].astype(x.dtype)
        sin = jnp.sin(ang)[None, :, None, None, :].astype(x.dtype)
        x1, x2 = xa[..., :n_freq], xa[..., n_freq:]
        return jnp.concatenate([x1 * cos - x2 * sin, x1 * sin + x2 * cos], axis=-1)

    return jnp.concatenate([rot(x[..., :axis_dim], row), rot(x[..., axis_dim:], col)], axis=-1)


def _split_q(p):
    b, l = p.shape[0], p.shape[1]
    return p[..., :Q_COLS].reshape(b, l, ATT_HEADS, 2, ATT_QK_DIM)


def _split_kv(kv):
    b, l = kv.shape[0], kv.shape[1]
    k = kv[..., :K_COLS].reshape(b, l, ATT_HEADS, 2, ATT_QK_DIM)
    v = kv[..., K_COLS:].reshape(b, l, ATT_HEADS, ATT_V_DIM)
    return k, v


def _diff_attn(q, k, v, lam):
    s = jnp.einsum("bqhcd,bkhcd->bhcqk", q, k, preferred_element_type=jnp.float32) * (ATT_QK_DIM ** -0.5)
    p = jax.nn.softmax(s, axis=-1)
    w = p[:, :, 0] - lam * p[:, :, 1]
    return jnp.einsum("bhqk,bkhd->bqhd", w.astype(v.dtype), v)


def _diff_attn_blocked(q, k, v, lam):
    b, l = q.shape[0], q.shape[1]
    nb = l // Q_BLOCK
    qb = jnp.moveaxis(q.reshape(b, nb, Q_BLOCK, ATT_HEADS, 2, ATT_QK_DIM), 1, 0)
    out = lax.map(lambda qq: _diff_attn(qq, k, v, lam), qb)
    return jnp.moveaxis(out, 0, 1).reshape(b, l, ATT_HEADS, ATT_V_DIM)


def _diff_post(o, g, lam_init):
    o = _rmsnorm(o, g, SUBLN_EPS) * (1.0 - lam_init)
    return o.reshape(o.shape[0], o.shape[1], ATT_HEADS * ATT_V_DIM)


def _short_conv3(u, w, b):
    up = jnp.pad(u, ((0, 0), (1, 1), (0, 0)))
    return up[:, :-2] * w[0] + up[:, 1:-1] * w[1] + up[:, 2:] * w[2] + b


def _hyena_filter(seq_len, w1, b1, f1, w2, b2, f2, w3):
    t = jnp.linspace(0.0, 1.0, seq_len, dtype=jnp.float32)[:, None]
    bands = (HY_POS_EMB - 1) // 2
    freqs = jnp.linspace(1e-4, bands - 1, bands, dtype=jnp.float32)
    ang = (2.0 * math.pi / seq_len) * jnp.arange(seq_len, dtype=jnp.float32)[:, None] * freqs[None, :]
    z = jnp.concatenate([t, jnp.cos(ang), -jnp.sin(ang)], axis=-1).astype(w1.dtype)
    hdn = jnp.sin(f1 * (z @ w1 + b1))
    hdn = jnp.sin(f2 * (hdn @ w2 + b2))
    h = (hdn @ w3).astype(jnp.float32).reshape(seq_len, 2, HY_WIDTH)
    max_decay = math.log(HY_DECAY_TARGET) / HY_FAST_DECAY_PCT
    min_decay = math.log(HY_DECAY_TARGET) / HY_SLOW_DECAY_PCT
    deltas = jnp.linspace(min_decay, max_decay, HY_WIDTH, dtype=jnp.float32)
    h = h * jnp.exp(-t * jnp.abs(deltas)[None, :])[:, None, :]
    return jnp.concatenate([h[:, 0], jnp.zeros((1, HY_WIDTH), jnp.float32), h[:0:-1, 1]], axis=0)


def _long_conv(u, k2, bias):
    l = u.shape[1]
    uf = u.astype(jnp.float32)
    U = jnp.fft.rfft(uf, n=2 * l, axis=1)
    K = jnp.fft.rfft(k2, axis=0)
    y = jnp.fft.irfft(U * K[None], n=2 * l, axis=1)[:, :l]
    return (y + uf * bias.astype(jnp.float32)).astype(u.dtype)


def _hyena(p, conv_w, conv_b, filt, bias):
    p = _short_conv3(p, conv_w, conv_b)
    x0, x1, v = jnp.split(p, 3, axis=-1)
    return x0 * _long_conv(v * x1, filt, bias)


def _multiscale_pool(u, w, scale):
    b, l, d = u.shape
    uf = u.astype(jnp.float32)
    cs = jnp.concatenate([jnp.zeros((b, 1, d), jnp.float32), jnp.cumsum(uf, axis=1)], axis=1)
    t = jnp.arange(l)
    outs = []
    for g, win in enumerate(POOL_WINDOWS):
        lo = jnp.clip(t - win // 2, 0, l)
        hi = jnp.clip(t + win - win // 2, 0, l)
        sl = slice(g * POOL_GROUP_DIM, (g + 1) * POOL_GROUP_DIM)
        csg = cs[..., sl]
        mean = (csg[:, hi] - csg[:, lo]) / (hi - lo).astype(jnp.float32)[None, :, None]
        outs.append(mean - uf[..., sl])
    dlt = jnp.stack(outs, axis=2).astype(u.dtype)
    y = jnp.einsum("blgc,gcd->blgd", dlt, w).reshape(b, l, d)
    return y * scale


def _sq_relu_mlp(h, w1, w2):
    return jnp.square(jax.nn.relu(h @ w1)) @ w2


def setup_inputs(seed: int = 0) -> dict:
    key = jax.random.key(seed)
    ks = iter(jax.random.split(key, 40))

    def nrm(shape, std):
        return std * jax.random.normal(next(ks), shape, jnp.float32)

    D = D_MODEL
    H = HY_FILTER_HIDDEN
    return {
        "x": nrm((BATCH, SEQ, D), 1.0),
        "c": nrm((BATCH, D), 1.0),
        "ctx": nrm((BATCH, CTX_LEN, D), 1.0),
        "c_ctx": nrm((D,), 1.0),
        "ada_w": nrm((DEPTH, D, N_MOD * D), 0.5 * D ** -0.5),
        "ada_b": nrm((DEPTH, N_MOD * D), 0.02),
        "norm1_g": 1.0 + nrm((DEPTH, D), 0.05),
        "norm2_g": 1.0 + nrm((DEPTH, D), 0.05),
        "mix_w_in": nrm((N_EVEN, D, IN_COLS), D ** -0.5),
        "mix_b_in": nrm((N_EVEN, IN_COLS), 0.02),
        "mix_w_out": nrm((N_EVEN, D, D), D ** -0.5),
        "mix_b_out": nrm((N_EVEN, D), 0.02),
        "lam_q1": nrm((N_EVEN, ATT_QK_DIM), 0.1),
        "lam_k1": nrm((N_EVEN, ATT_QK_DIM), 0.1),
        "lam_q2": nrm((N_EVEN, ATT_QK_DIM), 0.1),
        "lam_k2": nrm((N_EVEN, ATT_QK_DIM), 0.1),
        "subln_g": 1.0 + nrm((N_EVEN, ATT_V_DIM), 0.05),
        "hy_conv_w": nrm((N_EVEN, HY_SHORT_CONV, HY_COLS), HY_SHORT_CONV ** -0.5),
        "hy_conv_b": nrm((N_EVEN, HY_COLS), 0.02),
        "hy_pos_w1": nrm((N_EVEN, HY_POS_EMB, H), HY_POS_EMB ** -0.5),
        "hy_pos_b1": nrm((N_EVEN, H), 0.1),
        "hy_freq1": 1.0 + nrm((N_EVEN, H), 0.1),
        "hy_pos_w2": nrm((N_EVEN, H, H), H ** -0.5),
        "hy_pos_b2": nrm((N_EVEN, H), 0.1),
        "hy_freq2": 1.0 + nrm((N_EVEN, H), 0.1),
        "hy_pos_w3": nrm((N_EVEN, H, 2 * HY_WIDTH), HY_FILTER_STD),
        "hy_bias": nrm((N_EVEN, HY_WIDTH), 0.5),
        "pool_w": nrm((N_ODD, POOL_GROUPS, POOL_GROUP_DIM, POOL_GROUP_DIM), POOL_GROUP_DIM ** -0.5),
        "pool_scale": 1.0 + nrm((N_ODD, D), 0.05),
        "mlp_w1": nrm((DEPTH, D, D_FF), D ** -0.5),
        "mlp_w2": nrm((DEPTH, D_FF, D), D_FF ** -0.5),
        "final_g": 1.0 + nrm((D,), 0.05),
    }


def reference(x, c, ctx, c_ctx, ada_w, ada_b, norm1_g, norm2_g, mix_w_in, mix_b_in, mix_w_out, mix_b_out,
              lam_q1, lam_k1, lam_q2, lam_k2, subln_g, hy_conv_w, hy_conv_b, hy_pos_w1, hy_pos_b1, hy_freq1,
              hy_pos_w2, hy_pos_b2, hy_freq2, hy_pos_w3, hy_bias, pool_w, pool_scale, mlp_w1, mlp_w2, final_g):
    seq_len = x.shape[1]
    row, col = _grid_positions(seq_len)
    s_lat = jax.nn.silu(c)
    s_ctx = jax.nn.silu(c_ctx)
    h_lat, h_ctx = x, ctx
    for i in range(DEPTH):
        ctx_live = any(j % 2 == 0 for j in range(i + 1, DEPTH))
        mod_l = [m[:, None, :] for m in jnp.split(s_lat @ ada_w[i] + ada_b[i], N_MOD, axis=-1)]
        a_lat = _modulate(_rmsnorm(h_lat, norm1_g[i]), mod_l[0], mod_l[1])
        if i % 2 == 0 or ctx_live:
            mod_c = jnp.split(s_ctx @ ada_w[i] + ada_b[i], N_MOD, axis=-1)
            a_ctx = _modulate(_rmsnorm(h_ctx, norm1_g[i]), mod_c[0], mod_c[1])
        if i % 2 == 0:
            e = i // 2
            lam_init = 0.8 - 0.6 * math.exp(-0.3 * i)
            lam = (jnp.exp(jnp.sum(lam_q1[e] * lam_k1[e]).astype(jnp.float32))
                   - jnp.exp(jnp.sum(lam_q2[e] * lam_k2[e]).astype(jnp.float32)) + lam_init)
            filt = (hy_pos_w1[e], hy_pos_b1[e], hy_freq1[e], hy_pos_w2[e], hy_pos_b2[e], hy_freq2[e], hy_pos_w3[e])
            p_lat = a_lat @ mix_w_in[e] + mix_b_in[e]
            q_l = _axial_rope(_split_q(p_lat), row, col)
            k_l, v_l = _split_kv(p_lat[..., KV_START:HY_START])
            k_l = _axial_rope(k_l, row, col)
            if ctx_live:
                p_ctx = a_ctx @ mix_w_in[e] + mix_b_in[e]
                kv_ctx = p_ctx[..., KV_START:HY_START]
            else:
                kv_ctx = a_ctx @ mix_w_in[e][:, KV_START:HY_START] + mix_b_in[e][KV_START:HY_START]
            k_c, v_c = _split_kv(kv_ctx)
            k_all = jnp.concatenate([k_c, k_l], axis=1)
            v_all = jnp.concatenate([v_c, v_l], axis=1)
            att_l = _diff_post(_diff_attn_blocked(q_l, k_all, v_all, lam), subln_g[e], lam_init)
            hy_l = _hyena(p_lat[..., HY_START:], hy_conv_w[e], hy_conv_b[e], _hyena_filter(seq_len, *filt), hy_bias[e])
            y_lat = jnp.concatenate([att_l, hy_l], axis=-1) @ mix_w_out[e] + mix_b_out[e]
            if ctx_live:
                att_c = _diff_post(_diff_attn(_split_q(p_ctx), k_c, v_c, lam), subln_g[e], lam_init)
                hy_c = _hyena(p_ctx[..., HY_START:], hy_conv_w[e], hy_conv_b[e],
                              _hyena_filter(h_ctx.shape[1], *filt), hy_bias[e])
                y_ctx = jnp.concatenate([att_c, hy_c], axis=-1) @ mix_w_out[e] + mix_b_out[e]
        else:
            o = i // 2
            y_lat = _multiscale_pool(a_lat, pool_w[o], pool_scale[o])
            if ctx_live:
                y_ctx = _multiscale_pool(a_ctx, pool_w[o], pool_scale[o])
        h_lat = h_lat + mod_l[2] * y_lat
        h_lat = h_lat + mod_l[5] * _sq_relu_mlp(
            _modulate(_rmsnorm(h_lat, norm2_g[i]), mod_l[3], mod_l[4]), mlp_w1[i], mlp_w2[i])
        if ctx_live:
            h_ctx = h_ctx + mod_c[2] * y_ctx
            h_ctx = h_ctx + mod_c[5] * _sq_relu_mlp(
                _modulate(_rmsnorm(h_ctx, norm2_g[i]), mod_c[3], mod_c[4]), mlp_w1[i], mlp_w2[i])
    return _rmsnorm(h_lat, final_g)
```

```python
import functools
import math

import numpy as np
import jax
import jax.numpy as jnp
from jax import lax
from jax.experimental import pallas as pl
from jax.experimental.pallas import tpu as pltpu

F32 = jnp.float32
BF16 = jnp.bfloat16

GRID_W = 64
N_MOD = 6
ATT_HEADS = 4
ATT_V_DIM = 128
ATT_QK_DIM = 64
ROPE_BASE = 10000.0
HY_POS_EMB = 33
HY_DECAY_TARGET = 1e-2
HY_FAST_DECAY_PCT = 0.3
HY_SLOW_DECAY_PCT = 1.5
POOL_WINDOWS = (2, 4, 8, 16)
NORM_EPS = 1e-6
SUBLN_EPS = 1e-5
FFT_N2 = 64
POOL_HALO = 8
MOD_ROWS = 8

VMEM_LIMIT_BYTES = 56 * 1024 * 1024


def _params(n_grid_dims):
    return pltpu.CompilerParams(
        dimension_semantics=("arbitrary",) * n_grid_dims,
        vmem_limit_bytes=VMEM_LIMIT_BYTES,
    )


def _const_spec(shape):
    nd = len(shape)
    return pl.BlockSpec(shape, lambda *_: (0,) * nd, pipeline_mode=pl.Buffered(1))


def _split_bf16(a):
    hi = a.astype(BF16)
    lo = (a - hi.astype(F32)).astype(BF16)
    return hi, lo


def _dot(a, b):
    return jnp.dot(a, b, preferred_element_type=F32)


def _dot3(a_hi, a_lo, b_hi, b_lo):
    return _dot(a_hi, b_hi) + (_dot(a_lo, b_hi) + _dot(a_hi, b_lo))


def _dot3f(a, b):
    a_hi, a_lo = _split_bf16(a)
    b_hi, b_lo = _split_bf16(b)
    return _dot3(a_hi, a_lo, b_hi, b_lo)


def _rms(x, eps):
    return x * lax.rsqrt(jnp.mean(x * x, axis=-1, keepdims=True) + eps)


def _mod_kernel(cv_ref, w_ref, b_ref, o_ref):
    cv = cv_ref[...]
    s = cv / (1.0 + jnp.exp(-cv))
    o_ref[0] = _dot3f(s, w_ref[0]) + b_ref[0]


def _mod_call(cv, ada_w, ada_b, tn=1536):
    depth, d, n = ada_w.shape
    return pl.pallas_call(
        _mod_kernel,
        grid=(depth, n // tn),
        in_specs=[
            pl.BlockSpec((MOD_ROWS, d), lambda i, j: (0, 0)),
            pl.BlockSpec((1, d, tn), lambda i, j: (i, 0, j)),
            pl.BlockSpec((1, 1, tn), lambda i, j: (i, 0, j)),
        ],
        out_specs=pl.BlockSpec((1, MOD_ROWS, tn), lambda i, j: (i, 0, j)),
        out_shape=jax.ShapeDtypeStruct((depth, MOD_ROWS, n), F32),
        compiler_params=_params(2),
        name="mod",
    )(cv, ada_w, ada_b.reshape(depth, 1, n))


def _normproj_kernel(*refs, outs, use_rope):
    x_ref, sh_ref, sc_ref, g_ref, w_ref, b_ref = refs[:6]
    rest = refs[6:]
    if use_rope:
        cos_ref, sa_ref, sb_ref = rest[:3]
        rest = rest[3:]
    a = _rms(x_ref[0], NORM_EPS) * g_ref[...]
    a = a * (1.0 + sc_ref[0]) + sh_ref[0]
    p = _dot(a.astype(BF16), w_ref[...]) + b_ref[...]
    for o_ref, (lo, hi, rope, scale) in zip(rest, outs):
        if not rope:
            o_ref[0] = p[:, lo:hi].astype(o_ref.dtype)
            continue
        cos, sa, sb = cos_ref[...], sa_ref[...], sb_ref[...]
        for c0 in range(lo, hi, 128):
            blk = p[:, c0:c0 + 128]
            r = blk * cos + pltpu.roll(blk, 16, 1) * sa + pltpu.roll(blk, 112, 1) * sb
            if scale != 1.0:
                r = r * scale
            o_ref[0, :, c0 - lo:c0 - lo + 128] = r.astype(o_ref.dtype)


def _normproj_call(x, shift, scale, g, w_bf16, b, rope_tabs, outs, out_dtypes, tm, name):
    bsz, s, d = x.shape
    n = w_bf16.shape[1]
    per_batch = shift.shape[0] == bsz
    vec_map = (lambda bi, i: (bi, 0, 0)) if per_batch else (lambda bi, i: (0, 0, 0))
    in_specs = [
        pl.BlockSpec((1, tm, d), lambda bi, i: (bi, i, 0)),
        pl.BlockSpec((1, 1, d), vec_map),
        pl.BlockSpec((1, 1, d), vec_map),
        _const_spec((1, d)),
        _const_spec((d, n)),
        _const_spec((1, n)),
    ]
    args = [x, shift, scale, g.reshape(1, d), w_bf16, b.reshape(1, n)]
    use_rope = rope_tabs is not None
    if use_rope:
        in_specs += [pl.BlockSpec((tm, 128), lambda bi, i: (i, 0))] * 3
        args += list(rope_tabs)
    out_specs = [pl.BlockSpec((1, tm, hi - lo), lambda bi, i: (bi, i, 0)) for lo, hi, _, _ in outs]
    out_shape = [jax.ShapeDtypeStruct((bsz, s, hi - lo), dt) for (lo, hi, _, _), dt in zip(outs, out_dtypes)]
    return pl.pallas_call(
        functools.partial(_normproj_kernel, outs=tuple(outs), use_rope=use_rope),
        grid=(bsz, s // tm),
        in_specs=in_specs,
        out_specs=out_specs,
        out_shape=out_shape,
        compiler_params=_params(2),
        name=name,
    )(*args)


def _rope_tables(seq_len):
    axis_dim = ATT_QK_DIM // 2
    n_freq = axis_dim // 2
    inv = ROPE_BASE ** (-jnp.arange(n_freq, dtype=F32) / n_freq)
    t = jnp.arange(seq_len, dtype=jnp.int32)
    row, col = t // GRID_W, t % GRID_W
    lane = np.arange(128)
    jj = lane % ATT_QK_DIM
    is_col = (jj // axis_dim) == 1
    freq = jj % n_freq
    second = (jj % axis_dim) >= n_freq
    pos = jnp.where(jnp.asarray(is_col)[None, :], col[:, None], row[:, None]).astype(F32)
    ang = pos * inv[jnp.asarray(freq)][None, :]
    cos, sin = jnp.cos(ang), jnp.sin(ang)
    second = jnp.asarray(second)[None, :]
    return cos, jnp.where(second, sin, 0.0), jnp.where(second, 0.0, -sin)


def _attn_kernel(lam_ref, q_ref, kc_ref, vc_ref, kl_ref, vl_ref, g_ref, o_ref, *, tk, lam_init):
    q = q_ref[0]
    tq = q.shape[0]
    lane = lax.broadcasted_iota(jnp.int32, q.shape, 1)
    zero = jnp.zeros_like(q)
    q2 = jnp.concatenate([jnp.where(lane < ATT_QK_DIM, q, zero), jnp.where(lane >= ATT_QK_DIM, q, zero)], axis=0)

    def scores(k):
        return lax.dot_general(q2, k, (((1,), (1,)), ((), ())), preferred_element_type=F32)

    s = scores(kc_ref[0])
    m = jnp.max(s, axis=1, keepdims=True)
    p = jnp.exp(s - m)
    l = jnp.sum(p, axis=1, keepdims=True)
    acc = _dot(p.astype(BF16), vc_ref[0])

    def body(j, carry):
        m, l, acc = carry
        off = pl.multiple_of(j * tk, tk)
        s = scores(kl_ref[0, pl.ds(off, tk), :])
        m_new = jnp.maximum(m, jnp.max(s, axis=1, keepdims=True))
        alpha = jnp.exp(m - m_new)
        p = jnp.exp(s - m_new)
        l = alpha * l + jnp.sum(p, axis=1, keepdims=True)
        acc = alpha * acc + _dot(p.astype(BF16), vl_ref[0, pl.ds(off, tk), :])
        return m_new, l, acc

    m, l, acc = lax.fori_loop(0, kl_ref.shape[1] // tk, body, (m, l, acc))

    lamv = lam_ref[...]
    lam = (jnp.exp(jnp.sum(lamv[0:1] * lamv[1:2], axis=1, keepdims=True))
           - jnp.exp(jnp.sum(lamv[2:3] * lamv[3:4], axis=1, keepdims=True)) + lam_init)
    o_all = acc / l
    o = o_all[:tq] - lam * o_all[tq:]
    o = _rms(o, SUBLN_EPS) * g_ref[...] * (1.0 - lam_init)
    o_ref[0] = o.astype(o_ref.dtype)


def _attn_call(q, kc, vc, kl, vl, lamv, g, lam_init, tq=256, tk=512):
    bsz, l, width = q.shape
    heads = width // ATT_V_DIM
    lc = kc.shape[1]
    hd = ATT_V_DIM
    return pl.pallas_call(
        functools.partial(_attn_kernel, tk=tk, lam_init=lam_init),
        grid=(bsz, heads, l // tq),
        in_specs=[
            _const_spec(lamv.shape),
            pl.BlockSpec((1, tq, hd), lambda b, h, i: (b, i, h)),
            pl.BlockSpec((1, lc, hd), lambda b, h, i: (b, 0, h)),
            pl.BlockSpec((1, lc, hd), lambda b, h, i: (b, 0, h)),
            pl.BlockSpec((1, l, hd), lambda b, h, i: (b, 0, h)),
            pl.BlockSpec((1, l, hd), lambda b, h, i: (b, 0, h)),
            _const_spec((1, hd)),
        ],
        out_specs=pl.BlockSpec((1, tq, hd), lambda b, h, i: (b, i, h)),
        out_shape=jax.ShapeDtypeStruct((bsz, l, width), BF16),
        compiler_params=_params(3),
        name="diffattn",
    )(lamv, q, kc, vc, kl, vl, g.reshape(1, hd))


def _filter_kernel(z_ref, w1_ref, b1_ref, f1_ref, w2_ref, b2_ref, f2_ref, w3_ref, dl_ref, o_ref, *, seq_len):
    tm = z_ref.shape[0]
    h1 = jnp.sin(f1_ref[...] * (_dot3f(z_ref[...], w1_ref[...]) + b1_ref[...]))
    h2 = jnp.sin(f2_ref[...] * (_dot3f(h1, w2_ref[...]) + b2_ref[...]))
    h = _dot3f(h2, w3_ref[...])
    row = lax.broadcasted_iota(jnp.int32, h.shape, 0) + pl.program_id(0) * tm
    col = lax.broadcasted_iota(jnp.int32, h.shape, 1)
    t = row.astype(F32) * (1.0 / (seq_len - 1))
    h = h * jnp.exp(-t * dl_ref[...])
    half = h.shape[1] // 2
    o_ref[...] = jnp.where((row == 0) & (col >= half), 0.0, h)


def _filter_call(z, w1, b1, f1, w2, b2, f2, w3, absdelta2, tm=512):
    seq_len, kz = z.shape
    hid = w2.shape[0]
    n = w3.shape[1]
    return pl.pallas_call(
        functools.partial(_filter_kernel, seq_len=seq_len),
        grid=(seq_len // tm,),
        in_specs=[
            pl.BlockSpec((tm, kz), lambda i: (i, 0)),
            _const_spec((kz, hid)), _const_spec((1, hid)), _const_spec((1, hid)),
            _const_spec((hid, hid)), _const_spec((1, hid)), _const_spec((1, hid)),
            _const_spec((hid, n)), _const_spec((1, n)),
        ],
        out_specs=pl.BlockSpec((tm, n), lambda i: (i, 0)),
        out_shape=jax.ShapeDtypeStruct((seq_len, n), F32),
        compiler_params=_params(1),
        name="hyfilter",
    )(z, w1, b1.reshape(1, hid), f1.reshape(1, hid), w2, b2.reshape(1, hid), f2.reshape(1, hid), w3, absdelta2)


def _gate_kernel(x0_ref, x1_ref, v_ref, w_ref, b_ref, u_ref, x0o_ref):
    seq_len = x0_ref.shape[1]
    row = lax.broadcasted_iota(jnp.int32, x0_ref.shape[1:], 0)
    first, last = row == 0, row == seq_len - 1

    def conv(ref, s):
        p = ref[0]
        prev = jnp.where(first, 0.0, pltpu.roll(p, 1, 0))
        nxt = jnp.where(last, 0.0, pltpu.roll(p, seq_len - 1, 0))
        return prev * w_ref[0, s:s + 1] + p * w_ref[1, s:s + 1] + nxt * w_ref[2, s:s + 1] + b_ref[s:s + 1]

    x0o_ref[0] = conv(x0_ref, 0)
    u_ref[0] = conv(v_ref, 2) * conv(x1_ref, 1)


def _gate_call(p_hy, conv_w, conv_b, ct=128):
    bsz, seq_len, c3 = p_hy.shape
    c = c3 // 3
    nb = c // ct
    spec = lambda s: pl.BlockSpec((1, seq_len, ct), lambda b, j, s=s: (b, 0, s * nb + j))
    out_spec = pl.BlockSpec((1, seq_len, ct), lambda b, j: (b, 0, j))
    return pl.pallas_call(
        _gate_kernel,
        grid=(bsz, nb),
        in_specs=[spec(0), spec(1), spec(2),
                  pl.BlockSpec((3, 3, ct), lambda b, j: (0, 0, j)),
                  pl.BlockSpec((3, ct), lambda b, j: (0, j))],
        out_specs=[out_spec, out_spec],
        out_shape=[jax.ShapeDtypeStruct((bsz, seq_len, c), F32)] * 2,
        compiler_params=_params(2),
        name="hygate",
    )(p_hy, p_hy, p_hy, conv_w.reshape(3, 3, c), conv_b.reshape(3, c))


def _fft_major_kernel(fh_ref, fl_ref, z_ref, or_ref, oi_ref):
    zh, zl = _split_bf16(z_ref[0])
    out = _dot3(fh_ref[...], fl_ref[...], zh, zl)
    rows = out.shape[0] // 2
    or_ref[0] = out[:rows]
    oi_ref[0] = out[rows:]


def _fft_major_call(f_blk, z, tl, name):
    fh, fl = _split_bf16(f_blk)
    groups, rows_in, lanes = z.shape
    n_out_rows = f_blk.shape[0] // 2
    out_spec = pl.BlockSpec((1, n_out_rows, tl), lambda p, j: (p, 0, j))
    return pl.pallas_call(
        _fft_major_kernel,
        grid=(groups, lanes // tl),
        in_specs=[_const_spec(fh.shape), _const_spec(fl.shape),
                  pl.BlockSpec((1, rows_in, tl), lambda p, j: (p, 0, j))],
        out_specs=[out_spec, out_spec],
        out_shape=[jax.ShapeDtypeStruct((groups, n_out_rows, lanes), F32)] * 2,
        compiler_params=_params(2),
        name=name,
    )(fh, fl, z)


def _twiddle(ar, ai, twr, twi, reps):
    twr = jnp.concatenate([twr] * reps, axis=1)
    twi = jnp.concatenate([twi] * reps, axis=1)
    return ar * twr - ai * twi, ar * twi + ai * twr


def _fft_minor_filter_kernel(gh_ref, gl_ref, twr_ref, twi_ref, ar_ref, ai_ref, kr_ref, ki_ref):
    n2 = FFT_N2
    c = kr_ref.shape[1]
    reps = ar_ref.shape[2] // twr_ref.shape[1]

    def body(kk, _):
        rows = pl.ds(pl.multiple_of(kk * n2, n2), n2)
        sr, si = _twiddle(ar_ref[0, rows, :], ai_ref[0, rows, :], twr_ref[rows, :], twi_ref[rows, :], reps)
        sh, sl = _split_bf16(jnp.concatenate([sr, si], axis=0))
        x = _dot3(gh_ref[...], gl_ref[...], sh, sl)
        xr, xi = x[:n2], x[n2:]
        kr_ref[rows, :] = xr[:, :c] + xr[:, c:]
        ki_ref[rows, :] = xi[:, :c] - xi[:, c:]
        return 0

    lax.fori_loop(0, ar_ref.shape[1] // n2, body, 0)


def _fft_minor_conv_kernel(gh_ref, gl_ref, gch_ref, gcl_ref, twr_ref, twi_ref, kr_ref, ki_ref,
                           ar_ref, ai_ref, tr_ref, ti_ref):
    n2 = FFT_N2
    reps = ar_ref.shape[2] // twr_ref.shape[1]

    def body(kk, _):
        rows = pl.ds(pl.multiple_of(kk * n2, n2), n2)
        twr = jnp.concatenate([twr_ref[rows, :]] * reps, axis=1)
        twi = jnp.concatenate([twi_ref[rows, :]] * reps, axis=1)
        ar, ai = ar_ref[0, rows, :], ai_ref[0, rows, :]
        sr, si = ar * twr - ai * twi, ar * twi + ai * twr
        sh, sl = _split_bf16(jnp.concatenate([sr, si], axis=0))
        x = _dot3(gh_ref[...], gl_ref[...], sh, sl)
        xr, xi = x[:n2], x[n2:]
        kr, ki = kr_ref[rows, :], ki_ref[rows, :]
        yr, yi = xr * kr - xi * ki, xr * ki + xi * kr
        yh, yl = _split_bf16(jnp.concatenate([yr, yi], axis=0))
        t = _dot3(gch_ref[...], gcl_ref[...], yh, yl)
        tr, ti = t[:n2], t[n2:]
        tr_ref[0, rows, :] = tr * twr + ti * twi
        ti_ref[0, rows, :] = ti * twr - tr * twi
        return 0

    lax.fori_loop(0, ar_ref.shape[1] // n2, body, 0)


def _ifft_major_kernel(eh_ref, el_ref, tr_ref, ti_ref, u_ref, x0_ref, bias_ref, o_ref, *, inv_n):
    th, tl = _split_bf16(jnp.concatenate([tr_ref[0], ti_ref[0]], axis=0))
    y = _dot3(eh_ref[...], el_ref[...], th, tl) * inv_n
    o_ref[0] = (x0_ref[0] * (y + u_ref[0] * bias_ref[...])).astype(o_ref.dtype)


def _dft_blocks(seq_len):
    n = 2 * seq_len
    n2 = FFT_N2
    n1 = n // n2
    n1h = n1 // 2
    k1 = np.arange(n1)[:, None].astype(np.float64)
    a = 2.0 * np.pi * k1 * np.arange(n1h)[None, :] / n1
    fr, fi = np.cos(a), -np.sin(a)
    f_cplx = np.block([[fr, -fi], [fi, fr]])
    f_real = np.concatenate([fr, fi], axis=0)
    e_cplx = np.block([[fr.T, fi.T], [-fi.T, fr.T]])
    b = 2.0 * np.pi * np.arange(n2)[:, None] * np.arange(n2)[None, :] / n2
    gr, gi = np.cos(b), -np.sin(b)
    g_fwd = np.block([[gr, -gi], [gi, gr]])
    g_inv = np.block([[gr, gi], [-gi, gr]])
    kk = np.arange(n1)[:, None] * np.arange(n2)[None, :]
    ang = (2.0 * np.pi / n) * kk.reshape(n1 * n2, 1)
    twr = np.broadcast_to(np.cos(ang), (n1 * n2, 128))
    twi = np.broadcast_to(-np.sin(ang), (n1 * n2, 128))
    as32 = lambda m: jnp.asarray(np.ascontiguousarray(m), dtype=F32)
    return dict(n=n, n1=n1, n1h=n1h, f_cplx=as32(f_cplx), f_real=as32(f_real), e_cplx=as32(e_cplx),
                g_fwd=as32(g_fwd), g_inv=as32(g_inv), twr=as32(twr), twi=as32(twi))


def _hyena_call(p_hy, conv_w, conv_b, filt, hy_bias):
    bsz, seq_len, c3 = p_hy.shape
    c = c3 // 3
    n2 = FFT_N2
    dft = _dft_blocks(seq_len)
    n, n1, n1h = dft["n"], dft["n1"], dft["n1h"]
    pairs = bsz // 2
    gh, gl = _split_bf16(dft["g_fwd"])
    gch, gcl = _split_bf16(dft["g_inv"])

    w1, b1, f1, w2, b2, f2, w3 = filt
    hid = w2.shape[0]
    bands = (HY_POS_EMB - 1) // 2
    t = jnp.linspace(0.0, 1.0, seq_len, dtype=F32)[:, None]
    freqs = jnp.linspace(1e-4, bands - 1, bands, dtype=F32)
    ang = (2.0 * math.pi / seq_len) * jnp.arange(seq_len, dtype=F32)[:, None] * freqs[None, :]
    z = jnp.concatenate([t, jnp.cos(ang), -jnp.sin(ang), jnp.zeros((seq_len, hid - HY_POS_EMB), F32)], axis=-1)
    w1p = jnp.concatenate([w1, jnp.zeros((hid - HY_POS_EMB, hid), F32)], axis=0)
    max_decay = math.log(HY_DECAY_TARGET) / HY_FAST_DECAY_PCT
    min_decay = math.log(HY_DECAY_TARGET) / HY_SLOW_DECAY_PCT
    absdelta = jnp.abs(jnp.linspace(min_decay, max_decay, c, dtype=F32))
    absdelta2 = jnp.concatenate([absdelta, absdelta])[None, :]
    hcat = _filter_call(z, w1p, b1, f1, w2, b2, f2, w3, absdelta2)
    hr, hi = _fft_major_call(dft["f_real"], hcat.reshape(1, n1h, n2 * 2 * c), 4096, "hyfft_filt_major")
    rb = 8 * n2
    kr, ki = pl.pallas_call(
        _fft_minor_filter_kernel,
        grid=(n // rb,),
        in_specs=[_const_spec(gh.shape), _const_spec(gl.shape),
                  pl.BlockSpec((rb, 128), lambda i: (i, 0)), pl.BlockSpec((rb, 128), lambda i: (i, 0)),
                  pl.BlockSpec((1, rb, 2 * c), lambda i: (0, i, 0)), pl.BlockSpec((1, rb, 2 * c), lambda i: (0, i, 0))],
        out_specs=[pl.BlockSpec((rb, c), lambda i: (i, 0))] * 2,
        out_shape=[jax.ShapeDtypeStruct((n, c), F32)] * 2,
        compiler_params=_params(1),
        name="hyfft_filt_minor",
    )(gh, gl, dft["twr"], dft["twi"], hr.reshape(1, n, 2 * c), hi.reshape(1, n, 2 * c))

    u, x0 = _gate_call(p_hy, conv_w, conv_b)
    lanes = n2 * c
    u_l = u.reshape(pairs, 2 * n1h, lanes)
    x0_l = x0.reshape(pairs, 2 * n1h, lanes)
    ar, ai = _fft_major_call(dft["f_cplx"], u_l, 2048, "hyfft_major")
    blk = pl.BlockSpec((1, rb, c), lambda p, i: (p, i, 0))
    tr, ti = pl.pallas_call(
        _fft_minor_conv_kernel,
        grid=(pairs, n // rb),
        in_specs=[_const_spec(gh.shape), _const_spec(gl.shape), _const_spec(gch.shape), _const_spec(gcl.shape),
                  pl.BlockSpec((rb, 128), lambda p, i: (i, 0)), pl.BlockSpec((rb, 128), lambda p, i: (i, 0)),
                  pl.BlockSpec((rb, c), lambda p, i: (i, 0)), pl.BlockSpec((rb, c), lambda p, i: (i, 0)),
                  blk, blk],
        out_specs=[blk, blk],
        out_shape=[jax.ShapeDtypeStruct((pairs, n, c), F32)] * 2,
        compiler_params=_params(2),
        name="hyfft_minor",
    )(gh, gl, gch, gcl, dft["twr"], dft["twi"], kr, ki, ar.reshape(pairs, n, c), ai.reshape(pairs, n, c))

    tl = 2048
    eh, el = _split_bf16(dft["e_cplx"])
    bias_l = jnp.tile(hy_bias, tl // c)[None, :]
    t_spec = pl.BlockSpec((1, n1, tl), lambda p, j: (p, 0, j))
    pair_spec = pl.BlockSpec((1, 2 * n1h, tl), lambda p, j: (p, 0, j))
    hy = pl.pallas_call(
        functools.partial(_ifft_major_kernel, inv_n=1.0 / n),
        grid=(pairs, lanes // tl),
        in_specs=[_const_spec(eh.shape), _const_spec(el.shape), t_spec, t_spec, pair_spec, pair_spec,
                  _const_spec((1, tl))],
        out_specs=pair_spec,
        out_shape=jax.ShapeDtypeStruct((pairs, 2 * n1h, lanes), BF16),
        compiler_params=_params(2),
        name="hyifft_major",
    )(eh, el, tr.reshape(pairs, n1, lanes), ti.reshape(pairs, n1, lanes), u_l, x0_l, bias_l)
    return hy.reshape(bsz, seq_len, c)


def _mlp_tail(h, g2, shift, scale, gate, w1_ref, w2_ref, fc):
    a = (_rms(h, NORM_EPS) * g2) * (1.0 + scale) + shift
    a = a.astype(BF16)
    acc = None
    for c0 in range(0, w1_ref.shape[1], fc):
        hid = jnp.maximum(_dot(a, w1_ref[:, c0:c0 + fc]), 0.0)
        part = _dot((hid * hid).astype(BF16), w2_ref[c0:c0 + fc, :])
        acc = part if acc is None else acc + part
    return h + gate * acc


def _mixmlp_kernel(x_ref, att_ref, hy_ref, mod_ref, wa_ref, wh_ref, bo_ref, g2_ref, w1_ref, w2_ref, o_ref, *, fc):
    d = x_ref.shape[2]
    mod = mod_ref[0]
    y = _dot(att_ref[0], wa_ref[...]) + _dot(hy_ref[0], wh_ref[...]) + bo_ref[...]
    h = x_ref[0] + mod[:, 2 * d:3 * d] * y
    o_ref[0] = _mlp_tail(h, g2_ref[...], mod[:, 3 * d:4 * d], mod[:, 4 * d:5 * d], mod[:, 5 * d:6 * d],
                         w1_ref, w2_ref, fc)


def _mixmlp_call(x, att, hy, mod, w_out, b_out, g2, w1, w2, tm=512, fc=1024):
    bsz, l, d = x.shape
    wa = att.shape[2]
    dff = w1.shape[1]
    row = lambda w: pl.BlockSpec((1, tm, w), lambda b, i: (b, i, 0))
    return pl.pallas_call(
        functools.partial(_mixmlp_kernel, fc=fc),
        grid=(bsz, l // tm),
        in_specs=[row(d), row(wa), row(d - wa),
                  pl.BlockSpec((1, 1, N_MOD * d), lambda b, i: (b, 0, 0)),
                  _const_spec((wa, d)), _const_spec((d - wa, d)), _const_spec((1, d)), _const_spec((1, d)),
                  _const_spec((d, dff)), _const_spec((dff, d))],
        out_specs=row(d),
        out_shape=jax.ShapeDtypeStruct((bsz, l, d), F32),
        compiler_params=_params(2),
        name="mixmlp",
    )(x, att, hy, mod, w_out[:wa], w_out[wa:], b_out.reshape(1, d), g2.reshape(1, d), w1, w2)


def _poolmlp_kernel(h_ref, hp_ref, hn_ref, mod_ref, g1_ref, pw_ref, ps_ref, g2_ref, w1_ref, w2_ref, gf_ref, o_ref,
                    *, fc, seq_len):
    tm, d = h_ref.shape[1:]
    halo = POOL_HALO
    rows = tm + 2 * halo
    mod = mod_ref[0]
    h = h_ref[0]
    hx = jnp.concatenate([hp_ref[0], h, hn_ref[0]], axis=0)
    a = (_rms(hx, NORM_EPS) * g1_ref[...]) * (1.0 + mod[:, d:2 * d]) + mod[:, 0:d]
    t = lax.broadcasted_iota(jnp.int32, (rows, 1), 0) + (pl.program_id(1) * tm - halo)
    a = jnp.where((t >= 0) & (t < seq_len), a, 0.0)
    tc = t[halo:halo + tm]
    gd = d // len(POOL_WINDOWS)
    ys = []
    for g, win in enumerate(POOL_WINDOWS):
        ag = a[:, g * gd:(g + 1) * gd]
        s = ag + pltpu.roll(ag, 1, 0)
        w = 2
        while w < win:
            s = pltpu.roll(s, w // 2, 0) + pltpu.roll(s, rows - w // 2, 0)
            w *= 2
        cnt = (jnp.minimum(tc + (win - win // 2), seq_len) - jnp.maximum(tc - win // 2, 0)).astype(F32)
        dlt = s[halo:halo + tm] / cnt - ag[halo:halo + tm]
        ys.append(_dot(dlt.astype(BF16), pw_ref[g]))
    y = jnp.concatenate(ys, axis=1) * ps_ref[...]
    h1 = h + mod[:, 2 * d:3 * d] * y
    h2 = _mlp_tail(h1, g2_ref[...], mod[:, 3 * d:4 * d], mod[:, 4 * d:5 * d], mod[:, 5 * d:6 * d], w1_ref, w2_ref, fc)
    o_ref[0] = _rms(h2, NORM_EPS) * gf_ref[...]


def _poolmlp_call(h, mod, g1, pool_w, pool_scale, g2, w1, w2, gf, tm=512, fc=1024):
    bsz, l, d = h.shape
    dff = w1.shape[1]
    halo = POOL_HALO
    nb = tm // halo
    last = l // halo - 1
    row = pl.BlockSpec((1, tm, d), lambda b, i: (b, i, 0))
    return pl.pallas_call(
        functools.partial(_poolmlp_kernel, fc=fc, seq_len=l),
        grid=(bsz, l // tm),
        in_specs=[row,
                  pl.BlockSpec((1, halo, d), lambda b, i: (b, jnp.maximum(i * nb - 1, 0), 0)),
                  pl.BlockSpec((1, halo, d), lambda b, i: (b, jnp.minimum((i + 1) * nb, last), 0)),
                  pl.BlockSpec((1, 1, N_MOD * d), lambda b, i: (b, 0, 0)),
                  _const_spec((1, d)), _const_spec(pool_w.shape), _const_spec((1, d)), _const_spec((1, d)),
                  _const_spec((d, dff)), _const_spec((dff, d)), _const_spec((1, d))],
        out_specs=row,
        out_shape=jax.ShapeDtypeStruct((bsz, l, d), F32),
        compiler_params=_params(2),
        name="poolmlp",
    )(h, h, h, mod, g1.reshape(1, d), pool_w, pool_scale.reshape(1, d), g2.reshape(1, d), w1, w2, gf.reshape(1, d))


def kernel(x, c, ctx, c_ctx, ada_w, ada_b, norm1_g, norm2_g, mix_w_in, mix_b_in, mix_w_out, mix_b_out, lam_q1, lam_k1, lam_q2, lam_k2, subln_g, hy_conv_w, hy_conv_b, hy_pos_w1, hy_pos_b1, hy_freq1, hy_pos_w2, hy_pos_b2, hy_freq2, hy_pos_w3, hy_bias, pool_w, pool_scale, mlp_w1, mlp_w2, final_g):
    bsz, seq_len, d = x.shape
    depth = ada_w.shape[0]
    assert depth == 2 and bsz % 2 == 0 and bsz < MOD_ROWS and seq_len % GRID_W == 0
    att_w = ATT_HEADS * ATT_V_DIM
    q_cols = k_cols = ATT_HEADS * 2 * ATT_QK_DIM
    kv_start, hy_start = q_cols, q_cols + k_cols + att_w
    in_cols = mix_w_in.shape[2]

    cv = jnp.concatenate([c, c_ctx[None, :], jnp.zeros((MOD_ROWS - bsz - 1, d), F32)], axis=0)
    mod = _mod_call(cv, ada_w, ada_b)
    mod_l = [mod[i, :bsz].reshape(bsz, 1, N_MOD * d) for i in range(depth)]

    w_in = mix_w_in[0].astype(BF16)
    lam_init = 0.8 - 0.6 * math.exp(-0.3 * 0)
    q, k, v, p_hy = _normproj_call(
        x, mod_l[0][:, :, 0:d], mod_l[0][:, :, d:2 * d], norm1_g[0], w_in, mix_b_in[0], _rope_tables(seq_len),
        outs=[(0, q_cols, True, ATT_QK_DIM ** -0.5), (kv_start, kv_start + k_cols, True, 1.0),
              (kv_start + k_cols, hy_start, False, 1.0), (hy_start, in_cols, False, 1.0)],
        out_dtypes=[BF16, BF16, BF16, F32], tm=512, name="normproj_lat")
    mod_c = mod[0, bsz:bsz + 1].reshape(1, 1, N_MOD * d)
    kc, vc = _normproj_call(
        ctx, mod_c[:, :, 0:d], mod_c[:, :, d:2 * d], norm1_g[0], w_in[:, kv_start:hy_start],
        mix_b_in[0, kv_start:hy_start], None,
        outs=[(0, k_cols, False, 1.0), (k_cols, k_cols + att_w, False, 1.0)],
        out_dtypes=[BF16, BF16], tm=ctx.shape[1], name="normproj_ctx")
    lamv = jnp.stack([lam_q1[0], lam_k1[0], lam_q2[0], lam_k2[0]], axis=0)
    att = _attn_call(q, kc, vc, k, v, lamv, subln_g[0], lam_init)
    filt = (hy_pos_w1[0], hy_pos_b1[0], hy_freq1[0], hy_pos_w2[0], hy_pos_b2[0], hy_freq2[0], hy_pos_w3[0])
    hy = _hyena_call(p_hy, hy_conv_w[0], hy_conv_b[0], filt, hy_bias[0])
    h = _mixmlp_call(x, att, hy, mod_l[0], mix_w_out[0].astype(BF16), mix_b_out[0], norm2_g[0],
                     mlp_w1[0].astype(BF16), mlp_w2[0].astype(BF16))

    return _poolmlp_call(h, mod_l[1], norm1_g[1], pool_w[0].astype(BF16), pool_scale[0], norm2_g[1],
                         mlp_w1[1].astype(BF16), mlp_w2[1].astype(BF16), final_g)
```

```python
import functools
import math

import numpy as np
import jax
import jax.numpy as jnp
from jax import lax
from jax.experimental import pallas as pl
from jax.experimental.pallas import tpu as pltpu

F32 = jnp.float32
BF16 = jnp.bfloat16

GRID_W = 64
N_MOD = 6
ATT_HEADS = 4
ATT_V_DIM = 128
ATT_QK_DIM = 64
ROPE_BASE = 10000.0
HY_POS_EMB = 33
HY_DECAY_TARGET = 1e-2
HY_FAST_DECAY_PCT = 0.3
HY_SLOW_DECAY_PCT = 1.5
POOL_WINDOWS = (2, 4, 8, 16)
NORM_EPS = 1e-6
SUBLN_EPS = 1e-5
FFT_N2 = 64
POOL_HALO = 8
MOD_ROWS = 8

VMEM_LIMIT_BYTES = 56 * 1024 * 1024


def _params(n_grid_dims):
    return pltpu.CompilerParams(
        dimension_semantics=("arbitrary",) * n_grid_dims,
        vmem_limit_bytes=VMEM_LIMIT_BYTES,
    )


def _const_spec(shape):
    nd = len(shape)
    return pl.BlockSpec(shape, lambda *_: (0,) * nd, pipeline_mode=pl.Buffered(1))


def _split_bf16(a):
    hi = a.astype(BF16)
    lo = (a - hi.astype(F32)).astype(BF16)
    return hi, lo


def _dot(a, b):
    return jnp.dot(a, b, preferred_element_type=F32)


def _dot3(a_hi, a_lo, b_hi, b_lo):
    return _dot(a_hi, b_hi) + (_dot(a_lo, b_hi) + _dot(a_hi, b_lo))


def _dot3f(a, b):
    a_hi, a_lo = _split_bf16(a)
    b_hi, b_lo = _split_bf16(b)
    return _dot3(a_hi, a_lo, b_hi, b_lo)


def _rms(x, eps):
    return x * lax.rsqrt(jnp.mean(x * x, axis=-1, keepdims=True) + eps)


def _mod_kernel(cv_ref, w_ref, b_ref, o_ref):
    cv = cv_ref[...]
    s = cv / (1.0 + jnp.exp(-cv))
    o_ref[0] = _dot3f(s, w_ref[0]) + b_ref[0]


def _mod_call(cv, ada_w, ada_b, tn=1536):
    depth, d, n = ada_w.shape
    return pl.pallas_call(
        _mod_kernel,
        grid=(depth, n // tn),
        in_specs=[
            pl.BlockSpec((MOD_ROWS, d), lambda i, j: (0, 0)),
            pl.BlockSpec((1, d, tn), lambda i, j: (i, 0, j)),
            pl.BlockSpec((1, 1, tn), lambda i, j: (i, 0, j)),
        ],
        out_specs=pl.BlockSpec((1, MOD_ROWS, tn), lambda i, j: (i, 0, j)),
        out_shape=jax.ShapeDtypeStruct((depth, MOD_ROWS, n), F32),
        compiler_params=_params(2),
        name="mod",
    )(cv, ada_w, ada_b.reshape(depth, 1, n))


def _normproj_kernel(*refs, outs, use_rope):
    x_ref, sh_ref, sc_ref, g_ref, w_ref, b_ref = refs[:6]
    rest = refs[6:]
    if use_rope:
        cos_ref, sa_ref, sb_ref = rest[:3]
        rest = rest[3:]
    a = _rms(x_ref[0], NORM_EPS) * g_ref[...]
    a = a * (1.0 + sc_ref[0]) + sh_ref[0]
    p = _dot(a.astype(BF16), w_ref[...]) + b_ref[...]
    for o_ref, (lo, hi, rope, scale) in zip(rest, outs):
        if not rope:
            o_ref[0] = p[:, lo:hi].astype(o_ref.dtype)
            continue
        cos, sa, sb = cos_ref[...], sa_ref[...], sb_ref[...]
        for c0 in range(lo, hi, 128):
            blk = p[:, c0:c0 + 128]
            r = blk * cos + pltpu.roll(blk, 16, 1) * sa + pltpu.roll(blk, 112, 1) * sb
            if scale != 1.0:
                r = r * scale
            o_ref[0, :, c0 - lo:c0 - lo + 128] = r.astype(o_ref.dtype)


def _normproj_call(x, shift, scale, g, w_bf16, b, rope_tabs, outs, out_dtypes, tm, name):
    bsz, s, d = x.shape
    n = w_bf16.shape[1]
    per_batch = shift.shape[0] == bsz
    vec_map = (lambda bi, i: (bi, 0, 0)) if per_batch else (lambda bi, i: (0, 0, 0))
    in_specs = [
        pl.BlockSpec((1, tm, d), lambda bi, i: (bi, i, 0)),
        pl.BlockSpec((1, 1, d), vec_map),
        pl.BlockSpec((1, 1, d), vec_map),
        _const_spec((1, d)),
        _const_spec((d, n)),
        _const_spec((1, n)),
    ]
    args = [x, shift, scale, g.reshape(1, d), w_bf16, b.reshape(1, n)]
    use_rope = rope_tabs is not None
    if use_rope:
        in_specs += [pl.BlockSpec((tm, 128), lambda bi, i: (i, 0))] * 3
        args += list(rope_tabs)
    out_specs = [pl.BlockSpec((1, tm, hi - lo), lambda bi, i: (bi, i, 0)) for lo, hi, _, _ in outs]
    out_shape = [jax.ShapeDtypeStruct((bsz, s, hi - lo), dt) for (lo, hi, _, _), dt in zip(outs, out_dtypes)]
    return pl.pallas_call(
        functools.partial(_normproj_kernel, outs=tuple(outs), use_rope=use_rope),
        grid=(bsz, s // tm),
        in_specs=in_specs,
        out_specs=out_specs,
        out_shape=out_shape,
        compiler_params=_params(2),
        name=name,
    )(*args)


def _rope_tables(seq_len):
    axis_dim = ATT_QK_DIM // 2
    n_freq = axis_dim // 2
    inv = ROPE_BASE ** (-jnp.arange(n_freq, dtype=F32) / n_freq)
    t = jnp.arange(seq_len, dtype=jnp.int32)
    row, col = t // GRID_W, t % GRID_W
    lane = np.arange(128)
    jj = lane % ATT_QK_DIM
    is_col = (jj // axis_dim) == 1
    freq = jj % n_freq
    second = (jj % axis_dim) >= n_freq
    pos = jnp.where(jnp.asarray(is_col)[None, :], col[:, None], row[:, None]).astype(F32)
    ang = pos * inv[jnp.asarray(freq)][None, :]
    cos, sin = jnp.cos(ang), jnp.sin(ang)
    second = jnp.asarray(second)[None, :]
    return cos, jnp.where(second, sin, 0.0), jnp.where(second, 0.0, -sin)


def _attn_kernel(lam_ref, q_ref, kc_ref, vc_ref, kl_ref, vl_ref, g_ref, o_ref, *, tk, lam_init):
    q = q_ref[0]
    tq = q.shape[0]
    lane = lax.broadcasted_iota(jnp.int32, q.shape, 1)
    zero = jnp.zeros_like(q)
    q2 = jnp.concatenate([jnp.where(lane < ATT_QK_DIM, q, zero), jnp.where(lane >= ATT_QK_DIM, q, zero)], axis=0)

    def chunk(k, v, m, acc):
        s = lax.dot_general(q2, k, (((1,), (1,)), ((), ())), preferred_element_type=F32)
        rowmax = jnp.max(s, axis=1, keepdims=True)
        m_new = rowmax if m is None else jnp.maximum(m, rowmax)
        p = jnp.exp2(s - m_new).astype(BF16)
        pv = _dot(p, jnp.concatenate([v, jnp.ones_like(v)], axis=1))
        return m_new, (pv if acc is None else jnp.exp2(m - m_new) * acc + pv)

    m, acc = chunk(kc_ref[0], vc_ref[0], None, None)
    for j in range(kl_ref.shape[1] // tk):
        m, acc = chunk(kl_ref[0, j * tk:(j + 1) * tk, :], vl_ref[0, j * tk:(j + 1) * tk, :], m, acc)

    lamv = lam_ref[...]
    lam = (jnp.exp(jnp.sum(lamv[0:1] * lamv[1:2], axis=1, keepdims=True))
           - jnp.exp(jnp.sum(lamv[2:3] * lamv[3:4], axis=1, keepdims=True)) + lam_init)
    o_all = acc[:, :ATT_V_DIM] / acc[:, ATT_V_DIM:]
    o = o_all[:tq] - lam * o_all[tq:]
    o = _rms(o, SUBLN_EPS) * g_ref[...] * (1.0 - lam_init)
    o_ref[0] = o.astype(o_ref.dtype)


def _attn_call(q, kc, vc, kl, vl, lamv, g, lam_init, tq=256, tk=512):
    bsz, l, width = q.shape
    heads = width // ATT_V_DIM
    lc = kc.shape[1]
    hd = ATT_V_DIM
    return pl.pallas_call(
        functools.partial(_attn_kernel, tk=tk, lam_init=lam_init),
        grid=(bsz, heads, l // tq),
        in_specs=[
            _const_spec(lamv.shape),
            pl.BlockSpec((1, tq, hd), lambda b, h, i: (b, i, h)),
            pl.BlockSpec((1, lc, hd), lambda b, h, i: (b, 0, h)),
            pl.BlockSpec((1, lc, hd), lambda b, h, i: (b, 0, h)),
            pl.BlockSpec((1, l, hd), lambda b, h, i: (b, 0, h)),
            pl.BlockSpec((1, l, hd), lambda b, h, i: (b, 0, h)),
            _const_spec((1, hd)),
        ],
        out_specs=pl.BlockSpec((1, tq, hd), lambda b, h, i: (b, i, h)),
        out_shape=jax.ShapeDtypeStruct((bsz, l, width), BF16),
        compiler_params=_params(3),
        name="diffattn",
    )(lamv, q, kc, vc, kl, vl, g.reshape(1, hd))


def _filter_kernel(z_ref, w1_ref, b1_ref, f1_ref, w2_ref, b2_ref, f2_ref, w3_ref, dl_ref, o_ref, *, seq_len):
    tm = z_ref.shape[0]
    h1 = jnp.sin(f1_ref[...] * (_dot3f(z_ref[...], w1_ref[...]) + b1_ref[...]))
    h2 = jnp.sin(f2_ref[...] * (_dot3f(h1, w2_ref[...]) + b2_ref[...]))
    h = _dot3f(h2, w3_ref[...])
    row = lax.broadcasted_iota(jnp.int32, h.shape, 0) + pl.program_id(0) * tm
    col = lax.broadcasted_iota(jnp.int32, h.shape, 1)
    t = row.astype(F32) * (1.0 / (seq_len - 1))
    h = h * jnp.exp(-t * dl_ref[...])
    half = h.shape[1] // 2
    o_ref[...] = jnp.where((row == 0) & (col >= half), 0.0, h)


def _filter_call(z, w1, b1, f1, w2, b2, f2, w3, absdelta2, tm=512):
    seq_len, kz = z.shape
    hid = w2.shape[0]
    n = w3.shape[1]
    return pl.pallas_call(
        functools.partial(_filter_kernel, seq_len=seq_len),
        grid=(seq_len // tm,),
        in_specs=[
            pl.BlockSpec((tm, kz), lambda i: (i, 0)),
            _const_spec((kz, hid)), _const_spec((1, hid)), _const_spec((1, hid)),
            _const_spec((hid, hid)), _const_spec((1, hid)), _const_spec((1, hid)),
            _const_spec((hid, n)), _const_spec((1, n)),
        ],
        out_specs=pl.BlockSpec((tm, n), lambda i: (i, 0)),
        out_shape=jax.ShapeDtypeStruct((seq_len, n), F32),
        compiler_params=_params(1),
        name="hyfilter",
    )(z, w1, b1.reshape(1, hid), f1.reshape(1, hid), w2, b2.reshape(1, hid), f2.reshape(1, hid), w3, absdelta2)


def _gate_kernel(x0_ref, x1_ref, v_ref, w_ref, b_ref, u_ref, x0o_ref):
    seq_len = x0_ref.shape[1]
    row = lax.broadcasted_iota(jnp.int32, x0_ref.shape[1:], 0)
    first, last = row == 0, row == seq_len - 1

    def conv(ref, s):
        p = ref[0]
        prev = jnp.where(first, 0.0, pltpu.roll(p, 1, 0))
        nxt = jnp.where(last, 0.0, pltpu.roll(p, seq_len - 1, 0))
        return prev * w_ref[0, s:s + 1] + p * w_ref[1, s:s + 1] + nxt * w_ref[2, s:s + 1] + b_ref[s:s + 1]

    x0o_ref[0] = conv(x0_ref, 0)
    u_ref[0] = conv(v_ref, 2) * conv(x1_ref, 1)


def _gate_call(p_hy, conv_w, conv_b, ct=128):
    bsz, seq_len, c3 = p_hy.shape
    c = c3 // 3
    nb = c // ct
    spec = lambda s: pl.BlockSpec((1, seq_len, ct), lambda b, j, s=s: (b, 0, s * nb + j))
    out_spec = pl.BlockSpec((1, seq_len, ct), lambda b, j: (b, 0, j))
    return pl.pallas_call(
        _gate_kernel,
        grid=(bsz, nb),
        in_specs=[spec(0), spec(1), spec(2),
                  pl.BlockSpec((3, 3, ct), lambda b, j: (0, 0, j)),
                  pl.BlockSpec((3, ct), lambda b, j: (0, j))],
        out_specs=[out_spec, out_spec],
        out_shape=[jax.ShapeDtypeStruct((bsz, seq_len, c), F32)] * 2,
        compiler_params=_params(2),
        name="hygate",
    )(p_hy, p_hy, p_hy, conv_w.reshape(3, 3, c), conv_b.reshape(3, c))


def _fft_major_kernel(fh_ref, fl_ref, z_ref, or_ref, oi_ref):
    zh, zl = _split_bf16(z_ref[0])
    out = _dot3(fh_ref[...], fl_ref[...], zh, zl)
    rows = out.shape[0] // 2
    or_ref[0] = out[:rows]
    oi_ref[0] = out[rows:]


def _fft_major_call(f_blk, z, tl, name):
    fh, fl = _split_bf16(f_blk)
    groups, rows_in, lanes = z.shape
    n_out_rows = f_blk.shape[0] // 2
    out_spec = pl.BlockSpec((1, n_out_rows, tl), lambda p, j: (p, 0, j))
    return pl.pallas_call(
        _fft_major_kernel,
        grid=(groups, lanes // tl),
        in_specs=[_const_spec(fh.shape), _const_spec(fl.shape),
                  pl.BlockSpec((1, rows_in, tl), lambda p, j: (p, 0, j))],
        out_specs=[out_spec, out_spec],
        out_shape=[jax.ShapeDtypeStruct((groups, n_out_rows, lanes), F32)] * 2,
        compiler_params=_params(2),
        name=name,
    )(fh, fl, z)


def _twiddle(ar, ai, twr, twi, reps):
    twr = jnp.concatenate([twr] * reps, axis=1)
    twi = jnp.concatenate([twi] * reps, axis=1)
    return ar * twr - ai * twi, ar * twi + ai * twr


def _fft_minor_filter_kernel(gh_ref, gl_ref, twr_ref, twi_ref, ar_ref, ai_ref, kr_ref, ki_ref):
    n2 = FFT_N2
    c = kr_ref.shape[1]
    reps = ar_ref.shape[2] // twr_ref.shape[1]

    def body(kk, _):
        rows = pl.ds(pl.multiple_of(kk * n2, n2), n2)
        sr, si = _twiddle(ar_ref[0, rows, :], ai_ref[0, rows, :], twr_ref[rows, :], twi_ref[rows, :], reps)
        sh, sl = _split_bf16(jnp.concatenate([sr, si], axis=0))
        x = _dot3(gh_ref[...], gl_ref[...], sh, sl)
        xr, xi = x[:n2], x[n2:]
        kr_ref[rows, :] = xr[:, :c] + xr[:, c:]
        ki_ref[rows, :] = xi[:, :c] - xi[:, c:]
        return 0

    lax.fori_loop(0, ar_ref.shape[1] // n2, body, 0)


def _fft_minor_conv_kernel(gh_ref, gl_ref, gch_ref, gcl_ref, twr_ref, twi_ref, kr_ref, ki_ref,
                           ar_ref, ai_ref, tr_ref, ti_ref):
    n2 = FFT_N2
    reps = ar_ref.shape[2] // twr_ref.shape[1]

    def body(kk, _):
        rows = pl.ds(pl.multiple_of(kk * n2, n2), n2)
        twr = jnp.concatenate([twr_ref[rows, :]] * reps, axis=1)
        twi = jnp.concatenate([twi_ref[rows, :]] * reps, axis=1)
        ar, ai = ar_ref[0, rows, :], ai_ref[0, rows, :]
        sr, si = ar * twr - ai * twi, ar * twi + ai * twr
        sh, sl = _split_bf16(jnp.concatenate([sr, si], axis=0))
        x = _dot3(gh_ref[...], gl_ref[...], sh, sl)
        xr, xi = x[:n2], x[n2:]
        kr, ki = kr_ref[rows, :], ki_ref[rows, :]
        yr, yi = xr * kr - xi * ki, xr * ki + xi * kr
        yh, yl = _split_bf16(jnp.concatenate([yr, yi], axis=0))
        t = _dot3(gch_ref[...], gcl_ref[...], yh, yl)
        tr, ti = t[:n2], t[n2:]
        tr_ref[0, rows, :] = tr * twr + ti * twi
        ti_ref[0, rows, :] = ti * twr - tr * twi
        return 0

    lax.fori_loop(0, ar_ref.shape[1] // n2, body, 0)


def _ifft_major_kernel(eh_ref, el_ref, tr_ref, ti_ref, u_ref, x0_ref, bias_ref, o_ref, *, inv_n):
    th, tl = _split_bf16(jnp.concatenate([tr_ref[0], ti_ref[0]], axis=0))
    y = _dot3(eh_ref[...], el_ref[...], th, tl) * inv_n
    o_ref[0] = (x0_ref[0] * (y + u_ref[0] * bias_ref[...])).astype(o_ref.dtype)


def _dft_blocks(seq_len):
    n = 2 * seq_len
    n2 = FFT_N2
    n1 = n // n2
    n1h = n1 // 2
    k1 = np.arange(n1)[:, None].astype(np.float64)
    a = 2.0 * np.pi * k1 * np.arange(n1h)[None, :] / n1
    fr, fi = np.cos(a), -np.sin(a)
    f_cplx = np.block([[fr, -fi], [fi, fr]])
    f_real = np.concatenate([fr, fi], axis=0)
    e_cplx = np.block([[fr.T, fi.T], [-fi.T, fr.T]])
    b = 2.0 * np.pi * np.arange(n2)[:, None] * np.arange(n2)[None, :] / n2
    gr, gi = np.cos(b), -np.sin(b)
    g_fwd = np.block([[gr, -gi], [gi, gr]])
    g_inv = np.block([[gr, gi], [-gi, gr]])
    kk = np.arange(n1)[:, None] * np.arange(n2)[None, :]
    ang = (2.0 * np.pi / n) * kk.reshape(n1 * n2, 1)
    twr = np.broadcast_to(np.cos(ang), (n1 * n2, 128))
    twi = np.broadcast_to(-np.sin(ang), (n1 * n2, 128))
    as32 = lambda m: jnp.asarray(np.ascontiguousarray(m), dtype=F32)
    return dict(n=n, n1=n1, n1h=n1h, f_cplx=as32(f_cplx), f_real=as32(f_real), e_cplx=as32(e_cplx),
                g_fwd=as32(g_fwd), g_inv=as32(g_inv), twr=as32(twr), twi=as32(twi))


def _hyena_call(p_hy, conv_w, conv_b, filt, hy_bias):
    bsz, seq_len, c3 = p_hy.shape
    c = c3 // 3
    n2 = FFT_N2
    dft = _dft_blocks(seq_len)
    n, n1, n1h = dft["n"], dft["n1"], dft["n1h"]
    pairs = bsz // 2
    gh, gl = _split_bf16(dft["g_fwd"])
    gch, gcl = _split_bf16(dft["g_inv"])

    w1, b1, f1, w2, b2, f2, w3 = filt
    hid = w2.shape[0]
    bands = (HY_POS_EMB - 1) // 2
    t = jnp.linspace(0.0, 1.0, seq_len, dtype=F32)[:, None]
    freqs = jnp.linspace(1e-4, bands - 1, bands, dtype=F32)
    ang = (2.0 * math.pi / seq_len) * jnp.arange(seq_len, dtype=F32)[:, None] * freqs[None, :]
    z = jnp.concatenate([t, jnp.cos(ang), -jnp.sin(ang), jnp.zeros((seq_len, hid - HY_POS_EMB), F32)], axis=-1)
    w1p = jnp.concatenate([w1, jnp.zeros((hid - HY_POS_EMB, hid), F32)], axis=0)
    max_decay = math.log(HY_DECAY_TARGET) / HY_FAST_DECAY_PCT
    min_decay = math.log(HY_DECAY_TARGET) / HY_SLOW_DECAY_PCT
    absdelta = jnp.abs(jnp.linspace(min_decay, max_decay, c, dtype=F32))
    absdelta2 = jnp.concatenate([absdelta, absdelta])[None, :]
    hcat = _filter_call(z, w1p, b1, f1, w2, b2, f2, w3, absdelta2)
    hr, hi = _fft_major_call(dft["f_real"], hcat.reshape(1, n1h, n2 * 2 * c), 4096, "hyfft_filt_major")
    rb = 8 * n2
    kr, ki = pl.pallas_call(
        _fft_minor_filter_kernel,
        grid=(n // rb,),
        in_specs=[_const_spec(gh.shape), _const_spec(gl.shape),
                  pl.BlockSpec((rb, 128), lambda i: (i, 0)), pl.BlockSpec((rb, 128), lambda i: (i, 0)),
                  pl.BlockSpec((1, rb, 2 * c), lambda i: (0, i, 0)), pl.BlockSpec((1, rb, 2 * c), lambda i: (0, i, 0))],
        out_specs=[pl.BlockSpec((rb, c), lambda i: (i, 0))] * 2,
        out_shape=[jax.ShapeDtypeStruct((n, c), F32)] * 2,
        compiler_params=_params(1),
        name="hyfft_filt_minor",
    )(gh, gl, dft["twr"], dft["twi"], hr.reshape(1, n, 2 * c), hi.reshape(1, n, 2 * c))

    u, x0 = _gate_call(p_hy, conv_w, conv_b)
    lanes = n2 * c
    u_l = u.reshape(pairs, 2 * n1h, lanes)
    x0_l = x0.reshape(pairs, 2 * n1h, lanes)
    ar, ai = _fft_major_call(dft["f_cplx"], u_l, 2048, "hyfft_major")
    blk = pl.BlockSpec((1, rb, c), lambda p, i: (p, i, 0))
    tr, ti = pl.pallas_call(
        _fft_minor_conv_kernel,
        grid=(pairs, n // rb),
        in_specs=[_const_spec(gh.shape), _const_spec(gl.shape), _const_spec(gch.shape), _const_spec(gcl.shape),
                  pl.BlockSpec((rb, 128), lambda p, i: (i, 0)), pl.BlockSpec((rb, 128), lambda p, i: (i, 0)),
                  pl.BlockSpec((rb, c), lambda p, i: (i, 0)), pl.BlockSpec((rb, c), lambda p, i: (i, 0)),
                  blk, blk],
        out_specs=[blk, blk],
        out_shape=[jax.ShapeDtypeStruct((pairs, n, c), F32)] * 2,
        compiler_params=_params(2),
        name="hyfft_minor",
    )(gh, gl, gch, gcl, dft["twr"], dft["twi"], kr, ki, ar.reshape(pairs, n, c), ai.reshape(pairs, n, c))

    tl = 2048
    eh, el = _split_bf16(dft["e_cplx"])
    bias_l = jnp.tile(hy_bias, tl // c)[None, :]
    t_spec = pl.BlockSpec((1, n1, tl), lambda p, j: (p, 0, j))
    pair_spec = pl.BlockSpec((1, 2 * n1h, tl), lambda p, j: (p, 0, j))
    hy = pl.pallas_call(
        functools.partial(_ifft_major_kernel, inv_n=1.0 / n),
        grid=(pairs, lanes // tl),
        in_specs=[_const_spec(eh.shape), _const_spec(el.shape), t_spec, t_spec, pair_spec, pair_spec,
                  _const_spec((1, tl))],
        out_specs=pair_spec,
        out_shape=jax.ShapeDtypeStruct((pairs, 2 * n1h, lanes), BF16),
        compiler_params=_params(2),
        name="hyifft_major",
    )(eh, el, tr.reshape(pairs, n1, lanes), ti.reshape(pairs, n1, lanes), u_l, x0_l, bias_l)
    return hy.reshape(bsz, seq_len, c)


def _mlp_tail(h, g2, shift, scale, gate, w1_ref, w2_ref, fc):
    a = (_rms(h, NORM_EPS) * g2) * (1.0 + scale) + shift
    a = a.astype(BF16)
    acc = None
    for c0 in range(0, w1_ref.shape[1], fc):
        hid = jnp.maximum(_dot(a, w1_ref[:, c0:c0 + fc]), 0.0)
        part = _dot((hid * hid).astype(BF16), w2_ref[c0:c0 + fc, :])
        acc = part if acc is None else acc + part
    return h + gate * acc


def _mixmlp_kernel(x_ref, att_ref, hy_ref, mod_ref, wa_ref, wh_ref, bo_ref, g2_ref, w1_ref, w2_ref, o_ref, *, fc):
    d = x_ref.shape[2]
    mod = mod_ref[0]
    y = _dot(att_ref[0], wa_ref[...]) + _dot(hy_ref[0], wh_ref[...]) + bo_ref[...]
    h = x_ref[0] + mod[:, 2 * d:3 * d] * y
    o_ref[0] = _mlp_tail(h, g2_ref[...], mod[:, 3 * d:4 * d], mod[:, 4 * d:5 * d], mod[:, 5 * d:6 * d],
                         w1_ref, w2_ref, fc)


def _mixmlp_call(x, att, hy, mod, w_out, b_out, g2, w1, w2, tm=512, fc=1024):
    bsz, l, d = x.shape
    wa = att.shape[2]
    dff = w1.shape[1]
    row = lambda w: pl.BlockSpec((1, tm, w), lambda b, i: (b, i, 0))
    return pl.pallas_call(
        functools.partial(_mixmlp_kernel, fc=fc),
        grid=(bsz, l // tm),
        in_specs=[row(d), row(wa), row(d - wa),
                  pl.BlockSpec((1, 1, N_MOD * d), lambda b, i: (b, 0, 0)),
                  _const_spec((wa, d)), _const_spec((d - wa, d)), _const_spec((1, d)), _const_spec((1, d)),
                  _const_spec((d, dff)), _const_spec((dff, d))],
        out_specs=row(d),
        out_shape=jax.ShapeDtypeStruct((bsz, l, d), F32),
        compiler_params=_params(2),
        name="mixmlp",
    )(x, att, hy, mod, w_out[:wa], w_out[wa:], b_out.reshape(1, d), g2.reshape(1, d), w1, w2)


def _poolmlp_kernel(h_ref, hp_ref, hn_ref, mod_ref, g1_ref, pw_ref, ps_ref, g2_ref, w1_ref, w2_ref, gf_ref, o_ref,
                    *, fc, seq_len):
    tm, d = h_ref.shape[1:]
    halo = POOL_HALO
    rows = tm + 2 * halo
    mod = mod_ref[0]
    h = h_ref[0]
    hx = jnp.concatenate([hp_ref[0], h, hn_ref[0]], axis=0)
    a = (_rms(hx, NORM_EPS) * g1_ref[...]) * (1.0 + mod[:, d:2 * d]) + mod[:, 0:d]
    t = lax.broadcasted_iota(jnp.int32, (rows, 1), 0) + (pl.program_id(1) * tm - halo)
    a = jnp.where((t >= 0) & (t < seq_len), a, 0.0)
    tc = t[halo:halo + tm]
    gd = d // len(POOL_WINDOWS)
    ys = []
    for g, win in enumerate(POOL_WINDOWS):
        ag = a[:, g * gd:(g + 1) * gd]
        s = ag + pltpu.roll(ag, 1, 0)
        w = 2
        while w < win:
            s = pltpu.roll(s, w // 2, 0) + pltpu.roll(s, rows - w // 2, 0)
            w *= 2
        cnt = (jnp.minimum(tc + (win - win // 2), seq_len) - jnp.maximum(tc - win // 2, 0)).astype(F32)
        dlt = s[halo:halo + tm] / cnt - ag[halo:halo + tm]
        ys.append(_dot(dlt.astype(BF16), pw_ref[g]))
    y = jnp.concatenate(ys, axis=1) * ps_ref[...]
    h1 = h + mod[:, 2 * d:3 * d] * y
    h2 = _mlp_tail(h1, g2_ref[...], mod[:, 3 * d:4 * d], mod[:, 4 * d:5 * d], mod[:, 5 * d:6 * d], w1_ref, w2_ref, fc)
    o_ref[0] = _rms(h2, NORM_EPS) * gf_ref[...]


def _poolmlp_call(h, mod, g1, pool_w, pool_scale, g2, w1, w2, gf, tm=512, fc=1024):
    bsz, l, d = h.shape
    dff = w1.shape[1]
    halo = POOL_HALO
    nb = tm // halo
    last = l // halo - 1
    row = pl.BlockSpec((1, tm, d), lambda b, i: (b, i, 0))
    return pl.pallas_call(
        functools.partial(_poolmlp_kernel, fc=fc, seq_len=l),
        grid=(bsz, l // tm),
        in_specs=[row,
                  pl.BlockSpec((1, halo, d), lambda b, i: (b, jnp.maximum(i * nb - 1, 0), 0)),
                  pl.BlockSpec((1, halo, d), lambda b, i: (b, jnp.minimum((i + 1) * nb, last), 0)),
                  pl.BlockSpec((1, 1, N_MOD * d), lambda b, i: (b, 0, 0)),
                  _const_spec((1, d)), _const_spec(pool_w.shape), _const_spec((1, d)), _const_spec((1, d)),
                  _const_spec((d, dff)), _const_spec((dff, d)), _const_spec((1, d))],
        out_specs=row,
        out_shape=jax.ShapeDtypeStruct((bsz, l, d), F32),
        compiler_params=_params(2),
        name="poolmlp",
    )(h, h, h, mod, g1.reshape(1, d), pool_w, pool_scale.reshape(1, d), g2.reshape(1, d), w1, w2, gf.reshape(1, d))


def kernel(x, c, ctx, c_ctx, ada_w, ada_b, norm1_g, norm2_g, mix_w_in, mix_b_in, mix_w_out, mix_b_out, lam_q1, lam_k1, lam_q2, lam_k2, subln_g, hy_conv_w, hy_conv_b, hy_pos_w1, hy_pos_b1, hy_freq1, hy_pos_w2, hy_pos_b2, hy_freq2, hy_pos_w3, hy_bias, pool_w, pool_scale, mlp_w1, mlp_w2, final_g):
    bsz, seq_len, d = x.shape
    depth = ada_w.shape[0]
    assert depth == 2 and bsz % 2 == 0 and bsz < MOD_ROWS and seq_len % GRID_W == 0
    att_w = ATT_HEADS * ATT_V_DIM
    q_cols = k_cols = ATT_HEADS * 2 * ATT_QK_DIM
    kv_start, hy_start = q_cols, q_cols + k_cols + att_w
    in_cols = mix_w_in.shape[2]

    cv = jnp.concatenate([c, c_ctx[None, :], jnp.zeros((MOD_ROWS - bsz - 1, d), F32)], axis=0)
    mod = _mod_call(cv, ada_w, ada_b)
    mod_l = [mod[i, :bsz].reshape(bsz, 1, N_MOD * d) for i in range(depth)]

    w_in = mix_w_in[0].astype(BF16)
    lam_init = 0.8 - 0.6 * math.exp(-0.3 * 0)
    q, k, v, p_hy = _normproj_call(
        x, mod_l[0][:, :, 0:d], mod_l[0][:, :, d:2 * d], norm1_g[0], w_in, mix_b_in[0], _rope_tables(seq_len),
        outs=[(0, q_cols, True, ATT_QK_DIM ** -0.5 * math.log2(math.e)), (kv_start, kv_start + k_cols, True, 1.0),
              (kv_start + k_cols, hy_start, False, 1.0), (hy_start, in_cols, False, 1.0)],
        out_dtypes=[BF16, BF16, BF16, F32], tm=512, name="normproj_lat")
    mod_c = mod[0, bsz:bsz + 1].reshape(1, 1, N_MOD * d)
    kc, vc = _normproj_call(
        ctx, mod_c[:, :, 0:d], mod_c[:, :, d:2 * d], norm1_g[0], w_in[:, kv_start:hy_start],
        mix_b_in[0, kv_start:hy_start], None,
        outs=[(0, k_cols, False, 1.0), (k_cols, k_cols + att_w, False, 1.0)],
        out_dtypes=[BF16, BF16], tm=ctx.shape[1], name="normproj_ctx")
    lamv = jnp.stack([lam_q1[0], lam_k1[0], lam_q2[0], lam_k2[0]], axis=0)
    att = _attn_call(q, kc, vc, k, v, lamv, subln_g[0], lam_init)
    filt = (hy_pos_w1[0], hy_pos_b1[0], hy_freq1[0], hy_pos_w2[0], hy_pos_b2[0], hy_freq2[0], hy_pos_w3[0])
    hy = _hyena_call(p_hy, hy_conv_w[0], hy_conv_b[0], filt, hy_bias[0])
    h = _mixmlp_call(x, att, hy, mod_l[0], mix_w_out[0].astype(BF16), mix_b_out[0], norm2_g[0],
                     mlp_w1[0].astype(BF16), mlp_w2[0].astype(BF16))

    return _poolmlp_call(h, mod_l[1], norm1_g[1], pool_w[0].astype(BF16), pool_scale[0], norm2_g[1],
                         mlp_w1[1].astype(BF16), mlp_w2[1].astype(BF16), final_g)
```

```python
import functools
import math

import numpy as np
import jax
import jax.numpy as jnp
from jax import lax
from jax.experimental import pallas as pl
from jax.experimental.pallas import tpu as pltpu

F32 = jnp.float32
BF16 = jnp.bfloat16

GRID_W = 64
N_MOD = 6
ATT_HEADS = 4
ATT_V_DIM = 128
ATT_QK_DIM = 64
ROPE_BASE = 10000.0
HY_POS_EMB = 33
HY_DECAY_TARGET = 1e-2
HY_FAST_DECAY_PCT = 0.3
HY_SLOW_DECAY_PCT = 1.5
POOL_WINDOWS = (2, 4, 8, 16)
NORM_EPS = 1e-6
SUBLN_EPS = 1e-5
SUBLANES = 8
LANES = 128
FFT_N2 = 64
POOL_HALO = 8
MOD_ROWS = 8

VMEM_LIMIT_BYTES = 56 * 1024 * 1024


def _params(n_grid_dims):
    return pltpu.CompilerParams(
        dimension_semantics=("arbitrary",) * n_grid_dims,
        vmem_limit_bytes=VMEM_LIMIT_BYTES,
    )


def _const_spec(shape):
    nd = len(shape)
    return pl.BlockSpec(shape, lambda *_: (0,) * nd, pipeline_mode=pl.Buffered(1))


def _split_bf16(a):
    hi = a.astype(BF16)
    lo = (a - hi.astype(F32)).astype(BF16)
    return hi, lo


def _dot(a, b):
    return jnp.dot(a, b, preferred_element_type=F32)


def _dot3(a_hi, a_lo, b_hi, b_lo):
    return _dot(a_hi, b_hi) + (_dot(a_lo, b_hi) + _dot(a_hi, b_lo))


def _dot3f(a, b):
    a_hi, a_lo = _split_bf16(a)
    b_hi, b_lo = _split_bf16(b)
    return _dot3(a_hi, a_lo, b_hi, b_lo)


def _rms(x, eps):
    return x * lax.rsqrt(jnp.mean(x * x, axis=-1, keepdims=True) + eps)


def _mod_kernel(cv_ref, w_ref, b_ref, o_ref):
    cv = cv_ref[...]
    s = cv / (1.0 + jnp.exp(-cv))
    o_ref[0] = _dot3f(s, w_ref[0]) + b_ref[0]


def _mod_call(cv, ada_w, ada_b, tn=1536):
    depth, d, n = ada_w.shape
    return pl.pallas_call(
        _mod_kernel,
        grid=(depth, n // tn),
        in_specs=[
            pl.BlockSpec((MOD_ROWS, d), lambda i, j: (0, 0)),
            pl.BlockSpec((1, d, tn), lambda i, j: (i, 0, j)),
            pl.BlockSpec((1, 1, tn), lambda i, j: (i, 0, j)),
        ],
        out_specs=pl.BlockSpec((1, MOD_ROWS, tn), lambda i, j: (i, 0, j)),
        out_shape=jax.ShapeDtypeStruct((depth, MOD_ROWS, n), F32),
        compiler_params=_params(2),
        name="mod",
    )(cv, ada_w, ada_b.reshape(depth, 1, n))


def _normproj_kernel(*refs, outs, use_rope):
    x_ref, sh_ref, sc_ref, g_ref, w_ref, b_ref = refs[:6]
    rest = refs[6:]
    if use_rope:
        cos_ref, sa_ref, sb_ref = rest[:3]
        rest = rest[3:]
    a = _rms(x_ref[0], NORM_EPS) * g_ref[...]
    a = a * (1.0 + sc_ref[0]) + sh_ref[0]
    p = _dot(a.astype(BF16), w_ref[...]) + b_ref[...]
    for o_ref, (lo, hi, rope, scale) in zip(rest, outs):
        if not rope:
            o_ref[0] = p[:, lo:hi].astype(o_ref.dtype)
            continue
        cos, sa, sb = cos_ref[...], sa_ref[...], sb_ref[...]
        for c0 in range(lo, hi, 128):
            blk = p[:, c0:c0 + 128]
            r = blk * cos + pltpu.roll(blk, 16, 1) * sa + pltpu.roll(blk, 112, 1) * sb
            if scale != 1.0:
                r = r * scale
            o_ref[0, :, c0 - lo:c0 - lo + 128] = r.astype(o_ref.dtype)


def _normproj_call(x, shift, scale, g, w_bf16, b, rope_tabs, outs, out_dtypes, tm, name):
    bsz, s, d = x.shape
    n = w_bf16.shape[1]
    per_batch = shift.shape[0] == bsz
    vec_map = (lambda bi, i: (bi, 0, 0)) if per_batch else (lambda bi, i: (0, 0, 0))
    in_specs = [
        pl.BlockSpec((1, tm, d), lambda bi, i: (bi, i, 0)),
        pl.BlockSpec((1, 1, d), vec_map),
        pl.BlockSpec((1, 1, d), vec_map),
        _const_spec((1, d)),
        _const_spec((d, n)),
        _const_spec((1, n)),
    ]
    args = [x, shift, scale, g.reshape(1, d), w_bf16, b.reshape(1, n)]
    use_rope = rope_tabs is not None
    if use_rope:
        in_specs += [pl.BlockSpec((tm, 128), lambda bi, i: (i, 0))] * 3
        args += list(rope_tabs)
    out_specs = [pl.BlockSpec((1, tm, hi - lo), lambda bi, i: (bi, i, 0)) for lo, hi, _, _ in outs]
    out_shape = [jax.ShapeDtypeStruct((bsz, s, hi - lo), dt) for (lo, hi, _, _), dt in zip(outs, out_dtypes)]
    return pl.pallas_call(
        functools.partial(_normproj_kernel, outs=tuple(outs), use_rope=use_rope),
        grid=(bsz, s // tm),
        in_specs=in_specs,
        out_specs=out_specs,
        out_shape=out_shape,
        compiler_params=_params(2),
        name=name,
    )(*args)


def _rope_tables(seq_len):
    axis_dim = ATT_QK_DIM // 2
    n_freq = axis_dim // 2
    inv = ROPE_BASE ** (-jnp.arange(n_freq, dtype=F32) / n_freq)
    t = jnp.arange(seq_len, dtype=jnp.int32)
    row, col = t // GRID_W, t % GRID_W
    lane = np.arange(128)
    jj = lane % ATT_QK_DIM
    is_col = (jj // axis_dim) == 1
    freq = jj % n_freq
    second = (jj % axis_dim) >= n_freq
    pos = jnp.where(jnp.asarray(is_col)[None, :], col[:, None], row[:, None]).astype(F32)
    ang = pos * inv[jnp.asarray(freq)][None, :]
    cos, sin = jnp.cos(ang), jnp.sin(ang)
    second = jnp.asarray(second)[None, :]
    return cos, jnp.where(second, sin, 0.0), jnp.where(second, 0.0, -sin)


def _attn_kernel(lam_ref, q_ref, kc_ref, vc_ref, kl_ref, vl_ref, g_ref, o_ref, *, tk, lam_init):
    q = q_ref[0]
    tq = q.shape[0]
    lane = lax.broadcasted_iota(jnp.int32, q.shape, 1)
    zero = jnp.zeros_like(q)
    q2 = jnp.concatenate([jnp.where(lane < ATT_QK_DIM, q, zero), jnp.where(lane >= ATT_QK_DIM, q, zero)], axis=0)

    def chunk(k, v, m, acc):
        s = lax.dot_general(q2, k, (((1,), (1,)), ((), ())), preferred_element_type=F32)
        rowmax = jnp.max(s, axis=1, keepdims=True)
        m_new = rowmax if m is None else jnp.maximum(m, rowmax)
        p = jnp.exp2(s - m_new).astype(BF16)
        pv = _dot(p, jnp.concatenate([v, jnp.ones_like(v)], axis=1))
        return m_new, (pv if acc is None else jnp.exp2(m - m_new) * acc + pv)

    m, acc = chunk(kc_ref[0], vc_ref[0], None, None)
    for j in range(kl_ref.shape[1] // tk):
        m, acc = chunk(kl_ref[0, j * tk:(j + 1) * tk, :], vl_ref[0, j * tk:(j + 1) * tk, :], m, acc)

    lamv = lam_ref[...]
    lam = (jnp.exp(jnp.sum(lamv[0:1] * lamv[1:2], axis=1, keepdims=True))
           - jnp.exp(jnp.sum(lamv[2:3] * lamv[3:4], axis=1, keepdims=True)) + lam_init)
    o_all = acc[:, :ATT_V_DIM] / acc[:, ATT_V_DIM:]
    o = o_all[:tq] - lam * o_all[tq:]
    o = _rms(o, SUBLN_EPS) * g_ref[...] * (1.0 - lam_init)
    o_ref[0] = o.astype(o_ref.dtype)


def _attn_call(q, kc, vc, kl, vl, lamv, g, lam_init, tq=256, tk=512):
    bsz, l, width = q.shape
    heads = width // ATT_V_DIM
    lc = kc.shape[1]
    hd = ATT_V_DIM
    return pl.pallas_call(
        functools.partial(_attn_kernel, tk=tk, lam_init=lam_init),
        grid=(bsz, heads, l // tq),
        in_specs=[
            _const_spec(lamv.shape),
            pl.BlockSpec((1, tq, hd), lambda b, h, i: (b, i, h)),
            pl.BlockSpec((1, lc, hd), lambda b, h, i: (b, 0, h)),
            pl.BlockSpec((1, lc, hd), lambda b, h, i: (b, 0, h)),
            pl.BlockSpec((1, l, hd), lambda b, h, i: (b, 0, h)),
            pl.BlockSpec((1, l, hd), lambda b, h, i: (b, 0, h)),
            _const_spec((1, hd)),
        ],
        out_specs=pl.BlockSpec((1, tq, hd), lambda b, h, i: (b, i, h)),
        out_shape=jax.ShapeDtypeStruct((bsz, l, width), BF16),
        compiler_params=_params(3),
        name="diffattn",
    )(lamv, q, kc, vc, kl, vl, g.reshape(1, hd))


def _filter_kernel(z_ref, w1_ref, b1_ref, f1_ref, w2_ref, b2_ref, f2_ref, w3_ref, dl_ref, o_ref, *, seq_len):
    tm = z_ref.shape[0]
    h1 = jnp.sin(f1_ref[...] * (_dot3f(z_ref[...], w1_ref[...]) + b1_ref[...]))
    h2 = jnp.sin(f2_ref[...] * (_dot3f(h1, w2_ref[...]) + b2_ref[...]))
    h = _dot3f(h2, w3_ref[...])
    row = lax.broadcasted_iota(jnp.int32, h.shape, 0) + pl.program_id(0) * tm
    col = lax.broadcasted_iota(jnp.int32, h.shape, 1)
    t = row.astype(F32) * (1.0 / (seq_len - 1))
    h = h * jnp.exp(-t * dl_ref[...])
    half = h.shape[1] // 2
    o_ref[...] = jnp.where((row == 0) & (col >= half), 0.0, h)


def _filter_call(z, w1, b1, f1, w2, b2, f2, w3, absdelta2, tm=512):
    seq_len, kz = z.shape
    hid = w2.shape[0]
    n = w3.shape[1]
    return pl.pallas_call(
        functools.partial(_filter_kernel, seq_len=seq_len),
        grid=(seq_len // tm,),
        in_specs=[
            pl.BlockSpec((tm, kz), lambda i: (i, 0)),
            _const_spec((kz, hid)), _const_spec((1, hid)), _const_spec((1, hid)),
            _const_spec((hid, hid)), _const_spec((1, hid)), _const_spec((1, hid)),
            _const_spec((hid, n)), _const_spec((1, n)),
        ],
        out_specs=pl.BlockSpec((tm, n), lambda i: (i, 0)),
        out_shape=jax.ShapeDtypeStruct((seq_len, n), F32),
        compiler_params=_params(1),
        name="hyfilter",
    )(z, w1, b1.reshape(1, hid), f1.reshape(1, hid), w2, b2.reshape(1, hid), f2.reshape(1, hid), w3, absdelta2)


def _gate_kernel(x0_ref, x1_ref, v_ref, w_ref, b_ref, u_ref, x0o_ref):
    seq_len = x0_ref.shape[1]
    row = lax.broadcasted_iota(jnp.int32, x0_ref.shape[1:], 0)
    first, last = row == 0, row == seq_len - 1

    def conv(ref, s):
        p = ref[0]
        prev = jnp.where(first, 0.0, pltpu.roll(p, 1, 0))
        nxt = jnp.where(last, 0.0, pltpu.roll(p, seq_len - 1, 0))
        return prev * w_ref[0, s:s + 1] + p * w_ref[1, s:s + 1] + nxt * w_ref[2, s:s + 1] + b_ref[s:s + 1]

    x0o_ref[0] = conv(x0_ref, 0)
    u_ref[0] = conv(v_ref, 2) * conv(x1_ref, 1)


def _gate_call(p_hy, conv_w, conv_b, ct=128):
    bsz, seq_len, c3 = p_hy.shape
    c = c3 // 3
    nb = c // ct
    spec = lambda s: pl.BlockSpec((1, seq_len, ct), lambda b, j, s=s: (b, 0, s * nb + j))
    out_spec = pl.BlockSpec((1, seq_len, ct), lambda b, j: (b, 0, j))
    return pl.pallas_call(
        _gate_kernel,
        grid=(bsz, nb),
        in_specs=[spec(0), spec(1), spec(2),
                  pl.BlockSpec((3, 3, ct), lambda b, j: (0, 0, j)),
                  pl.BlockSpec((3, ct), lambda b, j: (0, j))],
        out_specs=[out_spec, out_spec],
        out_shape=[jax.ShapeDtypeStruct((bsz, seq_len, c), F32)] * 2,
        compiler_params=_params(2),
        name="hygate",
    )(p_hy, p_hy, p_hy, conv_w.reshape(3, 3, c), conv_b.reshape(3, c))


def _gather_minor(refs, n_rows):
    cols = []
    for j in range(SUBLANES):
        parts = [r[pl.ds(j, n_rows, stride=SUBLANES), :] for r in refs]
        cols.append(parts[0] if len(parts) == 1 else jnp.concatenate(parts, axis=0))
    return jnp.concatenate(cols, axis=1)


def _fft_major_kernel(fh_ref, fl_ref, z_ref, or_ref, oi_ref):
    ncomp, n1h = z_ref.shape[:2]
    zf = z_ref.reshape(ncomp, n1h * SUBLANES, LANES)
    zh, zl = _split_bf16(_gather_minor([zf.at[c] for c in range(ncomp)], n1h))
    out = _dot3(fh_ref[...], fl_ref[...], zh, zl)
    n1 = out.shape[0] // 2
    o_r = or_ref.reshape(n1 * SUBLANES, LANES)
    o_i = oi_ref.reshape(n1 * SUBLANES, LANES)
    for j in range(SUBLANES):
        o_r[pl.ds(j, n1, stride=SUBLANES), :] = out[:n1, j * LANES:(j + 1) * LANES]
        o_i[pl.ds(j, n1, stride=SUBLANES), :] = out[n1:, j * LANES:(j + 1) * LANES]


def _fft_major_call(f_blk, z, name):
    fh, fl = _split_bf16(f_blk)
    groups, ncomp, n1h, nb, _, lanes = z.shape
    n1 = f_blk.shape[0] // 2
    out_spec = pl.BlockSpec((None, n1, None, SUBLANES, LANES), lambda p, jb, c: (p, 0, jb, 0, c))
    return pl.pallas_call(
        _fft_major_kernel,
        grid=(groups, nb, lanes // LANES),
        in_specs=[_const_spec(fh.shape), _const_spec(fl.shape),
                  pl.BlockSpec((None, ncomp, n1h, None, SUBLANES, LANES), lambda p, jb, c: (p, 0, 0, jb, 0, c))],
        out_specs=[out_spec, out_spec],
        out_shape=[jax.ShapeDtypeStruct((groups, n1, nb, SUBLANES, lanes), F32)] * 2,
        compiler_params=_params(3),
        name=name,
    )(fh, fl, z)


def _twiddle(ar, ai, twr, twi, reps):
    twr = jnp.concatenate([twr] * reps, axis=1)
    twi = jnp.concatenate([twi] * reps, axis=1)
    return ar * twr - ai * twi, ar * twi + ai * twr


def _fft_minor_filter_kernel(gh_ref, gl_ref, twr_ref, twi_ref, ar_ref, ai_ref, kr_ref, ki_ref):
    n2 = FFT_N2
    c = kr_ref.shape[1]
    reps = ar_ref.shape[2] // twr_ref.shape[1]

    def body(kk, _):
        rows = pl.ds(pl.multiple_of(kk * n2, n2), n2)
        sr, si = _twiddle(ar_ref[0, rows, :], ai_ref[0, rows, :], twr_ref[rows, :], twi_ref[rows, :], reps)
        sh, sl = _split_bf16(jnp.concatenate([sr, si], axis=0))
        x = _dot3(gh_ref[...], gl_ref[...], sh, sl)
        xr, xi = x[:n2], x[n2:]
        kr_ref[rows, :] = xr[:, :c] + xr[:, c:]
        ki_ref[rows, :] = xi[:, :c] - xi[:, c:]
        return 0

    lax.fori_loop(0, ar_ref.shape[1] // n2, body, 0)


def _fft_minor_conv_kernel(gh_ref, gl_ref, gch_ref, gcl_ref, twr_ref, twi_ref, kr_ref, ki_ref,
                           ar_ref, ai_ref, tr_ref, ti_ref):
    n2 = FFT_N2
    reps = ar_ref.shape[2] // twr_ref.shape[1]

    def body(kk, _):
        rows = pl.ds(pl.multiple_of(kk * n2, n2), n2)
        twr = jnp.concatenate([twr_ref[rows, :]] * reps, axis=1)
        twi = jnp.concatenate([twi_ref[rows, :]] * reps, axis=1)
        ar, ai = ar_ref[0, rows, :], ai_ref[0, rows, :]
        sr, si = ar * twr - ai * twi, ar * twi + ai * twr
        sh, sl = _split_bf16(jnp.concatenate([sr, si], axis=0))
        x = _dot3(gh_ref[...], gl_ref[...], sh, sl)
        xr, xi = x[:n2], x[n2:]
        kr, ki = kr_ref[rows, :], ki_ref[rows, :]
        yr, yi = xr * kr - xi * ki, xr * ki + xi * kr
        yh, yl = _split_bf16(jnp.concatenate([yr, yi], axis=0))
        t = _dot3(gch_ref[...], gcl_ref[...], yh, yl)
        tr, ti = t[:n2], t[n2:]
        tr_ref[0, rows, :] = tr * twr + ti * twi
        ti_ref[0, rows, :] = ti * twr - tr * twi
        return 0

    lax.fori_loop(0, ar_ref.shape[1] // n2, body, 0)


def _ifft_major_kernel(eh_ref, el_ref, tr_ref, ti_ref, u_ref, x0_ref, bias_ref, o_ref, *, inv_n):
    n1 = tr_ref.shape[0]
    n1h = u_ref.shape[1]
    flat = lambda r, rows: r.reshape(rows * SUBLANES, LANES)
    th, tl = _split_bf16(_gather_minor([flat(tr_ref, n1), flat(ti_ref, n1)], n1))
    y = _dot3(eh_ref[...], el_ref[...], th, tl) * inv_n
    bias = bias_ref[...]
    for c in range(2):
        u_f, x0_f, o_f = flat(u_ref.at[c], n1h), flat(x0_ref.at[c], n1h), flat(o_ref.at[c], n1h)
        for j in range(SUBLANES):
            rows = pl.ds(j, n1h, stride=SUBLANES)
            yj = y[c * n1h:(c + 1) * n1h, j * LANES:(j + 1) * LANES]
            o_f[rows, :] = x0_f[rows, :] * (yj + u_f[rows, :] * bias)


def _dft_blocks(seq_len):
    n = 2 * seq_len
    n2 = FFT_N2
    n1 = n // n2
    n1h = n1 // 2
    k1 = np.arange(n1)[:, None].astype(np.float64)
    a = 2.0 * np.pi * k1 * np.arange(n1h)[None, :] / n1
    fr, fi = np.cos(a), -np.sin(a)
    f_cplx = np.block([[fr, -fi], [fi, fr]])
    f_real = np.concatenate([fr, fi], axis=0)
    e_cplx = np.block([[fr.T, fi.T], [-fi.T, fr.T]])
    b = 2.0 * np.pi * np.arange(n2)[:, None] * np.arange(n2)[None, :] / n2
    gr, gi = np.cos(b), -np.sin(b)
    g_fwd = np.block([[gr, -gi], [gi, gr]])
    g_inv = np.block([[gr, gi], [-gi, gr]])
    kk = np.arange(n1)[:, None] * np.arange(n2)[None, :]
    ang = (2.0 * np.pi / n) * kk.reshape(n1 * n2, 1)
    twr = np.broadcast_to(np.cos(ang), (n1 * n2, 128))
    twi = np.broadcast_to(-np.sin(ang), (n1 * n2, 128))
    as32 = lambda m: jnp.asarray(np.ascontiguousarray(m), dtype=F32)
    return dict(n=n, n1=n1, n1h=n1h, f_cplx=as32(f_cplx), f_real=as32(f_real), e_cplx=as32(e_cplx),
                g_fwd=as32(g_fwd), g_inv=as32(g_inv), twr=as32(twr), twi=as32(twi))


def _hyena_call(p_hy, conv_w, conv_b, filt, hy_bias):
    bsz, seq_len, c3 = p_hy.shape
    c = c3 // 3
    n2 = FFT_N2
    dft = _dft_blocks(seq_len)
    n, n1, n1h = dft["n"], dft["n1"], dft["n1h"]
    pairs = bsz // 2
    gh, gl = _split_bf16(dft["g_fwd"])
    gch, gcl = _split_bf16(dft["g_inv"])

    w1, b1, f1, w2, b2, f2, w3 = filt
    hid = w2.shape[0]
    bands = (HY_POS_EMB - 1) // 2
    t = jnp.linspace(0.0, 1.0, seq_len, dtype=F32)[:, None]
    freqs = jnp.linspace(1e-4, bands - 1, bands, dtype=F32)
    ang = (2.0 * math.pi / seq_len) * jnp.arange(seq_len, dtype=F32)[:, None] * freqs[None, :]
    z = jnp.concatenate([t, jnp.cos(ang), -jnp.sin(ang), jnp.zeros((seq_len, hid - HY_POS_EMB), F32)], axis=-1)
    w1p = jnp.concatenate([w1, jnp.zeros((hid - HY_POS_EMB, hid), F32)], axis=0)
    max_decay = math.log(HY_DECAY_TARGET) / HY_FAST_DECAY_PCT
    min_decay = math.log(HY_DECAY_TARGET) / HY_SLOW_DECAY_PCT
    absdelta = jnp.abs(jnp.linspace(min_decay, max_decay, c, dtype=F32))
    absdelta2 = jnp.concatenate([absdelta, absdelta])[None, :]
    hcat = _filter_call(z, w1p, b1, f1, w2, b2, f2, w3, absdelta2)
    nb = n2 // SUBLANES
    hr, hi = _fft_major_call(dft["f_real"], hcat.reshape(1, 1, n1h, nb, SUBLANES, 2 * c), "hyfft_filt_major")
    rb = 8 * n2
    kr, ki = pl.pallas_call(
        _fft_minor_filter_kernel,
        grid=(n // rb,),
        in_specs=[_const_spec(gh.shape), _const_spec(gl.shape),
                  pl.BlockSpec((rb, 128), lambda i: (i, 0)), pl.BlockSpec((rb, 128), lambda i: (i, 0)),
                  pl.BlockSpec((1, rb, 2 * c), lambda i: (0, i, 0)), pl.BlockSpec((1, rb, 2 * c), lambda i: (0, i, 0))],
        out_specs=[pl.BlockSpec((rb, c), lambda i: (i, 0))] * 2,
        out_shape=[jax.ShapeDtypeStruct((n, c), F32)] * 2,
        compiler_params=_params(1),
        name="hyfft_filt_minor",
    )(gh, gl, dft["twr"], dft["twi"], hr.reshape(1, n, 2 * c), hi.reshape(1, n, 2 * c))

    u, x0 = _gate_call(p_hy, conv_w, conv_b)
    u_b = u.reshape(pairs, 2, n1h, nb, SUBLANES, c)
    x0_b = x0.reshape(pairs, 2, n1h, nb, SUBLANES, c)
    ar, ai = _fft_major_call(dft["f_cplx"], u_b, "hyfft_major")
    blk = pl.BlockSpec((1, rb, c), lambda p, i: (p, i, 0))
    tr, ti = pl.pallas_call(
        _fft_minor_conv_kernel,
        grid=(pairs, n // rb),
        in_specs=[_const_spec(gh.shape), _const_spec(gl.shape), _const_spec(gch.shape), _const_spec(gcl.shape),
                  pl.BlockSpec((rb, 128), lambda p, i: (i, 0)), pl.BlockSpec((rb, 128), lambda p, i: (i, 0)),
                  pl.BlockSpec((rb, c), lambda p, i: (i, 0)), pl.BlockSpec((rb, c), lambda p, i: (i, 0)),
                  blk, blk],
        out_specs=[blk, blk],
        out_shape=[jax.ShapeDtypeStruct((pairs, n, c), F32)] * 2,
        compiler_params=_params(2),
        name="hyfft_minor",
    )(gh, gl, gch, gcl, dft["twr"], dft["twi"], kr, ki, ar.reshape(pairs, n, c), ai.reshape(pairs, n, c))

    eh, el = _split_bf16(dft["e_cplx"])
    t_spec = pl.BlockSpec((None, n1, None, SUBLANES, LANES), lambda p, jb, cb: (p, 0, jb, 0, cb))
    pair_spec = pl.BlockSpec((None, 2, n1h, None, SUBLANES, LANES), lambda p, jb, cb: (p, 0, 0, jb, 0, cb))
    hy = pl.pallas_call(
        functools.partial(_ifft_major_kernel, inv_n=1.0 / n),
        grid=(pairs, nb, c // LANES),
        in_specs=[_const_spec(eh.shape), _const_spec(el.shape), t_spec, t_spec, pair_spec, pair_spec,
                  pl.BlockSpec((1, LANES), lambda p, jb, cb: (0, cb))],
        out_specs=pair_spec,
        out_shape=jax.ShapeDtypeStruct((pairs, 2, n1h, nb, SUBLANES, c), F32),
        compiler_params=_params(3),
        name="hyifft_major",
    )(eh, el, tr.reshape(pairs, n1, nb, SUBLANES, c), ti.reshape(pairs, n1, nb, SUBLANES, c), u_b, x0_b,
      hy_bias.reshape(1, c))
    return hy.reshape(bsz, seq_len, c)


def _mlp_tail(h, g2, shift, scale, gate, w1_ref, w2_ref, fc):
    a = (_rms(h, NORM_EPS) * g2) * (1.0 + scale) + shift
    a = a.astype(BF16)
    acc = None
    for c0 in range(0, w1_ref.shape[1], fc):
        hid = jnp.maximum(_dot(a, w1_ref[:, c0:c0 + fc]), 0.0)
        part = _dot((hid * hid).astype(BF16), w2_ref[c0:c0 + fc, :])
        acc = part if acc is None else acc + part
    return h + gate * acc


def _mixmlp_kernel(x_ref, att_ref, hy_ref, mod_ref, wa_ref, wh_ref, bo_ref, g2_ref, w1_ref, w2_ref, o_ref, *, fc):
    d = x_ref.shape[2]
    mod = mod_ref[0]
    y = _dot(att_ref[0], wa_ref[...]) + _dot(hy_ref[0].astype(BF16), wh_ref[...]) + bo_ref[...]
    h = x_ref[0] + mod[:, 2 * d:3 * d] * y
    o_ref[0] = _mlp_tail(h, g2_ref[...], mod[:, 3 * d:4 * d], mod[:, 4 * d:5 * d], mod[:, 5 * d:6 * d],
                         w1_ref, w2_ref, fc)


def _mixmlp_call(x, att, hy, mod, w_out, b_out, g2, w1, w2, tm=512, fc=1024):
    bsz, l, d = x.shape
    wa = att.shape[2]
    dff = w1.shape[1]
    row = lambda w: pl.BlockSpec((1, tm, w), lambda b, i: (b, i, 0))
    return pl.pallas_call(
        functools.partial(_mixmlp_kernel, fc=fc),
        grid=(bsz, l // tm),
        in_specs=[row(d), row(wa), row(d - wa),
                  pl.BlockSpec((1, 1, N_MOD * d), lambda b, i: (b, 0, 0)),
                  _const_spec((wa, d)), _const_spec((d - wa, d)), _const_spec((1, d)), _const_spec((1, d)),
                  _const_spec((d, dff)), _const_spec((dff, d))],
        out_specs=row(d),
        out_shape=jax.ShapeDtypeStruct((bsz, l, d), F32),
        compiler_params=_params(2),
        name="mixmlp",
    )(x, att, hy, mod, w_out[:wa], w_out[wa:], b_out.reshape(1, d), g2.reshape(1, d), w1, w2)


def _poolmlp_kernel(h_ref, hp_ref, hn_ref, mod_ref, g1_ref, pw_ref, ps_ref, g2_ref, w1_ref, w2_ref, gf_ref, o_ref,
                    *, fc, seq_len):
    tm, d = h_ref.shape[1:]
    halo = POOL_HALO
    rows = tm + 2 * halo
    mod = mod_ref[0]
    h = h_ref[0]
    hx = jnp.concatenate([hp_ref[0], h, hn_ref[0]], axis=0)
    a = (_rms(hx, NORM_EPS) * g1_ref[...]) * (1.0 + mod[:, d:2 * d]) + mod[:, 0:d]
    t = lax.broadcasted_iota(jnp.int32, (rows, 1), 0) + (pl.program_id(1) * tm - halo)
    a = jnp.where((t >= 0) & (t < seq_len), a, 0.0)
    tc = t[halo:halo + tm]
    gd = d // len(POOL_WINDOWS)
    ys = []
    for g, win in enumerate(POOL_WINDOWS):
        ag = a[:, g * gd:(g + 1) * gd]
        s = ag + pltpu.roll(ag, 1, 0)
        w = 2
        while w < win:
            s = pltpu.roll(s, w // 2, 0) + pltpu.roll(s, rows - w // 2, 0)
            w *= 2
        cnt = (jnp.minimum(tc + (win - win // 2), seq_len) - jnp.maximum(tc - win // 2, 0)).astype(F32)
        dlt = s[halo:halo + tm] / cnt - ag[halo:halo + tm]
        ys.append(_dot(dlt.astype(BF16), pw_ref[g]))
    y = jnp.concatenate(ys, axis=1) * ps_ref[...]
    h1 = h + mod[:, 2 * d:3 * d] * y
    h2 = _mlp_tail(h1, g2_ref[...], mod[:, 3 * d:4 * d], mod[:, 4 * d:5 * d], mod[:, 5 * d:6 * d], w1_ref, w2_ref, fc)
    o_ref[0] = _rms(h2, NORM_EPS) * gf_ref[...]


def _poolmlp_call(h, mod, g1, pool_w, pool_scale, g2, w1, w2, gf, tm=512, fc=1024):
    bsz, l, d = h.shape
    dff = w1.shape[1]
    halo = POOL_HALO
    nb = tm // halo
    last = l // halo - 1
    row = pl.BlockSpec((1, tm, d), lambda b, i: (b, i, 0))
    return pl.pallas_call(
        functools.partial(_poolmlp_kernel, fc=fc, seq_len=l),
        grid=(bsz, l // tm),
        in_specs=[row,
                  pl.BlockSpec((1, halo, d), lambda b, i: (b, jnp.maximum(i * nb - 1, 0), 0)),
                  pl.BlockSpec((1, halo, d), lambda b, i: (b, jnp.minimum((i + 1) * nb, last), 0)),
                  pl.BlockSpec((1, 1, N_MOD * d), lambda b, i: (b, 0, 0)),
                  _const_spec((1, d)), _const_spec(pool_w.shape), _const_spec((1, d)), _const_spec((1, d)),
                  _const_spec((d, dff)), _const_spec((dff, d)), _const_spec((1, d))],
        out_specs=row,
        out_shape=jax.ShapeDtypeStruct((bsz, l, d), F32),
        compiler_params=_params(2),
        name="poolmlp",
    )(h, h, h, mod, g1.reshape(1, d), pool_w, pool_scale.reshape(1, d), g2.reshape(1, d), w1, w2, gf.reshape(1, d))


def kernel(x, c, ctx, c_ctx, ada_w, ada_b, norm1_g, norm2_g, mix_w_in, mix_b_in, mix_w_out, mix_b_out, lam_q1, lam_k1, lam_q2, lam_k2, subln_g, hy_conv_w, hy_conv_b, hy_pos_w1, hy_pos_b1, hy_freq1, hy_pos_w2, hy_pos_b2, hy_freq2, hy_pos_w3, hy_bias, pool_w, pool_scale, mlp_w1, mlp_w2, final_g):
    bsz, seq_len, d = x.shape
    depth = ada_w.shape[0]
    assert depth == 2 and bsz % 2 == 0 and bsz < MOD_ROWS and seq_len % GRID_W == 0
    att_w = ATT_HEADS * ATT_V_DIM
    q_cols = k_cols = ATT_HEADS * 2 * ATT_QK_DIM
    kv_start, hy_start = q_cols, q_cols + k_cols + att_w
    in_cols = mix_w_in.shape[2]

    cv = jnp.concatenate([c, c_ctx[None, :], jnp.zeros((MOD_ROWS - bsz - 1, d), F32)], axis=0)
    mod = _mod_call(cv, ada_w, ada_b)
    mod_l = [mod[i, :bsz].reshape(bsz, 1, N_MOD * d) for i in range(depth)]

    w_in = mix_w_in[0].astype(BF16)
    lam_init = 0.8 - 0.6 * math.exp(-0.3 * 0)
    q, k, v, p_hy = _normproj_call(
        x, mod_l[0][:, :, 0:d], mod_l[0][:, :, d:2 * d], norm1_g[0], w_in, mix_b_in[0], _rope_tables(seq_len),
        outs=[(0, q_cols, True, ATT_QK_DIM ** -0.5 * math.log2(math.e)), (kv_start, kv_start + k_cols, True, 1.0),
              (kv_start + k_cols, hy_start, False, 1.0), (hy_start, in_cols, False, 1.0)],
        out_dtypes=[BF16, BF16, BF16, F32], tm=512, name="normproj_lat")
    mod_c = mod[0, bsz:bsz + 1].reshape(1, 1, N_MOD * d)
    kc, vc = _normproj_call(
        ctx, mod_c[:, :, 0:d], mod_c[:, :, d:2 * d], norm1_g[0], w_in[:, kv_start:hy_start],
        mix_b_in[0, kv_start:hy_start], None,
        outs=[(0, k_cols, False, 1.0), (k_cols, k_cols + att_w, False, 1.0)],
        out_dtypes=[BF16, BF16], tm=ctx.shape[1], name="normproj_ctx")
    lamv = jnp.stack([lam_q1[0], lam_k1[0], lam_q2[0], lam_k2[0]], axis=0)
    att = _attn_call(q, kc, vc, k, v, lamv, subln_g[0], lam_init)
    filt = (hy_pos_w1[0], hy_pos_b1[0], hy_freq1[0], hy_pos_w2[0], hy_pos_b2[0], hy_freq2[0], hy_pos_w3[0])
    hy = _hyena_call(p_hy, hy_conv_w[0], hy_conv_b[0], filt, hy_bias[0])
    h = _mixmlp_call(x, att, hy, mod_l[0], mix_w_out[0].astype(BF16), mix_b_out[0], norm2_g[0],
                     mlp_w1[0].astype(BF16), mlp_w2[0].astype(BF16))

    return _poolmlp_call(h, mod_l[1], norm1_g[1], pool_w[0].astype(BF16), pool_scale[0], norm2_g[1],
                         mlp_w1[1].astype(BF16), mlp_w2[1].astype(BF16), final_g)
```

```python
import functools
import math

import numpy as np
import jax
import jax.numpy as jnp
from jax import lax
from jax.experimental import pallas as pl
from jax.experimental.pallas import tpu as pltpu

F32 = jnp.float32
BF16 = jnp.bfloat16

GRID_W = 64
N_MOD = 6
ATT_HEADS = 4
ATT_V_DIM = 128
ATT_QK_DIM = 64
ROPE_BASE = 10000.0
HY_POS_EMB = 33
HY_DECAY_TARGET = 1e-2
HY_FAST_DECAY_PCT = 0.3
HY_SLOW_DECAY_PCT = 1.5
POOL_WINDOWS = (2, 4, 8, 16)
NORM_EPS = 1e-6
SUBLN_EPS = 1e-5
SUBLANES = 8
LANES = 128
FFT_N2 = 64
POOL_HALO = 8
MOD_ROWS = 8

VMEM_LIMIT_BYTES = 56 * 1024 * 1024


def _params(n_grid_dims):
    return pltpu.CompilerParams(
        dimension_semantics=("arbitrary",) * n_grid_dims,
        vmem_limit_bytes=VMEM_LIMIT_BYTES,
    )


def _const_spec(shape):
    nd = len(shape)
    return pl.BlockSpec(shape, lambda *_: (0,) * nd, pipeline_mode=pl.Buffered(1))


def _split_bf16(a):
    hi = a.astype(BF16)
    lo = (a - hi.astype(F32)).astype(BF16)
    return hi, lo


def _dot(a, b):
    return jnp.dot(a, b, preferred_element_type=F32)


def _dot3(a_hi, a_lo, b_hi, b_lo):
    return _dot(a_hi, b_hi) + (_dot(a_lo, b_hi) + _dot(a_hi, b_lo))


def _dot3f(a, b):
    a_hi, a_lo = _split_bf16(a)
    b_hi, b_lo = _split_bf16(b)
    return _dot3(a_hi, a_lo, b_hi, b_lo)


def _stack_hilo(a):
    return jnp.concatenate(_split_bf16(a), axis=0)


def _dot3s(a_stack, b):
    m = a_stack.shape[0] // 2
    b_hi, b_lo = _split_bf16(b)
    r = _dot(a_stack, b_hi)
    return r[:m] + (r[m:] + _dot(a_stack[:m], b_lo))


def _rms(x, eps):
    return x * lax.rsqrt(jnp.mean(x * x, axis=-1, keepdims=True) + eps)


def _mod_kernel(cv_ref, w_ref, b_ref, o_ref):
    cv = cv_ref[...]
    s = cv / (1.0 + jnp.exp(-cv))
    o_ref[0] = _dot3f(s, w_ref[0]) + b_ref[0]


def _mod_call(cv, ada_w, ada_b, tn=1536):
    depth, d, n = ada_w.shape
    return pl.pallas_call(
        _mod_kernel,
        grid=(depth, n // tn),
        in_specs=[
            pl.BlockSpec((MOD_ROWS, d), lambda i, j: (0, 0)),
            pl.BlockSpec((1, d, tn), lambda i, j: (i, 0, j)),
            pl.BlockSpec((1, 1, tn), lambda i, j: (i, 0, j)),
        ],
        out_specs=pl.BlockSpec((1, MOD_ROWS, tn), lambda i, j: (i, 0, j)),
        out_shape=jax.ShapeDtypeStruct((depth, MOD_ROWS, n), F32),
        compiler_params=_params(2),
        name="mod",
    )(cv, ada_w, ada_b.reshape(depth, 1, n))


def _normproj_kernel(*refs, outs, use_rope):
    x_ref, sh_ref, sc_ref, g_ref, w_ref, b_ref = refs[:6]
    rest = refs[6:]
    if use_rope:
        cos_ref, sa_ref, sb_ref = rest[:3]
        rest = rest[3:]
    a = _rms(x_ref[0], NORM_EPS) * g_ref[...]
    a = a * (1.0 + sc_ref[0]) + sh_ref[0]
    p = _dot(a.astype(BF16), w_ref[...]) + b_ref[...]
    for o_ref, (lo, hi, rope, scale) in zip(rest, outs):
        if not rope:
            o_ref[0] = p[:, lo:hi].astype(o_ref.dtype)
            continue
        cos, sa, sb = cos_ref[...], sa_ref[...], sb_ref[...]
        for c0 in range(lo, hi, 128):
            blk = p[:, c0:c0 + 128]
            r = blk * cos + pltpu.roll(blk, 16, 1) * sa + pltpu.roll(blk, 112, 1) * sb
            if scale != 1.0:
                r = r * scale
            o_ref[0, :, c0 - lo:c0 - lo + 128] = r.astype(o_ref.dtype)


def _normproj_call(x, shift, scale, g, w_bf16, b, rope_tabs, outs, out_dtypes, tm, name):
    bsz, s, d = x.shape
    n = w_bf16.shape[1]
    per_batch = shift.shape[0] == bsz
    vec_map = (lambda bi, i: (bi, 0, 0)) if per_batch else (lambda bi, i: (0, 0, 0))
    in_specs = [
        pl.BlockSpec((1, tm, d), lambda bi, i: (bi, i, 0)),
        pl.BlockSpec((1, 1, d), vec_map),
        pl.BlockSpec((1, 1, d), vec_map),
        _const_spec((1, d)),
        _const_spec((d, n)),
        _const_spec((1, n)),
    ]
    args = [x, shift, scale, g.reshape(1, d), w_bf16, b.reshape(1, n)]
    use_rope = rope_tabs is not None
    if use_rope:
        in_specs += [pl.BlockSpec((tm, 128), lambda bi, i: (i, 0))] * 3
        args += list(rope_tabs)
    out_specs = [pl.BlockSpec((1, tm, hi - lo), lambda bi, i: (bi, i, 0)) for lo, hi, _, _ in outs]
    out_shape = [jax.ShapeDtypeStruct((bsz, s, hi - lo), dt) for (lo, hi, _, _), dt in zip(outs, out_dtypes)]
    return pl.pallas_call(
        functools.partial(_normproj_kernel, outs=tuple(outs), use_rope=use_rope),
        grid=(bsz, s // tm),
        in_specs=in_specs,
        out_specs=out_specs,
        out_shape=out_shape,
        compiler_params=_params(2),
        name=name,
    )(*args)


def _rope_tables(seq_len):
    axis_dim = ATT_QK_DIM // 2
    n_freq = axis_dim // 2
    inv = ROPE_BASE ** (-jnp.arange(n_freq, dtype=F32) / n_freq)
    t = jnp.arange(seq_len, dtype=jnp.int32)
    row, col = t // GRID_W, t % GRID_W
    lane = np.arange(128)
    jj = lane % ATT_QK_DIM
    is_col = (jj // axis_dim) == 1
    freq = jj % n_freq
    second = (jj % axis_dim) >= n_freq
    pos = jnp.where(jnp.asarray(is_col)[None, :], col[:, None], row[:, None]).astype(F32)
    ang = pos * inv[jnp.asarray(freq)][None, :]
    cos, sin = jnp.cos(ang), jnp.sin(ang)
    second = jnp.asarray(second)[None, :]
    return cos, jnp.where(second, sin, 0.0), jnp.where(second, 0.0, -sin)


def _attn_kernel(lam_ref, q_ref, kc_ref, vc_ref, kl_ref, vl_ref, g_ref, o_ref, *, tk, lam_init):
    q = q_ref[0]
    tq = q.shape[0]
    lane = lax.broadcasted_iota(jnp.int32, q.shape, 1)
    zero = jnp.zeros_like(q)
    q2 = jnp.concatenate([jnp.where(lane < ATT_QK_DIM, q, zero), jnp.where(lane >= ATT_QK_DIM, q, zero)], axis=0)

    def chunk(k, v, m, acc):
        s = lax.dot_general(q2, k, (((1,), (1,)), ((), ())), preferred_element_type=F32)
        rowmax = jnp.max(s, axis=1, keepdims=True)
        m_new = rowmax if m is None else jnp.maximum(m, rowmax)
        p = jnp.exp2(s - m_new).astype(BF16)
        pv = _dot(p, jnp.concatenate([v, jnp.ones_like(v)], axis=1))
        return m_new, (pv if acc is None else jnp.exp2(m - m_new) * acc + pv)

    m, acc = chunk(kc_ref[0], vc_ref[0], None, None)
    for j in range(kl_ref.shape[1] // tk):
        m, acc = chunk(kl_ref[0, j * tk:(j + 1) * tk, :], vl_ref[0, j * tk:(j + 1) * tk, :], m, acc)

    lamv = lam_ref[...]
    lam = (jnp.exp(jnp.sum(lamv[0:1] * lamv[1:2], axis=1, keepdims=True))
           - jnp.exp(jnp.sum(lamv[2:3] * lamv[3:4], axis=1, keepdims=True)) + lam_init)
    o_all = acc[:, :ATT_V_DIM] / acc[:, ATT_V_DIM:]
    o = o_all[:tq] - lam * o_all[tq:]
    o = _rms(o, SUBLN_EPS) * g_ref[...] * (1.0 - lam_init)
    o_ref[0] = o.astype(o_ref.dtype)


def _attn_call(q, kc, vc, kl, vl, lamv, g, lam_init, tq=1024, tk=256):
    bsz, l, width = q.shape
    heads = width // ATT_V_DIM
    lc = kc.shape[1]
    hd = ATT_V_DIM
    return pl.pallas_call(
        functools.partial(_attn_kernel, tk=tk, lam_init=lam_init),
        grid=(bsz, heads, l // tq),
        in_specs=[
            _const_spec(lamv.shape),
            pl.BlockSpec((1, tq, hd), lambda b, h, i: (b, i, h)),
            pl.BlockSpec((1, lc, hd), lambda b, h, i: (b, 0, h)),
            pl.BlockSpec((1, lc, hd), lambda b, h, i: (b, 0, h)),
            pl.BlockSpec((1, l, hd), lambda b, h, i: (b, 0, h)),
            pl.BlockSpec((1, l, hd), lambda b, h, i: (b, 0, h)),
            _const_spec((1, hd)),
        ],
        out_specs=pl.BlockSpec((1, tq, hd), lambda b, h, i: (b, i, h)),
        out_shape=jax.ShapeDtypeStruct((bsz, l, width), BF16),
        compiler_params=_params(3),
        name="diffattn",
    )(lamv, q, kc, vc, kl, vl, g.reshape(1, hd))


def _filter_kernel(z_ref, w1_ref, b1_ref, f1_ref, w2_ref, b2_ref, f2_ref, w3_ref, dl_ref, o_ref, *, seq_len):
    tm = z_ref.shape[0]
    h1 = jnp.sin(f1_ref[...] * (_dot3f(z_ref[...], w1_ref[...]) + b1_ref[...]))
    h2 = jnp.sin(f2_ref[...] * (_dot3f(h1, w2_ref[...]) + b2_ref[...]))
    h = _dot3f(h2, w3_ref[...])
    row = lax.broadcasted_iota(jnp.int32, h.shape, 0) + pl.program_id(0) * tm
    col = lax.broadcasted_iota(jnp.int32, h.shape, 1)
    t = row.astype(F32) * (1.0 / (seq_len - 1))
    h = h * jnp.exp(-t * dl_ref[...])
    half = h.shape[1] // 2
    o_ref[...] = jnp.where((row == 0) & (col >= half), 0.0, h)


def _filter_call(z, w1, b1, f1, w2, b2, f2, w3, absdelta2, tm=512):
    seq_len, kz = z.shape
    hid = w2.shape[0]
    n = w3.shape[1]
    return pl.pallas_call(
        functools.partial(_filter_kernel, seq_len=seq_len),
        grid=(seq_len // tm,),
        in_specs=[
            pl.BlockSpec((tm, kz), lambda i: (i, 0)),
            _const_spec((kz, hid)), _const_spec((1, hid)), _const_spec((1, hid)),
            _const_spec((hid, hid)), _const_spec((1, hid)), _const_spec((1, hid)),
            _const_spec((hid, n)), _const_spec((1, n)),
        ],
        out_specs=pl.BlockSpec((tm, n), lambda i: (i, 0)),
        out_shape=jax.ShapeDtypeStruct((seq_len, n), F32),
        compiler_params=_params(1),
        name="hyfilter",
    )(z, w1, b1.reshape(1, hid), f1.reshape(1, hid), w2, b2.reshape(1, hid), f2.reshape(1, hid), w3, absdelta2)


def _gate_kernel(x0_ref, x1_ref, v_ref, w_ref, b_ref, u_ref, x0o_ref):
    seq_len = x0_ref.shape[1]
    row = lax.broadcasted_iota(jnp.int32, x0_ref.shape[1:], 0)
    first, last = row == 0, row == seq_len - 1

    def conv(ref, s):
        p = ref[0]
        prev = jnp.where(first, 0.0, pltpu.roll(p, 1, 0))
        nxt = jnp.where(last, 0.0, pltpu.roll(p, seq_len - 1, 0))
        return prev * w_ref[0, s:s + 1] + p * w_ref[1, s:s + 1] + nxt * w_ref[2, s:s + 1] + b_ref[s:s + 1]

    x0o_ref[0] = conv(x0_ref, 0)
    u_ref[0] = conv(v_ref, 2) * conv(x1_ref, 1)


def _gate_call(p_hy, conv_w, conv_b, ct=128):
    bsz, seq_len, c3 = p_hy.shape
    c = c3 // 3
    nb = c // ct
    spec = lambda s: pl.BlockSpec((1, seq_len, ct), lambda b, j, s=s: (b, 0, s * nb + j))
    out_spec = pl.BlockSpec((1, seq_len, ct), lambda b, j: (b, 0, j))
    return pl.pallas_call(
        _gate_kernel,
        grid=(bsz, nb),
        in_specs=[spec(0), spec(1), spec(2),
                  pl.BlockSpec((3, 3, ct), lambda b, j: (0, 0, j)),
                  pl.BlockSpec((3, ct), lambda b, j: (0, j))],
        out_specs=[out_spec, out_spec],
        out_shape=[jax.ShapeDtypeStruct((bsz, seq_len, c), F32)] * 2,
        compiler_params=_params(2),
        name="hygate",
    )(p_hy, p_hy, p_hy, conv_w.reshape(3, 3, c), conv_b.reshape(3, c))


def _gather_minor(refs, n_rows):
    cols = []
    for j in range(SUBLANES):
        parts = [r[pl.ds(j, n_rows, stride=SUBLANES), :] for r in refs]
        cols.append(parts[0] if len(parts) == 1 else jnp.concatenate(parts, axis=0))
    return jnp.concatenate(cols, axis=1)


def _fft_major_kernel(fs_ref, z_ref, or_ref, oi_ref):
    ncomp, n1h = z_ref.shape[:2]
    zf = z_ref.reshape(ncomp, n1h * SUBLANES, LANES)
    out = _dot3s(fs_ref[...], _gather_minor([zf.at[c] for c in range(ncomp)], n1h))
    n1 = out.shape[0] // 2
    o_r = or_ref.reshape(n1 * SUBLANES, LANES)
    o_i = oi_ref.reshape(n1 * SUBLANES, LANES)
    for j in range(SUBLANES):
        o_r[pl.ds(j, n1, stride=SUBLANES), :] = out[:n1, j * LANES:(j + 1) * LANES]
        o_i[pl.ds(j, n1, stride=SUBLANES), :] = out[n1:, j * LANES:(j + 1) * LANES]


def _fft_major_call(f_blk, z, name):
    fs = _stack_hilo(f_blk)
    groups, ncomp, n1h, nb, _, lanes = z.shape
    n1 = f_blk.shape[0] // 2
    out_spec = pl.BlockSpec((None, n1, None, SUBLANES, LANES), lambda p, jb, c: (p, 0, jb, 0, c))
    return pl.pallas_call(
        _fft_major_kernel,
        grid=(groups, nb, lanes // LANES),
        in_specs=[_const_spec(fs.shape),
                  pl.BlockSpec((None, ncomp, n1h, None, SUBLANES, LANES), lambda p, jb, c: (p, 0, 0, jb, 0, c))],
        out_specs=[out_spec, out_spec],
        out_shape=[jax.ShapeDtypeStruct((groups, n1, nb, SUBLANES, lanes), F32)] * 2,
        compiler_params=_params(3),
        name=name,
    )(fs, z)


def _minor_blocks(fn, n_blocks):
    n2 = FFT_N2
    pieces = [fn(kk, slice(kk * n2, (kk + 1) * n2)) for kk in range(n_blocks)]
    return jnp.concatenate([jnp.concatenate(p, axis=0) for p in pieces], axis=1)


def _tw(tw_ref, rows, reps):
    return jnp.concatenate([tw_ref[rows, :]] * reps, axis=1)


def _fft_minor_filter_kernel(gs_ref, twr_ref, twi_ref, ar_ref, ai_ref, kr_ref, ki_ref):
    n2 = FFT_N2
    c = kr_ref.shape[1]
    w = ar_ref.shape[2]
    reps = w // twr_ref.shape[1]
    n_blocks = ar_ref.shape[1] // n2

    def twiddled(kk, rows):
        ar, ai, twr, twi = ar_ref[0, rows, :], ai_ref[0, rows, :], _tw(twr_ref, rows, reps), _tw(twi_ref, rows, reps)
        return ar * twr - ai * twi, ar * twi + ai * twr

    x = _dot3s(gs_ref[...], _minor_blocks(twiddled, n_blocks))
    for kk in range(n_blocks):
        rows = slice(kk * n2, (kk + 1) * n2)
        xr, xi = x[:n2, kk * w:(kk + 1) * w], x[n2:, kk * w:(kk + 1) * w]
        kr_ref[rows, :] = xr[:, :c] + xr[:, c:]
        ki_ref[rows, :] = xi[:, :c] - xi[:, c:]


def _fft_minor_conv_kernel(gs_ref, gcs_ref, twr_ref, twi_ref, kr_ref, ki_ref, ar_ref, ai_ref, tr_ref, ti_ref):
    n2 = FFT_N2
    w = ar_ref.shape[2]
    reps = w // twr_ref.shape[1]
    n_blocks = ar_ref.shape[1] // n2

    def twiddled(kk, rows):
        ar, ai, twr, twi = ar_ref[0, rows, :], ai_ref[0, rows, :], _tw(twr_ref, rows, reps), _tw(twi_ref, rows, reps)
        return ar * twr - ai * twi, ar * twi + ai * twr

    x = _dot3s(gs_ref[...], _minor_blocks(twiddled, n_blocks))

    def times_filter(kk, rows):
        xr, xi = x[:n2, kk * w:(kk + 1) * w], x[n2:, kk * w:(kk + 1) * w]
        kr, ki = kr_ref[rows, :], ki_ref[rows, :]
        return xr * kr - xi * ki, xr * ki + xi * kr

    t = _dot3s(gcs_ref[...], _minor_blocks(times_filter, n_blocks))
    for kk in range(n_blocks):
        rows = slice(kk * n2, (kk + 1) * n2)
        tr, ti = t[:n2, kk * w:(kk + 1) * w], t[n2:, kk * w:(kk + 1) * w]
        twr, twi = _tw(twr_ref, rows, reps), _tw(twi_ref, rows, reps)
        tr_ref[0, rows, :] = tr * twr + ti * twi
        ti_ref[0, rows, :] = ti * twr - tr * twi


def _ifft_major_kernel(es_ref, tr_ref, ti_ref, u_ref, x0_ref, bias_ref, o_ref, *, inv_n):
    n1 = tr_ref.shape[0]
    n1h = u_ref.shape[1]
    flat = lambda r, rows: r.reshape(rows * SUBLANES, LANES)
    y = _dot3s(es_ref[...], _gather_minor([flat(tr_ref, n1), flat(ti_ref, n1)], n1)) * inv_n
    bias = bias_ref[...]
    for c in range(2):
        u_f, x0_f, o_f = flat(u_ref.at[c], n1h), flat(x0_ref.at[c], n1h), flat(o_ref.at[c], n1h)
        for j in range(SUBLANES):
            rows = pl.ds(j, n1h, stride=SUBLANES)
            yj = y[c * n1h:(c + 1) * n1h, j * LANES:(j + 1) * LANES]
            o_f[rows, :] = x0_f[rows, :] * (yj + u_f[rows, :] * bias)


def _dft_blocks(seq_len):
    n = 2 * seq_len
    n2 = FFT_N2
    n1 = n // n2
    n1h = n1 // 2
    k1 = np.arange(n1)[:, None].astype(np.float64)
    a = 2.0 * np.pi * k1 * np.arange(n1h)[None, :] / n1
    fr, fi = np.cos(a), -np.sin(a)
    f_cplx = np.block([[fr, -fi], [fi, fr]])
    f_real = np.concatenate([fr, fi], axis=0)
    e_cplx = np.block([[fr.T, fi.T], [-fi.T, fr.T]])
    b = 2.0 * np.pi * np.arange(n2)[:, None] * np.arange(n2)[None, :] / n2
    gr, gi = np.cos(b), -np.sin(b)
    g_fwd = np.block([[gr, -gi], [gi, gr]])
    g_inv = np.block([[gr, gi], [-gi, gr]])
    kk = np.arange(n1)[:, None] * np.arange(n2)[None, :]
    ang = (2.0 * np.pi / n) * kk.reshape(n1 * n2, 1)
    twr = np.broadcast_to(np.cos(ang), (n1 * n2, 128))
    twi = np.broadcast_to(-np.sin(ang), (n1 * n2, 128))
    as32 = lambda m: jnp.asarray(np.ascontiguousarray(m), dtype=F32)
    return dict(n=n, n1=n1, n1h=n1h, f_cplx=as32(f_cplx), f_real=as32(f_real), e_cplx=as32(e_cplx),
                g_fwd=as32(g_fwd), g_inv=as32(g_inv), twr=as32(twr), twi=as32(twi))


def _hyena_call(p_hy, conv_w, conv_b, filt, hy_bias):
    bsz, seq_len, c3 = p_hy.shape
    c = c3 // 3
    n2 = FFT_N2
    dft = _dft_blocks(seq_len)
    n, n1, n1h = dft["n"], dft["n1"], dft["n1h"]
    pairs = bsz // 2
    gs, gcs, es = _stack_hilo(dft["g_fwd"]), _stack_hilo(dft["g_inv"]), _stack_hilo(dft["e_cplx"])

    w1, b1, f1, w2, b2, f2, w3 = filt
    hid = w2.shape[0]
    bands = (HY_POS_EMB - 1) // 2
    t = jnp.linspace(0.0, 1.0, seq_len, dtype=F32)[:, None]
    freqs = jnp.linspace(1e-4, bands - 1, bands, dtype=F32)
    ang = (2.0 * math.pi / seq_len) * jnp.arange(seq_len, dtype=F32)[:, None] * freqs[None, :]
    z = jnp.concatenate([t, jnp.cos(ang), -jnp.sin(ang), jnp.zeros((seq_len, hid - HY_POS_EMB), F32)], axis=-1)
    w1p = jnp.concatenate([w1, jnp.zeros((hid - HY_POS_EMB, hid), F32)], axis=0)
    max_decay = math.log(HY_DECAY_TARGET) / HY_FAST_DECAY_PCT
    min_decay = math.log(HY_DECAY_TARGET) / HY_SLOW_DECAY_PCT
    absdelta = jnp.abs(jnp.linspace(min_decay, max_decay, c, dtype=F32))
    absdelta2 = jnp.concatenate([absdelta, absdelta])[None, :]
    hcat = _filter_call(z, w1p, b1, f1, w2, b2, f2, w3, absdelta2)
    nb = n2 // SUBLANES
    hr, hi = _fft_major_call(dft["f_real"], hcat.reshape(1, 1, n1h, nb, SUBLANES, 2 * c), "hyfft_filt_major")
    rb = 8 * n2
    kr, ki = pl.pallas_call(
        _fft_minor_filter_kernel,
        grid=(n // rb,),
        in_specs=[_const_spec(gs.shape),
                  pl.BlockSpec((rb, 128), lambda i: (i, 0)), pl.BlockSpec((rb, 128), lambda i: (i, 0)),
                  pl.BlockSpec((1, rb, 2 * c), lambda i: (0, i, 0)), pl.BlockSpec((1, rb, 2 * c), lambda i: (0, i, 0))],
        out_specs=[pl.BlockSpec((rb, c), lambda i: (i, 0))] * 2,
        out_shape=[jax.ShapeDtypeStruct((n, c), F32)] * 2,
        compiler_params=_params(1),
        name="hyfft_filt_minor",
    )(gs, dft["twr"], dft["twi"], hr.reshape(1, n, 2 * c), hi.reshape(1, n, 2 * c))

    u, x0 = _gate_call(p_hy, conv_w, conv_b)
    u_b = u.reshape(pairs, 2, n1h, nb, SUBLANES, c)
    x0_b = x0.reshape(pairs, 2, n1h, nb, SUBLANES, c)
    ar, ai = _fft_major_call(dft["f_cplx"], u_b, "hyfft_major")
    blk = pl.BlockSpec((1, rb, c), lambda p, i: (p, i, 0))
    tr, ti = pl.pallas_call(
        _fft_minor_conv_kernel,
        grid=(pairs, n // rb),
        in_specs=[_const_spec(gs.shape), _const_spec(gcs.shape),
                  pl.BlockSpec((rb, 128), lambda p, i: (i, 0)), pl.BlockSpec((rb, 128), lambda p, i: (i, 0)),
                  pl.BlockSpec((rb, c), lambda p, i: (i, 0)), pl.BlockSpec((rb, c), lambda p, i: (i, 0)),
                  blk, blk],
        out_specs=[blk, blk],
        out_shape=[jax.ShapeDtypeStruct((pairs, n, c), F32)] * 2,
        compiler_params=_params(2),
        name="hyfft_minor",
    )(gs, gcs, dft["twr"], dft["twi"], kr, ki, ar.reshape(pairs, n, c), ai.reshape(pairs, n, c))

    t_spec = pl.BlockSpec((None, n1, None, SUBLANES, LANES), lambda p, jb, cb: (p, 0, jb, 0, cb))
    pair_spec = pl.BlockSpec((None, 2, n1h, None, SUBLANES, LANES), lambda p, jb, cb: (p, 0, 0, jb, 0, cb))
    hy = pl.pallas_call(
        functools.partial(_ifft_major_kernel, inv_n=1.0 / n),
        grid=(pairs, nb, c // LANES),
        in_specs=[_const_spec(es.shape), t_spec, t_spec, pair_spec, pair_spec,
                  pl.BlockSpec((1, LANES), lambda p, jb, cb: (0, cb))],
        out_specs=pair_spec,
        out_shape=jax.ShapeDtypeStruct((pairs, 2, n1h, nb, SUBLANES, c), F32),
        compiler_params=_params(3),
        name="hyifft_major",
    )(es, tr.reshape(pairs, n1, nb, SUBLANES, c), ti.reshape(pairs, n1, nb, SUBLANES, c), u_b, x0_b,
      hy_bias.reshape(1, c))
    return hy.reshape(bsz, seq_len, c)


def _mlp_tail(h, g2, shift, scale, gate, w1_ref, w2_ref, fc):
    a = (_rms(h, NORM_EPS) * g2) * (1.0 + scale) + shift
    a = a.astype(BF16)
    acc = None
    for c0 in range(0, w1_ref.shape[1], fc):
        hid = jnp.maximum(_dot(a, w1_ref[:, c0:c0 + fc]), 0.0)
        part = _dot((hid * hid).astype(BF16), w2_ref[c0:c0 + fc, :])
        acc = part if acc is None else acc + part
    return h + gate * acc


def _mixmlp_kernel(x_ref, att_ref, hy_ref, mod_ref, wa_ref, wh_ref, bo_ref, g2_ref, w1_ref, w2_ref, o_ref, *, fc):
    d = x_ref.shape[2]
    mod = mod_ref[0]
    y = _dot(att_ref[0], wa_ref[...]) + _dot(hy_ref[0].astype(BF16), wh_ref[...]) + bo_ref[...]
    h = x_ref[0] + mod[:, 2 * d:3 * d] * y
    o_ref[0] = _mlp_tail(h, g2_ref[...], mod[:, 3 * d:4 * d], mod[:, 4 * d:5 * d], mod[:, 5 * d:6 * d],
                         w1_ref, w2_ref, fc)


def _mixmlp_call(x, att, hy, mod, w_out, b_out, g2, w1, w2, tm=512, fc=1024):
    bsz, l, d = x.shape
    wa = att.shape[2]
    dff = w1.shape[1]
    row = lambda w: pl.BlockSpec((1, tm, w), lambda b, i: (b, i, 0))
    return pl.pallas_call(
        functools.partial(_mixmlp_kernel, fc=fc),
        grid=(bsz, l // tm),
        in_specs=[row(d), row(wa), row(d - wa),
                  pl.BlockSpec((1, 1, N_MOD * d), lambda b, i: (b, 0, 0)),
                  _const_spec((wa, d)), _const_spec((d - wa, d)), _const_spec((1, d)), _const_spec((1, d)),
                  _const_spec((d, dff)), _const_spec((dff, d))],
        out_specs=row(d),
        out_shape=jax.ShapeDtypeStruct((bsz, l, d), F32),
        compiler_params=_params(2),
        name="mixmlp",
    )(x, att, hy, mod, w_out[:wa], w_out[wa:], b_out.reshape(1, d), g2.reshape(1, d), w1, w2)


def _poolmlp_kernel(h_ref, hp_ref, hn_ref, mod_ref, g1_ref, pw_ref, ps_ref, g2_ref, w1_ref, w2_ref, gf_ref, o_ref,
                    *, fc, seq_len):
    tm, d = h_ref.shape[1:]
    halo = POOL_HALO
    rows = tm + 2 * halo
    mod = mod_ref[0]
    h = h_ref[0]
    hx = jnp.concatenate([hp_ref[0], h, hn_ref[0]], axis=0)
    a = (_rms(hx, NORM_EPS) * g1_ref[...]) * (1.0 + mod[:, d:2 * d]) + mod[:, 0:d]
    t = lax.broadcasted_iota(jnp.int32, (rows, 1), 0) + (pl.program_id(1) * tm - halo)
    a = jnp.where((t >= 0) & (t < seq_len), a, 0.0)
    tc = t[halo:halo + tm]
    gd = d // len(POOL_WINDOWS)
    ys = []
    for g, win in enumerate(POOL_WINDOWS):
        ag = a[:, g * gd:(g + 1) * gd]
        s = ag + pltpu.roll(ag, 1, 0)
        w = 2
        while w < win:
            s = pltpu.roll(s, w // 2, 0) + pltpu.roll(s, rows - w // 2, 0)
            w *= 2
        cnt = (jnp.minimum(tc + (win - win // 2), seq_len) - jnp.maximum(tc - win // 2, 0)).astype(F32)
        dlt = s[halo:halo + tm] / cnt - ag[halo:halo + tm]
        ys.append(_dot(dlt.astype(BF16), pw_ref[g]))
    y = jnp.concatenate(ys, axis=1) * ps_ref[...]
    h1 = h + mod[:, 2 * d:3 * d] * y
    h2 = _mlp_tail(h1, g2_ref[...], mod[:, 3 * d:4 * d], mod[:, 4 * d:5 * d], mod[:, 5 * d:6 * d], w1_ref, w2_ref, fc)
    o_ref[0] = _rms(h2, NORM_EPS) * gf_ref[...]


def _poolmlp_call(h, mod, g1, pool_w, pool_scale, g2, w1, w2, gf, tm=512, fc=1024):
    bsz, l, d = h.shape
    dff = w1.shape[1]
    halo = POOL_HALO
    nb = tm // halo
    last = l // halo - 1
    row = pl.BlockSpec((1, tm, d), lambda b, i: (b, i, 0))
    return pl.pallas_call(
        functools.partial(_poolmlp_kernel, fc=fc, seq_len=l),
        grid=(bsz, l // tm),
        in_specs=[row,
                  pl.BlockSpec((1, halo, d), lambda b, i: (b, jnp.maximum(i * nb - 1, 0), 0)),
                  pl.BlockSpec((1, halo, d), lambda b, i: (b, jnp.minimum((i + 1) * nb, last), 0)),
                  pl.BlockSpec((1, 1, N_MOD * d), lambda b, i: (b, 0, 0)),
                  _const_spec((1, d)), _const_spec(pool_w.shape), _const_spec((1, d)), _const_spec((1, d)),
                  _const_spec((d, dff)), _const_spec((dff, d)), _const_spec((1, d))],
        out_specs=row,
        out_shape=jax.ShapeDtypeStruct((bsz, l, d), F32),
        compiler_params=_params(2),
        name="poolmlp",
    )(h, h, h, mod, g1.reshape(1, d), pool_w, pool_scale.reshape(1, d), g2.reshape(1, d), w1, w2, gf.reshape(1, d))


def kernel(x, c, ctx, c_ctx, ada_w, ada_b, norm1_g, norm2_g, mix_w_in, mix_b_in, mix_w_out, mix_b_out, lam_q1, lam_k1, lam_q2, lam_k2, subln_g, hy_conv_w, hy_conv_b, hy_pos_w1, hy_pos_b1, hy_freq1, hy_pos_w2, hy_pos_b2, hy_freq2, hy_pos_w3, hy_bias, pool_w, pool_scale, mlp_w1, mlp_w2, final_g):
    bsz, seq_len, d = x.shape
    depth = ada_w.shape[0]
    assert depth == 2 and bsz % 2 == 0 and bsz < MOD_ROWS and seq_len % GRID_W == 0
    att_w = ATT_HEADS * ATT_V_DIM
    q_cols = k_cols = ATT_HEADS * 2 * ATT_QK_DIM
    kv_start, hy_start = q_cols, q_cols + k_cols + att_w
    in_cols = mix_w_in.shape[2]

    cv = jnp.concatenate([c, c_ctx[None, :], jnp.zeros((MOD_ROWS - bsz - 1, d), F32)], axis=0)
    mod = _mod_call(cv, ada_w, ada_b)
    mod_l = [mod[i, :bsz].reshape(bsz, 1, N_MOD * d) for i in range(depth)]

    w_in = mix_w_in[0].astype(BF16)
    lam_init = 0.8 - 0.6 * math.exp(-0.3 * 0)
    q, k, v, p_hy = _normproj_call(
        x, mod_l[0][:, :, 0:d], mod_l[0][:, :, d:2 * d], norm1_g[0], w_in, mix_b_in[0], _rope_tables(seq_len),
        outs=[(0, q_cols, True, ATT_QK_DIM ** -0.5 * math.log2(math.e)), (kv_start, kv_start + k_cols, True, 1.0),
              (kv_start + k_cols, hy_start, False, 1.0), (hy_start, in_cols, False, 1.0)],
        out_dtypes=[BF16, BF16, BF16, F32], tm=512, name="normproj_lat")
    mod_c = mod[0, bsz:bsz + 1].reshape(1, 1, N_MOD * d)
    kc, vc = _normproj_call(
        ctx, mod_c[:, :, 0:d], mod_c[:, :, d:2 * d], norm1_g[0], w_in[:, kv_start:hy_start],
        mix_b_in[0, kv_start:hy_start], None,
        outs=[(0, k_cols, False, 1.0), (k_cols, k_cols + att_w, False, 1.0)],
        out_dtypes=[BF16, BF16], tm=ctx.shape[1], name="normproj_ctx")
    lamv = jnp.stack([lam_q1[0], lam_k1[0], lam_q2[0], lam_k2[0]], axis=0)
    att = _attn_call(q, kc, vc, k, v, lamv, subln_g[0], lam_init)
    filt = (hy_pos_w1[0], hy_pos_b1[0], hy_freq1[0], hy_pos_w2[0], hy_pos_b2[0], hy_freq2[0], hy_pos_w3[0])
    hy = _hyena_call(p_hy, hy_conv_w[0], hy_conv_b[0], filt, hy_bias[0])
    h = _mixmlp_call(x, att, hy, mod_l[0], mix_w_out[0].astype(BF16), mix_b_out[0], norm2_g[0],
                     mlp_w1[0].astype(BF16), mlp_w2[0].astype(BF16))

    return _poolmlp_call(h, mod_l[1], norm1_g[1], pool_w[0].astype(BF16), pool_scale[0], norm2_g[1],
                         mlp_w1[1].astype(BF16), mlp_w2[1].astype(BF16), final_g)
```

```python
import functools
import math

import numpy as np
import jax
import jax.numpy as jnp
from jax import lax
from jax.experimental import pallas as pl
from jax.experimental.pallas import tpu as pltpu

F32 = jnp.float32
BF16 = jnp.bfloat16

GRID_W = 64
N_MOD = 6
ATT_HEADS = 4
ATT_V_DIM = 128
ATT_QK_DIM = 64
ROPE_BASE = 10000.0
HY_POS_EMB = 33
HY_DECAY_TARGET = 1e-2
HY_FAST_DECAY_PCT = 0.3
HY_SLOW_DECAY_PCT = 1.5
POOL_WINDOWS = (2, 4, 8, 16)
NORM_EPS = 1e-6
SUBLN_EPS = 1e-5
SUBLANES = 8
LANES = 128
FFT_N2 = 64
POOL_HALO = 8
MOD_ROWS = 8

VMEM_LIMIT_BYTES = 56 * 1024 * 1024


def _params(n_grid_dims):
    return pltpu.CompilerParams(
        dimension_semantics=("arbitrary",) * n_grid_dims,
        vmem_limit_bytes=VMEM_LIMIT_BYTES,
    )


def _const_spec(shape):
    nd = len(shape)
    return pl.BlockSpec(shape, lambda *_: (0,) * nd, pipeline_mode=pl.Buffered(1))


def _split_bf16(a):
    hi = a.astype(BF16)
    lo = (a - hi.astype(F32)).astype(BF16)
    return hi, lo


def _dot(a, b):
    return jnp.dot(a, b, preferred_element_type=F32)


def _dot3(a_hi, a_lo, b_hi, b_lo):
    return _dot(a_hi, b_hi) + (_dot(a_lo, b_hi) + _dot(a_hi, b_lo))


def _dot3f(a, b):
    a_hi, a_lo = _split_bf16(a)
    b_hi, b_lo = _split_bf16(b)
    return _dot3(a_hi, a_lo, b_hi, b_lo)


def _dft_dot(a_bf16, b):
    return _dot(a_bf16, b.astype(BF16))


def _rms(x, eps):
    return x * lax.rsqrt(jnp.mean(x * x, axis=-1, keepdims=True) + eps)


def _mod_kernel(cv_ref, w_ref, b_ref, o_ref):
    cv = cv_ref[...]
    s = cv / (1.0 + jnp.exp(-cv))
    o_ref[0] = _dot3f(s, w_ref[0]) + b_ref[0]


def _mod_call(cv, ada_w, ada_b, tn=1536):
    depth, d, n = ada_w.shape
    return pl.pallas_call(
        _mod_kernel,
        grid=(depth, n // tn),
        in_specs=[
            pl.BlockSpec((MOD_ROWS, d), lambda i, j: (0, 0)),
            pl.BlockSpec((1, d, tn), lambda i, j: (i, 0, j)),
            pl.BlockSpec((1, 1, tn), lambda i, j: (i, 0, j)),
        ],
        out_specs=pl.BlockSpec((1, MOD_ROWS, tn), lambda i, j: (i, 0, j)),
        out_shape=jax.ShapeDtypeStruct((depth, MOD_ROWS, n), F32),
        compiler_params=_params(2),
        name="mod",
    )(cv, ada_w, ada_b.reshape(depth, 1, n))


def _ctxproj_kernel(x_ref, sh_ref, sc_ref, g_ref, w_ref, b_ref, k_ref, v_ref):
    a = _rms(x_ref[0], NORM_EPS) * g_ref[...]
    a = a * (1.0 + sc_ref[0]) + sh_ref[0]
    p = _dot(a.astype(BF16), w_ref[...]) + b_ref[...]
    nk = k_ref.shape[2]
    k_ref[0] = p[:, :nk].astype(k_ref.dtype)
    v_ref[0] = p[:, nk:].astype(v_ref.dtype)


def _ctxproj_call(x, shift, scale, g, w_bf16, b, nk):
    bsz, s, d = x.shape
    n = w_bf16.shape[1]
    vec = pl.BlockSpec((1, 1, d), lambda bi: (0, 0, 0))
    return pl.pallas_call(
        _ctxproj_kernel,
        grid=(bsz,),
        in_specs=[pl.BlockSpec((1, s, d), lambda bi: (bi, 0, 0)), vec, vec,
                  _const_spec((1, d)), _const_spec((d, n)), _const_spec((1, n))],
        out_specs=[pl.BlockSpec((1, s, nk), lambda bi: (bi, 0, 0)), pl.BlockSpec((1, s, n - nk), lambda bi: (bi, 0, 0))],
        out_shape=[jax.ShapeDtypeStruct((bsz, s, nk), BF16), jax.ShapeDtypeStruct((bsz, s, n - nk), BF16)],
        compiler_params=_params(1),
        name="ctxproj",
    )(x, shift, scale, g.reshape(1, d), w_bf16, b.reshape(1, n))


def _latproj_kernel(x_ref, xp_ref, xn_ref, sh_ref, sc_ref, g_ref, w_ref, b_ref, cos_ref, sa_ref, sb_ref,
                    cw_ref, cb_ref, q_ref, k_ref, v_ref, u_ref, x0_ref, *, n_att, q_scale, seq_len):
    tm = x_ref.shape[1]
    halo = xp_ref.shape[1]
    rows = tm + 2 * halo
    xx = jnp.concatenate([xp_ref[0], x_ref[0], xn_ref[0]], axis=0)
    a = _rms(xx, NORM_EPS) * g_ref[...]
    a = a * (1.0 + sc_ref[0]) + sh_ref[0]

    ph = _dot(a.astype(BF16), w_ref[:, n_att:]) + b_ref[:, n_att:]
    t = lax.broadcasted_iota(jnp.int32, (rows, 1), 0) + (pl.program_id(1) * tm - halo)
    ph = jnp.where((t >= 0) & (t < seq_len), ph, 0.0)
    c = u_ref.shape[2]

    def conv(s):
        blk = ph[:, s * c:(s + 1) * c]
        prev = pltpu.roll(blk, 1, 0)[halo:halo + tm]
        nxt = pltpu.roll(blk, rows - 1, 0)[halo:halo + tm]
        cw = cw_ref[:, s * c:(s + 1) * c]
        return prev * cw[0:1] + blk[halo:halo + tm] * cw[1:2] + nxt * cw[2:3] + cb_ref[:, s * c:(s + 1) * c]

    x0_ref[0] = conv(0)
    u_ref[0] = conv(2) * conv(1)

    p = _dot(a[halo:halo + tm].astype(BF16), w_ref[:, :n_att]) + b_ref[:, :n_att]
    cos, sa, sb = cos_ref[...], sa_ref[...], sb_ref[...]
    qk = q_ref.shape[2]
    for o_ref, c_lo, scale in ((q_ref, 0, q_scale), (k_ref, qk, 1.0)):
        for c0 in range(0, qk, LANES):
            blk = p[:, c_lo + c0:c_lo + c0 + LANES]
            r = blk * cos + pltpu.roll(blk, 16, 1) * sa + pltpu.roll(blk, LANES - 16, 1) * sb
            o_ref[0, :, c0:c0 + LANES] = (r * scale).astype(o_ref.dtype)
    v_ref[0] = p[:, 2 * qk:].astype(v_ref.dtype)


def _latproj_call(x, shift, scale, g, w_bf16, b, rope_tabs, conv_w, conv_b, n_att, qk, q_scale, tm=512):
    bsz, s, d = x.shape
    n = w_bf16.shape[1]
    c = (n - n_att) // 3
    halo = SUBLANES
    nb = tm // halo
    last = s // halo - 1
    vec = pl.BlockSpec((1, 1, d), lambda bi, i: (bi, 0, 0))
    tab = pl.BlockSpec((tm, LANES), lambda bi, i: (i, 0))
    row = lambda w: pl.BlockSpec((1, tm, w), lambda bi, i: (bi, i, 0))
    return pl.pallas_call(
        functools.partial(_latproj_kernel, n_att=n_att, q_scale=q_scale, seq_len=s),
        grid=(bsz, s // tm),
        in_specs=[row(d),
                  pl.BlockSpec((1, halo, d), lambda bi, i: (bi, jnp.maximum(i * nb - 1, 0), 0)),
                  pl.BlockSpec((1, halo, d), lambda bi, i: (bi, jnp.minimum((i + 1) * nb, last), 0)),
                  vec, vec, _const_spec((1, d)), _const_spec((d, n)), _const_spec((1, n)), tab, tab, tab,
                  _const_spec(conv_w.shape), _const_spec((1, 3 * c))],
        out_specs=[row(qk), row(qk), row(n_att - 2 * qk), row(c), row(c)],
        out_shape=[jax.ShapeDtypeStruct((bsz, s, qk), BF16), jax.ShapeDtypeStruct((bsz, s, qk), BF16),
                   jax.ShapeDtypeStruct((bsz, s, n_att - 2 * qk), BF16),
                   jax.ShapeDtypeStruct((bsz, s, c), F32), jax.ShapeDtypeStruct((bsz, s, c), F32)],
        compiler_params=_params(2),
        name="latproj",
    )(x, x, x, shift, scale, g.reshape(1, d), w_bf16, b.reshape(1, n), *rope_tabs, conv_w, conv_b.reshape(1, 3 * c))


def _rope_tables(seq_len):
    axis_dim = ATT_QK_DIM // 2
    n_freq = axis_dim // 2
    inv = (ROPE_BASE ** (-np.arange(n_freq, dtype=np.float32) / n_freq)).astype(np.float32)
    t = np.arange(seq_len)
    row, col = t // GRID_W, t % GRID_W
    jj = np.arange(128) % ATT_QK_DIM
    is_col = (jj // axis_dim) == 1
    second = ((jj % axis_dim) >= n_freq)[None, :]
    pos = np.where(is_col[None, :], col[:, None], row[:, None]).astype(np.float32)
    ang = (pos * inv[jj % n_freq][None, :]).astype(np.float64)
    cos, sin = np.cos(ang), np.sin(ang)
    as32 = lambda m: jnp.asarray(m, dtype=F32)
    return as32(cos), as32(np.where(second, sin, 0.0)), as32(np.where(second, 0.0, -sin))


def _attn_kernel(lam_ref, q_ref, kc_ref, vc_ref, kl_ref, vl_ref, g_ref, o_ref, *, tk, lam_init):
    q = q_ref[0]
    tq = q.shape[0]
    lane = lax.broadcasted_iota(jnp.int32, q.shape, 1)
    zero = jnp.zeros_like(q)
    q2 = jnp.concatenate([jnp.where(lane < ATT_QK_DIM, q, zero), jnp.where(lane >= ATT_QK_DIM, q, zero)], axis=0)

    def chunk(k, v, m, acc):
        s = lax.dot_general(q2, k, (((1,), (1,)), ((), ())), preferred_element_type=F32)
        rowmax = jnp.max(s, axis=1, keepdims=True)
        m_new = rowmax if m is None else jnp.maximum(m, rowmax)
        p = jnp.exp2(s - m_new).astype(BF16)
        pv = _dot(p, jnp.concatenate([v, jnp.ones_like(v)], axis=1))
        return m_new, (pv if acc is None else jnp.exp2(m - m_new) * acc + pv)

    m, acc = chunk(kc_ref[0], vc_ref[0], None, None)
    for j in range(kl_ref.shape[1] // tk):
        m, acc = chunk(kl_ref[0, j * tk:(j + 1) * tk, :], vl_ref[0, j * tk:(j + 1) * tk, :], m, acc)

    lamv = lam_ref[...]
    lam = (jnp.exp(jnp.sum(lamv[0:1] * lamv[1:2], axis=1, keepdims=True))
           - jnp.exp(jnp.sum(lamv[2:3] * lamv[3:4], axis=1, keepdims=True)) + lam_init)
    o_all = acc[:, :ATT_V_DIM] / acc[:, ATT_V_DIM:]
    o = o_all[:tq] - lam * o_all[tq:]
    o = _rms(o, SUBLN_EPS) * g_ref[...] * (1.0 - lam_init)
    o_ref[0] = o.astype(o_ref.dtype)


def _attn_call(q, kc, vc, kl, vl, lamv, g, lam_init, tq=1024, tk=256):
    bsz, l, width = q.shape
    heads = width // ATT_V_DIM
    lc = kc.shape[1]
    hd = ATT_V_DIM
    return pl.pallas_call(
        functools.partial(_attn_kernel, tk=tk, lam_init=lam_init),
        grid=(bsz, heads, l // tq),
        in_specs=[
            _const_spec(lamv.shape),
            pl.BlockSpec((1, tq, hd), lambda b, h, i: (b, i, h)),
            pl.BlockSpec((1, lc, hd), lambda b, h, i: (b, 0, h)),
            pl.BlockSpec((1, lc, hd), lambda b, h, i: (b, 0, h)),
            pl.BlockSpec((1, l, hd), lambda b, h, i: (b, 0, h)),
            pl.BlockSpec((1, l, hd), lambda b, h, i: (b, 0, h)),
            _const_spec((1, hd)),
        ],
        out_specs=pl.BlockSpec((1, tq, hd), lambda b, h, i: (b, i, h)),
        out_shape=jax.ShapeDtypeStruct((bsz, l, width), BF16),
        compiler_params=_params(3),
        name="diffattn",
    )(lamv, q, kc, vc, kl, vl, g.reshape(1, hd))


def _filter_kernel(z_ref, w1_ref, b1_ref, f1_ref, w2_ref, b2_ref, f2_ref, w3_ref, dl_ref, o_ref, *, seq_len):
    tm = z_ref.shape[0]
    h1 = jnp.sin(f1_ref[...] * (_dot3f(z_ref[...], w1_ref[...]) + b1_ref[...]))
    h2 = jnp.sin(f2_ref[...] * (_dot3f(h1, w2_ref[...]) + b2_ref[...]))
    h = _dot3f(h2, w3_ref[...])
    row = lax.broadcasted_iota(jnp.int32, h.shape, 0) + pl.program_id(0) * tm
    col = lax.broadcasted_iota(jnp.int32, h.shape, 1)
    t = row.astype(F32) * (1.0 / (seq_len - 1))
    h = h * jnp.exp(-t * dl_ref[...])
    half = h.shape[1] // 2
    o_ref[...] = jnp.where((row == 0) & (col >= half), 0.0, h)


def _filter_call(z, w1, b1, f1, w2, b2, f2, w3, absdelta2, tm=512):
    seq_len, kz = z.shape
    hid = w2.shape[0]
    n = w3.shape[1]
    return pl.pallas_call(
        functools.partial(_filter_kernel, seq_len=seq_len),
        grid=(seq_len // tm,),
        in_specs=[
            pl.BlockSpec((tm, kz), lambda i: (i, 0)),
            _const_spec((kz, hid)), _const_spec((1, hid)), _const_spec((1, hid)),
            _const_spec((hid, hid)), _const_spec((1, hid)), _const_spec((1, hid)),
            _const_spec((hid, n)), _const_spec((1, n)),
        ],
        out_specs=pl.BlockSpec((tm, n), lambda i: (i, 0)),
        out_shape=jax.ShapeDtypeStruct((seq_len, n), F32),
        compiler_params=_params(1),
        name="hyfilter",
    )(z, w1, b1.reshape(1, hid), f1.reshape(1, hid), w2, b2.reshape(1, hid), f2.reshape(1, hid), w3, absdelta2)


def _gather_minor(refs, n_rows):
    cols = []
    for j in range(SUBLANES):
        parts = [r[pl.ds(j, n_rows, stride=SUBLANES), :] for r in refs]
        cols.append(parts[0] if len(parts) == 1 else jnp.concatenate(parts, axis=0))
    return jnp.concatenate(cols, axis=1)


def _fft_major_kernel(fs_ref, z_ref, or_ref, oi_ref):
    ncomp, n1h = z_ref.shape[:2]
    zf = z_ref.reshape(ncomp, n1h * SUBLANES, LANES)
    out = _dft_dot(fs_ref[...], _gather_minor([zf.at[c] for c in range(ncomp)], n1h))
    n1 = out.shape[0] // 2
    o_r = or_ref.reshape(n1 * SUBLANES, LANES)
    o_i = oi_ref.reshape(n1 * SUBLANES, LANES)
    for j in range(SUBLANES):
        o_r[pl.ds(j, n1, stride=SUBLANES), :] = out[:n1, j * LANES:(j + 1) * LANES]
        o_i[pl.ds(j, n1, stride=SUBLANES), :] = out[n1:, j * LANES:(j + 1) * LANES]


def _fft_major_call(f_blk, z, name):
    fs = f_blk.astype(BF16)
    groups, ncomp, n1h, nb, _, lanes = z.shape
    n1 = f_blk.shape[0] // 2
    out_spec = pl.BlockSpec((None, n1, None, SUBLANES, LANES), lambda p, jb, c: (p, 0, jb, 0, c))
    return pl.pallas_call(
        _fft_major_kernel,
        grid=(groups, nb, lanes // LANES),
        in_specs=[_const_spec(fs.shape),
                  pl.BlockSpec((None, ncomp, n1h, None, SUBLANES, LANES), lambda p, jb, c: (p, 0, 0, jb, 0, c))],
        out_specs=[out_spec, out_spec],
        out_shape=[jax.ShapeDtypeStruct((groups, n1, nb, SUBLANES, lanes), F32)] * 2,
        compiler_params=_params(3),
        name=name,
    )(fs, z)


def _minor_blocks(fn, n_blocks):
    n2 = FFT_N2
    pieces = [fn(kk, slice(kk * n2, (kk + 1) * n2)) for kk in range(n_blocks)]
    return jnp.concatenate([jnp.concatenate(p, axis=0) for p in pieces], axis=1)


def _tw(tw_ref, rows, reps):
    return jnp.concatenate([tw_ref[rows, :]] * reps, axis=1)


def _fft_minor_filter_kernel(gs_ref, twr_ref, twi_ref, ar_ref, ai_ref, kr_ref, ki_ref):
    n2 = FFT_N2
    c = kr_ref.shape[1]
    w = ar_ref.shape[2]
    reps = w // twr_ref.shape[1]
    n_blocks = ar_ref.shape[1] // n2

    def twiddled(kk, rows):
        ar, ai, twr, twi = ar_ref[0, rows, :], ai_ref[0, rows, :], _tw(twr_ref, rows, reps), _tw(twi_ref, rows, reps)
        return ar * twr - ai * twi, ar * twi + ai * twr

    x = _dft_dot(gs_ref[...], _minor_blocks(twiddled, n_blocks))
    for kk in range(n_blocks):
        rows = slice(kk * n2, (kk + 1) * n2)
        xr, xi = x[:n2, kk * w:(kk + 1) * w], x[n2:, kk * w:(kk + 1) * w]
        kr_ref[rows, :] = xr[:, :c] + xr[:, c:]
        ki_ref[rows, :] = xi[:, :c] - xi[:, c:]


def _fft_minor_conv_kernel(gs_ref, gcs_ref, twr_ref, twi_ref, kr_ref, ki_ref, ar_ref, ai_ref, tr_ref, ti_ref):
    n2 = FFT_N2
    w = ar_ref.shape[2]
    reps = w // twr_ref.shape[1]
    n_blocks = ar_ref.shape[1] // n2

    def twiddled(kk, rows):
        ar, ai, twr, twi = ar_ref[0, rows, :], ai_ref[0, rows, :], _tw(twr_ref, rows, reps), _tw(twi_ref, rows, reps)
        return ar * twr - ai * twi, ar * twi + ai * twr

    x = _dft_dot(gs_ref[...], _minor_blocks(twiddled, n_blocks))

    def times_filter(kk, rows):
        xr, xi = x[:n2, kk * w:(kk + 1) * w], x[n2:, kk * w:(kk + 1) * w]
        kr, ki = kr_ref[rows, :], ki_ref[rows, :]
        return xr * kr - xi * ki, xr * ki + xi * kr

    t = _dft_dot(gcs_ref[...], _minor_blocks(times_filter, n_blocks))
    for kk in range(n_blocks):
        rows = slice(kk * n2, (kk + 1) * n2)
        tr, ti = t[:n2, kk * w:(kk + 1) * w], t[n2:, kk * w:(kk + 1) * w]
        twr, twi = _tw(twr_ref, rows, reps), _tw(twi_ref, rows, reps)
        tr_ref[0, rows, :] = tr * twr + ti * twi
        ti_ref[0, rows, :] = ti * twr - tr * twi


def _ifft_major_kernel(es_ref, tr_ref, ti_ref, u_ref, x0_ref, bias_ref, o_ref, *, inv_n):
    n1 = tr_ref.shape[0]
    n1h = u_ref.shape[1]
    flat = lambda r, rows: r.reshape(rows * SUBLANES, LANES)
    y = _dft_dot(es_ref[...], _gather_minor([flat(tr_ref, n1), flat(ti_ref, n1)], n1)) * inv_n
    bias = bias_ref[...]
    for c in range(2):
        u_f, x0_f, o_f = flat(u_ref.at[c], n1h), flat(x0_ref.at[c], n1h), flat(o_ref.at[c], n1h)
        for j in range(SUBLANES):
            rows = pl.ds(j, n1h, stride=SUBLANES)
            yj = y[c * n1h:(c + 1) * n1h, j * LANES:(j + 1) * LANES]
            o_f[rows, :] = x0_f[rows, :] * (yj + u_f[rows, :] * bias)


def _dft_blocks(seq_len):
    n = 2 * seq_len
    n2 = FFT_N2
    n1 = n // n2
    n1h = n1 // 2
    k1 = np.arange(n1)[:, None].astype(np.float64)
    a = 2.0 * np.pi * k1 * np.arange(n1h)[None, :] / n1
    fr, fi = np.cos(a), -np.sin(a)
    f_cplx = np.block([[fr, -fi], [fi, fr]])
    f_real = np.concatenate([fr, fi], axis=0)
    e_cplx = np.block([[fr.T, fi.T], [-fi.T, fr.T]])
    b = 2.0 * np.pi * np.arange(n2)[:, None] * np.arange(n2)[None, :] / n2
    gr, gi = np.cos(b), -np.sin(b)
    g_fwd = np.block([[gr, -gi], [gi, gr]])
    g_inv = np.block([[gr, gi], [-gi, gr]])
    kk = np.arange(n1)[:, None] * np.arange(n2)[None, :]
    ang = (2.0 * np.pi / n) * kk.reshape(n1 * n2, 1)
    twr = np.broadcast_to(np.cos(ang), (n1 * n2, 128))
    twi = np.broadcast_to(-np.sin(ang), (n1 * n2, 128))
    as32 = lambda m: jnp.asarray(np.ascontiguousarray(m), dtype=F32)
    return dict(n=n, n1=n1, n1h=n1h, f_cplx=as32(f_cplx), f_real=as32(f_real), e_cplx=as32(e_cplx),
                g_fwd=as32(g_fwd), g_inv=as32(g_inv), twr=as32(twr), twi=as32(twi))


def _hyena_call(u, x0, filt, hy_bias):
    bsz, seq_len, c = u.shape
    n2 = FFT_N2
    dft = _dft_blocks(seq_len)
    n, n1, n1h = dft["n"], dft["n1"], dft["n1h"]
    pairs = bsz // 2
    gs, gcs, es = dft["g_fwd"].astype(BF16), dft["g_inv"].astype(BF16), dft["e_cplx"].astype(BF16)

    w1, b1, f1, w2, b2, f2, w3 = filt
    hid = w2.shape[0]
    bands = (HY_POS_EMB - 1) // 2
    t = np.linspace(0.0, 1.0, seq_len)[:, None]
    freqs = np.linspace(1e-4, bands - 1, bands)
    ang = (2.0 * math.pi / seq_len) * np.arange(seq_len)[:, None] * freqs[None, :]
    z = jnp.asarray(np.concatenate([t, np.cos(ang), -np.sin(ang), np.zeros((seq_len, hid - HY_POS_EMB))], axis=-1),
                    dtype=F32)
    w1p = jnp.concatenate([w1, jnp.zeros((hid - HY_POS_EMB, hid), F32)], axis=0)
    max_decay = math.log(HY_DECAY_TARGET) / HY_FAST_DECAY_PCT
    min_decay = math.log(HY_DECAY_TARGET) / HY_SLOW_DECAY_PCT
    absdelta = np.abs(np.linspace(min_decay, max_decay, c))
    absdelta2 = jnp.asarray(np.concatenate([absdelta, absdelta])[None, :], dtype=F32)
    hcat = _filter_call(z, w1p, b1, f1, w2, b2, f2, w3, absdelta2)
    nb = n2 // SUBLANES
    hr, hi = _fft_major_call(dft["f_real"], hcat.reshape(1, 1, n1h, nb, SUBLANES, 2 * c), "hyfft_filt_major")
    rb = 8 * n2
    kr, ki = pl.pallas_call(
        _fft_minor_filter_kernel,
        grid=(n // rb,),
        in_specs=[_const_spec(gs.shape),
                  pl.BlockSpec((rb, 128), lambda i: (i, 0)), pl.BlockSpec((rb, 128), lambda i: (i, 0)),
                  pl.BlockSpec((1, rb, 2 * c), lambda i: (0, i, 0)), pl.BlockSpec((1, rb, 2 * c), lambda i: (0, i, 0))],
        out_specs=[pl.BlockSpec((rb, c), lambda i: (i, 0))] * 2,
        out_shape=[jax.ShapeDtypeStruct((n, c), F32)] * 2,
        compiler_params=_params(1),
        name="hyfft_filt_minor",
    )(gs, dft["twr"], dft["twi"], hr.reshape(1, n, 2 * c), hi.reshape(1, n, 2 * c))

    u_b = u.reshape(pairs, 2, n1h, nb, SUBLANES, c)
    x0_b = x0.reshape(pairs, 2, n1h, nb, SUBLANES, c)
    ar, ai = _fft_major_call(dft["f_cplx"], u_b, "hyfft_major")
    blk = pl.BlockSpec((1, rb, c), lambda p, i: (p, i, 0))
    tr, ti = pl.pallas_call(
        _fft_minor_conv_kernel,
        grid=(pairs, n // rb),
        in_specs=[_const_spec(gs.shape), _const_spec(gcs.shape),
                  pl.BlockSpec((rb, 128), lambda p, i: (i, 0)), pl.BlockSpec((rb, 128), lambda p, i: (i, 0)),
                  pl.BlockSpec((rb, c), lambda p, i: (i, 0)), pl.BlockSpec((rb, c), lambda p, i: (i, 0)),
                  blk, blk],
        out_specs=[blk, blk],
        out_shape=[jax.ShapeDtypeStruct((pairs, n, c), F32)] * 2,
        compiler_params=_params(2),
        name="hyfft_minor",
    )(gs, gcs, dft["twr"], dft["twi"], kr, ki, ar.reshape(pairs, n, c), ai.reshape(pairs, n, c))

    t_spec = pl.BlockSpec((None, n1, None, SUBLANES, LANES), lambda p, jb, cb: (p, 0, jb, 0, cb))
    pair_spec = pl.BlockSpec((None, 2, n1h, None, SUBLANES, LANES), lambda p, jb, cb: (p, 0, 0, jb, 0, cb))
    hy = pl.pallas_call(
        functools.partial(_ifft_major_kernel, inv_n=1.0 / n),
        grid=(pairs, nb, c // LANES),
        in_specs=[_const_spec(es.shape), t_spec, t_spec, pair_spec, pair_spec,
                  pl.BlockSpec((1, LANES), lambda p, jb, cb: (0, cb))],
        out_specs=pair_spec,
        out_shape=jax.ShapeDtypeStruct((pairs, 2, n1h, nb, SUBLANES, c), F32),
        compiler_params=_params(3),
        name="hyifft_major",
    )(es, tr.reshape(pairs, n1, nb, SUBLANES, c), ti.reshape(pairs, n1, nb, SUBLANES, c), u_b, x0_b,
      hy_bias.reshape(1, c))
    return hy.reshape(bsz, seq_len, c)


def _mlp_tail(h, g2, shift, scale, gate, w1_ref, w2_ref, fc):
    a = (_rms(h, NORM_EPS) * g2) * (1.0 + scale) + shift
    a = a.astype(BF16)
    acc = None
    for c0 in range(0, w1_ref.shape[1], fc):
        hid = jnp.maximum(_dot(a, w1_ref[:, c0:c0 + fc]), 0.0)
        part = _dot((hid * hid).astype(BF16), w2_ref[c0:c0 + fc, :])
        acc = part if acc is None else acc + part
    return h + gate * acc


def _mixmlp_kernel(x_ref, att_ref, hy_ref, mod_ref, wa_ref, wh_ref, bo_ref, g2_ref, w1_ref, w2_ref, o_ref, *, fc):
    d = x_ref.shape[2]
    mod = mod_ref[0]
    y = _dot(att_ref[0], wa_ref[...]) + _dot(hy_ref[0].astype(BF16), wh_ref[...]) + bo_ref[...]
    h = x_ref[0] + mod[:, 2 * d:3 * d] * y
    o_ref[0] = _mlp_tail(h, g2_ref[...], mod[:, 3 * d:4 * d], mod[:, 4 * d:5 * d], mod[:, 5 * d:6 * d],
                         w1_ref, w2_ref, fc)


def _mixmlp_call(x, att, hy, mod, w_out, b_out, g2, w1, w2, tm=512, fc=1024):
    bsz, l, d = x.shape
    wa = att.shape[2]
    dff = w1.shape[1]
    row = lambda w: pl.BlockSpec((1, tm, w), lambda b, i: (b, i, 0))
    return pl.pallas_call(
        functools.partial(_mixmlp_kernel, fc=fc),
        grid=(bsz, l // tm),
        in_specs=[row(d), row(wa), row(d - wa),
                  pl.BlockSpec((1, 1, N_MOD * d), lambda b, i: (b, 0, 0)),
                  _const_spec((wa, d)), _const_spec((d - wa, d)), _const_spec((1, d)), _const_spec((1, d)),
                  _const_spec((d, dff)), _const_spec((dff, d))],
        out_specs=row(d),
        out_shape=jax.ShapeDtypeStruct((bsz, l, d), F32),
        compiler_params=_params(2),
        name="mixmlp",
    )(x, att, hy, mod, w_out[:wa], w_out[wa:], b_out.reshape(1, d), g2.reshape(1, d), w1, w2)


def _poolmlp_kernel(h_ref, hp_ref, hn_ref, mod_ref, g1_ref, pw_ref, ps_ref, g2_ref, w1_ref, w2_ref, gf_ref, o_ref,
                    *, fc, seq_len):
    tm, d = h_ref.shape[1:]
    halo = POOL_HALO
    rows = tm + 2 * halo
    mod = mod_ref[0]
    h = h_ref[0]
    hx = jnp.concatenate([hp_ref[0], h, hn_ref[0]], axis=0)
    a = (_rms(hx, NORM_EPS) * g1_ref[...]) * (1.0 + mod[:, d:2 * d]) + mod[:, 0:d]
    t = lax.broadcasted_iota(jnp.int32, (rows, 1), 0) + (pl.program_id(1) * tm - halo)
    a = jnp.where((t >= 0) & (t < seq_len), a, 0.0)
    tc = t[halo:halo + tm]
    gd = d // len(POOL_WINDOWS)
    ys = []
    for g, win in enumerate(POOL_WINDOWS):
        ag = a[:, g * gd:(g + 1) * gd]
        s = ag + pltpu.roll(ag, 1, 0)
        w = 2
        while w < win:
            s = pltpu.roll(s, w // 2, 0) + pltpu.roll(s, rows - w // 2, 0)
            w *= 2
        cnt = (jnp.minimum(tc + (win - win // 2), seq_len) - jnp.maximum(tc - win // 2, 0)).astype(F32)
        dlt = s[halo:halo + tm] / cnt - ag[halo:halo + tm]
        ys.append(_dot(dlt.astype(BF16), pw_ref[g]))
    y = jnp.concatenate(ys, axis=1) * ps_ref[...]
    h1 = h + mod[:, 2 * d:3 * d] * y
    h2 = _mlp_tail(h1, g2_ref[...], mod[:, 3 * d:4 * d], mod[:, 4 * d:5 * d], mod[:, 5 * d:6 * d], w1_ref, w2_ref, fc)
    o_ref[0] = _rms(h2, NORM_EPS) * gf_ref[...]


def _poolmlp_call(h, mod, g1, pool_w, pool_scale, g2, w1, w2, gf, tm=512, fc=1024):
    bsz, l, d = h.shape
    dff = w1.shape[1]
    halo = POOL_HALO
    nb = tm // halo
    last = l // halo - 1
    row = pl.BlockSpec((1, tm, d), lambda b, i: (b, i, 0))
    return pl.pallas_call(
        functools.partial(_poolmlp_kernel, fc=fc, seq_len=l),
        grid=(bsz, l // tm),
        in_specs=[row,
                  pl.BlockSpec((1, halo, d), lambda b, i: (b, jnp.maximum(i * nb - 1, 0), 0)),
                  pl.BlockSpec((1, halo, d), lambda b, i: (b, jnp.minimum((i + 1) * nb, last), 0)),
                  pl.BlockSpec((1, 1, N_MOD * d), lambda b, i: (b, 0, 0)),
                  _const_spec((1, d)), _const_spec(pool_w.shape), _const_spec((1, d)), _const_spec((1, d)),
                  _const_spec((d, dff)), _const_spec((dff, d)), _const_spec((1, d))],
        out_specs=row,
        out_shape=jax.ShapeDtypeStruct((bsz, l, d), F32),
        compiler_params=_params(2),
        name="poolmlp",
    )(h, h, h, mod, g1.reshape(1, d), pool_w, pool_scale.reshape(1, d), g2.reshape(1, d), w1, w2, gf.reshape(1, d))


def kernel(x, c, ctx, c_ctx, ada_w, ada_b, norm1_g, norm2_g, mix_w_in, mix_b_in, mix_w_out, mix_b_out, lam_q1, lam_k1, lam_q2, lam_k2, subln_g, hy_conv_w, hy_conv_b, hy_pos_w1, hy_pos_b1, hy_freq1, hy_pos_w2, hy_pos_b2, hy_freq2, hy_pos_w3, hy_bias, pool_w, pool_scale, mlp_w1, mlp_w2, final_g):
    bsz, seq_len, d = x.shape
    depth = ada_w.shape[0]
    assert depth == 2 and bsz % 2 == 0 and bsz < MOD_ROWS and seq_len % GRID_W == 0
    att_w = ATT_HEADS * ATT_V_DIM
    q_cols = k_cols = ATT_HEADS * 2 * ATT_QK_DIM
    kv_start, hy_start = q_cols, q_cols + k_cols + att_w
    in_cols = mix_w_in.shape[2]

    cv = jnp.concatenate([c, c_ctx[None, :], jnp.zeros((MOD_ROWS - bsz - 1, d), F32)], axis=0)
    mod = _mod_call(cv, ada_w, ada_b)
    mod_l = [mod[i, :bsz].reshape(bsz, 1, N_MOD * d) for i in range(depth)]

    w_in = mix_w_in[0].astype(BF16)
    lam_init = 0.8 - 0.6 * math.exp(-0.3 * 0)
    assert k_cols == q_cols and in_cols - hy_start == 3 * (d - att_w)
    q, k, v, u, x0 = _latproj_call(
        x, mod_l[0][:, :, 0:d], mod_l[0][:, :, d:2 * d], norm1_g[0], w_in, mix_b_in[0], _rope_tables(seq_len),
        hy_conv_w[0], hy_conv_b[0], n_att=hy_start, qk=q_cols, q_scale=ATT_QK_DIM ** -0.5 * math.log2(math.e))
    mod_c = mod[0, bsz:bsz + 1].reshape(1, 1, N_MOD * d)
    kc, vc = _ctxproj_call(ctx, mod_c[:, :, 0:d], mod_c[:, :, d:2 * d], norm1_g[0], w_in[:, kv_start:hy_start],
                           mix_b_in[0, kv_start:hy_start], k_cols)
    lamv = jnp.stack([lam_q1[0], lam_k1[0], lam_q2[0], lam_k2[0]], axis=0)
    att = _attn_call(q, kc, vc, k, v, lamv, subln_g[0], lam_init)
    filt = (hy_pos_w1[0], hy_pos_b1[0], hy_freq1[0], hy_pos_w2[0], hy_pos_b2[0], hy_freq2[0], hy_pos_w3[0])
    hy = _hyena_call(u, x0, filt, hy_bias[0])
    h = _mixmlp_call(x, att, hy, mod_l[0], mix_w_out[0].astype(BF16), mix_b_out[0], norm2_g[0],
                     mlp_w1[0].astype(BF16), mlp_w2[0].astype(BF16))

    return _poolmlp_call(h, mod_l[1], norm1_g[1], pool_w[0].astype(BF16), pool_scale[0], norm2_g[1],
                         mlp_w1[1].astype(BF16), mlp_w2[1].astype(BF16), final_g)
```

```python
import functools
import math

import numpy as np
import jax
import jax.numpy as jnp
from jax import lax
from jax.experimental import pallas as pl
from jax.experimental.pallas import tpu as pltpu

F32 = jnp.float32
BF16 = jnp.bfloat16

GRID_W = 64
N_MOD = 6
ATT_HEADS = 4
ATT_V_DIM = 128
ATT_QK_DIM = 64
ROPE_BASE = 10000.0
HY_POS_EMB = 33
HY_DECAY_TARGET = 1e-2
HY_FAST_DECAY_PCT = 0.3
HY_SLOW_DECAY_PCT = 1.5
POOL_WINDOWS = (2, 4, 8, 16)
NORM_EPS = 1e-6
SUBLN_EPS = 1e-5
SUBLANES = 8
LANES = 128
FFT_N2 = 64
POOL_HALO = 8
MOD_ROWS = 8

VMEM_LIMIT_BYTES = 56 * 1024 * 1024


def _params(n_grid_dims):
    return pltpu.CompilerParams(
        dimension_semantics=("arbitrary",) * n_grid_dims,
        vmem_limit_bytes=VMEM_LIMIT_BYTES,
    )


def _const_spec(shape):
    nd = len(shape)
    return pl.BlockSpec(shape, lambda *_: (0,) * nd, pipeline_mode=pl.Buffered(1))


def _split_bf16(a):
    hi = a.astype(BF16)
    lo = (a - hi.astype(F32)).astype(BF16)
    return hi, lo


def _dot(a, b):
    return jnp.dot(a, b, preferred_element_type=F32)


def _dot3(a_hi, a_lo, b_hi, b_lo):
    return _dot(a_hi, b_hi) + (_dot(a_lo, b_hi) + _dot(a_hi, b_lo))


def _dot3f(a, b):
    a_hi, a_lo = _split_bf16(a)
    b_hi, b_lo = _split_bf16(b)
    return _dot3(a_hi, a_lo, b_hi, b_lo)


def _dft_dot(a_bf16, b):
    return _dot(a_bf16, b.astype(BF16))


def _rms(x, eps):
    return x * lax.rsqrt(jnp.mean(x * x, axis=-1, keepdims=True) + eps)


def _mod_kernel(cv_ref, w_ref, b_ref, o_ref):
    cv = cv_ref[...]
    s = cv / (1.0 + jnp.exp(-cv))
    o_ref[0] = _dot3f(s, w_ref[0]) + b_ref[0]


def _mod_call(cv, ada_w, ada_b, tn=1536):
    depth, d, n = ada_w.shape
    return pl.pallas_call(
        _mod_kernel,
        grid=(depth, n // tn),
        in_specs=[
            pl.BlockSpec((MOD_ROWS, d), lambda i, j: (0, 0)),
            pl.BlockSpec((1, d, tn), lambda i, j: (i, 0, j)),
            pl.BlockSpec((1, 1, tn), lambda i, j: (i, 0, j)),
        ],
        out_specs=pl.BlockSpec((1, MOD_ROWS, tn), lambda i, j: (i, 0, j)),
        out_shape=jax.ShapeDtypeStruct((depth, MOD_ROWS, n), F32),
        compiler_params=_params(2),
        name="mod",
    )(cv, ada_w, ada_b.reshape(depth, 1, n))


def _ctxproj_kernel(x_ref, sh_ref, sc_ref, g_ref, w_ref, b_ref, k_ref, v_ref):
    a = _rms(x_ref[0], NORM_EPS) * g_ref[...]
    a = a * (1.0 + sc_ref[0]) + sh_ref[0]
    p = _dot(a.astype(BF16), w_ref[...]) + b_ref[...]
    nk = k_ref.shape[2]
    k_ref[0] = p[:, :nk].astype(k_ref.dtype)
    v_ref[0] = p[:, nk:].astype(v_ref.dtype)


def _ctxproj_call(x, shift, scale, g, w_bf16, b, nk):
    bsz, s, d = x.shape
    n = w_bf16.shape[1]
    vec = pl.BlockSpec((1, 1, d), lambda bi: (0, 0, 0))
    return pl.pallas_call(
        _ctxproj_kernel,
        grid=(bsz,),
        in_specs=[pl.BlockSpec((1, s, d), lambda bi: (bi, 0, 0)), vec, vec,
                  _const_spec((1, d)), _const_spec((d, n)), _const_spec((1, n))],
        out_specs=[pl.BlockSpec((1, s, nk), lambda bi: (bi, 0, 0)), pl.BlockSpec((1, s, n - nk), lambda bi: (bi, 0, 0))],
        out_shape=[jax.ShapeDtypeStruct((bsz, s, nk), BF16), jax.ShapeDtypeStruct((bsz, s, n - nk), BF16)],
        compiler_params=_params(1),
        name="ctxproj",
    )(x, shift, scale, g.reshape(1, d), w_bf16, b.reshape(1, n))


def _latproj_kernel(x_ref, xp_ref, xn_ref, sh_ref, sc_ref, g_ref, w_ref, b_ref, cos_ref, sa_ref, sb_ref,
                    cw_ref, cb_ref, q_ref, k_ref, v_ref, u_ref, x0_ref, *, n_att, q_scale, seq_len):
    tm = x_ref.shape[1]
    halo = xp_ref.shape[1]
    rows = tm + 2 * halo
    xx = jnp.concatenate([xp_ref[0], x_ref[0], xn_ref[0]], axis=0)
    a = _rms(xx, NORM_EPS) * g_ref[...]
    a = a * (1.0 + sc_ref[0]) + sh_ref[0]

    ph = _dot(a.astype(BF16), w_ref[:, n_att:]) + b_ref[:, n_att:]
    t = lax.broadcasted_iota(jnp.int32, (rows, 1), 0) + (pl.program_id(1) * tm - halo)
    ph = jnp.where((t >= 0) & (t < seq_len), ph, 0.0)
    ncb = u_ref.shape[1]
    c = ncb * LANES

    def conv(s):
        blk = ph[:, s * c:(s + 1) * c]
        prev = pltpu.roll(blk, 1, 0)[halo:halo + tm]
        nxt = pltpu.roll(blk, rows - 1, 0)[halo:halo + tm]
        cw = cw_ref[:, s * c:(s + 1) * c]
        return prev * cw[0:1] + blk[halo:halo + tm] * cw[1:2] + nxt * cw[2:3] + cb_ref[:, s * c:(s + 1) * c]

    x0 = conv(0)
    u = conv(2) * conv(1)
    for cb in range(ncb):
        x0_ref[0, cb] = x0[:, cb * LANES:(cb + 1) * LANES]
        u_ref[0, cb] = u[:, cb * LANES:(cb + 1) * LANES]

    p = _dot(a[halo:halo + tm].astype(BF16), w_ref[:, :n_att]) + b_ref[:, :n_att]
    cos, sa, sb = cos_ref[...], sa_ref[...], sb_ref[...]
    qk = q_ref.shape[2]
    for o_ref, c_lo, scale in ((q_ref, 0, q_scale), (k_ref, qk, 1.0)):
        for c0 in range(0, qk, LANES):
            blk = p[:, c_lo + c0:c_lo + c0 + LANES]
            r = blk * cos + pltpu.roll(blk, 16, 1) * sa + pltpu.roll(blk, LANES - 16, 1) * sb
            o_ref[0, :, c0:c0 + LANES] = (r * scale).astype(o_ref.dtype)
    v_ref[0] = p[:, 2 * qk:].astype(v_ref.dtype)


def _latproj_call(x, shift, scale, g, w_bf16, b, rope_tabs, conv_w, conv_b, n_att, qk, q_scale, tm=512):
    bsz, s, d = x.shape
    n = w_bf16.shape[1]
    c = (n - n_att) // 3
    halo = SUBLANES
    nb = tm // halo
    last = s // halo - 1
    vec = pl.BlockSpec((1, 1, d), lambda bi, i: (bi, 0, 0))
    tab = pl.BlockSpec((tm, LANES), lambda bi, i: (i, 0))
    row = lambda w: pl.BlockSpec((1, tm, w), lambda bi, i: (bi, i, 0))
    cblk = pl.BlockSpec((1, c // LANES, tm, LANES), lambda bi, i: (bi, 0, i, 0))
    return pl.pallas_call(
        functools.partial(_latproj_kernel, n_att=n_att, q_scale=q_scale, seq_len=s),
        grid=(bsz, s // tm),
        in_specs=[row(d),
                  pl.BlockSpec((1, halo, d), lambda bi, i: (bi, jnp.maximum(i * nb - 1, 0), 0)),
                  pl.BlockSpec((1, halo, d), lambda bi, i: (bi, jnp.minimum((i + 1) * nb, last), 0)),
                  vec, vec, _const_spec((1, d)), _const_spec((d, n)), _const_spec((1, n)), tab, tab, tab,
                  _const_spec(conv_w.shape), _const_spec((1, 3 * c))],
        out_specs=[row(qk), row(qk), row(n_att - 2 * qk), cblk, cblk],
        out_shape=[jax.ShapeDtypeStruct((bsz, s, qk), BF16), jax.ShapeDtypeStruct((bsz, s, qk), BF16),
                   jax.ShapeDtypeStruct((bsz, s, n_att - 2 * qk), BF16),
                   jax.ShapeDtypeStruct((bsz, c // LANES, s, LANES), F32),
                   jax.ShapeDtypeStruct((bsz, c // LANES, s, LANES), F32)],
        compiler_params=_params(2),
        name="latproj",
    )(x, x, x, shift, scale, g.reshape(1, d), w_bf16, b.reshape(1, n), *rope_tabs, conv_w, conv_b.reshape(1, 3 * c))


def _rope_tables(seq_len):
    axis_dim = ATT_QK_DIM // 2
    n_freq = axis_dim // 2
    inv = (ROPE_BASE ** (-np.arange(n_freq, dtype=np.float32) / n_freq)).astype(np.float32)
    t = np.arange(seq_len)
    row, col = t // GRID_W, t % GRID_W
    jj = np.arange(128) % ATT_QK_DIM
    is_col = (jj // axis_dim) == 1
    second = ((jj % axis_dim) >= n_freq)[None, :]
    pos = np.where(is_col[None, :], col[:, None], row[:, None]).astype(np.float32)
    ang = (pos * inv[jj % n_freq][None, :]).astype(np.float64)
    cos, sin = np.cos(ang), np.sin(ang)
    as32 = lambda m: jnp.asarray(m, dtype=F32)
    return as32(cos), as32(np.where(second, sin, 0.0)), as32(np.where(second, 0.0, -sin))


def _attn_kernel(lam_ref, q_ref, kc_ref, vc_ref, kl_ref, vl_ref, g_ref, o_ref, *, tk, lam_init):
    q = q_ref[0]
    tq = q.shape[0]
    lane = lax.broadcasted_iota(jnp.int32, q.shape, 1)
    zero = jnp.zeros_like(q)
    q2 = jnp.concatenate([jnp.where(lane < ATT_QK_DIM, q, zero), jnp.where(lane >= ATT_QK_DIM, q, zero)], axis=0)

    def chunk(k, v, m, acc):
        s = lax.dot_general(q2, k, (((1,), (1,)), ((), ())), preferred_element_type=F32)
        rowmax = jnp.max(s, axis=1, keepdims=True)
        m_new = rowmax if m is None else jnp.maximum(m, rowmax)
        p = jnp.exp2(s - m_new).astype(BF16)
        pv = _dot(p, jnp.concatenate([v, jnp.ones_like(v)], axis=1))
        return m_new, (pv if acc is None else jnp.exp2(m - m_new) * acc + pv)

    m, acc = chunk(kc_ref[0], vc_ref[0], None, None)
    for j in range(kl_ref.shape[1] // tk):
        m, acc = chunk(kl_ref[0, j * tk:(j + 1) * tk, :], vl_ref[0, j * tk:(j + 1) * tk, :], m, acc)

    lamv = lam_ref[...]
    lam = (jnp.exp(jnp.sum(lamv[0:1] * lamv[1:2], axis=1, keepdims=True))
           - jnp.exp(jnp.sum(lamv[2:3] * lamv[3:4], axis=1, keepdims=True)) + lam_init)
    o_all = acc[:, :ATT_V_DIM] / acc[:, ATT_V_DIM:]
    o = o_all[:tq] - lam * o_all[tq:]
    o = _rms(o, SUBLN_EPS) * g_ref[...] * (1.0 - lam_init)
    o_ref[0] = o.astype(o_ref.dtype)


def _attn_call(q, kc, vc, kl, vl, lamv, g, lam_init, tq=1024, tk=256):
    bsz, l, width = q.shape
    heads = width // ATT_V_DIM
    lc = kc.shape[1]
    hd = ATT_V_DIM
    return pl.pallas_call(
        functools.partial(_attn_kernel, tk=tk, lam_init=lam_init),
        grid=(bsz, heads, l // tq),
        in_specs=[
            _const_spec(lamv.shape),
            pl.BlockSpec((1, tq, hd), lambda b, h, i: (b, i, h)),
            pl.BlockSpec((1, lc, hd), lambda b, h, i: (b, 0, h)),
            pl.BlockSpec((1, lc, hd), lambda b, h, i: (b, 0, h)),
            pl.BlockSpec((1, l, hd), lambda b, h, i: (b, 0, h)),
            pl.BlockSpec((1, l, hd), lambda b, h, i: (b, 0, h)),
            _const_spec((1, hd)),
        ],
        out_specs=pl.BlockSpec((1, tq, hd), lambda b, h, i: (b, i, h)),
        out_shape=jax.ShapeDtypeStruct((bsz, l, width), BF16),
        compiler_params=_params(3),
        name="diffattn",
    )(lamv, q, kc, vc, kl, vl, g.reshape(1, hd))


def _filter_kernel(z_ref, w1_ref, b1_ref, f1_ref, w2_ref, b2_ref, f2_ref, w3_ref, dl_ref, o_ref, *, seq_len):
    tm = z_ref.shape[0]
    h1 = jnp.sin(f1_ref[...] * (_dot3f(z_ref[...], w1_ref[...]) + b1_ref[...]))
    h2 = jnp.sin(f2_ref[...] * (_dot3f(h1, w2_ref[...]) + b2_ref[...]))
    h = _dot3f(h2, w3_ref[...])
    row = lax.broadcasted_iota(jnp.int32, h.shape, 0) + pl.program_id(0) * tm
    col = lax.broadcasted_iota(jnp.int32, h.shape, 1)
    t = row.astype(F32) * (1.0 / (seq_len - 1))
    h = h * jnp.exp(-t * dl_ref[...])
    half = h.shape[1] // 2
    o_ref[...] = jnp.where((row == 0) & (col >= half), 0.0, h)


def _filter_call(z, w1, b1, f1, w2, b2, f2, w3, absdelta2, tm=512):
    seq_len, kz = z.shape
    hid = w2.shape[0]
    n = w3.shape[1]
    return pl.pallas_call(
        functools.partial(_filter_kernel, seq_len=seq_len),
        grid=(seq_len // tm,),
        in_specs=[
            pl.BlockSpec((tm, kz), lambda i: (i, 0)),
            _const_spec((kz, hid)), _const_spec((1, hid)), _const_spec((1, hid)),
            _const_spec((hid, hid)), _const_spec((1, hid)), _const_spec((1, hid)),
            _const_spec((hid, n)), _const_spec((1, n)),
        ],
        out_specs=pl.BlockSpec((tm, n), lambda i: (i, 0)),
        out_shape=jax.ShapeDtypeStruct((seq_len, n), F32),
        compiler_params=_params(1),
        name="hyfilter",
    )(z, w1, b1.reshape(1, hid), f1.reshape(1, hid), w2, b2.reshape(1, hid), f2.reshape(1, hid), w3, absdelta2)


def _gather_minor(refs, n_rows):
    cols = []
    for j in range(SUBLANES):
        parts = [r[pl.ds(j, n_rows, stride=SUBLANES), :] for r in refs]
        cols.append(parts[0] if len(parts) == 1 else jnp.concatenate(parts, axis=0))
    return jnp.concatenate(cols, axis=1)


def _slab_gather(src_ref, jb, slab_ref):
    rows = src_ref.shape[0]
    slab_ref[...] = src_ref[:, jb].reshape(rows * SUBLANES, LANES)
    return _gather_minor([slab_ref], rows)


def _slab_scatter(val, slab_ref, dst_ref, jb):
    rows = dst_ref.shape[0]
    for j in range(SUBLANES):
        slab_ref[pl.ds(j, rows, stride=SUBLANES), :] = val[:, j * LANES:(j + 1) * LANES]
    dst_ref[:, jb] = slab_ref[...].reshape(rows, SUBLANES, LANES)


def _filter_fft_kernel(fr_ref, g_ref, twr_ref, twi_ref, hf_ref, hb_ref, kr_ref, ki_ref,
                       slab_in, slab_out, a_ref):
    n1 = a_ref.shape[1]
    n2 = FFT_N2
    w = SUBLANES * LANES
    for jb in range(n2 // SUBLANES):
        z = jnp.concatenate([_slab_gather(hf_ref, jb, slab_in.at[0]), _slab_gather(hb_ref, jb, slab_in.at[1])], axis=1)
        out = _dft_dot(fr_ref[...], z)
        for t in range(4):
            _slab_scatter(out[(t % 2) * n1:(t % 2 + 1) * n1, (t // 2) * w:(t // 2 + 1) * w], slab_out.at[t], a_ref.at[t], jb)

    def body(kb, _):
        k0 = pl.multiple_of(kb * SUBLANES, SUBLANES)
        tw_rows = pl.ds(pl.multiple_of(kb * (SUBLANES * n2), SUBLANES * n2), SUBLANES * n2)
        twr = twr_ref[tw_rows, :].reshape(SUBLANES, n2, LANES)
        twi = twi_ref[tw_rows, :].reshape(SUBLANES, n2, LANES)
        a = [a_ref[t, pl.ds(k0, SUBLANES)].reshape(SUBLANES, n2, LANES) for t in range(4)]
        cols = []
        for t in range(2):
            ar, ai = a[2 * t], a[2 * t + 1]
            sr, si = ar * twr - ai * twi, ar * twi + ai * twr
            cols += [jnp.concatenate([sr[i], si[i]], axis=0) for i in range(SUBLANES)]
        x = _dft_dot(g_ref[...], jnp.concatenate(cols, axis=1))
        for i in range(SUBLANES):
            xf = x[:, i * LANES:(i + 1) * LANES]
            xb = x[:, (SUBLANES + i) * LANES:(SUBLANES + i + 1) * LANES]
            rows = pl.ds(pl.multiple_of((k0 + i) * n2, n2), n2)
            kr_ref[rows, :] = xf[:n2] + xb[:n2]
            ki_ref[rows, :] = xf[n2:] - xb[n2:]
        return 0

    lax.fori_loop(0, n1 // SUBLANES, body, 0)


def _filter_fft_call(dft, hcat):
    seq_len, c2 = hcat.shape
    c = c2 // 2
    n, n1, n1h = dft["n"], dft["n1"], dft["n1h"]
    nb = FFT_N2 // SUBLANES
    ncb = c // LANES
    fr, g = dft["f_real"].astype(BF16), dft["g_fwd"].astype(BF16)
    hv = hcat.reshape(n1h, nb, SUBLANES, c2)
    blk = lambda off: pl.BlockSpec((n1h, nb, SUBLANES, LANES), lambda i, off=off: (0, 0, 0, i + off))
    out = pl.BlockSpec((None, n, LANES), lambda i: (i, 0, 0))
    return pl.pallas_call(
        _filter_fft_kernel,
        grid=(ncb,),
        in_specs=[_const_spec(fr.shape), _const_spec(g.shape), _const_spec((n, LANES)), _const_spec((n, LANES)),
                  blk(0), blk(ncb)],
        out_specs=[out, out],
        out_shape=[jax.ShapeDtypeStruct((ncb, n, LANES), F32)] * 2,
        scratch_shapes=[pltpu.VMEM((2, n1h * SUBLANES, LANES), F32), pltpu.VMEM((4, n1 * SUBLANES, LANES), F32),
                        pltpu.VMEM((4, n1, nb, SUBLANES, LANES), F32)],
        compiler_params=_params(1),
        name="hyfilter_fft",
    )(fr, g, dft["twr"], dft["twi"], hv, hv)


def _hyena_conv_kernel(fc_ref, g_ref, gc_ref, e_ref, twr_ref, twi_ref, kr_ref, ki_ref, u_ref, x0_ref, bias_ref, o_ref,
                       slab_in, slab_a, a_ref, *, inv_n):
    n1 = a_ref.shape[1]
    n1h = u_ref.shape[1]
    n2 = FFT_N2
    nb = n2 // SUBLANES

    for jb in range(nb):
        z = jnp.concatenate([_slab_gather(u_ref.at[c], jb, slab_in.at[c]) for c in range(2)], axis=0)
        out = _dft_dot(fc_ref[...], z)
        for c in range(2):
            _slab_scatter(out[c * n1:(c + 1) * n1], slab_a.at[c], a_ref.at[c], jb)

    def body(kb, _):
        k0 = pl.multiple_of(kb * SUBLANES, SUBLANES)
        rows = pl.ds(pl.multiple_of(kb * (SUBLANES * n2), SUBLANES * n2), SUBLANES * n2)
        blocks = lambda ref: ref[rows, :].reshape(SUBLANES, n2, LANES)
        twr, twi, kr, ki = blocks(twr_ref), blocks(twi_ref), blocks(kr_ref), blocks(ki_ref)
        ar = a_ref[0, pl.ds(k0, SUBLANES)].reshape(SUBLANES, n2, LANES)
        ai = a_ref[1, pl.ds(k0, SUBLANES)].reshape(SUBLANES, n2, LANES)
        sr, si = ar * twr - ai * twi, ar * twi + ai * twr
        lanes = lambda re, im: jnp.concatenate([jnp.concatenate([re[i], im[i]], axis=0) for i in range(SUBLANES)], axis=1)
        x = _dft_dot(g_ref[...], lanes(sr, si))
        unl = lambda v, lo: jnp.stack([v[lo:lo + n2, i * LANES:(i + 1) * LANES] for i in range(SUBLANES)], axis=0)
        xr, xi = unl(x, 0), unl(x, n2)
        t = _dft_dot(gc_ref[...], lanes(xr * kr - xi * ki, xr * ki + xi * kr))
        tr, ti = unl(t, 0), unl(t, n2)
        a_ref[0, pl.ds(k0, SUBLANES)] = (tr * twr + ti * twi).reshape(SUBLANES, nb, SUBLANES, LANES)
        a_ref[1, pl.ds(k0, SUBLANES)] = (ti * twr - tr * twi).reshape(SUBLANES, nb, SUBLANES, LANES)
        return 0

    lax.fori_loop(0, n1 // SUBLANES, body, 0)

    for jb in range(nb):
        t = jnp.concatenate([_slab_gather(a_ref.at[c], jb, slab_a.at[c]) for c in range(2)], axis=0)
        y = _dft_dot(e_ref[...], t) * inv_n
        for c in range(2):
            _slab_scatter(y[c * n1h:(c + 1) * n1h], slab_in.at[c], o_ref.at[c], jb)
    o_ref[...] = x0_ref[...] * (o_ref[...] + u_ref[...] * bias_ref[...])


def _dft_blocks(seq_len):
    n = 2 * seq_len
    n2 = FFT_N2
    n1 = n // n2
    n1h = n1 // 2
    k1 = np.arange(n1)[:, None].astype(np.float64)
    a = 2.0 * np.pi * k1 * np.arange(n1h)[None, :] / n1
    fr, fi = np.cos(a), -np.sin(a)
    f_cplx = np.block([[fr, -fi], [fi, fr]])
    f_real = np.concatenate([fr, fi], axis=0)
    e_cplx = np.block([[fr.T, fi.T], [-fi.T, fr.T]])
    b = 2.0 * np.pi * np.arange(n2)[:, None] * np.arange(n2)[None, :] / n2
    gr, gi = np.cos(b), -np.sin(b)
    g_fwd = np.block([[gr, -gi], [gi, gr]])
    g_inv = np.block([[gr, gi], [-gi, gr]])
    kk = np.arange(n1)[:, None] * np.arange(n2)[None, :]
    ang = (2.0 * np.pi / n) * kk.reshape(n1 * n2, 1)
    twr = np.broadcast_to(np.cos(ang), (n1 * n2, 128))
    twi = np.broadcast_to(-np.sin(ang), (n1 * n2, 128))
    as32 = lambda m: jnp.asarray(np.ascontiguousarray(m), dtype=F32)
    return dict(n=n, n1=n1, n1h=n1h, f_cplx=as32(f_cplx), f_real=as32(f_real), e_cplx=as32(e_cplx),
                g_fwd=as32(g_fwd), g_inv=as32(g_inv), twr=as32(twr), twi=as32(twi))


def _hyena_call(u, x0, filt, hy_bias):
    bsz, ncb, seq_len, _ = u.shape
    c = ncb * LANES
    n2 = FFT_N2
    dft = _dft_blocks(seq_len)
    n, n1, n1h = dft["n"], dft["n1"], dft["n1h"]
    pairs = bsz // 2

    w1, b1, f1, w2, b2, f2, w3 = filt
    hid = w2.shape[0]
    bands = (HY_POS_EMB - 1) // 2
    t = np.linspace(0.0, 1.0, seq_len)[:, None]
    freqs = np.linspace(1e-4, bands - 1, bands)
    ang = (2.0 * math.pi / seq_len) * np.arange(seq_len)[:, None] * freqs[None, :]
    z = jnp.asarray(np.concatenate([t, np.cos(ang), -np.sin(ang), np.zeros((seq_len, hid - HY_POS_EMB))], axis=-1),
                    dtype=F32)
    w1p = jnp.concatenate([w1, jnp.zeros((hid - HY_POS_EMB, hid), F32)], axis=0)
    max_decay = math.log(HY_DECAY_TARGET) / HY_FAST_DECAY_PCT
    min_decay = math.log(HY_DECAY_TARGET) / HY_SLOW_DECAY_PCT
    absdelta = np.abs(np.linspace(min_decay, max_decay, c))
    absdelta2 = jnp.asarray(np.concatenate([absdelta, absdelta])[None, :], dtype=F32)
    hcat = _filter_call(z, w1p, b1, f1, w2, b2, f2, w3, absdelta2)
    nb = n2 // SUBLANES
    kr, ki = _filter_fft_call(dft, hcat)

    view = lambda a: a.reshape(pairs, 2, ncb, n1h, nb, SUBLANES, LANES)
    pair_spec = pl.BlockSpec((None, 2, None, n1h, nb, SUBLANES, LANES), lambda cb, p: (p, 0, cb, 0, 0, 0, 0))
    k_spec = pl.BlockSpec((None, n, LANES), lambda cb, p: (cb, 0, 0), pipeline_mode=pl.Buffered(1))
    fc, g, gc, e = (dft[k].astype(BF16) for k in ("f_cplx", "g_fwd", "g_inv", "e_cplx"))
    hy = pl.pallas_call(
        functools.partial(_hyena_conv_kernel, inv_n=1.0 / n),
        grid=(ncb, pairs),
        in_specs=[_const_spec(fc.shape), _const_spec(g.shape), _const_spec(gc.shape), _const_spec(e.shape),
                  _const_spec((n, LANES)), _const_spec((n, LANES)), k_spec, k_spec, pair_spec, pair_spec,
                  pl.BlockSpec((1, LANES), lambda cb, p: (0, cb))],
        out_specs=pair_spec,
        out_shape=jax.ShapeDtypeStruct((pairs, 2, ncb, n1h, nb, SUBLANES, LANES), F32),
        scratch_shapes=[pltpu.VMEM((2, n1h * SUBLANES, LANES), F32), pltpu.VMEM((2, n1 * SUBLANES, LANES), F32),
                        pltpu.VMEM((2, n1, nb, SUBLANES, LANES), F32)],
        compiler_params=_params(2),
        name="hyconv",
    )(fc, g, gc, e, dft["twr"], dft["twi"], kr, ki, view(u), view(x0), hy_bias.reshape(1, c))
    return hy.reshape(bsz, ncb, seq_len, LANES)


def _mlp_tail(h, g2, shift, scale, gate, w1_ref, w2_ref, fc):
    a = (_rms(h, NORM_EPS) * g2) * (1.0 + scale) + shift
    a = a.astype(BF16)
    acc = None
    for c0 in range(0, w1_ref.shape[1], fc):
        hid = jnp.maximum(_dot(a, w1_ref[:, c0:c0 + fc]), 0.0)
        part = _dot((hid * hid).astype(BF16), w2_ref[c0:c0 + fc, :])
        acc = part if acc is None else acc + part
    return h + gate * acc


def _mixmlp_kernel(x_ref, att_ref, hy_ref, mod_ref, wa_ref, wh_ref, bo_ref, g2_ref, w1_ref, w2_ref, o_ref, *, fc):
    d = x_ref.shape[2]
    mod = mod_ref[0]
    hy = jnp.concatenate([hy_ref[0, cb] for cb in range(hy_ref.shape[1])], axis=1).astype(BF16)
    y = _dot(att_ref[0], wa_ref[...]) + _dot(hy, wh_ref[...]) + bo_ref[...]
    h = x_ref[0] + mod[:, 2 * d:3 * d] * y
    o_ref[0] = _mlp_tail(h, g2_ref[...], mod[:, 3 * d:4 * d], mod[:, 4 * d:5 * d], mod[:, 5 * d:6 * d],
                         w1_ref, w2_ref, fc)


def _mixmlp_call(x, att, hy, mod, w_out, b_out, g2, w1, w2, tm=512, fc=1024):
    bsz, l, d = x.shape
    wa = att.shape[2]
    dff = w1.shape[1]
    row = lambda w: pl.BlockSpec((1, tm, w), lambda b, i: (b, i, 0))
    return pl.pallas_call(
        functools.partial(_mixmlp_kernel, fc=fc),
        grid=(bsz, l // tm),
        in_specs=[row(d), row(wa), pl.BlockSpec((1, hy.shape[1], tm, LANES), lambda b, i: (b, 0, i, 0)),
                  pl.BlockSpec((1, 1, N_MOD * d), lambda b, i: (b, 0, 0)),
                  _const_spec((wa, d)), _const_spec((d - wa, d)), _const_spec((1, d)), _const_spec((1, d)),
                  _const_spec((d, dff)), _const_spec((dff, d))],
        out_specs=row(d),
        out_shape=jax.ShapeDtypeStruct((bsz, l, d), F32),
        compiler_params=_params(2),
        name="mixmlp",
    )(x, att, hy, mod, w_out[:wa], w_out[wa:], b_out.reshape(1, d), g2.reshape(1, d), w1, w2)


def _poolmlp_kernel(h_ref, hp_ref, hn_ref, mod_ref, g1_ref, pw_ref, ps_ref, g2_ref, w1_ref, w2_ref, gf_ref, o_ref,
                    *, fc, seq_len):
    tm, d = h_ref.shape[1:]
    halo = POOL_HALO
    rows = tm + 2 * halo
    mod = mod_ref[0]
    h = h_ref[0]
    hx = jnp.concatenate([hp_ref[0], h, hn_ref[0]], axis=0)
    a = (_rms(hx, NORM_EPS) * g1_ref[...]) * (1.0 + mod[:, d:2 * d]) + mod[:, 0:d]
    t = lax.broadcasted_iota(jnp.int32, (rows, 1), 0) + (pl.program_id(1) * tm - halo)
    a = jnp.where((t >= 0) & (t < seq_len), a, 0.0)
    tc = t[halo:halo + tm]
    gd = d // len(POOL_WINDOWS)
    ys = []
    for g, win in enumerate(POOL_WINDOWS):
        ag = a[:, g * gd:(g + 1) * gd]
        s = ag + pltpu.roll(ag, 1, 0)
        w = 2
        while w < win:
            s = pltpu.roll(s, w // 2, 0) + pltpu.roll(s, rows - w // 2, 0)
            w *= 2
        cnt = (jnp.minimum(tc + (win - win // 2), seq_len) - jnp.maximum(tc - win // 2, 0)).astype(F32)
        dlt = s[halo:halo + tm] / cnt - ag[halo:halo + tm]
        ys.append(_dot(dlt.astype(BF16), pw_ref[g]))
    y = jnp.concatenate(ys, axis=1) * ps_ref[...]
    h1 = h + mod[:, 2 * d:3 * d] * y
    h2 = _mlp_tail(h1, g2_ref[...], mod[:, 3 * d:4 * d], mod[:, 4 * d:5 * d], mod[:, 5 * d:6 * d], w1_ref, w2_ref, fc)
    o_ref[0] = _rms(h2, NORM_EPS) * gf_ref[...]


def _poolmlp_call(h, mod, g1, pool_w, pool_scale, g2, w1, w2, gf, tm=512, fc=1024):
    bsz, l, d = h.shape
    dff = w1.shape[1]
    halo = POOL_HALO
    nb = tm // halo
    last = l // halo - 1
    row = pl.BlockSpec((1, tm, d), lambda b, i: (b, i, 0))
    return pl.pallas_call(
        functools.partial(_poolmlp_kernel, fc=fc, seq_len=l),
        grid=(bsz, l // tm),
        in_specs=[row,
                  pl.BlockSpec((1, halo, d), lambda b, i: (b, jnp.maximum(i * nb - 1, 0), 0)),
                  pl.BlockSpec((1, halo, d), lambda b, i: (b, jnp.minimum((i + 1) * nb, last), 0)),
                  pl.BlockSpec((1, 1, N_MOD * d), lambda b, i: (b, 0, 0)),
                  _const_spec((1, d)), _const_spec(pool_w.shape), _const_spec((1, d)), _const_spec((1, d)),
                  _const_spec((d, dff)), _const_spec((dff, d)), _const_spec((1, d))],
        out_specs=row,
        out_shape=jax.ShapeDtypeStruct((bsz, l, d), F32),
        compiler_params=_params(2),
        name="poolmlp",
    )(h, h, h, mod, g1.reshape(1, d), pool_w, pool_scale.reshape(1, d), g2.reshape(1, d), w1, w2, gf.reshape(1, d))


def kernel(x, c, ctx, c_ctx, ada_w, ada_b, norm1_g, norm2_g, mix_w_in, mix_b_in, mix_w_out, mix_b_out, lam_q1, lam_k1, lam_q2, lam_k2, subln_g, hy_conv_w, hy_conv_b, hy_pos_w1, hy_pos_b1, hy_freq1, hy_pos_w2, hy_pos_b2, hy_freq2, hy_pos_w3, hy_bias, pool_w, pool_scale, mlp_w1, mlp_w2, final_g):
    bsz, seq_len, d = x.shape
    depth = ada_w.shape[0]
    assert depth == 2 and bsz % 2 == 0 and bsz < MOD_ROWS and seq_len % GRID_W == 0
    att_w = ATT_HEADS * ATT_V_DIM
    q_cols = k_cols = ATT_HEADS * 2 * ATT_QK_DIM
    kv_start, hy_start = q_cols, q_cols + k_cols + att_w
    in_cols = mix_w_in.shape[2]

    cv = jnp.concatenate([c, c_ctx[None, :], jnp.zeros((MOD_ROWS - bsz - 1, d), F32)], axis=0)
    mod = _mod_call(cv, ada_w, ada_b)
    mod_l = [mod[i, :bsz].reshape(bsz, 1, N_MOD * d) for i in range(depth)]

    w_in = mix_w_in[0].astype(BF16)
    lam_init = 0.8 - 0.6 * math.exp(-0.3 * 0)
    assert k_cols == q_cols and in_cols - hy_start == 3 * (d - att_w)
    q, k, v, u, x0 = _latproj_call(
        x, mod_l[0][:, :, 0:d], mod_l[0][:, :, d:2 * d], norm1_g[0], w_in, mix_b_in[0], _rope_tables(seq_len),
        hy_conv_w[0], hy_conv_b[0], n_att=hy_start, qk=q_cols, q_scale=ATT_QK_DIM ** -0.5 * math.log2(math.e))
    mod_c = mod[0, bsz:bsz + 1].reshape(1, 1, N_MOD * d)
    kc, vc = _ctxproj_call(ctx, mod_c[:, :, 0:d], mod_c[:, :, d:2 * d], norm1_g[0], w_in[:, kv_start:hy_start],
                           mix_b_in[0, kv_start:hy_start], k_cols)
    lamv = jnp.stack([lam_q1[0], lam_k1[0], lam_q2[0], lam_k2[0]], axis=0)
    att = _attn_call(q, kc, vc, k, v, lamv, subln_g[0], lam_init)
    filt = (hy_pos_w1[0], hy_pos_b1[0], hy_freq1[0], hy_pos_w2[0], hy_pos_b2[0], hy_freq2[0], hy_pos_w3[0])
    hy = _hyena_call(u, x0, filt, hy_bias[0])
    h = _mixmlp_call(x, att, hy, mod_l[0], mix_w_out[0].astype(BF16), mix_b_out[0], norm2_g[0],
                     mlp_w1[0].astype(BF16), mlp_w2[0].astype(BF16))

    return _poolmlp_call(h, mod_l[1], norm1_g[1], pool_w[0].astype(BF16), pool_scale[0], norm2_g[1],
                         mlp_w1[1].astype(BF16), mlp_w2[1].astype(BF16), final_g)
```

```python
import functools
import math

import numpy as np
import jax
import jax.numpy as jnp
from jax import lax
from jax.experimental import pallas as pl
from jax.experimental.pallas import tpu as pltpu

F32 = jnp.float32
BF16 = jnp.bfloat16

GRID_W = 64
N_MOD = 6
ATT_HEADS = 4
ATT_V_DIM = 128
ATT_QK_DIM = 64
ROPE_BASE = 10000.0
HY_POS_EMB = 33
HY_DECAY_TARGET = 1e-2
HY_FAST_DECAY_PCT = 0.3
HY_SLOW_DECAY_PCT = 1.5
POOL_WINDOWS = (2, 4, 8, 16)
NORM_EPS = 1e-6
SUBLN_EPS = 1e-5
SUBLANES = 8
LANES = 128
FFT_N2 = 64
POOL_HALO = 8
WEIGHT_STAGE_BYTES = 2 * 1024 * 1024
MOD_ROWS = 8

VMEM_LIMIT_BYTES = 56 * 1024 * 1024


def _params(n_grid_dims):
    return pltpu.CompilerParams(
        dimension_semantics=("arbitrary",) * n_grid_dims,
        vmem_limit_bytes=VMEM_LIMIT_BYTES,
    )


def _const_spec(shape):
    nd = len(shape)
    return pl.BlockSpec(shape, lambda *_: (0,) * nd, pipeline_mode=pl.Buffered(1))


def _split_bf16(a):
    hi = a.astype(BF16)
    lo = (a - hi.astype(F32)).astype(BF16)
    return hi, lo


def _dot(a, b):
    return jnp.dot(a, b, preferred_element_type=F32)


def _dot3(a_hi, a_lo, b_hi, b_lo):
    return _dot(a_hi, b_hi) + (_dot(a_lo, b_hi) + _dot(a_hi, b_lo))


def _dot3f(a, b):
    a_hi, a_lo = _split_bf16(a)
    b_hi, b_lo = _split_bf16(b)
    return _dot3(a_hi, a_lo, b_hi, b_lo)


def _dft_dot(a_bf16, b):
    return _dot(a_bf16, b.astype(BF16))


def _rms(x, eps):
    return x * lax.rsqrt(jnp.mean(x * x, axis=-1, keepdims=True) + eps)


def _mod_kernel(cv_ref, w_ref, b_ref, o_ref):
    cv = cv_ref[...]
    s = cv / (1.0 + jnp.exp(-cv))
    o_ref[0] = _dot3f(s, w_ref[0]) + b_ref[0]


def _mod_call(cv, ada_w, ada_b, tn=1536):
    depth, d, n = ada_w.shape
    return pl.pallas_call(
        _mod_kernel,
        grid=(depth, n // tn),
        in_specs=[
            pl.BlockSpec((MOD_ROWS, d), lambda i, j: (0, 0)),
            pl.BlockSpec((1, d, tn), lambda i, j: (i, 0, j)),
            pl.BlockSpec((1, 1, tn), lambda i, j: (i, 0, j)),
        ],
        out_specs=pl.BlockSpec((1, MOD_ROWS, tn), lambda i, j: (i, 0, j)),
        out_shape=jax.ShapeDtypeStruct((depth, MOD_ROWS, n), F32),
        compiler_params=_params(2),
        name="mod",
    )(cv, ada_w, ada_b.reshape(depth, 1, n))


def _ctxproj_kernel(x_ref, sh_ref, sc_ref, g_ref, w_ref, b_ref, k_ref, v_ref):
    a = _rms(x_ref[0], NORM_EPS) * g_ref[...]
    a = a * (1.0 + sc_ref[0]) + sh_ref[0]
    p = _dot(a.astype(BF16), w_ref[...]) + b_ref[...]
    nk = k_ref.shape[2]
    k_ref[0] = p[:, :nk].astype(k_ref.dtype)
    v_ref[0] = p[:, nk:].astype(v_ref.dtype)


def _ctxproj_call(x, shift, scale, g, w_bf16, b, nk):
    bsz, s, d = x.shape
    n = w_bf16.shape[1]
    vec = pl.BlockSpec((1, 1, d), lambda bi: (0, 0, 0))
    return pl.pallas_call(
        _ctxproj_kernel,
        grid=(bsz,),
        in_specs=[pl.BlockSpec((1, s, d), lambda bi: (bi, 0, 0)), vec, vec,
                  _const_spec((1, d)), _const_spec((d, n)), _const_spec((1, n))],
        out_specs=[pl.BlockSpec((1, s, nk), lambda bi: (bi, 0, 0)), pl.BlockSpec((1, s, n - nk), lambda bi: (bi, 0, 0))],
        out_shape=[jax.ShapeDtypeStruct((bsz, s, nk), BF16), jax.ShapeDtypeStruct((bsz, s, n - nk), BF16)],
        compiler_params=_params(1),
        name="ctxproj",
    )(x, shift, scale, g.reshape(1, d), w_bf16, b.reshape(1, n))


def _latproj_kernel(x_ref, xp_ref, xn_ref, sh_ref, sc_ref, g_ref, w_hbm, b_ref, cos_ref, sa_ref, sb_ref,
                    cw_ref, cb_ref, q_ref, k_ref, v_ref, u_ref, x0_ref, w_ref, *, n_att, q_scale, seq_len):
    @pl.when((pl.program_id(0) == 0) & (pl.program_id(1) == 0))
    def _():
        _stage_weights_bf16([(w_hbm.at[0], w_ref)])

    tm = x_ref.shape[1]
    halo = xp_ref.shape[1]
    rows = tm + 2 * halo
    xx = jnp.concatenate([xp_ref[0], x_ref[0], xn_ref[0]], axis=0)
    a = _rms(xx, NORM_EPS) * g_ref[...]
    a = a * (1.0 + sc_ref[0]) + sh_ref[0]

    ph = _dot(a.astype(BF16), w_ref[:, n_att:]) + b_ref[:, n_att:]
    t = lax.broadcasted_iota(jnp.int32, (rows, 1), 0) + (pl.program_id(1) * tm - halo)
    ph = jnp.where((t >= 0) & (t < seq_len), ph, 0.0)
    ncb = u_ref.shape[1]
    c = ncb * LANES

    def conv(s):
        blk = ph[:, s * c:(s + 1) * c]
        prev = pltpu.roll(blk, 1, 0)[halo:halo + tm]
        nxt = pltpu.roll(blk, rows - 1, 0)[halo:halo + tm]
        cw = cw_ref[:, s * c:(s + 1) * c]
        return prev * cw[0:1] + blk[halo:halo + tm] * cw[1:2] + nxt * cw[2:3] + cb_ref[:, s * c:(s + 1) * c]

    x0 = conv(0)
    u = conv(2) * conv(1)
    for cb in range(ncb):
        x0_ref[0, cb] = x0[:, cb * LANES:(cb + 1) * LANES]
        u_ref[0, cb] = u[:, cb * LANES:(cb + 1) * LANES]

    p = _dot(a[halo:halo + tm].astype(BF16), w_ref[:, :n_att]) + b_ref[:, :n_att]
    cos, sa, sb = cos_ref[...], sa_ref[...], sb_ref[...]
    qk = q_ref.shape[2]
    for o_ref, c_lo, scale in ((q_ref, 0, q_scale), (k_ref, qk, 1.0)):
        for c0 in range(0, qk, LANES):
            blk = p[:, c_lo + c0:c_lo + c0 + LANES]
            r = blk * cos + pltpu.roll(blk, 16, 1) * sa + pltpu.roll(blk, LANES - 16, 1) * sb
            o_ref[0, :, c0:c0 + LANES] = (r * scale).astype(o_ref.dtype)
    v_ref[0] = p[:, 2 * qk:].astype(v_ref.dtype)


def _latproj_call(x, shift, scale, g, w_all, b, rope_tabs, conv_w, conv_b, n_att, qk, q_scale, tm=512):
    bsz, s, d = x.shape
    n = w_all.shape[2]
    c = (n - n_att) // 3
    halo = SUBLANES
    nb = tm // halo
    last = s // halo - 1
    vec = pl.BlockSpec((1, 1, d), lambda bi, i: (bi, 0, 0))
    tab = pl.BlockSpec((tm, LANES), lambda bi, i: (i, 0))
    row = lambda w: pl.BlockSpec((1, tm, w), lambda bi, i: (bi, i, 0))
    cblk = pl.BlockSpec((1, c // LANES, tm, LANES), lambda bi, i: (bi, 0, i, 0))
    return pl.pallas_call(
        functools.partial(_latproj_kernel, n_att=n_att, q_scale=q_scale, seq_len=s),
        grid=(bsz, s // tm),
        in_specs=[row(d),
                  pl.BlockSpec((1, halo, d), lambda bi, i: (bi, jnp.maximum(i * nb - 1, 0), 0)),
                  pl.BlockSpec((1, halo, d), lambda bi, i: (bi, jnp.minimum((i + 1) * nb, last), 0)),
                  vec, vec, _const_spec((1, d)), pl.BlockSpec(memory_space=pl.ANY), _const_spec((1, n)), tab, tab, tab,
                  _const_spec(conv_w.shape), _const_spec((1, 3 * c))],
        out_specs=[row(qk), row(qk), row(n_att - 2 * qk), cblk, cblk],
        out_shape=[jax.ShapeDtypeStruct((bsz, s, qk), BF16), jax.ShapeDtypeStruct((bsz, s, qk), BF16),
                   jax.ShapeDtypeStruct((bsz, s, n_att - 2 * qk), BF16),
                   jax.ShapeDtypeStruct((bsz, c // LANES, s, LANES), F32),
                   jax.ShapeDtypeStruct((bsz, c // LANES, s, LANES), F32)],
        scratch_shapes=[pltpu.VMEM((d, n), BF16)],
        compiler_params=_params(2),
        name="latproj",
    )(x, x, x, shift, scale, g.reshape(1, d), w_all, b.reshape(1, n), *rope_tabs, conv_w, conv_b.reshape(1, 3 * c))


def _rope_tables(seq_len):
    axis_dim = ATT_QK_DIM // 2
    n_freq = axis_dim // 2
    inv = (ROPE_BASE ** (-np.arange(n_freq, dtype=np.float32) / n_freq)).astype(np.float32)
    t = np.arange(seq_len)
    row, col = t // GRID_W, t % GRID_W
    jj = np.arange(128) % ATT_QK_DIM
    is_col = (jj // axis_dim) == 1
    second = ((jj % axis_dim) >= n_freq)[None, :]
    pos = np.where(is_col[None, :], col[:, None], row[:, None]).astype(np.float32)
    ang = (pos * inv[jj % n_freq][None, :]).astype(np.float64)
    cos, sin = np.cos(ang), np.sin(ang)
    as32 = lambda m: jnp.asarray(m, dtype=F32)
    return as32(cos), as32(np.where(second, sin, 0.0)), as32(np.where(second, 0.0, -sin))


def _attn_kernel(lam_ref, q_ref, kc_ref, vc_ref, kl_ref, vl_ref, g_ref, o_ref, *, tk, lam_init):
    q = q_ref[0]
    tq = q.shape[0]
    lane = lax.broadcasted_iota(jnp.int32, q.shape, 1)
    zero = jnp.zeros_like(q)
    q2 = jnp.concatenate([jnp.where(lane < ATT_QK_DIM, q, zero), jnp.where(lane >= ATT_QK_DIM, q, zero)], axis=0)

    def chunk(k, v, m, acc):
        s = lax.dot_general(q2, k, (((1,), (1,)), ((), ())), preferred_element_type=F32)
        rowmax = jnp.max(s, axis=1, keepdims=True)
        m_new = rowmax if m is None else jnp.maximum(m, rowmax)
        p = jnp.exp2(s - m_new).astype(BF16)
        pv = _dot(p, jnp.concatenate([v, jnp.ones_like(v)], axis=1))
        return m_new, (pv if acc is None else jnp.exp2(m - m_new) * acc + pv)

    m, acc = chunk(kc_ref[0], vc_ref[0], None, None)
    for j in range(kl_ref.shape[1] // tk):
        m, acc = chunk(kl_ref[0, j * tk:(j + 1) * tk, :], vl_ref[0, j * tk:(j + 1) * tk, :], m, acc)

    lamv = lam_ref[...]
    lam = (jnp.exp(jnp.sum(lamv[0:1] * lamv[1:2], axis=1, keepdims=True))
           - jnp.exp(jnp.sum(lamv[2:3] * lamv[3:4], axis=1, keepdims=True)) + lam_init)
    o_all = acc[:, :ATT_V_DIM] / acc[:, ATT_V_DIM:]
    o = o_all[:tq] - lam * o_all[tq:]
    o = _rms(o, SUBLN_EPS) * g_ref[...] * (1.0 - lam_init)
    o_ref[0] = o.astype(o_ref.dtype)


def _attn_call(q, kc, vc, kl, vl, lamv, g, lam_init, tq=1024, tk=256):
    bsz, l, width = q.shape
    heads = width // ATT_V_DIM
    lc = kc.shape[1]
    hd = ATT_V_DIM
    return pl.pallas_call(
        functools.partial(_attn_kernel, tk=tk, lam_init=lam_init),
        grid=(bsz, heads, l // tq),
        in_specs=[
            _const_spec(lamv.shape),
            pl.BlockSpec((1, tq, hd), lambda b, h, i: (b, i, h)),
            pl.BlockSpec((1, lc, hd), lambda b, h, i: (b, 0, h)),
            pl.BlockSpec((1, lc, hd), lambda b, h, i: (b, 0, h)),
            pl.BlockSpec((1, l, hd), lambda b, h, i: (b, 0, h)),
            pl.BlockSpec((1, l, hd), lambda b, h, i: (b, 0, h)),
            _const_spec((1, hd)),
        ],
        out_specs=pl.BlockSpec((1, tq, hd), lambda b, h, i: (b, i, h)),
        out_shape=jax.ShapeDtypeStruct((bsz, l, width), BF16),
        compiler_params=_params(3),
        name="diffattn",
    )(lamv, q, kc, vc, kl, vl, g.reshape(1, hd))


def _filter_kernel(z_ref, w1_ref, b1_ref, f1_ref, w2_ref, b2_ref, f2_ref, w3_ref, dl_ref, o_ref, *, seq_len):
    tm = z_ref.shape[0]
    h1 = jnp.sin(f1_ref[...] * (_dot3f(z_ref[...], w1_ref[...]) + b1_ref[...]))
    h2 = jnp.sin(f2_ref[...] * (_dot3f(h1, w2_ref[...]) + b2_ref[...]))
    h = _dot3f(h2, w3_ref[...])
    row = lax.broadcasted_iota(jnp.int32, h.shape, 0) + pl.program_id(0) * tm
    col = lax.broadcasted_iota(jnp.int32, h.shape, 1)
    t = row.astype(F32) * (1.0 / (seq_len - 1))
    h = h * jnp.exp(-t * dl_ref[...])
    half = h.shape[1] // 2
    o_ref[...] = jnp.where((row == 0) & (col >= half), 0.0, h)


def _filter_call(z, w1, b1, f1, w2, b2, f2, w3, absdelta2, tm=512):
    seq_len, kz = z.shape
    hid = w2.shape[0]
    n = w3.shape[1]
    return pl.pallas_call(
        functools.partial(_filter_kernel, seq_len=seq_len),
        grid=(seq_len // tm,),
        in_specs=[
            pl.BlockSpec((tm, kz), lambda i: (i, 0)),
            _const_spec((kz, hid)), _const_spec((1, hid)), _const_spec((1, hid)),
            _const_spec((hid, hid)), _const_spec((1, hid)), _const_spec((1, hid)),
            _const_spec((hid, n)), _const_spec((1, n)),
        ],
        out_specs=pl.BlockSpec((tm, n), lambda i: (i, 0)),
        out_shape=jax.ShapeDtypeStruct((seq_len, n), F32),
        compiler_params=_params(1),
        name="hyfilter",
    )(z, w1, b1.reshape(1, hid), f1.reshape(1, hid), w2, b2.reshape(1, hid), f2.reshape(1, hid), w3, absdelta2)


def _gather_minor(refs, n_rows):
    cols = []
    for j in range(SUBLANES):
        parts = [r[pl.ds(j, n_rows, stride=SUBLANES), :] for r in refs]
        cols.append(parts[0] if len(parts) == 1 else jnp.concatenate(parts, axis=0))
    return jnp.concatenate(cols, axis=1)


def _slab_gather(src_ref, jb, slab_ref):
    rows = src_ref.shape[0]
    slab_ref[...] = src_ref[:, jb].reshape(rows * SUBLANES, LANES)
    return _gather_minor([slab_ref], rows)


def _slab_scatter(val, slab_ref, dst_ref, jb):
    rows = dst_ref.shape[0]
    for j in range(SUBLANES):
        slab_ref[pl.ds(j, rows, stride=SUBLANES), :] = val[:, j * LANES:(j + 1) * LANES]
    dst_ref[:, jb] = slab_ref[...].reshape(rows, SUBLANES, LANES)


def _filter_fft_kernel(fr_ref, g_ref, twr_ref, twi_ref, hf_ref, hb_ref, kr_ref, ki_ref,
                       slab_in, slab_out, a_ref):
    n1 = a_ref.shape[1]
    n2 = FFT_N2
    w = SUBLANES * LANES
    for jb in range(n2 // SUBLANES):
        z = jnp.concatenate([_slab_gather(hf_ref, jb, slab_in.at[0]), _slab_gather(hb_ref, jb, slab_in.at[1])], axis=1)
        out = _dft_dot(fr_ref[...], z)
        for t in range(4):
            _slab_scatter(out[(t % 2) * n1:(t % 2 + 1) * n1, (t // 2) * w:(t // 2 + 1) * w], slab_out.at[t], a_ref.at[t], jb)

    def body(kb, _):
        k0 = pl.multiple_of(kb * SUBLANES, SUBLANES)
        tw_rows = pl.ds(pl.multiple_of(kb * (SUBLANES * n2), SUBLANES * n2), SUBLANES * n2)
        twr = twr_ref[tw_rows, :].reshape(SUBLANES, n2, LANES)
        twi = twi_ref[tw_rows, :].reshape(SUBLANES, n2, LANES)
        a = [a_ref[t, pl.ds(k0, SUBLANES)].reshape(SUBLANES, n2, LANES) for t in range(4)]
        cols = []
        for t in range(2):
            ar, ai = a[2 * t], a[2 * t + 1]
            sr, si = ar * twr - ai * twi, ar * twi + ai * twr
            cols += [jnp.concatenate([sr[i], si[i]], axis=0) for i in range(SUBLANES)]
        x = _dft_dot(g_ref[...], jnp.concatenate(cols, axis=1))
        for i in range(SUBLANES):
            xf = x[:, i * LANES:(i + 1) * LANES]
            xb = x[:, (SUBLANES + i) * LANES:(SUBLANES + i + 1) * LANES]
            rows = pl.ds(pl.multiple_of((k0 + i) * n2, n2), n2)
            kr_ref[rows, :] = xf[:n2] + xb[:n2]
            ki_ref[rows, :] = xf[n2:] - xb[n2:]
        return 0

    lax.fori_loop(0, n1 // SUBLANES, body, 0)


def _filter_fft_call(dft, hcat):
    seq_len, c2 = hcat.shape
    c = c2 // 2
    n, n1, n1h = dft["n"], dft["n1"], dft["n1h"]
    nb = FFT_N2 // SUBLANES
    ncb = c // LANES
    fr, g = dft["f_real"].astype(BF16), dft["g_fwd"].astype(BF16)
    hv = hcat.reshape(n1h, nb, SUBLANES, c2)
    blk = lambda off: pl.BlockSpec((n1h, nb, SUBLANES, LANES), lambda i, off=off: (0, 0, 0, i + off))
    out = pl.BlockSpec((None, n, LANES), lambda i: (i, 0, 0))
    return pl.pallas_call(
        _filter_fft_kernel,
        grid=(ncb,),
        in_specs=[_const_spec(fr.shape), _const_spec(g.shape), _const_spec((n, LANES)), _const_spec((n, LANES)),
                  blk(0), blk(ncb)],
        out_specs=[out, out],
        out_shape=[jax.ShapeDtypeStruct((ncb, n, LANES), F32)] * 2,
        scratch_shapes=[pltpu.VMEM((2, n1h * SUBLANES, LANES), F32), pltpu.VMEM((4, n1 * SUBLANES, LANES), F32),
                        pltpu.VMEM((4, n1, nb, SUBLANES, LANES), F32)],
        compiler_params=_params(1),
        name="hyfilter_fft",
    )(fr, g, dft["twr"], dft["twi"], hv, hv)


def _hyena_conv_kernel(fc_ref, g_ref, gc_ref, e_ref, twr_ref, twi_ref, kr_ref, ki_ref, u_ref, x0_ref, bias_ref, o_ref,
                       slab_in, slab_a, a_ref, *, inv_n):
    n1 = a_ref.shape[1]
    n1h = u_ref.shape[1]
    n2 = FFT_N2
    nb = n2 // SUBLANES

    for jb in range(nb):
        z = jnp.concatenate([_slab_gather(u_ref.at[c], jb, slab_in.at[c]) for c in range(2)], axis=0)
        out = _dft_dot(fc_ref[...], z)
        for c in range(2):
            _slab_scatter(out[c * n1:(c + 1) * n1], slab_a.at[c], a_ref.at[c], jb)

    def body(kb, _):
        k0 = pl.multiple_of(kb * SUBLANES, SUBLANES)
        rows = pl.ds(pl.multiple_of(kb * (SUBLANES * n2), SUBLANES * n2), SUBLANES * n2)
        blocks = lambda ref: ref[rows, :].reshape(SUBLANES, n2, LANES)
        twr, twi, kr, ki = blocks(twr_ref), blocks(twi_ref), blocks(kr_ref), blocks(ki_ref)
        ar = a_ref[0, pl.ds(k0, SUBLANES)].reshape(SUBLANES, n2, LANES)
        ai = a_ref[1, pl.ds(k0, SUBLANES)].reshape(SUBLANES, n2, LANES)
        sr, si = ar * twr - ai * twi, ar * twi + ai * twr
        lanes = lambda re, im: jnp.concatenate([jnp.concatenate([re[i], im[i]], axis=0) for i in range(SUBLANES)], axis=1)
        x = _dft_dot(g_ref[...], lanes(sr, si))
        unl = lambda v, lo: jnp.stack([v[lo:lo + n2, i * LANES:(i + 1) * LANES] for i in range(SUBLANES)], axis=0)
        xr, xi = unl(x, 0), unl(x, n2)
        t = _dft_dot(gc_ref[...], lanes(xr * kr - xi * ki, xr * ki + xi * kr))
        tr, ti = unl(t, 0), unl(t, n2)
        a_ref[0, pl.ds(k0, SUBLANES)] = (tr * twr + ti * twi).reshape(SUBLANES, nb, SUBLANES, LANES)
        a_ref[1, pl.ds(k0, SUBLANES)] = (ti * twr - tr * twi).reshape(SUBLANES, nb, SUBLANES, LANES)
        return 0

    lax.fori_loop(0, n1 // SUBLANES, body, 0)

    for jb in range(nb):
        t = jnp.concatenate([_slab_gather(a_ref.at[c], jb, slab_a.at[c]) for c in range(2)], axis=0)
        y = _dft_dot(e_ref[...], t) * inv_n
        for c in range(2):
            _slab_scatter(y[c * n1h:(c + 1) * n1h], slab_in.at[c], o_ref.at[c], jb)
    o_ref[...] = x0_ref[...] * (o_ref[...] + u_ref[...] * bias_ref[...])


def _dft_blocks(seq_len):
    n = 2 * seq_len
    n2 = FFT_N2
    n1 = n // n2
    n1h = n1 // 2
    k1 = np.arange(n1)[:, None].astype(np.float64)
    a = 2.0 * np.pi * k1 * np.arange(n1h)[None, :] / n1
    fr, fi = np.cos(a), -np.sin(a)
    f_cplx = np.block([[fr, -fi], [fi, fr]])
    f_real = np.concatenate([fr, fi], axis=0)
    e_cplx = np.block([[fr.T, fi.T], [-fi.T, fr.T]])
    b = 2.0 * np.pi * np.arange(n2)[:, None] * np.arange(n2)[None, :] / n2
    gr, gi = np.cos(b), -np.sin(b)
    g_fwd = np.block([[gr, -gi], [gi, gr]])
    g_inv = np.block([[gr, gi], [-gi, gr]])
    kk = np.arange(n1)[:, None] * np.arange(n2)[None, :]
    ang = (2.0 * np.pi / n) * kk.reshape(n1 * n2, 1)
    twr = np.broadcast_to(np.cos(ang), (n1 * n2, 128))
    twi = np.broadcast_to(-np.sin(ang), (n1 * n2, 128))
    as32 = lambda m: jnp.asarray(np.ascontiguousarray(m), dtype=F32)
    return dict(n=n, n1=n1, n1h=n1h, f_cplx=as32(f_cplx), f_real=as32(f_real), e_cplx=as32(e_cplx),
                g_fwd=as32(g_fwd), g_inv=as32(g_inv), twr=as32(twr), twi=as32(twi))


def _hyena_call(u, x0, filt, hy_bias):
    bsz, ncb, seq_len, _ = u.shape
    c = ncb * LANES
    n2 = FFT_N2
    dft = _dft_blocks(seq_len)
    n, n1, n1h = dft["n"], dft["n1"], dft["n1h"]
    pairs = bsz // 2

    w1, b1, f1, w2, b2, f2, w3 = filt
    hid = w2.shape[0]
    bands = (HY_POS_EMB - 1) // 2
    t = np.linspace(0.0, 1.0, seq_len)[:, None]
    freqs = np.linspace(1e-4, bands - 1, bands)
    ang = (2.0 * math.pi / seq_len) * np.arange(seq_len)[:, None] * freqs[None, :]
    z = jnp.asarray(np.concatenate([t, np.cos(ang), -np.sin(ang), np.zeros((seq_len, hid - HY_POS_EMB))], axis=-1),
                    dtype=F32)
    w1p = jnp.concatenate([w1, jnp.zeros((hid - HY_POS_EMB, hid), F32)], axis=0)
    max_decay = math.log(HY_DECAY_TARGET) / HY_FAST_DECAY_PCT
    min_decay = math.log(HY_DECAY_TARGET) / HY_SLOW_DECAY_PCT
    absdelta = np.abs(np.linspace(min_decay, max_decay, c))
    absdelta2 = jnp.asarray(np.concatenate([absdelta, absdelta])[None, :], dtype=F32)
    hcat = _filter_call(z, w1p, b1, f1, w2, b2, f2, w3, absdelta2)
    nb = n2 // SUBLANES
    kr, ki = _filter_fft_call(dft, hcat)

    view = lambda a: a.reshape(pairs, 2, ncb, n1h, nb, SUBLANES, LANES)
    pair_spec = pl.BlockSpec((None, 2, None, n1h, nb, SUBLANES, LANES), lambda cb, p: (p, 0, cb, 0, 0, 0, 0))
    k_spec = pl.BlockSpec((None, n, LANES), lambda cb, p: (cb, 0, 0), pipeline_mode=pl.Buffered(1))
    fc, g, gc, e = (dft[k].astype(BF16) for k in ("f_cplx", "g_fwd", "g_inv", "e_cplx"))
    hy = pl.pallas_call(
        functools.partial(_hyena_conv_kernel, inv_n=1.0 / n),
        grid=(ncb, pairs),
        in_specs=[_const_spec(fc.shape), _const_spec(g.shape), _const_spec(gc.shape), _const_spec(e.shape),
                  _const_spec((n, LANES)), _const_spec((n, LANES)), k_spec, k_spec, pair_spec, pair_spec,
                  pl.BlockSpec((1, LANES), lambda cb, p: (0, cb))],
        out_specs=pair_spec,
        out_shape=jax.ShapeDtypeStruct((pairs, 2, ncb, n1h, nb, SUBLANES, LANES), F32),
        scratch_shapes=[pltpu.VMEM((2, n1h * SUBLANES, LANES), F32), pltpu.VMEM((2, n1 * SUBLANES, LANES), F32),
                        pltpu.VMEM((2, n1, nb, SUBLANES, LANES), F32)],
        compiler_params=_params(2),
        name="hyconv",
    )(fc, g, gc, e, dft["twr"], dft["twi"], kr, ki, view(u), view(x0), hy_bias.reshape(1, c))
    return hy.reshape(bsz, ncb, seq_len, LANES)


def _stage_weights_bf16(pairs):
    for src, dst in pairs:
        n_rows, n_cols = src.shape
        rows = 1 << ((WEIGHT_STAGE_BYTES // (4 * n_cols)).bit_length() - 1)
        n_chunks = n_rows // rows
        assert rows % (2 * SUBLANES) == 0 and n_chunks * rows == n_rows

        def run(stage, sem, src=src, dst=dst, rows=rows, n_chunks=n_chunks):
            copy = lambda k: pltpu.make_async_copy(src.at[pl.ds(k * rows, rows), :], stage.at[k % 2], sem.at[k % 2])
            copy(0).start()
            for k in range(n_chunks):
                if k + 1 < n_chunks:
                    copy(k + 1).start()
                copy(k).wait()
                dst[pl.ds(k * rows, rows), :] = stage[k % 2].astype(BF16)

        pl.run_scoped(run, pltpu.VMEM((2, rows, n_cols), F32), pltpu.SemaphoreType.DMA((2,)))


def _mlp_tail(h, g2, shift, scale, gate, w1_ref, w2_ref, fc):
    a = (_rms(h, NORM_EPS) * g2) * (1.0 + scale) + shift
    a = a.astype(BF16)
    acc = None
    for c0 in range(0, w1_ref.shape[1], fc):
        hid = jnp.maximum(_dot(a, w1_ref[:, c0:c0 + fc]), 0.0)
        part = _dot((hid * hid).astype(BF16), w2_ref[c0:c0 + fc, :])
        acc = part if acc is None else acc + part
    return h + gate * acc


def _mixmlp_kernel(x_ref, att_ref, hy_ref, mod_ref, wo_hbm, bo_ref, g2_ref, w1_hbm, w2_hbm, o_ref,
                   wo_s, w1_s, w2_s, *, fc, layer):
    @pl.when((pl.program_id(0) == 0) & (pl.program_id(1) == 0))
    def _():
        _stage_weights_bf16([(wo_hbm.at[layer // 2], wo_s), (w1_hbm.at[layer], w1_s), (w2_hbm.at[layer], w2_s)])

    d = x_ref.shape[2]
    wa = att_ref.shape[2]
    mod = mod_ref[0]
    hy = jnp.concatenate([hy_ref[0, cb] for cb in range(hy_ref.shape[1])], axis=1).astype(BF16)
    y = _dot(att_ref[0], wo_s[:wa, :]) + _dot(hy, wo_s[wa:, :]) + bo_ref[...]
    h = x_ref[0] + mod[:, 2 * d:3 * d] * y
    o_ref[0] = _mlp_tail(h, g2_ref[...], mod[:, 3 * d:4 * d], mod[:, 4 * d:5 * d], mod[:, 5 * d:6 * d],
                         w1_s, w2_s, fc)


def _mixmlp_call(x, att, hy, mod, w_out, b_out, g2, w1, w2, layer, tm=512, fc=1024):
    bsz, l, d = x.shape
    wa = att.shape[2]
    dff = w1.shape[2]
    row = lambda w: pl.BlockSpec((1, tm, w), lambda b, i: (b, i, 0))
    hbm = pl.BlockSpec(memory_space=pl.ANY)
    return pl.pallas_call(
        functools.partial(_mixmlp_kernel, fc=fc, layer=layer),
        grid=(bsz, l // tm),
        in_specs=[row(d), row(wa), pl.BlockSpec((1, hy.shape[1], tm, LANES), lambda b, i: (b, 0, i, 0)),
                  pl.BlockSpec((1, 1, N_MOD * d), lambda b, i: (b, 0, 0)),
                  hbm, _const_spec((1, d)), _const_spec((1, d)), hbm, hbm],
        out_specs=row(d),
        out_shape=jax.ShapeDtypeStruct((bsz, l, d), F32),
        scratch_shapes=[pltpu.VMEM((d, d), BF16), pltpu.VMEM((d, dff), BF16), pltpu.VMEM((dff, d), BF16)],
        compiler_params=_params(2),
        name="mixmlp",
    )(x, att, hy, mod, w_out, b_out.reshape(1, d), g2.reshape(1, d), w1, w2)


def _poolmlp_kernel(h_ref, hp_ref, hn_ref, mod_ref, g1_ref, pw_ref, ps_ref, g2_ref, w1_hbm, w2_hbm, gf_ref, o_ref,
                    w1_s, w2_s, *, fc, seq_len, layer):
    @pl.when((pl.program_id(0) == 0) & (pl.program_id(1) == 0))
    def _():
        _stage_weights_bf16([(w1_hbm.at[layer], w1_s), (w2_hbm.at[layer], w2_s)])

    tm, d = h_ref.shape[1:]
    halo = POOL_HALO
    rows = tm + 2 * halo
    mod = mod_ref[0]
    h = h_ref[0]
    hx = jnp.concatenate([hp_ref[0], h, hn_ref[0]], axis=0)
    a = (_rms(hx, NORM_EPS) * g1_ref[...]) * (1.0 + mod[:, d:2 * d]) + mod[:, 0:d]
    t = lax.broadcasted_iota(jnp.int32, (rows, 1), 0) + (pl.program_id(1) * tm - halo)
    a = jnp.where((t >= 0) & (t < seq_len), a, 0.0)
    tc = t[halo:halo + tm]
    gd = d // len(POOL_WINDOWS)
    ys = []
    for g, win in enumerate(POOL_WINDOWS):
        ag = a[:, g * gd:(g + 1) * gd]
        s = ag + pltpu.roll(ag, 1, 0)
        w = 2
        while w < win:
            s = pltpu.roll(s, w // 2, 0) + pltpu.roll(s, rows - w // 2, 0)
            w *= 2
        cnt = (jnp.minimum(tc + (win - win // 2), seq_len) - jnp.maximum(tc - win // 2, 0)).astype(F32)
        dlt = s[halo:halo + tm] / cnt - ag[halo:halo + tm]
        ys.append(_dot(dlt.astype(BF16), pw_ref[g]))
    y = jnp.concatenate(ys, axis=1) * ps_ref[...]
    h1 = h + mod[:, 2 * d:3 * d] * y
    h2 = _mlp_tail(h1, g2_ref[...], mod[:, 3 * d:4 * d], mod[:, 4 * d:5 * d], mod[:, 5 * d:6 * d], w1_s, w2_s, fc)
    o_ref[0] = _rms(h2, NORM_EPS) * gf_ref[...]


def _poolmlp_call(h, mod, g1, pool_w, pool_scale, g2, w1, w2, gf, layer, tm=512, fc=1024):
    bsz, l, d = h.shape
    dff = w1.shape[2]
    halo = POOL_HALO
    nb = tm // halo
    last = l // halo - 1
    row = pl.BlockSpec((1, tm, d), lambda b, i: (b, i, 0))
    hbm = pl.BlockSpec(memory_space=pl.ANY)
    return pl.pallas_call(
        functools.partial(_poolmlp_kernel, fc=fc, seq_len=l, layer=layer),
        grid=(bsz, l // tm),
        in_specs=[row,
                  pl.BlockSpec((1, halo, d), lambda b, i: (b, jnp.maximum(i * nb - 1, 0), 0)),
                  pl.BlockSpec((1, halo, d), lambda b, i: (b, jnp.minimum((i + 1) * nb, last), 0)),
                  pl.BlockSpec((1, 1, N_MOD * d), lambda b, i: (b, 0, 0)),
                  _const_spec((1, d)), _const_spec(pool_w.shape), _const_spec((1, d)), _const_spec((1, d)),
                  hbm, hbm, _const_spec((1, d))],
        out_specs=row,
        out_shape=jax.ShapeDtypeStruct((bsz, l, d), F32),
        scratch_shapes=[pltpu.VMEM((d, dff), BF16), pltpu.VMEM((dff, d), BF16)],
        compiler_params=_params(2),
        name="poolmlp",
    )(h, h, h, mod, g1.reshape(1, d), pool_w, pool_scale.reshape(1, d), g2.reshape(1, d), w1, w2, gf.reshape(1, d))


def kernel(x, c, ctx, c_ctx, ada_w, ada_b, norm1_g, norm2_g, mix_w_in, mix_b_in, mix_w_out, mix_b_out, lam_q1, lam_k1, lam_q2, lam_k2, subln_g, hy_conv_w, hy_conv_b, hy_pos_w1, hy_pos_b1, hy_freq1, hy_pos_w2, hy_pos_b2, hy_freq2, hy_pos_w3, hy_bias, pool_w, pool_scale, mlp_w1, mlp_w2, final_g):
    bsz, seq_len, d = x.shape
    depth = ada_w.shape[0]
    assert depth == 2 and bsz % 2 == 0 and bsz < MOD_ROWS and seq_len % GRID_W == 0
    att_w = ATT_HEADS * ATT_V_DIM
    q_cols = k_cols = ATT_HEADS * 2 * ATT_QK_DIM
    kv_start, hy_start = q_cols, q_cols + k_cols + att_w
    in_cols = mix_w_in.shape[2]

    cv = jnp.concatenate([c, c_ctx[None, :], jnp.zeros((MOD_ROWS - bsz - 1, d), F32)], axis=0)
    mod = _mod_call(cv, ada_w, ada_b)
    mod_l = [mod[i, :bsz].reshape(bsz, 1, N_MOD * d) for i in range(depth)]

    lam_init = 0.8 - 0.6 * math.exp(-0.3 * 0)
    assert k_cols == q_cols and in_cols - hy_start == 3 * (d - att_w)
    q, k, v, u, x0 = _latproj_call(
        x, mod_l[0][:, :, 0:d], mod_l[0][:, :, d:2 * d], norm1_g[0], mix_w_in, mix_b_in[0], _rope_tables(seq_len),
        hy_conv_w[0], hy_conv_b[0], n_att=hy_start, qk=q_cols, q_scale=ATT_QK_DIM ** -0.5 * math.log2(math.e))
    mod_c = mod[0, bsz:bsz + 1].reshape(1, 1, N_MOD * d)
    kc, vc = _ctxproj_call(ctx, mod_c[:, :, 0:d], mod_c[:, :, d:2 * d], norm1_g[0],
                           mix_w_in[0, :, kv_start:hy_start].astype(BF16), mix_b_in[0, kv_start:hy_start], k_cols)
    lamv = jnp.stack([lam_q1[0], lam_k1[0], lam_q2[0], lam_k2[0]], axis=0)
    att = _attn_call(q, kc, vc, k, v, lamv, subln_g[0], lam_init)
    filt = (hy_pos_w1[0], hy_pos_b1[0], hy_freq1[0], hy_pos_w2[0], hy_pos_b2[0], hy_freq2[0], hy_pos_w3[0])
    hy = _hyena_call(u, x0, filt, hy_bias[0])
    h = _mixmlp_call(x, att, hy, mod_l[0], mix_w_out, mix_b_out[0], norm2_g[0], mlp_w1, mlp_w2, layer=0)

    return _poolmlp_call(h, mod_l[1], norm1_g[1], pool_w[0].astype(BF16), pool_scale[0], norm2_g[1],
                         mlp_w1, mlp_w2, final_g, layer=1)
```

```python
import functools
import math

import numpy as np
import jax
import jax.numpy as jnp
from jax import lax
from jax.experimental import pallas as pl
from jax.experimental.pallas import tpu as pltpu

F32 = jnp.float32
BF16 = jnp.bfloat16

GRID_W = 64
N_MOD = 6
ATT_HEADS = 4
ATT_V_DIM = 128
ATT_QK_DIM = 64
ROPE_BASE = 10000.0
HY_POS_EMB = 33
HY_DECAY_TARGET = 1e-2
HY_FAST_DECAY_PCT = 0.3
HY_SLOW_DECAY_PCT = 1.5
POOL_WINDOWS = (2, 4, 8, 16)
NORM_EPS = 1e-6
SUBLN_EPS = 1e-5
SUBLANES = 8
LANES = 128
FFT_N2 = 64
POOL_HALO = 8
WEIGHT_STAGE_BYTES = 2 * 1024 * 1024
MOD_ROWS = 8

VMEM_LIMIT_BYTES = 56 * 1024 * 1024


def _params(n_grid_dims):
    return pltpu.CompilerParams(
        dimension_semantics=("arbitrary",) * n_grid_dims,
        vmem_limit_bytes=VMEM_LIMIT_BYTES,
    )


def _const_spec(shape):
    nd = len(shape)
    return pl.BlockSpec(shape, lambda *_: (0,) * nd, pipeline_mode=pl.Buffered(1))


def _split_bf16(a):
    hi = a.astype(BF16)
    lo = (a - hi.astype(F32)).astype(BF16)
    return hi, lo


def _dot(a, b):
    return jnp.dot(a, b, preferred_element_type=F32)


def _dot3(a_hi, a_lo, b_hi, b_lo):
    return _dot(a_hi, b_hi) + (_dot(a_lo, b_hi) + _dot(a_hi, b_lo))


def _dot3f(a, b):
    a_hi, a_lo = _split_bf16(a)
    b_hi, b_lo = _split_bf16(b)
    return _dot3(a_hi, a_lo, b_hi, b_lo)


def _dft_dot(a_bf16, b):
    return _dot(a_bf16, b.astype(BF16))


def _rms(x, eps):
    return x * lax.rsqrt(jnp.mean(x * x, axis=-1, keepdims=True) + eps)


def _mod_kernel(cv_ref, w_ref, b_ref, o_ref):
    cv = cv_ref[...]
    s = cv / (1.0 + jnp.exp(-cv))
    o_ref[0] = _dot3f(s, w_ref[0]) + b_ref[0]


def _mod_call(cv, ada_w, ada_b, tn=1536):
    depth, d, n = ada_w.shape
    return pl.pallas_call(
        _mod_kernel,
        grid=(depth, n // tn),
        in_specs=[
            pl.BlockSpec((MOD_ROWS, d), lambda i, j: (0, 0)),
            pl.BlockSpec((1, d, tn), lambda i, j: (i, 0, j)),
            pl.BlockSpec((1, 1, tn), lambda i, j: (i, 0, j)),
        ],
        out_specs=pl.BlockSpec((1, MOD_ROWS, tn), lambda i, j: (i, 0, j)),
        out_shape=jax.ShapeDtypeStruct((depth, MOD_ROWS, n), F32),
        compiler_params=_params(2),
        name="mod",
    )(cv, ada_w, ada_b.reshape(depth, 1, n))


def _ctxproj_kernel(x_ref, sh_ref, sc_ref, g_ref, w_ref, b_ref, k_ref, v_ref):
    a = _rms(x_ref[0], NORM_EPS) * g_ref[...]
    a = a * (1.0 + sc_ref[0]) + sh_ref[0]
    p = _dot(a.astype(BF16), w_ref[...]) + b_ref[...]
    nk = k_ref.shape[2]
    k_ref[0] = p[:, :nk].astype(k_ref.dtype)
    v_ref[0] = p[:, nk:].astype(v_ref.dtype)


def _ctxproj_call(x, shift, scale, g, w_bf16, b, nk):
    bsz, s, d = x.shape
    n = w_bf16.shape[1]
    vec = pl.BlockSpec((1, 1, d), lambda bi: (0, 0, 0))
    return pl.pallas_call(
        _ctxproj_kernel,
        grid=(bsz,),
        in_specs=[pl.BlockSpec((1, s, d), lambda bi: (bi, 0, 0)), vec, vec,
                  _const_spec((1, d)), _const_spec((d, n)), _const_spec((1, n))],
        out_specs=[pl.BlockSpec((1, s, nk), lambda bi: (bi, 0, 0)), pl.BlockSpec((1, s, n - nk), lambda bi: (bi, 0, 0))],
        out_shape=[jax.ShapeDtypeStruct((bsz, s, nk), BF16), jax.ShapeDtypeStruct((bsz, s, n - nk), BF16)],
        compiler_params=_params(1),
        name="ctxproj",
    )(x, shift, scale, g.reshape(1, d), w_bf16, b.reshape(1, n))


def _latproj_kernel(x_ref, xp_ref, xn_ref, sh_ref, sc_ref, g_ref, w_hbm, b_ref, cos_ref, sa_ref, sb_ref,
                    cw_ref, cb_ref, q_ref, k_ref, v_ref, u_ref, x0_ref, w_ref, *, n_att, q_scale, seq_len):
    @pl.when((pl.program_id(0) == 0) & (pl.program_id(1) == 0))
    def _():
        _stage_weights_bf16([(w_hbm.at[0], w_ref)])

    tm = x_ref.shape[1]
    halo = xp_ref.shape[1]
    rows = tm + 2 * halo
    xx = jnp.concatenate([xp_ref[0], x_ref[0], xn_ref[0]], axis=0)
    a = _rms(xx, NORM_EPS) * g_ref[...]
    a = a * (1.0 + sc_ref[0]) + sh_ref[0]

    ph = _dot(a.astype(BF16), w_ref[:, n_att:]) + b_ref[:, n_att:]
    t = lax.broadcasted_iota(jnp.int32, (rows, 1), 0) + (pl.program_id(1) * tm - halo)
    ph = jnp.where((t >= 0) & (t < seq_len), ph, 0.0)
    ncb = u_ref.shape[1]
    c = ncb * LANES

    def conv(s):
        blk = ph[:, s * c:(s + 1) * c]
        prev = pltpu.roll(blk, 1, 0)[halo:halo + tm]
        nxt = pltpu.roll(blk, rows - 1, 0)[halo:halo + tm]
        cw = cw_ref[:, s * c:(s + 1) * c]
        return prev * cw[0:1] + blk[halo:halo + tm] * cw[1:2] + nxt * cw[2:3] + cb_ref[:, s * c:(s + 1) * c]

    x0 = conv(0)
    u = conv(2) * conv(1)
    for cb in range(ncb):
        x0_ref[0, cb] = x0[:, cb * LANES:(cb + 1) * LANES]
        u_ref[0, cb] = u[:, cb * LANES:(cb + 1) * LANES]

    p = _dot(a[halo:halo + tm].astype(BF16), w_ref[:, :n_att]) + b_ref[:, :n_att]
    cos, sa, sb = cos_ref[...], sa_ref[...], sb_ref[...]
    qk = q_ref.shape[2]
    for o_ref, c_lo, scale in ((q_ref, 0, q_scale), (k_ref, qk, 1.0)):
        for c0 in range(0, qk, LANES):
            blk = p[:, c_lo + c0:c_lo + c0 + LANES]
            r = blk * cos + pltpu.roll(blk, 16, 1) * sa + pltpu.roll(blk, LANES - 16, 1) * sb
            o_ref[0, :, c0:c0 + LANES] = (r * scale).astype(o_ref.dtype)
    v_ref[0] = p[:, 2 * qk:].astype(v_ref.dtype)


def _latproj_call(x, shift, scale, g, w_all, b, rope_tabs, conv_w, conv_b, n_att, qk, q_scale, tm=512):
    bsz, s, d = x.shape
    n = w_all.shape[2]
    c = (n - n_att) // 3
    halo = SUBLANES
    nb = tm // halo
    last = s // halo - 1
    vec = pl.BlockSpec((1, 1, d), lambda bi, i: (bi, 0, 0))
    tab = pl.BlockSpec((tm, LANES), lambda bi, i: (i, 0))
    row = lambda w: pl.BlockSpec((1, tm, w), lambda bi, i: (bi, i, 0))
    cblk = pl.BlockSpec((1, c // LANES, tm, LANES), lambda bi, i: (bi, 0, i, 0))
    return pl.pallas_call(
        functools.partial(_latproj_kernel, n_att=n_att, q_scale=q_scale, seq_len=s),
        grid=(bsz, s // tm),
        in_specs=[row(d),
                  pl.BlockSpec((1, halo, d), lambda bi, i: (bi, jnp.maximum(i * nb - 1, 0), 0)),
                  pl.BlockSpec((1, halo, d), lambda bi, i: (bi, jnp.minimum((i + 1) * nb, last), 0)),
                  vec, vec, _const_spec((1, d)), pl.BlockSpec(memory_space=pl.ANY), _const_spec((1, n)), tab, tab, tab,
                  _const_spec(conv_w.shape), _const_spec((1, 3 * c))],
        out_specs=[row(qk), row(qk), row(n_att - 2 * qk), cblk, cblk],
        out_shape=[jax.ShapeDtypeStruct((bsz, s, qk), BF16), jax.ShapeDtypeStruct((bsz, s, qk), BF16),
                   jax.ShapeDtypeStruct((bsz, s, n_att - 2 * qk), BF16),
                   jax.ShapeDtypeStruct((bsz, c // LANES, s, LANES), F32),
                   jax.ShapeDtypeStruct((bsz, c // LANES, s, LANES), F32)],
        scratch_shapes=[pltpu.VMEM((d, n), BF16)],
        compiler_params=_params(2),
        name="latproj",
    )(x, x, x, shift, scale, g.reshape(1, d), w_all, b.reshape(1, n), *rope_tabs, conv_w, conv_b.reshape(1, 3 * c))


def _rope_tables(seq_len):
    axis_dim = ATT_QK_DIM // 2
    n_freq = axis_dim // 2
    inv = (ROPE_BASE ** (-np.arange(n_freq, dtype=np.float32) / n_freq)).astype(np.float32)
    t = np.arange(seq_len)
    row, col = t // GRID_W, t % GRID_W
    jj = np.arange(128) % ATT_QK_DIM
    is_col = (jj // axis_dim) == 1
    second = ((jj % axis_dim) >= n_freq)[None, :]
    pos = np.where(is_col[None, :], col[:, None], row[:, None]).astype(np.float32)
    ang = (pos * inv[jj % n_freq][None, :]).astype(np.float64)
    cos, sin = np.cos(ang), np.sin(ang)
    as32 = lambda m: jnp.asarray(m, dtype=F32)
    return as32(cos), as32(np.where(second, sin, 0.0)), as32(np.where(second, 0.0, -sin))


def _attn_kernel(lam_ref, q_ref, kc_ref, vc_ref, kl_ref, vl_ref, g_ref, o_ref, *, tk, lam_init):
    q = q_ref[0]
    tq = q.shape[0]
    lane = lax.broadcasted_iota(jnp.int32, q.shape, 1)
    zero = jnp.zeros_like(q)
    q2 = jnp.concatenate([jnp.where(lane < ATT_QK_DIM, q, zero), jnp.where(lane >= ATT_QK_DIM, q, zero)], axis=0)

    def chunk(k, v, m, acc):
        s = lax.dot_general(q2, k, (((1,), (1,)), ((), ())), preferred_element_type=F32)
        rowmax = jnp.max(s, axis=1, keepdims=True)
        m_new = rowmax if m is None else jnp.maximum(m, rowmax)
        p = jnp.exp2(s - m_new).astype(BF16)
        pv = _dot(p, jnp.concatenate([v, jnp.ones_like(v)], axis=1))
        return m_new, (pv if acc is None else jnp.exp2(m - m_new) * acc + pv)

    m, acc = chunk(kc_ref[0], vc_ref[0], None, None)
    for j in range(kl_ref.shape[1] // tk):
        m, acc = chunk(kl_ref[0, j * tk:(j + 1) * tk, :], vl_ref[0, j * tk:(j + 1) * tk, :], m, acc)

    lamv = lam_ref[...]
    lam = (jnp.exp(jnp.sum(lamv[0:1] * lamv[1:2], axis=1, keepdims=True))
           - jnp.exp(jnp.sum(lamv[2:3] * lamv[3:4], axis=1, keepdims=True)) + lam_init)
    o_all = acc[:, :ATT_V_DIM] / acc[:, ATT_V_DIM:]
    o = o_all[:tq] - lam * o_all[tq:]
    o = _rms(o, SUBLN_EPS) * g_ref[...] * (1.0 - lam_init)
    o_ref[0] = o.astype(o_ref.dtype)


def _attn_call(q, kc, vc, kl, vl, lamv, g, lam_init, tq=1024, tk=256):
    bsz, l, width = q.shape
    heads = width // ATT_V_DIM
    lc = kc.shape[1]
    hd = ATT_V_DIM
    return pl.pallas_call(
        functools.partial(_attn_kernel, tk=tk, lam_init=lam_init),
        grid=(bsz, heads, l // tq),
        in_specs=[
            _const_spec(lamv.shape),
            pl.BlockSpec((1, tq, hd), lambda b, h, i: (b, i, h)),
            pl.BlockSpec((1, lc, hd), lambda b, h, i: (b, 0, h)),
            pl.BlockSpec((1, lc, hd), lambda b, h, i: (b, 0, h)),
            pl.BlockSpec((1, l, hd), lambda b, h, i: (b, 0, h)),
            pl.BlockSpec((1, l, hd), lambda b, h, i: (b, 0, h)),
            _const_spec((1, hd)),
        ],
        out_specs=pl.BlockSpec((1, tq, hd), lambda b, h, i: (b, i, h)),
        out_shape=jax.ShapeDtypeStruct((bsz, l, width), BF16),
        compiler_params=_params(3),
        name="diffattn",
    )(lamv, q, kc, vc, kl, vl, g.reshape(1, hd))


def _filter_kernel(z_ref, w1_ref, b1_ref, f1_ref, w2_ref, b2_ref, f2_ref, w3_ref, dl_ref, o_ref, *, seq_len):
    tm = z_ref.shape[0]
    h1 = jnp.sin(f1_ref[...] * (_dot3f(z_ref[...], w1_ref[...]) + b1_ref[...]))
    h2 = jnp.sin(f2_ref[...] * (_dot3f(h1, w2_ref[...]) + b2_ref[...]))
    h = _dot3f(h2, w3_ref[...])
    row = lax.broadcasted_iota(jnp.int32, h.shape, 0) + pl.program_id(0) * tm
    col = lax.broadcasted_iota(jnp.int32, h.shape, 1)
    t = row.astype(F32) * (1.0 / (seq_len - 1))
    h = h * jnp.exp(-t * dl_ref[...])
    half = h.shape[1] // 2
    o_ref[...] = jnp.where((row == 0) & (col >= half), 0.0, h)


def _filter_call(z, w1, b1, f1, w2, b2, f2, w3, absdelta2, tm=512):
    seq_len, kz = z.shape
    hid = w2.shape[0]
    n = w3.shape[1]
    return pl.pallas_call(
        functools.partial(_filter_kernel, seq_len=seq_len),
        grid=(seq_len // tm,),
        in_specs=[
            pl.BlockSpec((tm, kz), lambda i: (i, 0)),
            _const_spec((kz, hid)), _const_spec((1, hid)), _const_spec((1, hid)),
            _const_spec((hid, hid)), _const_spec((1, hid)), _const_spec((1, hid)),
            _const_spec((hid, n)), _const_spec((1, n)),
        ],
        out_specs=pl.BlockSpec((tm, n), lambda i: (i, 0)),
        out_shape=jax.ShapeDtypeStruct((seq_len, n), F32),
        compiler_params=_params(1),
        name="hyfilter",
    )(z, w1, b1.reshape(1, hid), f1.reshape(1, hid), w2, b2.reshape(1, hid), f2.reshape(1, hid), w3, absdelta2)


def _gather_minor(refs, n_rows):
    cols = []
    for j in range(SUBLANES):
        parts = [r[pl.ds(j, n_rows, stride=SUBLANES), :] for r in refs]
        cols.append(parts[0] if len(parts) == 1 else jnp.concatenate(parts, axis=0))
    return jnp.concatenate(cols, axis=1)


def _slab_gather(src_ref, jb, slab_ref):
    rows = src_ref.shape[0]
    slab_ref[...] = src_ref[:, jb].reshape(rows * SUBLANES, LANES)
    return _gather_minor([slab_ref], rows)


def _slab_scatter(val, slab_ref, dst_ref, jb):
    rows = dst_ref.shape[0]
    for j in range(SUBLANES):
        slab_ref[pl.ds(j, rows, stride=SUBLANES), :] = val[:, j * LANES:(j + 1) * LANES]
    dst_ref[:, jb] = slab_ref[...].reshape(rows, SUBLANES, LANES)


def _spectrum_store(val, spec_ref, jb):
    rows = val.shape[0]
    for j in range(SUBLANES):
        spec_ref[jb, pl.ds(j, rows, stride=SUBLANES), :] = val[:, j * LANES:(j + 1) * LANES]


def _spectrum_blocks(spec_ref, k0):
    x = spec_ref[:, pl.ds(pl.multiple_of(k0 * SUBLANES, SUBLANES * SUBLANES), SUBLANES * SUBLANES), :]
    return jnp.stack([x[:, i * SUBLANES:(i + 1) * SUBLANES, :].reshape(FFT_N2, LANES) for i in range(SUBLANES)], axis=0)


def _spectrum_put(spec_ref, k0, val):
    nb = FFT_N2 // SUBLANES
    v = val.reshape(SUBLANES, nb, SUBLANES, LANES)
    x = jnp.stack([v[:, jb].reshape(SUBLANES * SUBLANES, LANES) for jb in range(nb)], axis=0)
    spec_ref[:, pl.ds(pl.multiple_of(k0 * SUBLANES, SUBLANES * SUBLANES), SUBLANES * SUBLANES), :] = x


def _filter_fft_kernel(fr_ref, g_ref, twr_ref, twi_ref, hf_ref, hb_ref, kr_ref, ki_ref, slab_in, a_ref):
    n1 = a_ref.shape[2] // SUBLANES
    n2 = FFT_N2
    w = SUBLANES * LANES
    for jb in range(n2 // SUBLANES):
        z = jnp.concatenate([_slab_gather(hf_ref, jb, slab_in.at[0]), _slab_gather(hb_ref, jb, slab_in.at[1])], axis=1)
        out = _dft_dot(fr_ref[...], z)
        for t in range(4):
            _spectrum_store(out[(t % 2) * n1:(t % 2 + 1) * n1, (t // 2) * w:(t // 2 + 1) * w], a_ref.at[t], jb)

    def body(kb, _):
        k0 = pl.multiple_of(kb * SUBLANES, SUBLANES)
        tw_rows = pl.ds(pl.multiple_of(kb * (SUBLANES * n2), SUBLANES * n2), SUBLANES * n2)
        twr = twr_ref[tw_rows, :].reshape(SUBLANES, n2, LANES)
        twi = twi_ref[tw_rows, :].reshape(SUBLANES, n2, LANES)
        a = [_spectrum_blocks(a_ref.at[t], k0) for t in range(4)]
        cols = []
        for t in range(2):
            ar, ai = a[2 * t], a[2 * t + 1]
            sr, si = ar * twr - ai * twi, ar * twi + ai * twr
            cols += [jnp.concatenate([sr[i], si[i]], axis=0) for i in range(SUBLANES)]
        x = _dft_dot(g_ref[...], jnp.concatenate(cols, axis=1))
        for i in range(SUBLANES):
            xf = x[:, i * LANES:(i + 1) * LANES]
            xb = x[:, (SUBLANES + i) * LANES:(SUBLANES + i + 1) * LANES]
            rows = pl.ds(pl.multiple_of((k0 + i) * n2, n2), n2)
            kr_ref[rows, :] = xf[:n2] + xb[:n2]
            ki_ref[rows, :] = xf[n2:] - xb[n2:]
        return 0

    lax.fori_loop(0, n1 // SUBLANES, body, 0, unroll=2)


def _filter_fft_call(dft, hcat):
    seq_len, c2 = hcat.shape
    c = c2 // 2
    n, n1, n1h = dft["n"], dft["n1"], dft["n1h"]
    nb = FFT_N2 // SUBLANES
    ncb = c // LANES
    fr, g = dft["f_real"].astype(BF16), dft["g_fwd"].astype(BF16)
    hv = hcat.reshape(n1h, nb, SUBLANES, c2)
    blk = lambda off: pl.BlockSpec((n1h, nb, SUBLANES, LANES), lambda i, off=off: (0, 0, 0, i + off))
    out = pl.BlockSpec((None, n, LANES), lambda i: (i, 0, 0))
    return pl.pallas_call(
        _filter_fft_kernel,
        grid=(ncb,),
        in_specs=[_const_spec(fr.shape), _const_spec(g.shape), _const_spec((n, LANES)), _const_spec((n, LANES)),
                  blk(0), blk(ncb)],
        out_specs=[out, out],
        out_shape=[jax.ShapeDtypeStruct((ncb, n, LANES), F32)] * 2,
        scratch_shapes=[pltpu.VMEM((2, n1h * SUBLANES, LANES), F32), pltpu.VMEM((4, nb, n1 * SUBLANES, LANES), F32)],
        compiler_params=_params(1),
        name="hyfilter_fft",
    )(fr, g, dft["twr"], dft["twi"], hv, hv)


def _hyena_conv_kernel(fc_ref, g_ref, gc_ref, e_ref, twr_ref, twi_ref, kr_ref, ki_ref, u_ref, x0_ref, bias_ref, o_ref,
                       slab_in, a_ref, *, inv_n):
    n1 = a_ref.shape[2] // SUBLANES
    n1h = u_ref.shape[1]
    n2 = FFT_N2
    nb = n2 // SUBLANES

    for jb in range(nb):
        z = jnp.concatenate([_slab_gather(u_ref.at[c], jb, slab_in.at[c]) for c in range(2)], axis=0)
        out = _dft_dot(fc_ref[...], z)
        for c in range(2):
            _spectrum_store(out[c * n1:(c + 1) * n1], a_ref.at[c], jb)

    def body(kb, _):
        k0 = pl.multiple_of(kb * SUBLANES, SUBLANES)
        rows = pl.ds(pl.multiple_of(kb * (SUBLANES * n2), SUBLANES * n2), SUBLANES * n2)
        blocks = lambda ref: ref[rows, :].reshape(SUBLANES, n2, LANES)
        twr, twi, kr, ki = blocks(twr_ref), blocks(twi_ref), blocks(kr_ref), blocks(ki_ref)
        ar, ai = _spectrum_blocks(a_ref.at[0], k0), _spectrum_blocks(a_ref.at[1], k0)
        sr, si = ar * twr - ai * twi, ar * twi + ai * twr
        lanes = lambda re, im: jnp.concatenate([jnp.concatenate([re[i], im[i]], axis=0) for i in range(SUBLANES)], axis=1)
        x = _dft_dot(g_ref[...], lanes(sr, si))
        unl = lambda v, lo: jnp.stack([v[lo:lo + n2, i * LANES:(i + 1) * LANES] for i in range(SUBLANES)], axis=0)
        xr, xi = unl(x, 0), unl(x, n2)
        t = _dft_dot(gc_ref[...], lanes(xr * kr - xi * ki, xr * ki + xi * kr))
        tr, ti = unl(t, 0), unl(t, n2)
        _spectrum_put(a_ref.at[0], k0, tr * twr + ti * twi)
        _spectrum_put(a_ref.at[1], k0, ti * twr - tr * twi)
        return 0

    lax.fori_loop(0, n1 // SUBLANES, body, 0, unroll=2)

    for jb in range(nb):
        t = _gather_minor([a_ref.at[0, jb], a_ref.at[1, jb]], n1)
        y = _dft_dot(e_ref[...], t) * inv_n
        for c in range(2):
            _slab_scatter(y[c * n1h:(c + 1) * n1h], slab_in.at[c], o_ref.at[c], jb)
    o_ref[...] = x0_ref[...] * (o_ref[...] + u_ref[...] * bias_ref[...])


def _dft_blocks(seq_len):
    n = 2 * seq_len
    n2 = FFT_N2
    n1 = n // n2
    n1h = n1 // 2
    k1 = np.arange(n1)[:, None].astype(np.float64)
    a = 2.0 * np.pi * k1 * np.arange(n1h)[None, :] / n1
    fr, fi = np.cos(a), -np.sin(a)
    f_cplx = np.block([[fr, -fi], [fi, fr]])
    f_real = np.concatenate([fr, fi], axis=0)
    e_cplx = np.block([[fr.T, fi.T], [-fi.T, fr.T]])
    b = 2.0 * np.pi * np.arange(n2)[:, None] * np.arange(n2)[None, :] / n2
    gr, gi = np.cos(b), -np.sin(b)
    g_fwd = np.block([[gr, -gi], [gi, gr]])
    g_inv = np.block([[gr, gi], [-gi, gr]])
    kk = np.arange(n1)[:, None] * np.arange(n2)[None, :]
    ang = (2.0 * np.pi / n) * kk.reshape(n1 * n2, 1)
    twr = np.broadcast_to(np.cos(ang), (n1 * n2, 128))
    twi = np.broadcast_to(-np.sin(ang), (n1 * n2, 128))
    as32 = lambda m: jnp.asarray(np.ascontiguousarray(m), dtype=F32)
    return dict(n=n, n1=n1, n1h=n1h, f_cplx=as32(f_cplx), f_real=as32(f_real), e_cplx=as32(e_cplx),
                g_fwd=as32(g_fwd), g_inv=as32(g_inv), twr=as32(twr), twi=as32(twi))


def _hyena_call(u, x0, filt, hy_bias):
    bsz, ncb, seq_len, _ = u.shape
    c = ncb * LANES
    n2 = FFT_N2
    dft = _dft_blocks(seq_len)
    n, n1, n1h = dft["n"], dft["n1"], dft["n1h"]
    pairs = bsz // 2

    w1, b1, f1, w2, b2, f2, w3 = filt
    hid = w2.shape[0]
    bands = (HY_POS_EMB - 1) // 2
    t = np.linspace(0.0, 1.0, seq_len)[:, None]
    freqs = np.linspace(1e-4, bands - 1, bands)
    ang = (2.0 * math.pi / seq_len) * np.arange(seq_len)[:, None] * freqs[None, :]
    z = jnp.asarray(np.concatenate([t, np.cos(ang), -np.sin(ang), np.zeros((seq_len, hid - HY_POS_EMB))], axis=-1),
                    dtype=F32)
    w1p = jnp.concatenate([w1, jnp.zeros((hid - HY_POS_EMB, hid), F32)], axis=0)
    max_decay = math.log(HY_DECAY_TARGET) / HY_FAST_DECAY_PCT
    min_decay = math.log(HY_DECAY_TARGET) / HY_SLOW_DECAY_PCT
    absdelta = np.abs(np.linspace(min_decay, max_decay, c))
    absdelta2 = jnp.asarray(np.concatenate([absdelta, absdelta])[None, :], dtype=F32)
    hcat = _filter_call(z, w1p, b1, f1, w2, b2, f2, w3, absdelta2)
    nb = n2 // SUBLANES
    kr, ki = _filter_fft_call(dft, hcat)

    view = lambda a: a.reshape(pairs, 2, ncb, n1h, nb, SUBLANES, LANES)
    pair_spec = pl.BlockSpec((None, 2, None, n1h, nb, SUBLANES, LANES), lambda cb, p: (p, 0, cb, 0, 0, 0, 0))
    k_spec = pl.BlockSpec((None, n, LANES), lambda cb, p: (cb, 0, 0), pipeline_mode=pl.Buffered(1))
    fc, g, gc, e = (dft[k].astype(BF16) for k in ("f_cplx", "g_fwd", "g_inv", "e_cplx"))
    hy = pl.pallas_call(
        functools.partial(_hyena_conv_kernel, inv_n=1.0 / n),
        grid=(ncb, pairs),
        in_specs=[_const_spec(fc.shape), _const_spec(g.shape), _const_spec(gc.shape), _const_spec(e.shape),
                  _const_spec((n, LANES)), _const_spec((n, LANES)), k_spec, k_spec, pair_spec, pair_spec,
                  pl.BlockSpec((1, LANES), lambda cb, p: (0, cb))],
        out_specs=pair_spec,
        out_shape=jax.ShapeDtypeStruct((pairs, 2, ncb, n1h, nb, SUBLANES, LANES), F32),
        scratch_shapes=[pltpu.VMEM((2, n1h * SUBLANES, LANES), F32), pltpu.VMEM((2, nb, n1 * SUBLANES, LANES), F32)],
        compiler_params=_params(2),
        name="hyconv",
    )(fc, g, gc, e, dft["twr"], dft["twi"], kr, ki, view(u), view(x0), hy_bias.reshape(1, c))
    return hy.reshape(bsz, ncb, seq_len, LANES)


def _stage_weights_bf16(pairs):
    for src, dst in pairs:
        n_rows, n_cols = src.shape
        rows = 1 << ((WEIGHT_STAGE_BYTES // (4 * n_cols)).bit_length() - 1)
        n_chunks = n_rows // rows
        assert rows % (2 * SUBLANES) == 0 and n_chunks * rows == n_rows

        def run(stage, sem, src=src, dst=dst, rows=rows, n_chunks=n_chunks):
            copy = lambda k: pltpu.make_async_copy(src.at[pl.ds(k * rows, rows), :], stage.at[k % 2], sem.at[k % 2])
            copy(0).start()
            for k in range(n_chunks):
                if k + 1 < n_chunks:
                    copy(k + 1).start()
                copy(k).wait()
                dst[pl.ds(k * rows, rows), :] = stage[k % 2].astype(BF16)

        pl.run_scoped(run, pltpu.VMEM((2, rows, n_cols), F32), pltpu.SemaphoreType.DMA((2,)))


def _mlp_tail(h, g2, shift, scale, gate, w1_ref, w2_ref, fc):
    a = (_rms(h, NORM_EPS) * g2) * (1.0 + scale) + shift
    a = a.astype(BF16)
    acc = None
    for c0 in range(0, w1_ref.shape[1], fc):
        hid = jnp.maximum(_dot(a, w1_ref[:, c0:c0 + fc]), 0.0)
        part = _dot((hid * hid).astype(BF16), w2_ref[c0:c0 + fc, :])
        acc = part if acc is None else acc + part
    return h + gate * acc


def _mixmlp_kernel(x_ref, att_ref, hy_ref, mod_ref, wo_hbm, bo_ref, g2_ref, w1_hbm, w2_hbm, o_ref,
                   wo_s, w1_s, w2_s, *, fc, layer):
    @pl.when((pl.program_id(0) == 0) & (pl.program_id(1) == 0))
    def _():
        _stage_weights_bf16([(wo_hbm.at[layer // 2], wo_s), (w1_hbm.at[layer], w1_s), (w2_hbm.at[layer], w2_s)])

    d = x_ref.shape[2]
    wa = att_ref.shape[2]
    mod = mod_ref[0]
    hy = jnp.concatenate([hy_ref[0, cb] for cb in range(hy_ref.shape[1])], axis=1).astype(BF16)
    y = _dot(att_ref[0], wo_s[:wa, :]) + _dot(hy, wo_s[wa:, :]) + bo_ref[...]
    h = x_ref[0] + mod[:, 2 * d:3 * d] * y
    o_ref[0] = _mlp_tail(h, g2_ref[...], mod[:, 3 * d:4 * d], mod[:, 4 * d:5 * d], mod[:, 5 * d:6 * d],
                         w1_s, w2_s, fc)


def _mixmlp_call(x, att, hy, mod, w_out, b_out, g2, w1, w2, layer, tm=512, fc=1024):
    bsz, l, d = x.shape
    wa = att.shape[2]
    dff = w1.shape[2]
    row = lambda w: pl.BlockSpec((1, tm, w), lambda b, i: (b, i, 0))
    hbm = pl.BlockSpec(memory_space=pl.ANY)
    return pl.pallas_call(
        functools.partial(_mixmlp_kernel, fc=fc, layer=layer),
        grid=(bsz, l // tm),
        in_specs=[row(d), row(wa), pl.BlockSpec((1, hy.shape[1], tm, LANES), lambda b, i: (b, 0, i, 0)),
                  pl.BlockSpec((1, 1, N_MOD * d), lambda b, i: (b, 0, 0)),
                  hbm, _const_spec((1, d)), _const_spec((1, d)), hbm, hbm],
        out_specs=row(d),
        out_shape=jax.ShapeDtypeStruct((bsz, l, d), F32),
        scratch_shapes=[pltpu.VMEM((d, d), BF16), pltpu.VMEM((d, dff), BF16), pltpu.VMEM((dff, d), BF16)],
        compiler_params=_params(2),
        name="mixmlp",
    )(x, att, hy, mod, w_out, b_out.reshape(1, d), g2.reshape(1, d), w1, w2)


def _poolmlp_kernel(h_ref, hp_ref, hn_ref, mod_ref, g1_ref, pw_ref, ps_ref, g2_ref, w1_hbm, w2_hbm, gf_ref, o_ref,
                    w1_s, w2_s, *, fc, seq_len, layer):
    @pl.when((pl.program_id(0) == 0) & (pl.program_id(1) == 0))
    def _():
        _stage_weights_bf16([(w1_hbm.at[layer], w1_s), (w2_hbm.at[layer], w2_s)])

    tm, d = h_ref.shape[1:]
    halo = POOL_HALO
    rows = tm + 2 * halo
    mod = mod_ref[0]
    h = h_ref[0]
    hx = jnp.concatenate([hp_ref[0], h, hn_ref[0]], axis=0)
    a = (_rms(hx, NORM_EPS) * g1_ref[...]) * (1.0 + mod[:, d:2 * d]) + mod[:, 0:d]
    t = lax.broadcasted_iota(jnp.int32, (rows, 1), 0) + (pl.program_id(1) * tm - halo)
    a = jnp.where((t >= 0) & (t < seq_len), a, 0.0)
    tc = t[halo:halo + tm]
    gd = d // len(POOL_WINDOWS)
    ys = []
    for g, win in enumerate(POOL_WINDOWS):
        ag = a[:, g * gd:(g + 1) * gd]
        s = ag + pltpu.roll(ag, 1, 0)
        w = 2
        while w < win:
            s = pltpu.roll(s, w // 2, 0) + pltpu.roll(s, rows - w // 2, 0)
            w *= 2
        cnt = (jnp.minimum(tc + (win - win // 2), seq_len) - jnp.maximum(tc - win // 2, 0)).astype(F32)
        dlt = s[halo:halo + tm] / cnt - ag[halo:halo + tm]
        ys.append(_dot(dlt.astype(BF16), pw_ref[g]))
    y = jnp.concatenate(ys, axis=1) * ps_ref[...]
    h1 = h + mod[:, 2 * d:3 * d] * y
    h2 = _mlp_tail(h1, g2_ref[...], mod[:, 3 * d:4 * d], mod[:, 4 * d:5 * d], mod[:, 5 * d:6 * d], w1_s, w2_s, fc)
    o_ref[0] = _rms(h2, NORM_EPS) * gf_ref[...]


def _poolmlp_call(h, mod, g1, pool_w, pool_scale, g2, w1, w2, gf, layer, tm=512, fc=1024):
    bsz, l, d = h.shape
    dff = w1.shape[2]
    halo = POOL_HALO
    nb = tm // halo
    last = l // halo - 1
    row = pl.BlockSpec((1, tm, d), lambda b, i: (b, i, 0))
    hbm = pl.BlockSpec(memory_space=pl.ANY)
    return pl.pallas_call(
        functools.partial(_poolmlp_kernel, fc=fc, seq_len=l, layer=layer),
        grid=(bsz, l // tm),
        in_specs=[row,
                  pl.BlockSpec((1, halo, d), lambda b, i: (b, jnp.maximum(i * nb - 1, 0), 0)),
                  pl.BlockSpec((1, halo, d), lambda b, i: (b, jnp.minimum((i + 1) * nb, last), 0)),
                  pl.BlockSpec((1, 1, N_MOD * d), lambda b, i: (b, 0, 0)),
                  _const_spec((1, d)), _const_spec(pool_w.shape), _const_spec((1, d)), _const_spec((1, d)),
                  hbm, hbm, _const_spec((1, d))],
        out_specs=row,
        out_shape=jax.ShapeDtypeStruct((bsz, l, d), F32),
        scratch_shapes=[pltpu.VMEM((d, dff), BF16), pltpu.VMEM((dff, d), BF16)],
        compiler_params=_params(2),
        name="poolmlp",
    )(h, h, h, mod, g1.reshape(1, d), pool_w, pool_scale.reshape(1, d), g2.reshape(1, d), w1, w2, gf.reshape(1, d))


def kernel(x, c, ctx, c_ctx, ada_w, ada_b, norm1_g, norm2_g, mix_w_in, mix_b_in, mix_w_out, mix_b_out, lam_q1, lam_k1, lam_q2, lam_k2, subln_g, hy_conv_w, hy_conv_b, hy_pos_w1, hy_pos_b1, hy_freq1, hy_pos_w2, hy_pos_b2, hy_freq2, hy_pos_w3, hy_bias, pool_w, pool_scale, mlp_w1, mlp_w2, final_g):
    bsz, seq_len, d = x.shape
    depth = ada_w.shape[0]
    assert depth == 2 and bsz % 2 == 0 and bsz < MOD_ROWS and seq_len % GRID_W == 0
    att_w = ATT_HEADS * ATT_V_DIM
    q_cols = k_cols = ATT_HEADS * 2 * ATT_QK_DIM
    kv_start, hy_start = q_cols, q_cols + k_cols + att_w
    in_cols = mix_w_in.shape[2]

    cv = jnp.concatenate([c, c_ctx[None, :], jnp.zeros((MOD_ROWS - bsz - 1, d), F32)], axis=0)
    mod = _mod_call(cv, ada_w, ada_b)
    mod_l = [mod[i, :bsz].reshape(bsz, 1, N_MOD * d) for i in range(depth)]

    lam_init = 0.8 - 0.6 * math.exp(-0.3 * 0)
    assert k_cols == q_cols and in_cols - hy_start == 3 * (d - att_w)
    q, k, v, u, x0 = _latproj_call(
        x, mod_l[0][:, :, 0:d], mod_l[0][:, :, d:2 * d], norm1_g[0], mix_w_in, mix_b_in[0], _rope_tables(seq_len),
        hy_conv_w[0], hy_conv_b[0], n_att=hy_start, qk=q_cols, q_scale=ATT_QK_DIM ** -0.5 * math.log2(math.e))
    mod_c = mod[0, bsz:bsz + 1].reshape(1, 1, N_MOD * d)
    kc, vc = _ctxproj_call(ctx, mod_c[:, :, 0:d], mod_c[:, :, d:2 * d], norm1_g[0],
                           mix_w_in[0, :, kv_start:hy_start].astype(BF16), mix_b_in[0, kv_start:hy_start], k_cols)
    lamv = jnp.stack([lam_q1[0], lam_k1[0], lam_q2[0], lam_k2[0]], axis=0)
    att = _attn_call(q, kc, vc, k, v, lamv, subln_g[0], lam_init)
    filt = (hy_pos_w1[0], hy_pos_b1[0], hy_freq1[0], hy_pos_w2[0], hy_pos_b2[0], hy_freq2[0], hy_pos_w3[0])
    hy = _hyena_call(u, x0, filt, hy_bias[0])
    h = _mixmlp_call(x, att, hy, mod_l[0], mix_w_out, mix_b_out[0], norm2_g[0], mlp_w1, mlp_w2, layer=0)

    return _poolmlp_call(h, mod_l[1], norm1_g[1], pool_w[0].astype(BF16), pool_scale[0], norm2_g[1],
                         mlp_w1, mlp_w2, final_g, layer=1)
```

```python
import functools
import math

import numpy as np
import jax
import jax.numpy as jnp
from jax import lax
from jax.experimental import pallas as pl
from jax.experimental.pallas import tpu as pltpu

F32 = jnp.float32
BF16 = jnp.bfloat16

GRID_W = 64
N_MOD = 6
ATT_HEADS = 4
ATT_V_DIM = 128
ATT_QK_DIM = 64
ROPE_BASE = 10000.0
HY_POS_EMB = 33
HY_DECAY_TARGET = 1e-2
HY_FAST_DECAY_PCT = 0.3
HY_SLOW_DECAY_PCT = 1.5
POOL_WINDOWS = (2, 4, 8, 16)
NORM_EPS = 1e-6
SUBLN_EPS = 1e-5
SUBLANES = 8
LANES = 128
FFT_N2 = 64
POOL_HALO = 8
WEIGHT_STAGE_BYTES = 2 * 1024 * 1024
MOD_ROWS = 8

VMEM_LIMIT_BYTES = 56 * 1024 * 1024


def _params(n_grid_dims):
    return pltpu.CompilerParams(
        dimension_semantics=("arbitrary",) * n_grid_dims,
        vmem_limit_bytes=VMEM_LIMIT_BYTES,
    )


def _const_spec(shape):
    nd = len(shape)
    return pl.BlockSpec(shape, lambda *_: (0,) * nd, pipeline_mode=pl.Buffered(1))


def _split_bf16(a):
    hi = a.astype(BF16)
    lo = (a - hi.astype(F32)).astype(BF16)
    return hi, lo


def _dot(a, b):
    return jnp.dot(a, b, preferred_element_type=F32)


def _dot3(a_hi, a_lo, b_hi, b_lo):
    return _dot(a_hi, b_hi) + (_dot(a_lo, b_hi) + _dot(a_hi, b_lo))


def _dot3f(a, b):
    a_hi, a_lo = _split_bf16(a)
    b_hi, b_lo = _split_bf16(b)
    return _dot3(a_hi, a_lo, b_hi, b_lo)


def _dft_dot(a_bf16, b):
    return _dot(a_bf16, b.astype(BF16))


def _rms(x, eps):
    return x * lax.rsqrt(jnp.mean(x * x, axis=-1, keepdims=True) + eps)


def _mod_kernel(cv_ref, w_ref, b_ref, o_ref):
    cv = cv_ref[...]
    s = cv / (1.0 + jnp.exp(-cv))
    o_ref[0] = _dot3f(s, w_ref[0]) + b_ref[0]


def _mod_call(cv, ada_w, ada_b, tn=1536):
    depth, d, n = ada_w.shape
    return pl.pallas_call(
        _mod_kernel,
        grid=(depth, n // tn),
        in_specs=[
            pl.BlockSpec((MOD_ROWS, d), lambda i, j: (0, 0)),
            pl.BlockSpec((1, d, tn), lambda i, j: (i, 0, j)),
            pl.BlockSpec((1, 1, tn), lambda i, j: (i, 0, j)),
        ],
        out_specs=pl.BlockSpec((1, MOD_ROWS, tn), lambda i, j: (i, 0, j)),
        out_shape=jax.ShapeDtypeStruct((depth, MOD_ROWS, n), F32),
        compiler_params=_params(2),
        name="mod",
    )(cv, ada_w, ada_b.reshape(depth, 1, n))


def _ctxproj_kernel(x_ref, sh_ref, sc_ref, g_ref, w_ref, b_ref, k_ref, v_ref):
    a = _rms(x_ref[0], NORM_EPS) * g_ref[...]
    a = a * (1.0 + sc_ref[0]) + sh_ref[0]
    p = _dot(a.astype(BF16), w_ref[...]) + b_ref[...]
    nk = k_ref.shape[2]
    k_ref[0] = p[:, :nk].astype(k_ref.dtype)
    v_ref[0] = p[:, nk:].astype(v_ref.dtype)


def _ctxproj_call(x, shift, scale, g, w_bf16, b, nk):
    bsz, s, d = x.shape
    n = w_bf16.shape[1]
    vec = pl.BlockSpec((1, 1, d), lambda bi: (0, 0, 0))
    return pl.pallas_call(
        _ctxproj_kernel,
        grid=(bsz,),
        in_specs=[pl.BlockSpec((1, s, d), lambda bi: (bi, 0, 0)), vec, vec,
                  _const_spec((1, d)), _const_spec((d, n)), _const_spec((1, n))],
        out_specs=[pl.BlockSpec((1, s, nk), lambda bi: (bi, 0, 0)), pl.BlockSpec((1, s, n - nk), lambda bi: (bi, 0, 0))],
        out_shape=[jax.ShapeDtypeStruct((bsz, s, nk), BF16), jax.ShapeDtypeStruct((bsz, s, n - nk), BF16)],
        compiler_params=_params(1),
        name="ctxproj",
    )(x, shift, scale, g.reshape(1, d), w_bf16, b.reshape(1, n))


def _latproj_kernel(x_ref, xp_ref, xn_ref, sh_ref, sc_ref, g_ref, w_hbm, b_ref, cos_ref, sa_ref, sb_ref,
                    cw_ref, cb_ref, q_ref, k_ref, v_ref, u_ref, x0_ref, w_ref, *, n_att, q_scale, seq_len):
    @pl.when((pl.program_id(0) == 0) & (pl.program_id(1) == 0))
    def _():
        _stage_weights_bf16([(w_hbm.at[0], w_ref)])

    tm = x_ref.shape[1]
    halo = xp_ref.shape[1]
    rows = tm + 2 * halo
    xx = jnp.concatenate([xp_ref[0], x_ref[0], xn_ref[0]], axis=0)
    a = _rms(xx, NORM_EPS) * g_ref[...]
    a = a * (1.0 + sc_ref[0]) + sh_ref[0]

    ph = _dot(a.astype(BF16), w_ref[:, n_att:]) + b_ref[:, n_att:]
    t = lax.broadcasted_iota(jnp.int32, (rows, 1), 0) + (pl.program_id(1) * tm - halo)
    ph = jnp.where((t >= 0) & (t < seq_len), ph, 0.0)
    ncb = u_ref.shape[1]
    c = ncb * LANES

    def conv(s):
        blk = ph[:, s * c:(s + 1) * c]
        prev = pltpu.roll(blk, 1, 0)[halo:halo + tm]
        nxt = pltpu.roll(blk, rows - 1, 0)[halo:halo + tm]
        cw = cw_ref[:, s * c:(s + 1) * c]
        return prev * cw[0:1] + blk[halo:halo + tm] * cw[1:2] + nxt * cw[2:3] + cb_ref[:, s * c:(s + 1) * c]

    x0 = conv(0)
    u = conv(2) * conv(1)
    for cb in range(ncb):
        x0_ref[0, cb] = x0[:, cb * LANES:(cb + 1) * LANES]
        u_ref[0, cb] = u[:, cb * LANES:(cb + 1) * LANES]

    p = _dot(a[halo:halo + tm].astype(BF16), w_ref[:, :n_att]) + b_ref[:, :n_att]
    cos, sa, sb = cos_ref[...], sa_ref[...], sb_ref[...]
    qk = q_ref.shape[2]
    for o_ref, c_lo, scale in ((q_ref, 0, q_scale), (k_ref, qk, 1.0)):
        for c0 in range(0, qk, LANES):
            blk = p[:, c_lo + c0:c_lo + c0 + LANES]
            r = blk * cos + pltpu.roll(blk, 16, 1) * sa + pltpu.roll(blk, LANES - 16, 1) * sb
            o_ref[0, :, c0:c0 + LANES] = (r * scale).astype(o_ref.dtype)
    v_ref[0] = p[:, 2 * qk:].astype(v_ref.dtype)


def _latproj_call(x, shift, scale, g, w_all, b, rope_tabs, conv_w, conv_b, n_att, qk, q_scale, tm=512):
    bsz, s, d = x.shape
    n = w_all.shape[2]
    c = (n - n_att) // 3
    halo = SUBLANES
    nb = tm // halo
    last = s // halo - 1
    vec = pl.BlockSpec((1, 1, d), lambda bi, i: (bi, 0, 0))
    tab = pl.BlockSpec((tm, LANES), lambda bi, i: (i, 0))
    row = lambda w: pl.BlockSpec((1, tm, w), lambda bi, i: (bi, i, 0))
    cblk = pl.BlockSpec((1, c // LANES, tm, LANES), lambda bi, i: (bi, 0, i, 0))
    return pl.pallas_call(
        functools.partial(_latproj_kernel, n_att=n_att, q_scale=q_scale, seq_len=s),
        grid=(bsz, s // tm),
        in_specs=[row(d),
                  pl.BlockSpec((1, halo, d), lambda bi, i: (bi, jnp.maximum(i * nb - 1, 0), 0)),
                  pl.BlockSpec((1, halo, d), lambda bi, i: (bi, jnp.minimum((i + 1) * nb, last), 0)),
                  vec, vec, _const_spec((1, d)), pl.BlockSpec(memory_space=pl.ANY), _const_spec((1, n)), tab, tab, tab,
                  _const_spec(conv_w.shape), _const_spec((1, 3 * c))],
        out_specs=[row(qk), row(qk), row(n_att - 2 * qk), cblk, cblk],
        out_shape=[jax.ShapeDtypeStruct((bsz, s, qk), BF16), jax.ShapeDtypeStruct((bsz, s, qk), BF16),
                   jax.ShapeDtypeStruct((bsz, s, n_att - 2 * qk), BF16),
                   jax.ShapeDtypeStruct((bsz, c // LANES, s, LANES), F32),
                   jax.ShapeDtypeStruct((bsz, c // LANES, s, LANES), F32)],
        scratch_shapes=[pltpu.VMEM((d, n), BF16)],
        compiler_params=_params(2),
        name="latproj",
    )(x, x, x, shift, scale, g.reshape(1, d), w_all, b.reshape(1, n), *rope_tabs, conv_w, conv_b.reshape(1, 3 * c))


def _rope_tables(seq_len):
    axis_dim = ATT_QK_DIM // 2
    n_freq = axis_dim // 2
    inv = (ROPE_BASE ** (-np.arange(n_freq, dtype=np.float32) / n_freq)).astype(np.float32)
    t = np.arange(seq_len)
    row, col = t // GRID_W, t % GRID_W
    jj = np.arange(128) % ATT_QK_DIM
    is_col = (jj // axis_dim) == 1
    second = ((jj % axis_dim) >= n_freq)[None, :]
    pos = np.where(is_col[None, :], col[:, None], row[:, None]).astype(np.float32)
    ang = (pos * inv[jj % n_freq][None, :]).astype(np.float64)
    cos, sin = np.cos(ang), np.sin(ang)
    as32 = lambda m: jnp.asarray(m, dtype=F32)
    return as32(cos), as32(np.where(second, sin, 0.0)), as32(np.where(second, 0.0, -sin))


def _attn_kernel(lam_ref, q_ref, kc_ref, vc_ref, kl_ref, vl_ref, g_ref, o_ref, *, tk, lam_init):
    q = q_ref[0]
    tq = q.shape[0]
    lane = lax.broadcasted_iota(jnp.int32, q.shape, 1)
    zero = jnp.zeros_like(q)
    q2 = jnp.concatenate([jnp.where(lane < ATT_QK_DIM, q, zero), jnp.where(lane >= ATT_QK_DIM, q, zero)], axis=0)

    def chunk(k, v, m, acc):
        s = lax.dot_general(q2, k, (((1,), (1,)), ((), ())), preferred_element_type=F32)
        rowmax = jnp.max(s, axis=1, keepdims=True)
        m_new = rowmax if m is None else jnp.maximum(m, rowmax)
        p = jnp.exp2(s - m_new).astype(BF16)
        pv = _dot(p, jnp.concatenate([v, jnp.ones_like(v)], axis=1))
        return m_new, (pv if acc is None else jnp.exp2(m - m_new) * acc + pv)

    m, acc = chunk(kc_ref[0], vc_ref[0], None, None)
    for j in range(kl_ref.shape[1] // tk):
        m, acc = chunk(kl_ref[0, j * tk:(j + 1) * tk, :], vl_ref[0, j * tk:(j + 1) * tk, :], m, acc)

    lamv = lam_ref[...]
    lam = (jnp.exp(jnp.sum(lamv[0:1] * lamv[1:2], axis=1, keepdims=True))
           - jnp.exp(jnp.sum(lamv[2:3] * lamv[3:4], axis=1, keepdims=True)) + lam_init)
    o_all = acc[:, :ATT_V_DIM] / acc[:, ATT_V_DIM:]
    o = o_all[:tq] - lam * o_all[tq:]
    o = _rms(o, SUBLN_EPS) * g_ref[...] * (1.0 - lam_init)
    o_ref[0] = o.astype(o_ref.dtype)


def _attn_call(q, kc, vc, kl, vl, lamv, g, lam_init, tq=1024, tk=256):
    bsz, l, width = q.shape
    heads = width // ATT_V_DIM
    lc = kc.shape[1]
    hd = ATT_V_DIM
    return pl.pallas_call(
        functools.partial(_attn_kernel, tk=tk, lam_init=lam_init),
        grid=(bsz, heads, l // tq),
        in_specs=[
            _const_spec(lamv.shape),
            pl.BlockSpec((1, tq, hd), lambda b, h, i: (b, i, h)),
            pl.BlockSpec((1, lc, hd), lambda b, h, i: (b, 0, h)),
            pl.BlockSpec((1, lc, hd), lambda b, h, i: (b, 0, h)),
            pl.BlockSpec((1, l, hd), lambda b, h, i: (b, 0, h)),
            pl.BlockSpec((1, l, hd), lambda b, h, i: (b, 0, h)),
            _const_spec((1, hd)),
        ],
        out_specs=pl.BlockSpec((1, tq, hd), lambda b, h, i: (b, i, h)),
        out_shape=jax.ShapeDtypeStruct((bsz, l, width), BF16),
        compiler_params=_params(3),
        name="diffattn",
    )(lamv, q, kc, vc, kl, vl, g.reshape(1, hd))


def _filter_kernel(z_ref, w1_ref, b1_ref, f1_ref, w2_ref, b2_ref, f2_ref, w3_ref, dl_ref, o_ref, *, seq_len):
    tm = z_ref.shape[0]
    h1 = jnp.sin(f1_ref[...] * (_dot3f(z_ref[...], w1_ref[...]) + b1_ref[...]))
    h2 = jnp.sin(f2_ref[...] * (_dot3f(h1, w2_ref[...]) + b2_ref[...]))
    h = _dot3f(h2, w3_ref[...])
    row = lax.broadcasted_iota(jnp.int32, h.shape, 0) + pl.program_id(0) * tm
    col = lax.broadcasted_iota(jnp.int32, h.shape, 1)
    t = row.astype(F32) * (1.0 / (seq_len - 1))
    h = h * jnp.exp(-t * dl_ref[...])
    half = h.shape[1] // 2
    o_ref[...] = jnp.where((row == 0) & (col >= half), 0.0, h)


def _filter_call(z, w1, b1, f1, w2, b2, f2, w3, absdelta2, tm=512):
    seq_len, kz = z.shape
    hid = w2.shape[0]
    n = w3.shape[1]
    return pl.pallas_call(
        functools.partial(_filter_kernel, seq_len=seq_len),
        grid=(seq_len // tm,),
        in_specs=[
            pl.BlockSpec((tm, kz), lambda i: (i, 0)),
            _const_spec((kz, hid)), _const_spec((1, hid)), _const_spec((1, hid)),
            _const_spec((hid, hid)), _const_spec((1, hid)), _const_spec((1, hid)),
            _const_spec((hid, n)), _const_spec((1, n)),
        ],
        out_specs=pl.BlockSpec((tm, n), lambda i: (i, 0)),
        out_shape=jax.ShapeDtypeStruct((seq_len, n), F32),
        compiler_params=_params(1),
        name="hyfilter",
    )(z, w1, b1.reshape(1, hid), f1.reshape(1, hid), w2, b2.reshape(1, hid), f2.reshape(1, hid), w3, absdelta2)


def _gather_minor(refs, n_rows):
    cols = []
    for j in range(SUBLANES):
        parts = [r[pl.ds(j, n_rows, stride=SUBLANES), :] for r in refs]
        cols.append(parts[0] if len(parts) == 1 else jnp.concatenate(parts, axis=0))
    return jnp.concatenate(cols, axis=1)


def _slab_gather(src_ref, jb, slab_ref):
    rows = src_ref.shape[0]
    slab_ref[...] = src_ref[:, jb].reshape(rows * SUBLANES, LANES)
    return _gather_minor([slab_ref], rows)


def _slab_scatter(val, slab_ref, dst_ref, jb):
    rows = dst_ref.shape[0]
    for j in range(SUBLANES):
        slab_ref[pl.ds(j, rows, stride=SUBLANES), :] = val[:, j * LANES:(j + 1) * LANES]
    dst_ref[:, jb] = slab_ref[...].reshape(rows, SUBLANES, LANES)


def _spectrum_store(val, spec_ref, jb):
    rows = val.shape[0]
    for j in range(SUBLANES):
        spec_ref[jb, pl.ds(j, rows, stride=SUBLANES), :] = val[:, j * LANES:(j + 1) * LANES]


def _spectrum_blocks(spec_ref, k0):
    x = spec_ref[:, pl.ds(pl.multiple_of(k0 * SUBLANES, SUBLANES * SUBLANES), SUBLANES * SUBLANES), :]
    return jnp.stack([x[:, i * SUBLANES:(i + 1) * SUBLANES, :].reshape(FFT_N2, LANES) for i in range(SUBLANES)], axis=0)


def _spectrum_put(spec_ref, k0, val):
    nb = FFT_N2 // SUBLANES
    v = val.reshape(SUBLANES, nb, SUBLANES, LANES)
    x = jnp.stack([v[:, jb].reshape(SUBLANES * SUBLANES, LANES) for jb in range(nb)], axis=0)
    spec_ref[:, pl.ds(pl.multiple_of(k0 * SUBLANES, SUBLANES * SUBLANES), SUBLANES * SUBLANES), :] = x


def _filter_fft_kernel(fr_ref, g_ref, twr_ref, twi_ref, hf_ref, hb_ref, kr_ref, ki_ref, slab_in, a_ref):
    n1 = a_ref.shape[2] // SUBLANES
    n2 = FFT_N2
    w = SUBLANES * LANES
    for jb in range(n2 // SUBLANES):
        z = jnp.concatenate([_slab_gather(hf_ref, jb, slab_in.at[0]), _slab_gather(hb_ref, jb, slab_in.at[1])], axis=1)
        out = _dft_dot(fr_ref[...], z)
        for t in range(4):
            _spectrum_store(out[(t % 2) * n1:(t % 2 + 1) * n1, (t // 2) * w:(t // 2 + 1) * w], a_ref.at[t], jb)

    def body(kb, _):
        k0 = pl.multiple_of(kb * SUBLANES, SUBLANES)
        tw_rows = pl.ds(pl.multiple_of(kb * (SUBLANES * n2), SUBLANES * n2), SUBLANES * n2)
        twr = twr_ref[tw_rows, :].reshape(SUBLANES, n2, LANES)
        twi = twi_ref[tw_rows, :].reshape(SUBLANES, n2, LANES)
        a = [_spectrum_blocks(a_ref.at[t], k0) for t in range(4)]
        cols = []
        for t in range(2):
            ar, ai = a[2 * t], a[2 * t + 1]
            sr, si = ar * twr - ai * twi, ar * twi + ai * twr
            cols += [jnp.concatenate([sr[i], si[i]], axis=0) for i in range(SUBLANES)]
        x = _dft_dot(g_ref[...], jnp.concatenate(cols, axis=1))
        for i in range(SUBLANES):
            xf = x[:, i * LANES:(i + 1) * LANES]
            xb = x[:, (SUBLANES + i) * LANES:(SUBLANES + i + 1) * LANES]
            rows = pl.ds(pl.multiple_of((k0 + i) * n2, n2), n2)
            kr_ref[rows, :] = xf[:n2] + xb[:n2]
            ki_ref[rows, :] = xf[n2:] - xb[n2:]
        return 0

    lax.fori_loop(0, n1 // SUBLANES, body, 0, unroll=2)


def _filter_fft_call(dft, hcat):
    seq_len, c2 = hcat.shape
    c = c2 // 2
    n, n1, n1h = dft["n"], dft["n1"], dft["n1h"]
    nb = FFT_N2 // SUBLANES
    ncb = c // LANES
    fr, g = dft["f_real"].astype(BF16), dft["g_fwd"].astype(BF16)
    hv = hcat.reshape(n1h, nb, SUBLANES, c2)
    blk = lambda off: pl.BlockSpec((n1h, nb, SUBLANES, LANES), lambda i, off=off: (0, 0, 0, i + off))
    out = pl.BlockSpec((None, n, LANES), lambda i: (i, 0, 0))
    return pl.pallas_call(
        _filter_fft_kernel,
        grid=(ncb,),
        in_specs=[_const_spec(fr.shape), _const_spec(g.shape), _const_spec((n, LANES)), _const_spec((n, LANES)),
                  blk(0), blk(ncb)],
        out_specs=[out, out],
        out_shape=[jax.ShapeDtypeStruct((ncb, n, LANES), F32)] * 2,
        scratch_shapes=[pltpu.VMEM((2, n1h * SUBLANES, LANES), F32), pltpu.VMEM((4, nb, n1 * SUBLANES, LANES), F32)],
        compiler_params=_params(1),
        name="hyfilter_fft",
    )(fr, g, dft["twr"], dft["twi"], hv, hv)


def _hyena_conv_kernel(fc_ref, g_ref, gc_ref, e_ref, twr_ref, twi_ref, kr_ref, ki_ref, u_ref, x0_ref, bias_ref, o_ref,
                       slab_in, a_ref, *, inv_n):
    n1 = a_ref.shape[2] // SUBLANES
    n1h = u_ref.shape[1]
    n2 = FFT_N2
    nb = n2 // SUBLANES

    for jb in range(nb):
        z = jnp.concatenate([_slab_gather(u_ref.at[c], jb, slab_in.at[c]) for c in range(2)], axis=0)
        out = _dft_dot(fc_ref[...], z)
        for c in range(2):
            _spectrum_store(out[c * n1:(c + 1) * n1], a_ref.at[c], jb)

    def body(kb, _):
        k0 = pl.multiple_of(kb * SUBLANES, SUBLANES)
        rows = pl.ds(pl.multiple_of(kb * (SUBLANES * n2), SUBLANES * n2), SUBLANES * n2)
        blocks = lambda ref: ref[rows, :].reshape(SUBLANES, n2, LANES)
        twr, twi, kr, ki = blocks(twr_ref), blocks(twi_ref), blocks(kr_ref), blocks(ki_ref)
        ar, ai = _spectrum_blocks(a_ref.at[0], k0), _spectrum_blocks(a_ref.at[1], k0)
        sr, si = ar * twr - ai * twi, ar * twi + ai * twr
        lanes = lambda re, im: jnp.concatenate([jnp.concatenate([re[i], im[i]], axis=0) for i in range(SUBLANES)], axis=1)
        x = _dft_dot(g_ref[...], lanes(sr, si))
        unl = lambda v, lo: jnp.stack([v[lo:lo + n2, i * LANES:(i + 1) * LANES] for i in range(SUBLANES)], axis=0)
        xr, xi = unl(x, 0), unl(x, n2)
        t = _dft_dot(gc_ref[...], lanes(xr * kr - xi * ki, xr * ki + xi * kr))
        tr, ti = unl(t, 0), unl(t, n2)
        _spectrum_put(a_ref.at[0], k0, tr * twr + ti * twi)
        _spectrum_put(a_ref.at[1], k0, ti * twr - tr * twi)
        return 0

    lax.fori_loop(0, n1 // SUBLANES, body, 0, unroll=2)

    for jb in range(nb):
        t = _gather_minor([a_ref.at[0, jb], a_ref.at[1, jb]], n1)
        y = _dft_dot(e_ref[...], t) * inv_n
        for c in range(2):
            _slab_scatter(y[c * n1h:(c + 1) * n1h], slab_in.at[c], o_ref.at[c], jb)
    o_ref[...] = x0_ref[...] * (o_ref[...] + u_ref[...] * bias_ref[...])


def _dft_blocks(seq_len):
    n = 2 * seq_len
    n2 = FFT_N2
    n1 = n // n2
    n1h = n1 // 2
    k1 = np.arange(n1)[:, None].astype(np.float64)
    a = 2.0 * np.pi * k1 * np.arange(n1h)[None, :] / n1
    fr, fi = np.cos(a), -np.sin(a)
    f_cplx = np.block([[fr, -fi], [fi, fr]])
    f_real = np.concatenate([fr, fi], axis=0)
    e_cplx = np.block([[fr.T, fi.T], [-fi.T, fr.T]])
    b = 2.0 * np.pi * np.arange(n2)[:, None] * np.arange(n2)[None, :] / n2
    gr, gi = np.cos(b), -np.sin(b)
    g_fwd = np.block([[gr, -gi], [gi, gr]])
    g_inv = np.block([[gr, gi], [-gi, gr]])
    kk = np.arange(n1)[:, None] * np.arange(n2)[None, :]
    ang = (2.0 * np.pi / n) * kk.reshape(n1 * n2, 1)
    twr = np.broadcast_to(np.cos(ang), (n1 * n2, 128))
    twi = np.broadcast_to(-np.sin(ang), (n1 * n2, 128))
    as32 = lambda m: jnp.asarray(np.ascontiguousarray(m), dtype=F32)
    return dict(n=n, n1=n1, n1h=n1h, f_cplx=as32(f_cplx), f_real=as32(f_real), e_cplx=as32(e_cplx),
                g_fwd=as32(g_fwd), g_inv=as32(g_inv), twr=as32(twr), twi=as32(twi))


def _hyena_call(u, x0, filt, hy_bias):
    bsz, ncb, seq_len, _ = u.shape
    c = ncb * LANES
    n2 = FFT_N2
    dft = _dft_blocks(seq_len)
    n, n1, n1h = dft["n"], dft["n1"], dft["n1h"]
    pairs = bsz // 2

    w1, b1, f1, w2, b2, f2, w3 = filt
    hid = w2.shape[0]
    bands = (HY_POS_EMB - 1) // 2
    t = np.linspace(0.0, 1.0, seq_len)[:, None]
    freqs = np.linspace(1e-4, bands - 1, bands)
    ang = (2.0 * math.pi / seq_len) * np.arange(seq_len)[:, None] * freqs[None, :]
    z = jnp.asarray(np.concatenate([t, np.cos(ang), -np.sin(ang), np.zeros((seq_len, hid - HY_POS_EMB))], axis=-1),
                    dtype=F32)
    w1p = jnp.concatenate([w1, jnp.zeros((hid - HY_POS_EMB, hid), F32)], axis=0)
    max_decay = math.log(HY_DECAY_TARGET) / HY_FAST_DECAY_PCT
    min_decay = math.log(HY_DECAY_TARGET) / HY_SLOW_DECAY_PCT
    absdelta = np.abs(np.linspace(min_decay, max_decay, c))
    absdelta2 = jnp.asarray(np.concatenate([absdelta, absdelta])[None, :], dtype=F32)
    hcat = _filter_call(z, w1p, b1, f1, w2, b2, f2, w3, absdelta2)
    nb = n2 // SUBLANES
    kr, ki = _filter_fft_call(dft, hcat)

    view = lambda a: a.reshape(pairs, 2, ncb, n1h, nb, SUBLANES, LANES)
    pair_spec = pl.BlockSpec((None, 2, None, n1h, nb, SUBLANES, LANES), lambda cb, p: (p, 0, cb, 0, 0, 0, 0))
    k_spec = pl.BlockSpec((None, n, LANES), lambda cb, p: (cb, 0, 0), pipeline_mode=pl.Buffered(1))
    fc, g, gc, e = (dft[k].astype(BF16) for k in ("f_cplx", "g_fwd", "g_inv", "e_cplx"))
    hy = pl.pallas_call(
        functools.partial(_hyena_conv_kernel, inv_n=1.0 / n),
        grid=(ncb, pairs),
        in_specs=[_const_spec(fc.shape), _const_spec(g.shape), _const_spec(gc.shape), _const_spec(e.shape),
                  _const_spec((n, LANES)), _const_spec((n, LANES)), k_spec, k_spec, pair_spec, pair_spec,
                  pl.BlockSpec((1, LANES), lambda cb, p: (0, cb))],
        out_specs=pair_spec,
        out_shape=jax.ShapeDtypeStruct((pairs, 2, ncb, n1h, nb, SUBLANES, LANES), F32),
        scratch_shapes=[pltpu.VMEM((2, n1h * SUBLANES, LANES), F32), pltpu.VMEM((2, nb, n1 * SUBLANES, LANES), F32)],
        compiler_params=_params(2),
        name="hyconv",
    )(fc, g, gc, e, dft["twr"], dft["twi"], kr, ki, view(u), view(x0), hy_bias.reshape(1, c))
    return hy.reshape(bsz, ncb, seq_len, LANES)


def _stage_weights_bf16(pairs):
    for src, dst in pairs:
        n_rows, n_cols = src.shape
        rows = 1 << ((WEIGHT_STAGE_BYTES // (4 * n_cols)).bit_length() - 1)
        n_chunks = n_rows // rows
        assert rows % (2 * SUBLANES) == 0 and n_chunks * rows == n_rows

        def run(stage, sem, src=src, dst=dst, rows=rows, n_chunks=n_chunks):
            copy = lambda k: pltpu.make_async_copy(src.at[pl.ds(k * rows, rows), :], stage.at[k % 2], sem.at[k % 2])
            copy(0).start()
            for k in range(n_chunks):
                if k + 1 < n_chunks:
                    copy(k + 1).start()
                copy(k).wait()
                dst[pl.ds(k * rows, rows), :] = stage[k % 2].astype(BF16)

        pl.run_scoped(run, pltpu.VMEM((2, rows, n_cols), F32), pltpu.SemaphoreType.DMA((2,)))


def _mlp_tail(h, g2, shift, scale, gate, w1_ref, w2_ref, fc):
    a = (_rms(h, NORM_EPS) * g2) * (1.0 + scale) + shift
    a = a.astype(BF16)
    acc = None
    for c0 in range(0, w1_ref.shape[1], fc):
        hid = jnp.maximum(_dot(a, w1_ref[:, c0:c0 + fc]), 0.0)
        part = _dot((hid * hid).astype(BF16), w2_ref[c0:c0 + fc, :])
        acc = part if acc is None else acc + part
    return h + gate * acc


def _mixmlp_kernel(x_ref, att_ref, hy_ref, mod_ref, wo_hbm, bo_ref, g2_ref, w1_hbm, w2_hbm, o_ref,
                   wo_s, w1_s, w2_s, *, fc, layer):
    @pl.when((pl.program_id(0) == 0) & (pl.program_id(1) == 0))
    def _():
        _stage_weights_bf16([(wo_hbm.at[layer // 2], wo_s), (w1_hbm.at[layer], w1_s), (w2_hbm.at[layer], w2_s)])

    d = x_ref.shape[2]
    wa = att_ref.shape[2]
    mod = mod_ref[0]
    hy = jnp.concatenate([hy_ref[0, cb] for cb in range(hy_ref.shape[1])], axis=1).astype(BF16)
    y = _dot(att_ref[0], wo_s[:wa, :]) + _dot(hy, wo_s[wa:, :]) + bo_ref[...]
    h = x_ref[0] + mod[:, 2 * d:3 * d] * y
    o_ref[0] = _mlp_tail(h, g2_ref[...], mod[:, 3 * d:4 * d], mod[:, 4 * d:5 * d], mod[:, 5 * d:6 * d],
                         w1_s, w2_s, fc)


def _mixmlp_call(x, att, hy, mod, w_out, b_out, g2, w1, w2, layer, tm=512, fc=1024):
    bsz, l, d = x.shape
    wa = att.shape[2]
    dff = w1.shape[2]
    row = lambda w: pl.BlockSpec((1, tm, w), lambda b, i: (b, i, 0))
    hbm = pl.BlockSpec(memory_space=pl.ANY)
    return pl.pallas_call(
        functools.partial(_mixmlp_kernel, fc=fc, layer=layer),
        grid=(bsz, l // tm),
        in_specs=[row(d), row(wa), pl.BlockSpec((1, hy.shape[1], tm, LANES), lambda b, i: (b, 0, i, 0)),
                  pl.BlockSpec((1, 1, N_MOD * d), lambda b, i: (b, 0, 0)),
                  hbm, _const_spec((1, d)), _const_spec((1, d)), hbm, hbm],
        out_specs=row(d),
        out_shape=jax.ShapeDtypeStruct((bsz, l, d), F32),
        scratch_shapes=[pltpu.VMEM((d, d), BF16), pltpu.VMEM((d, dff), BF16), pltpu.VMEM((dff, d), BF16)],
        compiler_params=_params(2),
        name="mixmlp",
    )(x, att, hy, mod, w_out, b_out.reshape(1, d), g2.reshape(1, d), w1, w2)


def _poolmlp_kernel(h_ref, hp_ref, hn_ref, mod_ref, g1_ref, pw_ref, ps_ref, g2_ref, w1_hbm, w2_hbm, gf_ref, o_ref,
                    w1_s, w2_s, *, fc, seq_len, layer):
    @pl.when((pl.program_id(0) == 0) & (pl.program_id(1) == 0))
    def _():
        _stage_weights_bf16([(w1_hbm.at[layer], w1_s), (w2_hbm.at[layer], w2_s)])

    tm, d = h_ref.shape[1:]
    halo = POOL_HALO
    rows = tm + 2 * halo
    mod = mod_ref[0]
    h = h_ref[0]
    hx = jnp.concatenate([hp_ref[0], h, hn_ref[0]], axis=0)
    a = (_rms(hx, NORM_EPS) * g1_ref[...]) * (1.0 + mod[:, d:2 * d]) + mod[:, 0:d]
    t = lax.broadcasted_iota(jnp.int32, (rows, 1), 0) + (pl.program_id(1) * tm - halo)
    a = jnp.where((t >= 0) & (t < seq_len), a, 0.0)
    tc = t[halo:halo + tm]
    gd = d // len(POOL_WINDOWS)
    ys = []
    for g, win in enumerate(POOL_WINDOWS):
        ag = a[:, g * gd:(g + 1) * gd]
        f, m = ag, 1
        while 2 * m <= win // 2:
            f = f + pltpu.roll(f, rows - m, 0)
            m *= 2
        s = pltpu.roll(f, win // 2, 0) + f
        cnt = (jnp.minimum(tc + (win - win // 2), seq_len) - jnp.maximum(tc - win // 2, 0)).astype(F32)
        dlt = s[halo:halo + tm] / cnt - ag[halo:halo + tm]
        ys.append(_dot(dlt.astype(BF16), pw_ref[g]))
    y = jnp.concatenate(ys, axis=1) * ps_ref[...]
    h1 = h + mod[:, 2 * d:3 * d] * y
    h2 = _mlp_tail(h1, g2_ref[...], mod[:, 3 * d:4 * d], mod[:, 4 * d:5 * d], mod[:, 5 * d:6 * d], w1_s, w2_s, fc)
    o_ref[0] = _rms(h2, NORM_EPS) * gf_ref[...]


def _poolmlp_call(h, mod, g1, pool_w, pool_scale, g2, w1, w2, gf, layer, tm=512, fc=1024):
    bsz, l, d = h.shape
    dff = w1.shape[2]
    halo = POOL_HALO
    nb = tm // halo
    last = l // halo - 1
    row = pl.BlockSpec((1, tm, d), lambda b, i: (b, i, 0))
    hbm = pl.BlockSpec(memory_space=pl.ANY)
    return pl.pallas_call(
        functools.partial(_poolmlp_kernel, fc=fc, seq_len=l, layer=layer),
        grid=(bsz, l // tm),
        in_specs=[row,
                  pl.BlockSpec((1, halo, d), lambda b, i: (b, jnp.maximum(i * nb - 1, 0), 0)),
                  pl.BlockSpec((1, halo, d), lambda b, i: (b, jnp.minimum((i + 1) * nb, last), 0)),
                  pl.BlockSpec((1, 1, N_MOD * d), lambda b, i: (b, 0, 0)),
                  _const_spec((1, d)), _const_spec(pool_w.shape), _const_spec((1, d)), _const_spec((1, d)),
                  hbm, hbm, _const_spec((1, d))],
        out_specs=row,
        out_shape=jax.ShapeDtypeStruct((bsz, l, d), F32),
        scratch_shapes=[pltpu.VMEM((d, dff), BF16), pltpu.VMEM((dff, d), BF16)],
        compiler_params=_params(2),
        name="poolmlp",
    )(h, h, h, mod, g1.reshape(1, d), pool_w, pool_scale.reshape(1, d), g2.reshape(1, d), w1, w2, gf.reshape(1, d))


def kernel(x, c, ctx, c_ctx, ada_w, ada_b, norm1_g, norm2_g, mix_w_in, mix_b_in, mix_w_out, mix_b_out, lam_q1, lam_k1, lam_q2, lam_k2, subln_g, hy_conv_w, hy_conv_b, hy_pos_w1, hy_pos_b1, hy_freq1, hy_pos_w2, hy_pos_b2, hy_freq2, hy_pos_w3, hy_bias, pool_w, pool_scale, mlp_w1, mlp_w2, final_g):
    bsz, seq_len, d = x.shape
    depth = ada_w.shape[0]
    assert depth == 2 and bsz % 2 == 0 and bsz < MOD_ROWS and seq_len % GRID_W == 0
    att_w = ATT_HEADS * ATT_V_DIM
    q_cols = k_cols = ATT_HEADS * 2 * ATT_QK_DIM
    kv_start, hy_start = q_cols, q_cols + k_cols + att_w
    in_cols = mix_w_in.shape[2]

    cv = jnp.concatenate([c, c_ctx[None, :], jnp.zeros((MOD_ROWS - bsz - 1, d), F32)], axis=0)
    mod = _mod_call(cv, ada_w, ada_b)
    mod_l = [mod[i, :bsz].reshape(bsz, 1, N_MOD * d) for i in range(depth)]

    lam_init = 0.8 - 0.6 * math.exp(-0.3 * 0)
    assert k_cols == q_cols and in_cols - hy_start == 3 * (d - att_w)
    q, k, v, u, x0 = _latproj_call(
        x, mod_l[0][:, :, 0:d], mod_l[0][:, :, d:2 * d], norm1_g[0], mix_w_in, mix_b_in[0], _rope_tables(seq_len),
        hy_conv_w[0], hy_conv_b[0], n_att=hy_start, qk=q_cols, q_scale=ATT_QK_DIM ** -0.5 * math.log2(math.e))
    mod_c = mod[0, bsz:bsz + 1].reshape(1, 1, N_MOD * d)
    kc, vc = _ctxproj_call(ctx, mod_c[:, :, 0:d], mod_c[:, :, d:2 * d], norm1_g[0],
                           mix_w_in[0, :, kv_start:hy_start].astype(BF16), mix_b_in[0, kv_start:hy_start], k_cols)
    lamv = jnp.stack([lam_q1[0], lam_k1[0], lam_q2[0], lam_k2[0]], axis=0)
    att = _attn_call(q, kc, vc, k, v, lamv, subln_g[0], lam_init)
    filt = (hy_pos_w1[0], hy_pos_b1[0], hy_freq1[0], hy_pos_w2[0], hy_pos_b2[0], hy_freq2[0], hy_pos_w3[0])
    hy = _hyena_call(u, x0, filt, hy_bias[0])
    h = _mixmlp_call(x, att, hy, mod_l[0], mix_w_out, mix_b_out[0], norm2_g[0], mlp_w1, mlp_w2, layer=0)

    return _poolmlp_call(h, mod_l[1], norm1_g[1], pool_w[0].astype(BF16), pool_scale[0], norm2_g[1],
                         mlp_w1, mlp_w2, final_g, layer=1)
```

```python
import functools
import math

import numpy as np
import jax
import jax.numpy as jnp
from jax import lax
from jax.experimental import pallas as pl
from jax.experimental.pallas import tpu as pltpu

F32 = jnp.float32
BF16 = jnp.bfloat16

GRID_W = 64
N_MOD = 6
ATT_HEADS = 4
ATT_V_DIM = 128
ATT_QK_DIM = 64
ROPE_BASE = 10000.0
HY_POS_EMB = 33
HY_DECAY_TARGET = 1e-2
HY_FAST_DECAY_PCT = 0.3
HY_SLOW_DECAY_PCT = 1.5
POOL_WINDOWS = (2, 4, 8, 16)
NORM_EPS = 1e-6
SUBLN_EPS = 1e-5
SUBLANES = 8
LANES = 128
FFT_N2 = 64
POOL_HALO = 8
WEIGHT_STAGE_BYTES = 2 * 1024 * 1024
MOD_ROWS = 8

VMEM_LIMIT_BYTES = 56 * 1024 * 1024


def _params(n_grid_dims):
    return pltpu.CompilerParams(
        dimension_semantics=("arbitrary",) * n_grid_dims,
        vmem_limit_bytes=VMEM_LIMIT_BYTES,
    )


def _const_spec(shape):
    nd = len(shape)
    return pl.BlockSpec(shape, lambda *_: (0,) * nd, pipeline_mode=pl.Buffered(1))


def _split_bf16(a):
    hi = a.astype(BF16)
    lo = (a - hi.astype(F32)).astype(BF16)
    return hi, lo


def _dot(a, b):
    return jnp.dot(a, b, preferred_element_type=F32)


def _dot3(a_hi, a_lo, b_hi, b_lo):
    return _dot(a_hi, b_hi) + (_dot(a_lo, b_hi) + _dot(a_hi, b_lo))


def _dot3f(a, b):
    a_hi, a_lo = _split_bf16(a)
    b_hi, b_lo = _split_bf16(b)
    return _dot3(a_hi, a_lo, b_hi, b_lo)


def _dft_dot(a_bf16, b):
    return _dot(a_bf16, b.astype(BF16))


def _rms(x, eps):
    return x * lax.rsqrt(jnp.mean(x * x, axis=-1, keepdims=True) + eps)


def _mod_kernel(cv_ref, w_ref, b_ref, o_ref):
    cv = cv_ref[...]
    s = cv / (1.0 + jnp.exp(-cv))
    o_ref[0] = _dot3f(s, w_ref[0]) + b_ref[0]


def _mod_call(cv, ada_w, ada_b, tn=1536):
    depth, d, n = ada_w.shape
    return pl.pallas_call(
        _mod_kernel,
        grid=(depth, n // tn),
        in_specs=[
            pl.BlockSpec((MOD_ROWS, d), lambda i, j: (0, 0)),
            pl.BlockSpec((1, d, tn), lambda i, j: (i, 0, j)),
            pl.BlockSpec((1, 1, tn), lambda i, j: (i, 0, j)),
        ],
        out_specs=pl.BlockSpec((1, MOD_ROWS, tn), lambda i, j: (i, 0, j)),
        out_shape=jax.ShapeDtypeStruct((depth, MOD_ROWS, n), F32),
        compiler_params=_params(2),
        name="mod",
    )(cv, ada_w, ada_b.reshape(depth, 1, n))


def _ctxproj_kernel(x_ref, sh_ref, sc_ref, g_ref, w_ref, b_ref, k_ref, v_ref):
    a = _rms(x_ref[0], NORM_EPS) * g_ref[...]
    a = a * (1.0 + sc_ref[0]) + sh_ref[0]
    p = _dot(a.astype(BF16), w_ref[...]) + b_ref[...]
    nk = k_ref.shape[2]
    k_ref[0] = p[:, :nk].astype(k_ref.dtype)
    v_ref[0] = p[:, nk:].astype(v_ref.dtype)


def _ctxproj_call(x, shift, scale, g, w_bf16, b, nk):
    bsz, s, d = x.shape
    n = w_bf16.shape[1]
    vec = pl.BlockSpec((1, 1, d), lambda bi: (0, 0, 0))
    return pl.pallas_call(
        _ctxproj_kernel,
        grid=(bsz,),
        in_specs=[pl.BlockSpec((1, s, d), lambda bi: (bi, 0, 0)), vec, vec,
                  _const_spec((1, d)), _const_spec((d, n)), _const_spec((1, n))],
        out_specs=[pl.BlockSpec((1, s, nk), lambda bi: (bi, 0, 0)), pl.BlockSpec((1, s, n - nk), lambda bi: (bi, 0, 0))],
        out_shape=[jax.ShapeDtypeStruct((bsz, s, nk), BF16), jax.ShapeDtypeStruct((bsz, s, n - nk), BF16)],
        compiler_params=_params(1),
        name="ctxproj",
    )(x, shift, scale, g.reshape(1, d), w_bf16, b.reshape(1, n))


def _latproj_kernel(x_ref, xp_ref, xn_ref, sh_ref, sc_ref, g_ref, w_hbm, b_ref, cos_ref, sa_ref, sb_ref,
                    cw_ref, cb_ref, q_ref, k_ref, v_ref, u_ref, x0_ref, w_ref, *, n_att, q_scale, seq_len):
    @pl.when((pl.program_id(0) == 0) & (pl.program_id(1) == 0))
    def _():
        _stage_weights_bf16([(w_hbm.at[0], w_ref)])

    tm = x_ref.shape[1]
    halo = xp_ref.shape[1]
    rows = tm + 2 * halo
    xx = jnp.concatenate([xp_ref[0], x_ref[0], xn_ref[0]], axis=0)
    a = _rms(xx, NORM_EPS) * g_ref[...]
    a = a * (1.0 + sc_ref[0]) + sh_ref[0]

    ph = _dot(a.astype(BF16), w_ref[:, n_att:]) + b_ref[:, n_att:]
    t = lax.broadcasted_iota(jnp.int32, (rows, 1), 0) + (pl.program_id(1) * tm - halo)
    ph = jnp.where((t >= 0) & (t < seq_len), ph, 0.0)
    ncb = u_ref.shape[1]
    c = ncb * LANES

    def conv(s):
        blk = ph[:, s * c:(s + 1) * c]
        prev = pltpu.roll(blk, 1, 0)[halo:halo + tm]
        nxt = pltpu.roll(blk, rows - 1, 0)[halo:halo + tm]
        cw = cw_ref[:, s * c:(s + 1) * c]
        return prev * cw[0:1] + blk[halo:halo + tm] * cw[1:2] + nxt * cw[2:3] + cb_ref[:, s * c:(s + 1) * c]

    x0 = conv(0)
    u = conv(2) * conv(1)
    for cb in range(ncb):
        x0_ref[0, cb] = x0[:, cb * LANES:(cb + 1) * LANES]
        u_ref[0, cb] = u[:, cb * LANES:(cb + 1) * LANES]

    p = _dot(a[halo:halo + tm].astype(BF16), w_ref[:, :n_att]) + b_ref[:, :n_att]
    cos, sa, sb = cos_ref[...], sa_ref[...], sb_ref[...]
    qk = q_ref.shape[2]
    for o_ref, c_lo, scale in ((q_ref, 0, q_scale), (k_ref, qk, 1.0)):
        for c0 in range(0, qk, LANES):
            blk = p[:, c_lo + c0:c_lo + c0 + LANES]
            r = blk * cos + pltpu.roll(blk, 16, 1) * sa + pltpu.roll(blk, LANES - 16, 1) * sb
            o_ref[0, :, c0:c0 + LANES] = (r * scale).astype(o_ref.dtype)
    v_ref[0] = p[:, 2 * qk:].astype(v_ref.dtype)


def _latproj_call(x, shift, scale, g, w_all, b, rope_tabs, conv_w, conv_b, n_att, qk, q_scale, tm=512):
    bsz, s, d = x.shape
    n = w_all.shape[2]
    c = (n - n_att) // 3
    halo = SUBLANES
    nb = tm // halo
    last = s // halo - 1
    vec = pl.BlockSpec((1, 1, d), lambda bi, i: (bi, 0, 0))
    tab = pl.BlockSpec((tm, LANES), lambda bi, i: (i, 0))
    row = lambda w: pl.BlockSpec((1, tm, w), lambda bi, i: (bi, i, 0))
    cblk = pl.BlockSpec((1, c // LANES, tm, LANES), lambda bi, i: (bi, 0, i, 0))
    return pl.pallas_call(
        functools.partial(_latproj_kernel, n_att=n_att, q_scale=q_scale, seq_len=s),
        grid=(bsz, s // tm),
        in_specs=[row(d),
                  pl.BlockSpec((1, halo, d), lambda bi, i: (bi, jnp.maximum(i * nb - 1, 0), 0)),
                  pl.BlockSpec((1, halo, d), lambda bi, i: (bi, jnp.minimum((i + 1) * nb, last), 0)),
                  vec, vec, _const_spec((1, d)), pl.BlockSpec(memory_space=pl.ANY), _const_spec((1, n)), tab, tab, tab,
                  _const_spec(conv_w.shape), _const_spec((1, 3 * c))],
        out_specs=[row(qk), row(qk), row(n_att - 2 * qk), cblk, cblk],
        out_shape=[jax.ShapeDtypeStruct((bsz, s, qk), BF16), jax.ShapeDtypeStruct((bsz, s, qk), BF16),
                   jax.ShapeDtypeStruct((bsz, s, n_att - 2 * qk), BF16),
                   jax.ShapeDtypeStruct((bsz, c // LANES, s, LANES), F32),
                   jax.ShapeDtypeStruct((bsz, c // LANES, s, LANES), F32)],
        scratch_shapes=[pltpu.VMEM((d, n), BF16)],
        compiler_params=_params(2),
        name="latproj",
    )(x, x, x, shift, scale, g.reshape(1, d), w_all, b.reshape(1, n), *rope_tabs, conv_w, conv_b.reshape(1, 3 * c))


def _rope_tables(seq_len):
    axis_dim = ATT_QK_DIM // 2
    n_freq = axis_dim // 2
    inv = (ROPE_BASE ** (-np.arange(n_freq, dtype=np.float32) / n_freq)).astype(np.float32)
    t = np.arange(seq_len)
    row, col = t // GRID_W, t % GRID_W
    jj = np.arange(128) % ATT_QK_DIM
    is_col = (jj // axis_dim) == 1
    second = ((jj % axis_dim) >= n_freq)[None, :]
    pos = np.where(is_col[None, :], col[:, None], row[:, None]).astype(np.float32)
    ang = (pos * inv[jj % n_freq][None, :]).astype(np.float64)
    cos, sin = np.cos(ang), np.sin(ang)
    as32 = lambda m: jnp.asarray(m, dtype=F32)
    return as32(cos), as32(np.where(second, sin, 0.0)), as32(np.where(second, 0.0, -sin))


def _attn_kernel(lam_ref, q_ref, kc_ref, vc_ref, kl_ref, vl_ref, g_ref, o_ref, *, tk, lam_init):
    q = q_ref[0]
    tq = q.shape[0]
    lane = lax.broadcasted_iota(jnp.int32, q.shape, 1)
    zero = jnp.zeros_like(q)
    q2 = jnp.concatenate([jnp.where(lane < ATT_QK_DIM, q, zero), jnp.where(lane >= ATT_QK_DIM, q, zero)], axis=0)

    def chunk(k, v, m, acc):
        s = lax.dot_general(q2, k, (((1,), (1,)), ((), ())), preferred_element_type=F32)
        rowmax = jnp.max(s, axis=1, keepdims=True)
        m_new = rowmax if m is None else jnp.maximum(m, rowmax)
        p = jnp.exp2(s - m_new).astype(BF16)
        pv = _dot(p, jnp.concatenate([v, jnp.ones_like(v)], axis=1))
        return m_new, (pv if acc is None else jnp.exp2(m - m_new) * acc + pv)

    m, acc = chunk(kc_ref[0], vc_ref[0], None, None)
    for j in range(kl_ref.shape[1] // tk):
        m, acc = chunk(kl_ref[0, j * tk:(j + 1) * tk, :], vl_ref[0, j * tk:(j + 1) * tk, :], m, acc)

    lamv = lam_ref[...]
    lam = (jnp.exp(jnp.sum(lamv[0:1] * lamv[1:2], axis=1, keepdims=True))
           - jnp.exp(jnp.sum(lamv[2:3] * lamv[3:4], axis=1, keepdims=True)) + lam_init)
    o_all = acc[:, :ATT_V_DIM] / acc[:, ATT_V_DIM:]
    o = o_all[:tq] - lam * o_all[tq:]
    o = _rms(o, SUBLN_EPS) * g_ref[...] * (1.0 - lam_init)
    o_ref[0] = o.astype(o_ref.dtype)


def _attn_call(q, kc, vc, kl, vl, lamv, g, lam_init, tq=2048, tk=256):
    bsz, l, width = q.shape
    heads = width // ATT_V_DIM
    lc = kc.shape[1]
    hd = ATT_V_DIM
    tq = min(tq, l)
    return pl.pallas_call(
        functools.partial(_attn_kernel, tk=tk, lam_init=lam_init),
        grid=(bsz, heads, l // tq),
        in_specs=[
            _const_spec(lamv.shape),
            pl.BlockSpec((1, tq, hd), lambda b, h, i: (b, i, h)),
            pl.BlockSpec((1, lc, hd), lambda b, h, i: (b, 0, h)),
            pl.BlockSpec((1, lc, hd), lambda b, h, i: (b, 0, h)),
            pl.BlockSpec((1, l, hd), lambda b, h, i: (b, 0, h)),
            pl.BlockSpec((1, l, hd), lambda b, h, i: (b, 0, h)),
            _const_spec((1, hd)),
        ],
        out_specs=pl.BlockSpec((1, tq, hd), lambda b, h, i: (b, i, h)),
        out_shape=jax.ShapeDtypeStruct((bsz, l, width), BF16),
        compiler_params=_params(3),
        name="diffattn",
    )(lamv, q, kc, vc, kl, vl, g.reshape(1, hd))


def _filter_kernel(z_ref, w1_ref, b1_ref, f1_ref, w2_ref, b2_ref, f2_ref, w3_ref, dl_ref, o_ref, *, seq_len):
    tm = z_ref.shape[0]
    h1 = jnp.sin(f1_ref[...] * (_dot3f(z_ref[...], w1_ref[...]) + b1_ref[...]))
    h2 = jnp.sin(f2_ref[...] * (_dot3f(h1, w2_ref[...]) + b2_ref[...]))
    h = _dot3f(h2, w3_ref[...])
    row = lax.broadcasted_iota(jnp.int32, h.shape, 0) + pl.program_id(0) * tm
    col = lax.broadcasted_iota(jnp.int32, h.shape, 1)
    t = row.astype(F32) * (1.0 / (seq_len - 1))
    h = h * jnp.exp(-t * dl_ref[...])
    half = h.shape[1] // 2
    o_ref[...] = jnp.where((row == 0) & (col >= half), 0.0, h)


def _filter_call(z, w1, b1, f1, w2, b2, f2, w3, absdelta2, tm=512):
    seq_len, kz = z.shape
    hid = w2.shape[0]
    n = w3.shape[1]
    return pl.pallas_call(
        functools.partial(_filter_kernel, seq_len=seq_len),
        grid=(seq_len // tm,),
        in_specs=[
            pl.BlockSpec((tm, kz), lambda i: (i, 0)),
            _const_spec((kz, hid)), _const_spec((1, hid)), _const_spec((1, hid)),
            _const_spec((hid, hid)), _const_spec((1, hid)), _const_spec((1, hid)),
            _const_spec((hid, n)), _const_spec((1, n)),
        ],
        out_specs=pl.BlockSpec((tm, n), lambda i: (i, 0)),
        out_shape=jax.ShapeDtypeStruct((seq_len, n), F32),
        compiler_params=_params(1),
        name="hyfilter",
    )(z, w1, b1.reshape(1, hid), f1.reshape(1, hid), w2, b2.reshape(1, hid), f2.reshape(1, hid), w3, absdelta2)


def _gather_minor(refs, n_rows):
    cols = []
    for j in range(SUBLANES):
        parts = [r[pl.ds(j, n_rows, stride=SUBLANES), :] for r in refs]
        cols.append(parts[0] if len(parts) == 1 else jnp.concatenate(parts, axis=0))
    return jnp.concatenate(cols, axis=1)


def _slab_gather(src_ref, jb, slab_ref):
    rows = src_ref.shape[0]
    slab_ref[...] = src_ref[:, jb].reshape(rows * SUBLANES, LANES)
    return _gather_minor([slab_ref], rows)


def _slab_scatter(val, slab_ref, dst_ref, jb):
    rows = dst_ref.shape[0]
    for j in range(SUBLANES):
        slab_ref[pl.ds(j, rows, stride=SUBLANES), :] = val[:, j * LANES:(j + 1) * LANES]
    dst_ref[:, jb] = slab_ref[...].reshape(rows, SUBLANES, LANES)


def _spectrum_store(val, spec_ref, jb):
    rows = val.shape[0]
    for j in range(SUBLANES):
        spec_ref[jb, pl.ds(j, rows, stride=SUBLANES), :] = val[:, j * LANES:(j + 1) * LANES]


def _spectrum_blocks(spec_ref, k0):
    x = spec_ref[:, pl.ds(pl.multiple_of(k0 * SUBLANES, SUBLANES * SUBLANES), SUBLANES * SUBLANES), :]
    return jnp.stack([x[:, i * SUBLANES:(i + 1) * SUBLANES, :].reshape(FFT_N2, LANES) for i in range(SUBLANES)], axis=0)


def _spectrum_put(spec_ref, k0, val):
    nb = FFT_N2 // SUBLANES
    v = val.reshape(SUBLANES, nb, SUBLANES, LANES)
    x = jnp.stack([v[:, jb].reshape(SUBLANES * SUBLANES, LANES) for jb in range(nb)], axis=0)
    spec_ref[:, pl.ds(pl.multiple_of(k0 * SUBLANES, SUBLANES * SUBLANES), SUBLANES * SUBLANES), :] = x


def _twiddle_block(twc_ref, twf_ref, kb):
    n2 = FFT_N2
    rows = pl.ds(pl.multiple_of(kb * n2, n2), n2)
    cr, ci = twc_ref[0, rows, :][None], twc_ref[1, rows, :][None]
    fr, fi = twf_ref[0].reshape(SUBLANES, n2, LANES), twf_ref[1].reshape(SUBLANES, n2, LANES)
    return cr * fr - ci * fi, cr * fi + ci * fr


def _filter_fft_kernel(fr_ref, g_ref, twc_ref, twf_ref, hf_ref, hb_ref, kr_ref, ki_ref, slab_in, a_ref):
    n1 = a_ref.shape[2] // SUBLANES
    n2 = FFT_N2
    w = SUBLANES * LANES
    for jb in range(n2 // SUBLANES):
        z = jnp.concatenate([_slab_gather(hf_ref, jb, slab_in.at[0]), _slab_gather(hb_ref, jb, slab_in.at[1])], axis=1)
        out = _dft_dot(fr_ref[...], z)
        for t in range(4):
            _spectrum_store(out[(t % 2) * n1:(t % 2 + 1) * n1, (t // 2) * w:(t // 2 + 1) * w], a_ref.at[t], jb)

    def body(kb, _):
        k0 = pl.multiple_of(kb * SUBLANES, SUBLANES)
        twr, twi = _twiddle_block(twc_ref, twf_ref, kb)
        a = [_spectrum_blocks(a_ref.at[t], k0) for t in range(4)]
        cols = []
        for t in range(2):
            ar, ai = a[2 * t], a[2 * t + 1]
            sr, si = ar * twr - ai * twi, ar * twi + ai * twr
            cols += [jnp.concatenate([sr[i], si[i]], axis=0) for i in range(SUBLANES)]
        x = _dft_dot(g_ref[...], jnp.concatenate(cols, axis=1))
        for i in range(SUBLANES):
            xf = x[:, i * LANES:(i + 1) * LANES]
            xb = x[:, (SUBLANES + i) * LANES:(SUBLANES + i + 1) * LANES]
            rows = pl.ds(pl.multiple_of((k0 + i) * n2, n2), n2)
            kr_ref[rows, :] = xf[:n2] + xb[:n2]
            ki_ref[rows, :] = xf[n2:] - xb[n2:]
        return 0

    lax.fori_loop(0, n1 // SUBLANES, body, 0, unroll=2)


def _filter_fft_call(dft, hcat):
    seq_len, c2 = hcat.shape
    c = c2 // 2
    n, n1, n1h = dft["n"], dft["n1"], dft["n1h"]
    nb = FFT_N2 // SUBLANES
    ncb = c // LANES
    fr, g = dft["f_real"].astype(BF16), dft["g_fwd"].astype(BF16)
    twc, twf = dft["tw_coarse"], dft["tw_fine"]
    hv = hcat.reshape(n1h, nb, SUBLANES, c2)
    blk = lambda off: pl.BlockSpec((n1h, nb, SUBLANES, LANES), lambda i, off=off: (0, 0, 0, i + off))
    out = pl.BlockSpec((None, n, LANES), lambda i: (i, 0, 0))
    return pl.pallas_call(
        _filter_fft_kernel,
        grid=(ncb,),
        in_specs=[_const_spec(fr.shape), _const_spec(g.shape), _const_spec(twc.shape), _const_spec(twf.shape),
                  blk(0), blk(ncb)],
        out_specs=[out, out],
        out_shape=[jax.ShapeDtypeStruct((ncb, n, LANES), F32)] * 2,
        scratch_shapes=[pltpu.VMEM((2, n1h * SUBLANES, LANES), F32), pltpu.VMEM((4, nb, n1 * SUBLANES, LANES), F32)],
        compiler_params=_params(1),
        name="hyfilter_fft",
    )(fr, g, twc, twf, hv, hv)


def _hyena_conv_kernel(fc_ref, g_ref, gc_ref, e_ref, twc_ref, twf_ref, kr_ref, ki_ref, u_ref, x0_ref, bias_ref, o_ref,
                       slab_in, a_ref, *, inv_n):
    n1 = a_ref.shape[2] // SUBLANES
    n1h = u_ref.shape[1]
    n2 = FFT_N2
    nb = n2 // SUBLANES

    for jb in range(nb):
        z = jnp.concatenate([_slab_gather(u_ref.at[c], jb, slab_in.at[c]) for c in range(2)], axis=0)
        out = _dft_dot(fc_ref[...], z)
        for c in range(2):
            _spectrum_store(out[c * n1:(c + 1) * n1], a_ref.at[c], jb)

    def body(kb, _):
        k0 = pl.multiple_of(kb * SUBLANES, SUBLANES)
        rows = pl.ds(pl.multiple_of(kb * (SUBLANES * n2), SUBLANES * n2), SUBLANES * n2)
        blocks = lambda ref: ref[rows, :].reshape(SUBLANES, n2, LANES)
        (twr, twi), kr, ki = _twiddle_block(twc_ref, twf_ref, kb), blocks(kr_ref), blocks(ki_ref)
        ar, ai = _spectrum_blocks(a_ref.at[0], k0), _spectrum_blocks(a_ref.at[1], k0)
        sr, si = ar * twr - ai * twi, ar * twi + ai * twr
        lanes = lambda re, im: jnp.concatenate([jnp.concatenate([re[i], im[i]], axis=0) for i in range(SUBLANES)], axis=1)
        x = _dft_dot(g_ref[...], lanes(sr, si))
        unl = lambda v, lo: jnp.stack([v[lo:lo + n2, i * LANES:(i + 1) * LANES] for i in range(SUBLANES)], axis=0)
        xr, xi = unl(x, 0), unl(x, n2)
        t = _dft_dot(gc_ref[...], lanes(xr * kr - xi * ki, xr * ki + xi * kr))
        tr, ti = unl(t, 0), unl(t, n2)
        _spectrum_put(a_ref.at[0], k0, tr * twr + ti * twi)
        _spectrum_put(a_ref.at[1], k0, ti * twr - tr * twi)
        return 0

    lax.fori_loop(0, n1 // SUBLANES, body, 0, unroll=2)

    for jb in range(nb):
        t = _gather_minor([a_ref.at[0, jb], a_ref.at[1, jb]], n1)
        y = _dft_dot(e_ref[...], t) * inv_n
        for c in range(2):
            _slab_scatter(y[c * n1h:(c + 1) * n1h], slab_in.at[c], o_ref.at[c], jb)
    o_ref[...] = x0_ref[...] * (o_ref[...] + u_ref[...] * bias_ref[...])


def _dft_blocks(seq_len):
    n = 2 * seq_len
    n2 = FFT_N2
    n1 = n // n2
    n1h = n1 // 2
    k1 = np.arange(n1)[:, None].astype(np.float64)
    a = 2.0 * np.pi * k1 * np.arange(n1h)[None, :] / n1
    fr, fi = np.cos(a), -np.sin(a)
    f_cplx = np.block([[fr, -fi], [fi, fr]])
    f_real = np.concatenate([fr, fi], axis=0)
    e_cplx = np.block([[fr.T, fi.T], [-fi.T, fr.T]])
    b = 2.0 * np.pi * np.arange(n2)[:, None] * np.arange(n2)[None, :] / n2
    gr, gi = np.cos(b), -np.sin(b)
    g_fwd = np.block([[gr, -gi], [gi, gr]])
    g_inv = np.block([[gr, gi], [-gi, gr]])
    def table(k1s):
        ang = (2.0 * np.pi / n) * (k1s[:, None] * np.arange(n2)[None, :]).reshape(-1, 1)
        return np.broadcast_to(np.stack([np.cos(ang), -np.sin(ang)]), (2, k1s.size * n2, LANES))
    as32 = lambda m: jnp.asarray(np.ascontiguousarray(m), dtype=F32)
    return dict(n=n, n1=n1, n1h=n1h, f_cplx=as32(f_cplx), f_real=as32(f_real), e_cplx=as32(e_cplx),
                g_fwd=as32(g_fwd), g_inv=as32(g_inv), tw_coarse=as32(table(np.arange(0, n1, SUBLANES))),
                tw_fine=as32(table(np.arange(SUBLANES))))


def _hyena_call(u, x0, filt, hy_bias):
    bsz, ncb, seq_len, _ = u.shape
    c = ncb * LANES
    n2 = FFT_N2
    dft = _dft_blocks(seq_len)
    n, n1, n1h = dft["n"], dft["n1"], dft["n1h"]
    pairs = bsz // 2

    w1, b1, f1, w2, b2, f2, w3 = filt
    hid = w2.shape[0]
    bands = (HY_POS_EMB - 1) // 2
    t = np.linspace(0.0, 1.0, seq_len)[:, None]
    freqs = np.linspace(1e-4, bands - 1, bands)
    ang = (2.0 * math.pi / seq_len) * np.arange(seq_len)[:, None] * freqs[None, :]
    z = jnp.asarray(np.concatenate([t, np.cos(ang), -np.sin(ang), np.zeros((seq_len, hid - HY_POS_EMB))], axis=-1),
                    dtype=F32)
    w1p = jnp.concatenate([w1, jnp.zeros((hid - HY_POS_EMB, hid), F32)], axis=0)
    max_decay = math.log(HY_DECAY_TARGET) / HY_FAST_DECAY_PCT
    min_decay = math.log(HY_DECAY_TARGET) / HY_SLOW_DECAY_PCT
    absdelta = np.abs(np.linspace(min_decay, max_decay, c))
    absdelta2 = jnp.asarray(np.concatenate([absdelta, absdelta])[None, :], dtype=F32)
    hcat = _filter_call(z, w1p, b1, f1, w2, b2, f2, w3, absdelta2)
    nb = n2 // SUBLANES
    kr, ki = _filter_fft_call(dft, hcat)

    view = lambda a: a.reshape(pairs, 2, ncb, n1h, nb, SUBLANES, LANES)
    pair_spec = pl.BlockSpec((None, 2, None, n1h, nb, SUBLANES, LANES), lambda cb, p: (p, 0, cb, 0, 0, 0, 0))
    k_spec = pl.BlockSpec((None, n, LANES), lambda cb, p: (cb, 0, 0))
    twc, twf = dft["tw_coarse"], dft["tw_fine"]
    fc, g, gc, e = (dft[k].astype(BF16) for k in ("f_cplx", "g_fwd", "g_inv", "e_cplx"))
    hy = pl.pallas_call(
        functools.partial(_hyena_conv_kernel, inv_n=1.0 / n),
        grid=(ncb, pairs),
        in_specs=[_const_spec(fc.shape), _const_spec(g.shape), _const_spec(gc.shape), _const_spec(e.shape),
                  _const_spec(twc.shape), _const_spec(twf.shape), k_spec, k_spec, pair_spec, pair_spec,
                  pl.BlockSpec((1, LANES), lambda cb, p: (0, cb))],
        out_specs=pair_spec,
        out_shape=jax.ShapeDtypeStruct((pairs, 2, ncb, n1h, nb, SUBLANES, LANES), F32),
        scratch_shapes=[pltpu.VMEM((2, n1h * SUBLANES, LANES), F32), pltpu.VMEM((2, nb, n1 * SUBLANES, LANES), F32)],
        compiler_params=_params(2),
        name="hyconv",
    )(fc, g, gc, e, twc, twf, kr, ki, view(u), view(x0), hy_bias.reshape(1, c))
    return hy.reshape(bsz, ncb, seq_len, LANES)


def _stage_weights_bf16(pairs):
    for src, dst in pairs:
        n_rows, n_cols = src.shape
        rows = 1 << ((WEIGHT_STAGE_BYTES // (4 * n_cols)).bit_length() - 1)
        n_chunks = n_rows // rows
        assert rows % (2 * SUBLANES) == 0 and n_chunks * rows == n_rows

        def run(stage, sem, src=src, dst=dst, rows=rows, n_chunks=n_chunks):
            copy = lambda k: pltpu.make_async_copy(src.at[pl.ds(k * rows, rows), :], stage.at[k % 2], sem.at[k % 2])
            copy(0).start()
            for k in range(n_chunks):
                if k + 1 < n_chunks:
                    copy(k + 1).start()
                copy(k).wait()
                dst[pl.ds(k * rows, rows), :] = stage[k % 2].astype(BF16)

        pl.run_scoped(run, pltpu.VMEM((2, rows, n_cols), F32), pltpu.SemaphoreType.DMA((2,)))


def _mlp_tail(h, g2, shift, scale, gate, w1_ref, w2_ref, fc):
    a = (_rms(h, NORM_EPS) * g2) * (1.0 + scale) + shift
    a = a.astype(BF16)
    acc = None
    for c0 in range(0, w1_ref.shape[1], fc):
        hid = jnp.maximum(_dot(a, w1_ref[:, c0:c0 + fc]), 0.0)
        part = _dot((hid * hid).astype(BF16), w2_ref[c0:c0 + fc, :])
        acc = part if acc is None else acc + part
    return h + gate * acc


def _mixmlp_kernel(x_ref, att_ref, hy_ref, mod_ref, wo_hbm, bo_ref, g2_ref, w1_hbm, w2_hbm, o_ref,
                   wo_s, w1_s, w2_s, *, fc, layer):
    @pl.when((pl.program_id(0) == 0) & (pl.program_id(1) == 0))
    def _():
        _stage_weights_bf16([(wo_hbm.at[layer // 2], wo_s), (w1_hbm.at[layer], w1_s), (w2_hbm.at[layer], w2_s)])

    d = x_ref.shape[2]
    wa = att_ref.shape[2]
    mod = mod_ref[0]
    hy = jnp.concatenate([hy_ref[0, cb] for cb in range(hy_ref.shape[1])], axis=1).astype(BF16)
    y = _dot(att_ref[0], wo_s[:wa, :]) + _dot(hy, wo_s[wa:, :]) + bo_ref[...]
    h = x_ref[0] + mod[:, 2 * d:3 * d] * y
    o_ref[0] = _mlp_tail(h, g2_ref[...], mod[:, 3 * d:4 * d], mod[:, 4 * d:5 * d], mod[:, 5 * d:6 * d],
                         w1_s, w2_s, fc)


def _mixmlp_call(x, att, hy, mod, w_out, b_out, g2, w1, w2, layer, tm=512, fc=1024):
    bsz, l, d = x.shape
    wa = att.shape[2]
    dff = w1.shape[2]
    row = lambda w: pl.BlockSpec((1, tm, w), lambda b, i: (b, i, 0))
    hbm = pl.BlockSpec(memory_space=pl.ANY)
    return pl.pallas_call(
        functools.partial(_mixmlp_kernel, fc=fc, layer=layer),
        grid=(bsz, l // tm),
        in_specs=[row(d), row(wa), pl.BlockSpec((1, hy.shape[1], tm, LANES), lambda b, i: (b, 0, i, 0)),
                  pl.BlockSpec((1, 1, N_MOD * d), lambda b, i: (b, 0, 0)),
                  hbm, _const_spec((1, d)), _const_spec((1, d)), hbm, hbm],
        out_specs=row(d),
        out_shape=jax.ShapeDtypeStruct((bsz, l, d), F32),
        scratch_shapes=[pltpu.VMEM((d, d), BF16), pltpu.VMEM((d, dff), BF16), pltpu.VMEM((dff, d), BF16)],
        compiler_params=_params(2),
        name="mixmlp",
    )(x, att, hy, mod, w_out, b_out.reshape(1, d), g2.reshape(1, d), w1, w2)


def _poolmlp_kernel(h_ref, hp_ref, hn_ref, mod_ref, g1_ref, pw_ref, ps_ref, g2_ref, w1_hbm, w2_hbm, gf_ref, o_ref,
                    w1_s, w2_s, *, fc, seq_len, layer):
    @pl.when((pl.program_id(0) == 0) & (pl.program_id(1) == 0))
    def _():
        _stage_weights_bf16([(w1_hbm.at[layer], w1_s), (w2_hbm.at[layer], w2_s)])

    tm, d = h_ref.shape[1:]
    halo = POOL_HALO
    rows = tm + 2 * halo
    mod = mod_ref[0]
    h = h_ref[0]
    hx = jnp.concatenate([hp_ref[0], h, hn_ref[0]], axis=0)
    a = (_rms(hx, NORM_EPS) * g1_ref[...]) * (1.0 + mod[:, d:2 * d]) + mod[:, 0:d]
    t = lax.broadcasted_iota(jnp.int32, (rows, 1), 0) + (pl.program_id(1) * tm - halo)
    a = jnp.where((t >= 0) & (t < seq_len), a, 0.0)
    tc = t[halo:halo + tm]
    gd = d // len(POOL_WINDOWS)
    ys = []
    for g, win in enumerate(POOL_WINDOWS):
        ag = a[:, g * gd:(g + 1) * gd]
        f, m = ag, 1
        while 2 * m <= win // 2:
            f = f + pltpu.roll(f, rows - m, 0)
            m *= 2
        s = pltpu.roll(f, win // 2, 0) + f
        cnt = (jnp.minimum(tc + (win - win // 2), seq_len) - jnp.maximum(tc - win // 2, 0)).astype(F32)
        dlt = s[halo:halo + tm] / cnt - ag[halo:halo + tm]
        ys.append(_dot(dlt.astype(BF16), pw_ref[g]))
    y = jnp.concatenate(ys, axis=1) * ps_ref[...]
    h1 = h + mod[:, 2 * d:3 * d] * y
    h2 = _mlp_tail(h1, g2_ref[...], mod[:, 3 * d:4 * d], mod[:, 4 * d:5 * d], mod[:, 5 * d:6 * d], w1_s, w2_s, fc)
    o_ref[0] = _rms(h2, NORM_EPS) * gf_ref[...]


def _poolmlp_call(h, mod, g1, pool_w, pool_scale, g2, w1, w2, gf, layer, tm=512, fc=1024):
    bsz, l, d = h.shape
    dff = w1.shape[2]
    halo = POOL_HALO
    nb = tm // halo
    last = l // halo - 1
    row = pl.BlockSpec((1, tm, d), lambda b, i: (b, i, 0))
    hbm = pl.BlockSpec(memory_space=pl.ANY)
    return pl.pallas_call(
        functools.partial(_poolmlp_kernel, fc=fc, seq_len=l, layer=layer),
        grid=(bsz, l // tm),
        in_specs=[row,
                  pl.BlockSpec((1, halo, d), lambda b, i: (b, jnp.maximum(i * nb - 1, 0), 0)),
                  pl.BlockSpec((1, halo, d), lambda b, i: (b, jnp.minimum((i + 1) * nb, last), 0)),
                  pl.BlockSpec((1, 1, N_MOD * d), lambda b, i: (b, 0, 0)),
                  _const_spec((1, d)), _const_spec(pool_w.shape), _const_spec((1, d)), _const_spec((1, d)),
                  hbm, hbm, _const_spec((1, d))],
        out_specs=row,
        out_shape=jax.ShapeDtypeStruct((bsz, l, d), F32),
        scratch_shapes=[pltpu.VMEM((d, dff), BF16), pltpu.VMEM((dff, d), BF16)],
        compiler_params=_params(2),
        name="poolmlp",
    )(h, h, h, mod, g1.reshape(1, d), pool_w, pool_scale.reshape(1, d), g2.reshape(1, d), w1, w2, gf.reshape(1, d))


def kernel(x, c, ctx, c_ctx, ada_w, ada_b, norm1_g, norm2_g, mix_w_in, mix_b_in, mix_w_out, mix_b_out, lam_q1, lam_k1, lam_q2, lam_k2, subln_g, hy_conv_w, hy_conv_b, hy_pos_w1, hy_pos_b1, hy_freq1, hy_pos_w2, hy_pos_b2, hy_freq2, hy_pos_w3, hy_bias, pool_w, pool_scale, mlp_w1, mlp_w2, final_g):
    bsz, seq_len, d = x.shape
    depth = ada_w.shape[0]
    assert depth == 2 and bsz % 2 == 0 and bsz < MOD_ROWS and seq_len % GRID_W == 0
    att_w = ATT_HEADS * ATT_V_DIM
    q_cols = k_cols = ATT_HEADS * 2 * ATT_QK_DIM
    kv_start, hy_start = q_cols, q_cols + k_cols + att_w
    in_cols = mix_w_in.shape[2]

    cv = jnp.concatenate([c, c_ctx[None, :], jnp.zeros((MOD_ROWS - bsz - 1, d), F32)], axis=0)
    mod = _mod_call(cv, ada_w, ada_b)
    mod_l = [mod[i, :bsz].reshape(bsz, 1, N_MOD * d) for i in range(depth)]

    lam_init = 0.8 - 0.6 * math.exp(-0.3 * 0)
    assert k_cols == q_cols and in_cols - hy_start == 3 * (d - att_w)
    q, k, v, u, x0 = _latproj_call(
        x, mod_l[0][:, :, 0:d], mod_l[0][:, :, d:2 * d], norm1_g[0], mix_w_in, mix_b_in[0], _rope_tables(seq_len),
        hy_conv_w[0], hy_conv_b[0], n_att=hy_start, qk=q_cols, q_scale=ATT_QK_DIM ** -0.5 * math.log2(math.e))
    mod_c = mod[0, bsz:bsz + 1].reshape(1, 1, N_MOD * d)
    kc, vc = _ctxproj_call(ctx, mod_c[:, :, 0:d], mod_c[:, :, d:2 * d], norm1_g[0],
                           mix_w_in[0, :, kv_start:hy_start].astype(BF16), mix_b_in[0, kv_start:hy_start], k_cols)
    lamv = jnp.stack([lam_q1[0], lam_k1[0], lam_q2[0], lam_k2[0]], axis=0)
    att = _attn_call(q, kc, vc, k, v, lamv, subln_g[0], lam_init)
    filt = (hy_pos_w1[0], hy_pos_b1[0], hy_freq1[0], hy_pos_w2[0], hy_pos_b2[0], hy_freq2[0], hy_pos_w3[0])
    hy = _hyena_call(u, x0, filt, hy_bias[0])
    h = _mixmlp_call(x, att, hy, mod_l[0], mix_w_out, mix_b_out[0], norm2_g[0], mlp_w1, mlp_w2, layer=0)

    return _poolmlp_call(h, mod_l[1], norm1_g[1], pool_w[0].astype(BF16), pool_scale[0], norm2_g[1],
                         mlp_w1, mlp_w2, final_g, layer=1)
```

```python
import functools
import math

import numpy as np
import jax
import jax.numpy as jnp
from jax import lax
from jax.experimental import pallas as pl
from jax.experimental.pallas import tpu as pltpu

F32 = jnp.float32
BF16 = jnp.bfloat16

GRID_W = 64
N_MOD = 6
ATT_HEADS = 4
ATT_V_DIM = 128
ATT_QK_DIM = 64
ROPE_BASE = 10000.0
HY_POS_EMB = 33
HY_DECAY_TARGET = 1e-2
HY_FAST_DECAY_PCT = 0.3
HY_SLOW_DECAY_PCT = 1.5
POOL_WINDOWS = (2, 4, 8, 16)
NORM_EPS = 1e-6
SUBLN_EPS = 1e-5
SUBLANES = 8
LANES = 128
FFT_N2 = 64
POOL_HALO = 8
WEIGHT_STAGE_BYTES = 2 * 1024 * 1024
MOD_ROWS = 8

VMEM_LIMIT_BYTES = 56 * 1024 * 1024


def _params(n_grid_dims):
    return pltpu.CompilerParams(
        dimension_semantics=("arbitrary",) * n_grid_dims,
        vmem_limit_bytes=VMEM_LIMIT_BYTES,
    )


def _const_spec(shape):
    nd = len(shape)
    return pl.BlockSpec(shape, lambda *_: (0,) * nd, pipeline_mode=pl.Buffered(1))


def _split_bf16(a):
    hi = a.astype(BF16)
    lo = (a - hi.astype(F32)).astype(BF16)
    return hi, lo


def _dot(a, b):
    return jnp.dot(a, b, preferred_element_type=F32)


def _dot3(a_hi, a_lo, b_hi, b_lo):
    return _dot(a_hi, b_hi) + (_dot(a_lo, b_hi) + _dot(a_hi, b_lo))


def _dot3f(a, b):
    a_hi, a_lo = _split_bf16(a)
    b_hi, b_lo = _split_bf16(b)
    return _dot3(a_hi, a_lo, b_hi, b_lo)


def _dft_dot(a_bf16, b):
    return _dot(a_bf16, b.astype(BF16))


def _rms(x, eps):
    return x * lax.rsqrt(jnp.mean(x * x, axis=-1, keepdims=True) + eps)


def _mod_kernel(cv_ref, w_ref, b_ref, o_ref):
    cv = cv_ref[...]
    s = cv / (1.0 + jnp.exp(-cv))
    o_ref[0] = _dot3f(s, w_ref[0]) + b_ref[0]


def _mod_call(cv, ada_w, ada_b, tn=1536):
    depth, d, n = ada_w.shape
    return pl.pallas_call(
        _mod_kernel,
        grid=(depth, n // tn),
        in_specs=[
            pl.BlockSpec((MOD_ROWS, d), lambda i, j: (0, 0)),
            pl.BlockSpec((1, d, tn), lambda i, j: (i, 0, j)),
            pl.BlockSpec((1, 1, tn), lambda i, j: (i, 0, j)),
        ],
        out_specs=pl.BlockSpec((1, MOD_ROWS, tn), lambda i, j: (i, 0, j)),
        out_shape=jax.ShapeDtypeStruct((depth, MOD_ROWS, n), F32),
        compiler_params=_params(2),
        name="mod",
    )(cv, ada_w, ada_b.reshape(depth, 1, n))


def _ctxproj_kernel(x_ref, sh_ref, sc_ref, g_ref, w_ref, b_ref, k_ref, v_ref):
    a = _rms(x_ref[0], NORM_EPS) * g_ref[...]
    a = a * (1.0 + sc_ref[0]) + sh_ref[0]
    p = _dot(a.astype(BF16), w_ref[...]) + b_ref[...]
    nk = k_ref.shape[2]
    k_ref[0] = p[:, :nk].astype(k_ref.dtype)
    v_ref[0] = p[:, nk:].astype(v_ref.dtype)


def _ctxproj_call(x, shift, scale, g, w_bf16, b, nk):
    bsz, s, d = x.shape
    n = w_bf16.shape[1]
    vec = pl.BlockSpec((1, 1, d), lambda bi: (0, 0, 0))
    return pl.pallas_call(
        _ctxproj_kernel,
        grid=(bsz,),
        in_specs=[pl.BlockSpec((1, s, d), lambda bi: (bi, 0, 0)), vec, vec,
                  _const_spec((1, d)), _const_spec((d, n)), _const_spec((1, n))],
        out_specs=[pl.BlockSpec((1, s, nk), lambda bi: (bi, 0, 0)), pl.BlockSpec((1, s, n - nk), lambda bi: (bi, 0, 0))],
        out_shape=[jax.ShapeDtypeStruct((bsz, s, nk), BF16), jax.ShapeDtypeStruct((bsz, s, n - nk), BF16)],
        compiler_params=_params(1),
        name="ctxproj",
    )(x, shift, scale, g.reshape(1, d), w_bf16, b.reshape(1, n))


def _latproj_kernel(x_ref, xp_ref, xn_ref, sh_ref, sc_ref, g_ref, w_hbm, b_ref, cos_ref, sa_ref, sb_ref,
                    cw_ref, cb_ref, q_ref, k_ref, v_ref, u_ref, x0_ref, w_ref, *, n_att, q_scale, seq_len):
    @pl.when((pl.program_id(0) == 0) & (pl.program_id(1) == 0))
    def _():
        _stage_weights_bf16([(w_hbm.at[0], w_ref)])

    tm = x_ref.shape[1]
    halo = xp_ref.shape[1]
    rows = tm + 2 * halo
    xx = jnp.concatenate([xp_ref[0], x_ref[0], xn_ref[0]], axis=0)
    a = _rms(xx, NORM_EPS) * g_ref[...]
    a = a * (1.0 + sc_ref[0]) + sh_ref[0]

    ph = _dot(a.astype(BF16), w_ref[:, n_att:]) + b_ref[:, n_att:]
    t = lax.broadcasted_iota(jnp.int32, (rows, 1), 0) + (pl.program_id(1) * tm - halo)
    ph = jnp.where((t >= 0) & (t < seq_len), ph, 0.0)
    ncb = u_ref.shape[1]
    c = ncb * LANES

    def conv(s):
        blk = ph[:, s * c:(s + 1) * c]
        prev = pltpu.roll(blk, 1, 0)[halo:halo + tm]
        nxt = pltpu.roll(blk, rows - 1, 0)[halo:halo + tm]
        cw = cw_ref[:, s * c:(s + 1) * c]
        return prev * cw[0:1] + blk[halo:halo + tm] * cw[1:2] + nxt * cw[2:3] + cb_ref[:, s * c:(s + 1) * c]

    x0 = conv(0)
    u = conv(2) * conv(1)
    for cb in range(ncb):
        x0_ref[0, cb] = x0[:, cb * LANES:(cb + 1) * LANES]
        u_ref[0, cb] = u[:, cb * LANES:(cb + 1) * LANES]

    p = _dot(a[halo:halo + tm].astype(BF16), w_ref[:, :n_att]) + b_ref[:, :n_att]
    cos, sa, sb = cos_ref[...], sa_ref[...], sb_ref[...]
    qk = q_ref.shape[2]
    for o_ref, c_lo, scale in ((q_ref, 0, q_scale), (k_ref, qk, 1.0)):
        for c0 in range(0, qk, LANES):
            blk = p[:, c_lo + c0:c_lo + c0 + LANES]
            r = blk * cos + pltpu.roll(blk, 16, 1) * sa + pltpu.roll(blk, LANES - 16, 1) * sb
            o_ref[0, :, c0:c0 + LANES] = (r * scale).astype(o_ref.dtype)
    v_ref[0] = p[:, 2 * qk:].astype(v_ref.dtype)


def _latproj_call(x, shift, scale, g, w_all, b, rope_tabs, conv_w, conv_b, n_att, qk, q_scale, tm=512):
    bsz, s, d = x.shape
    n = w_all.shape[2]
    c = (n - n_att) // 3
    halo = SUBLANES
    nb = tm // halo
    last = s // halo - 1
    vec = pl.BlockSpec((1, 1, d), lambda bi, i: (bi, 0, 0))
    tab = pl.BlockSpec((tm, LANES), lambda bi, i: (i, 0))
    row = lambda w: pl.BlockSpec((1, tm, w), lambda bi, i: (bi, i, 0))
    cblk = pl.BlockSpec((1, c // LANES, tm, LANES), lambda bi, i: (bi, 0, i, 0))
    return pl.pallas_call(
        functools.partial(_latproj_kernel, n_att=n_att, q_scale=q_scale, seq_len=s),
        grid=(bsz, s // tm),
        in_specs=[row(d),
                  pl.BlockSpec((1, halo, d), lambda bi, i: (bi, jnp.maximum(i * nb - 1, 0), 0)),
                  pl.BlockSpec((1, halo, d), lambda bi, i: (bi, jnp.minimum((i + 1) * nb, last), 0)),
                  vec, vec, _const_spec((1, d)), pl.BlockSpec(memory_space=pl.ANY), _const_spec((1, n)), tab, tab, tab,
                  _const_spec(conv_w.shape), _const_spec((1, 3 * c))],
        out_specs=[row(qk), row(qk), row(n_att - 2 * qk), cblk, cblk],
        out_shape=[jax.ShapeDtypeStruct((bsz, s, qk), BF16), jax.ShapeDtypeStruct((bsz, s, qk), BF16),
                   jax.ShapeDtypeStruct((bsz, s, n_att - 2 * qk), BF16),
                   jax.ShapeDtypeStruct((bsz, c // LANES, s, LANES), F32),
                   jax.ShapeDtypeStruct((bsz, c // LANES, s, LANES), F32)],
        scratch_shapes=[pltpu.VMEM((d, n), BF16)],
        compiler_params=_params(2),
        name="latproj",
    )(x, x, x, shift, scale, g.reshape(1, d), w_all, b.reshape(1, n), *rope_tabs, conv_w, conv_b.reshape(1, 3 * c))


def _rope_tables(seq_len):
    axis_dim = ATT_QK_DIM // 2
    n_freq = axis_dim // 2
    inv = (ROPE_BASE ** (-np.arange(n_freq, dtype=np.float32) / n_freq)).astype(np.float32)
    t = np.arange(seq_len)
    row, col = t // GRID_W, t % GRID_W
    jj = np.arange(128) % ATT_QK_DIM
    is_col = (jj // axis_dim) == 1
    second = ((jj % axis_dim) >= n_freq)[None, :]
    pos = np.where(is_col[None, :], col[:, None], row[:, None]).astype(np.float32)
    ang = (pos * inv[jj % n_freq][None, :]).astype(np.float64)
    cos, sin = np.cos(ang), np.sin(ang)
    as32 = lambda m: jnp.asarray(m, dtype=F32)
    return as32(cos), as32(np.where(second, sin, 0.0)), as32(np.where(second, 0.0, -sin))


def _attn_kernel(lam_ref, q_ref, kc_ref, vc_ref, kl_ref, vl_ref, g_ref, o_ref, *, tk, lam_init):
    q = q_ref[0]
    tq = q.shape[0]
    lane = lax.broadcasted_iota(jnp.int32, q.shape, 1)
    zero = jnp.zeros_like(q)
    q2 = jnp.concatenate([jnp.where(lane < ATT_QK_DIM, q, zero), jnp.where(lane >= ATT_QK_DIM, q, zero)], axis=0)

    def chunk(k, v, m, acc):
        s = lax.dot_general(q2, k, (((1,), (1,)), ((), ())), preferred_element_type=F32)
        rowmax = jnp.max(s, axis=1, keepdims=True)
        m_new = rowmax if m is None else jnp.maximum(m, rowmax)
        p = jnp.exp2(s - m_new).astype(BF16)
        pv = _dot(p, jnp.concatenate([v, jnp.ones_like(v)], axis=1))
        return m_new, (pv if acc is None else jnp.exp2(m - m_new) * acc + pv)

    m, acc = chunk(kc_ref[0], vc_ref[0], None, None)
    for j in range(kl_ref.shape[1] // tk):
        m, acc = chunk(kl_ref[0, j * tk:(j + 1) * tk, :], vl_ref[0, j * tk:(j + 1) * tk, :], m, acc)

    lamv = lam_ref[...]
    lam = (jnp.exp(jnp.sum(lamv[0:1] * lamv[1:2], axis=1, keepdims=True))
           - jnp.exp(jnp.sum(lamv[2:3] * lamv[3:4], axis=1, keepdims=True)) + lam_init)
    o_all = acc[:, :ATT_V_DIM] / acc[:, ATT_V_DIM:]
    o = o_all[:tq] - lam * o_all[tq:]
    o = _rms(o, SUBLN_EPS) * g_ref[...] * (1.0 - lam_init)
    o_ref[0] = o.astype(o_ref.dtype)


def _attn_call(q, kc, vc, kl, vl, lamv, g, lam_init, tq=1024, tk=256):
    bsz, l, width = q.shape
    heads = width // ATT_V_DIM
    lc = kc.shape[1]
    hd = ATT_V_DIM
    tq = min(tq, l)
    return pl.pallas_call(
        functools.partial(_attn_kernel, tk=tk, lam_init=lam_init),
        grid=(bsz, heads, l // tq),
        in_specs=[
            _const_spec(lamv.shape),
            pl.BlockSpec((1, tq, hd), lambda b, h, i: (b, i, h)),
            pl.BlockSpec((1, lc, hd), lambda b, h, i: (b, 0, h)),
            pl.BlockSpec((1, lc, hd), lambda b, h, i: (b, 0, h)),
            pl.BlockSpec((1, l, hd), lambda b, h, i: (b, 0, h)),
            pl.BlockSpec((1, l, hd), lambda b, h, i: (b, 0, h)),
            _const_spec((1, hd)),
        ],
        out_specs=pl.BlockSpec((1, tq, hd), lambda b, h, i: (b, i, h)),
        out_shape=jax.ShapeDtypeStruct((bsz, l, width), BF16),
        compiler_params=_params(3),
        name="diffattn",
    )(lamv, q, kc, vc, kl, vl, g.reshape(1, hd))


def _gather_minor(refs, n_rows):
    cols = []
    for j in range(SUBLANES):
        parts = [r[pl.ds(j, n_rows, stride=SUBLANES), :] for r in refs]
        cols.append(parts[0] if len(parts) == 1 else jnp.concatenate(parts, axis=0))
    return jnp.concatenate(cols, axis=1)


def _slab_gather(src_ref, jb, slab_ref):
    rows = src_ref.shape[0]
    slab_ref[...] = src_ref[:, jb].reshape(rows * SUBLANES, LANES)
    return _gather_minor([slab_ref], rows)


def _slab_scatter(val, slab_ref, dst_ref, jb):
    rows = dst_ref.shape[0]
    for j in range(SUBLANES):
        slab_ref[pl.ds(j, rows, stride=SUBLANES), :] = val[:, j * LANES:(j + 1) * LANES]
    dst_ref[:, jb] = slab_ref[...].reshape(rows, SUBLANES, LANES)


def _spectrum_store(val, spec_ref, jb):
    rows = val.shape[0]
    for j in range(SUBLANES):
        spec_ref[jb, pl.ds(j, rows, stride=SUBLANES), :] = val[:, j * LANES:(j + 1) * LANES]


def _spectrum_blocks(spec_ref, k0):
    x = spec_ref[:, pl.ds(pl.multiple_of(k0 * SUBLANES, SUBLANES * SUBLANES), SUBLANES * SUBLANES), :]
    return jnp.stack([x[:, i * SUBLANES:(i + 1) * SUBLANES, :].reshape(FFT_N2, LANES) for i in range(SUBLANES)], axis=0)


def _spectrum_put(spec_ref, k0, val):
    nb = FFT_N2 // SUBLANES
    v = val.reshape(SUBLANES, nb, SUBLANES, LANES)
    x = jnp.stack([v[:, jb].reshape(SUBLANES * SUBLANES, LANES) for jb in range(nb)], axis=0)
    spec_ref[:, pl.ds(pl.multiple_of(k0 * SUBLANES, SUBLANES * SUBLANES), SUBLANES * SUBLANES), :] = x


def _twiddle_block(twc_ref, twf_ref, kb):
    n2 = FFT_N2
    rows = pl.ds(pl.multiple_of(kb * n2, n2), n2)
    cr, ci = twc_ref[0, rows, :][None], twc_ref[1, rows, :][None]
    fr, fi = twf_ref[0].reshape(SUBLANES, n2, LANES), twf_ref[1].reshape(SUBLANES, n2, LANES)
    return cr * fr - ci * fi, cr * fi + ci * fr


def _filter_fft_kernel(z_ref, w1_ref, b1_ref, f1_ref, w2_ref, b2_ref, f2_ref, w3f_ref, w3b_ref, dl_ref,
                       fr_ref, g_ref, twc_ref, twf_ref, kr_ref, ki_ref, h2_ref, hs_ref, a_ref, *, tm):
    seq_len = z_ref.shape[0]
    n1 = a_ref.shape[2] // SUBLANES
    n1h = hs_ref.shape[2] // SUBLANES
    n2 = FFT_N2
    nb = n2 // SUBLANES
    w = SUBLANES * LANES

    @pl.when(pl.program_id(0) == 0)
    def _():
        def features(i, _):
            r = pl.ds(pl.multiple_of(i * tm, tm), tm)
            h1 = jnp.sin(f1_ref[...] * (_dot3f(z_ref[r, :], w1_ref[...]) + b1_ref[...]))
            h2_ref[r, :] = jnp.sin(f2_ref[...] * (_dot3f(h1, w2_ref[...]) + b2_ref[...]))
            return 0

        lax.fori_loop(0, seq_len // tm, features, 0)

    def taps(i, _):
        h2 = h2_ref[pl.ds(pl.multiple_of(i * tm, tm), tm), :]
        row = lax.broadcasted_iota(jnp.int32, (tm, LANES), 0) + i * tm
        decay = jnp.exp(-(row.astype(F32) * (1.0 / (seq_len - 1))) * dl_ref[...])
        for t, w3_ref in enumerate((w3f_ref, w3b_ref)):
            h = _dot3f(h2, w3_ref[...]) * decay
            if t == 1:
                h = jnp.where(row == 0, 0.0, h)
            h4 = h.reshape(tm // n2, nb, SUBLANES, LANES)
            for jb in range(nb):
                hs_ref[t, jb, pl.ds(pl.multiple_of(i * (tm // nb), tm // nb), tm // nb), :] = (
                    h4[:, jb].reshape(tm // nb, LANES))
        return 0

    lax.fori_loop(0, seq_len // tm, taps, 0)

    for jb in range(nb):
        z = jnp.concatenate([_gather_minor([hs_ref.at[t, jb]], n1h) for t in range(2)], axis=1)
        out = _dft_dot(fr_ref[...], z)
        for t in range(4):
            _spectrum_store(out[(t % 2) * n1:(t % 2 + 1) * n1, (t // 2) * w:(t // 2 + 1) * w], a_ref.at[t], jb)

    def body(kb, _):
        k0 = pl.multiple_of(kb * SUBLANES, SUBLANES)
        twr, twi = _twiddle_block(twc_ref, twf_ref, kb)
        a = [_spectrum_blocks(a_ref.at[t], k0) for t in range(4)]
        cols = []
        for t in range(2):
            ar, ai = a[2 * t], a[2 * t + 1]
            sr, si = ar * twr - ai * twi, ar * twi + ai * twr
            cols += [jnp.concatenate([sr[i], si[i]], axis=0) for i in range(SUBLANES)]
        x = _dft_dot(g_ref[...], jnp.concatenate(cols, axis=1))
        for i in range(SUBLANES):
            xf = x[:, i * LANES:(i + 1) * LANES]
            xb = x[:, (SUBLANES + i) * LANES:(SUBLANES + i + 1) * LANES]
            rows = pl.ds(pl.multiple_of((k0 + i) * n2, n2), n2)
            kr_ref[rows, :] = xf[:n2] + xb[:n2]
            ki_ref[rows, :] = xf[n2:] - xb[n2:]
        return 0

    lax.fori_loop(0, n1 // SUBLANES, body, 0, unroll=2)


def _filter_fft_call(dft, z, w1, b1, f1, w2, b2, f2, w3, absdelta, tm=512):
    seq_len, hid = z.shape
    c = w3.shape[1] // 2
    n, n1, n1h = dft["n"], dft["n1"], dft["n1h"]
    nb = FFT_N2 // SUBLANES
    ncb = c // LANES
    fr, g = dft["f_real"].astype(BF16), dft["g_fwd"].astype(BF16)
    twc, twf = dft["tw_coarse"], dft["tw_fine"]
    vec = _const_spec((1, hid))
    w3_blk = lambda off: pl.BlockSpec((hid, LANES), lambda i, off=off: (0, i + off))
    out = pl.BlockSpec((None, n, LANES), lambda i: (i, 0, 0))
    return pl.pallas_call(
        functools.partial(_filter_fft_kernel, tm=tm),
        grid=(ncb,),
        in_specs=[_const_spec(z.shape), _const_spec((hid, hid)), vec, vec, _const_spec((hid, hid)), vec, vec,
                  w3_blk(0), w3_blk(ncb), pl.BlockSpec((1, LANES), lambda i: (0, i)),
                  _const_spec(fr.shape), _const_spec(g.shape), _const_spec(twc.shape), _const_spec(twf.shape)],
        out_specs=[out, out],
        out_shape=[jax.ShapeDtypeStruct((ncb, n, LANES), F32)] * 2,
        scratch_shapes=[pltpu.VMEM((seq_len, hid), F32), pltpu.VMEM((2, nb, n1h * SUBLANES, LANES), F32),
                        pltpu.VMEM((4, nb, n1 * SUBLANES, LANES), F32)],
        compiler_params=_params(1),
        name="hyfilter_fft",
    )(z, w1, b1.reshape(1, hid), f1.reshape(1, hid), w2, b2.reshape(1, hid), f2.reshape(1, hid), w3, w3, absdelta,
      fr, g, twc, twf)


def _hyena_conv_kernel(fc_ref, g_ref, gc_ref, e_ref, twc_ref, twf_ref, kr_ref, ki_ref, u_ref, x0_ref, bias_ref, o_ref,
                       slab_in, a_ref, *, inv_n):
    n1 = a_ref.shape[2] // SUBLANES
    n1h = u_ref.shape[1]
    n2 = FFT_N2
    nb = n2 // SUBLANES

    for jb in range(nb):
        z = jnp.concatenate([_slab_gather(u_ref.at[c], jb, slab_in.at[c]) for c in range(2)], axis=0)
        out = _dft_dot(fc_ref[...], z)
        for c in range(2):
            _spectrum_store(out[c * n1:(c + 1) * n1], a_ref.at[c], jb)

    def body(kb, _):
        k0 = pl.multiple_of(kb * SUBLANES, SUBLANES)
        rows = pl.ds(pl.multiple_of(kb * (SUBLANES * n2), SUBLANES * n2), SUBLANES * n2)
        blocks = lambda ref: ref[rows, :].reshape(SUBLANES, n2, LANES)
        (twr, twi), kr, ki = _twiddle_block(twc_ref, twf_ref, kb), blocks(kr_ref), blocks(ki_ref)
        ar, ai = _spectrum_blocks(a_ref.at[0], k0), _spectrum_blocks(a_ref.at[1], k0)
        sr, si = ar * twr - ai * twi, ar * twi + ai * twr
        lanes = lambda re, im: jnp.concatenate([jnp.concatenate([re[i], im[i]], axis=0) for i in range(SUBLANES)], axis=1)
        x = _dft_dot(g_ref[...], lanes(sr, si))
        unl = lambda v, lo: jnp.stack([v[lo:lo + n2, i * LANES:(i + 1) * LANES] for i in range(SUBLANES)], axis=0)
        xr, xi = unl(x, 0), unl(x, n2)
        t = _dft_dot(gc_ref[...], lanes(xr * kr - xi * ki, xr * ki + xi * kr))
        tr, ti = unl(t, 0), unl(t, n2)
        _spectrum_put(a_ref.at[0], k0, tr * twr + ti * twi)
        _spectrum_put(a_ref.at[1], k0, ti * twr - tr * twi)
        return 0

    lax.fori_loop(0, n1 // SUBLANES, body, 0, unroll=2)

    for jb in range(nb):
        t = _gather_minor([a_ref.at[0, jb], a_ref.at[1, jb]], n1)
        y = _dft_dot(e_ref[...], t) * inv_n
        for c in range(2):
            _slab_scatter(y[c * n1h:(c + 1) * n1h], slab_in.at[c], o_ref.at[c], jb)
    o_ref[...] = x0_ref[...] * (o_ref[...] + u_ref[...] * bias_ref[...])


def _dft_blocks(seq_len):
    n = 2 * seq_len
    n2 = FFT_N2
    n1 = n // n2
    n1h = n1 // 2
    k1 = np.arange(n1)[:, None].astype(np.float64)
    a = 2.0 * np.pi * k1 * np.arange(n1h)[None, :] / n1
    fr, fi = np.cos(a), -np.sin(a)
    f_cplx = np.block([[fr, -fi], [fi, fr]])
    f_real = np.concatenate([fr, fi], axis=0)
    e_cplx = np.block([[fr.T, fi.T], [-fi.T, fr.T]])
    b = 2.0 * np.pi * np.arange(n2)[:, None] * np.arange(n2)[None, :] / n2
    gr, gi = np.cos(b), -np.sin(b)
    g_fwd = np.block([[gr, -gi], [gi, gr]])
    g_inv = np.block([[gr, gi], [-gi, gr]])
    def table(k1s):
        ang = (2.0 * np.pi / n) * (k1s[:, None] * np.arange(n2)[None, :]).reshape(-1, 1)
        return np.broadcast_to(np.stack([np.cos(ang), -np.sin(ang)]), (2, k1s.size * n2, LANES))
    as32 = lambda m: jnp.asarray(np.ascontiguousarray(m), dtype=F32)
    return dict(n=n, n1=n1, n1h=n1h, f_cplx=as32(f_cplx), f_real=as32(f_real), e_cplx=as32(e_cplx),
                g_fwd=as32(g_fwd), g_inv=as32(g_inv), tw_coarse=as32(table(np.arange(0, n1, SUBLANES))),
                tw_fine=as32(table(np.arange(SUBLANES))))


def _hyena_call(u, x0, filt, hy_bias):
    bsz, ncb, seq_len, _ = u.shape
    c = ncb * LANES
    n2 = FFT_N2
    dft = _dft_blocks(seq_len)
    n, n1, n1h = dft["n"], dft["n1"], dft["n1h"]
    pairs = bsz // 2

    w1, b1, f1, w2, b2, f2, w3 = filt
    hid = w2.shape[0]
    bands = (HY_POS_EMB - 1) // 2
    t = np.linspace(0.0, 1.0, seq_len)[:, None]
    freqs = np.linspace(1e-4, bands - 1, bands)
    ang = (2.0 * math.pi / seq_len) * np.arange(seq_len)[:, None] * freqs[None, :]
    z = jnp.asarray(np.concatenate([t, np.cos(ang), -np.sin(ang), np.zeros((seq_len, hid - HY_POS_EMB))], axis=-1),
                    dtype=F32)
    w1p = jnp.concatenate([w1, jnp.zeros((hid - HY_POS_EMB, hid), F32)], axis=0)
    max_decay = math.log(HY_DECAY_TARGET) / HY_FAST_DECAY_PCT
    min_decay = math.log(HY_DECAY_TARGET) / HY_SLOW_DECAY_PCT
    absdelta = jnp.asarray(np.abs(np.linspace(min_decay, max_decay, c))[None, :], dtype=F32)
    nb = n2 // SUBLANES
    kr, ki = _filter_fft_call(dft, z, w1p, b1, f1, w2, b2, f2, w3, absdelta)

    view = lambda a: a.reshape(pairs, 2, ncb, n1h, nb, SUBLANES, LANES)
    pair_spec = pl.BlockSpec((None, 2, None, n1h, nb, SUBLANES, LANES), lambda cb, p: (p, 0, cb, 0, 0, 0, 0))
    k_spec = pl.BlockSpec((None, n, LANES), lambda cb, p: (cb, 0, 0))
    twc, twf = dft["tw_coarse"], dft["tw_fine"]
    fc, g, gc, e = (dft[k].astype(BF16) for k in ("f_cplx", "g_fwd", "g_inv", "e_cplx"))
    hy = pl.pallas_call(
        functools.partial(_hyena_conv_kernel, inv_n=1.0 / n),
        grid=(ncb, pairs),
        in_specs=[_const_spec(fc.shape), _const_spec(g.shape), _const_spec(gc.shape), _const_spec(e.shape),
                  _const_spec(twc.shape), _const_spec(twf.shape), k_spec, k_spec, pair_spec, pair_spec,
                  pl.BlockSpec((1, LANES), lambda cb, p: (0, cb))],
        out_specs=pair_spec,
        out_shape=jax.ShapeDtypeStruct((pairs, 2, ncb, n1h, nb, SUBLANES, LANES), F32),
        scratch_shapes=[pltpu.VMEM((2, n1h * SUBLANES, LANES), F32), pltpu.VMEM((2, nb, n1 * SUBLANES, LANES), F32)],
        compiler_params=_params(2),
        name="hyconv",
    )(fc, g, gc, e, twc, twf, kr, ki, view(u), view(x0), hy_bias.reshape(1, c))
    return hy.reshape(bsz, ncb, seq_len, LANES)


def _stage_weights_bf16(pairs):
    for src, dst in pairs:
        n_rows, n_cols = src.shape
        rows = 1 << ((WEIGHT_STAGE_BYTES // (4 * n_cols)).bit_length() - 1)
        n_chunks = n_rows // rows
        assert rows % (2 * SUBLANES) == 0 and n_chunks * rows == n_rows

        def run(stage, sem, src=src, dst=dst, rows=rows, n_chunks=n_chunks):
            copy = lambda k: pltpu.make_async_copy(src.at[pl.ds(k * rows, rows), :], stage.at[k % 2], sem.at[k % 2])
            copy(0).start()
            for k in range(n_chunks):
                if k + 1 < n_chunks:
                    copy(k + 1).start()
                copy(k).wait()
                dst[pl.ds(k * rows, rows), :] = stage[k % 2].astype(BF16)

        pl.run_scoped(run, pltpu.VMEM((2, rows, n_cols), F32), pltpu.SemaphoreType.DMA((2,)))


def _mlp_tail(h, g2, shift, scale, gate, w1_ref, w2_ref, fc):
    a = (_rms(h, NORM_EPS) * g2) * (1.0 + scale) + shift
    a = a.astype(BF16)
    acc = None
    for c0 in range(0, w1_ref.shape[1], fc):
        hid = jnp.maximum(_dot(a, w1_ref[:, c0:c0 + fc]), 0.0)
        part = _dot((hid * hid).astype(BF16), w2_ref[c0:c0 + fc, :])
        acc = part if acc is None else acc + part
    return h + gate * acc


def _mixmlp_kernel(x_ref, att_ref, hy_ref, mod_ref, wo_hbm, bo_ref, g2_ref, w1_hbm, w2_hbm, o_ref,
                   wo_s, w1_s, w2_s, *, fc, layer):
    @pl.when((pl.program_id(0) == 0) & (pl.program_id(1) == 0))
    def _():
        _stage_weights_bf16([(wo_hbm.at[layer // 2], wo_s), (w1_hbm.at[layer], w1_s), (w2_hbm.at[layer], w2_s)])

    d = x_ref.shape[2]
    wa = att_ref.shape[2]
    mod = mod_ref[0]
    hy = jnp.concatenate([hy_ref[0, cb] for cb in range(hy_ref.shape[1])], axis=1).astype(BF16)
    y = _dot(att_ref[0], wo_s[:wa, :]) + _dot(hy, wo_s[wa:, :]) + bo_ref[...]
    h = x_ref[0] + mod[:, 2 * d:3 * d] * y
    o_ref[0] = _mlp_tail(h, g2_ref[...], mod[:, 3 * d:4 * d], mod[:, 4 * d:5 * d], mod[:, 5 * d:6 * d],
                         w1_s, w2_s, fc)


def _mixmlp_call(x, att, hy, mod, w_out, b_out, g2, w1, w2, layer, tm=512, fc=1024):
    bsz, l, d = x.shape
    wa = att.shape[2]
    dff = w1.shape[2]
    row = lambda w: pl.BlockSpec((1, tm, w), lambda b, i: (b, i, 0))
    hbm = pl.BlockSpec(memory_space=pl.ANY)
    return pl.pallas_call(
        functools.partial(_mixmlp_kernel, fc=fc, layer=layer),
        grid=(bsz, l // tm),
        in_specs=[row(d), row(wa), pl.BlockSpec((1, hy.shape[1], tm, LANES), lambda b, i: (b, 0, i, 0)),
                  pl.BlockSpec((1, 1, N_MOD * d), lambda b, i: (b, 0, 0)),
                  hbm, _const_spec((1, d)), _const_spec((1, d)), hbm, hbm],
        out_specs=row(d),
        out_shape=jax.ShapeDtypeStruct((bsz, l, d), F32),
        scratch_shapes=[pltpu.VMEM((d, d), BF16), pltpu.VMEM((d, dff), BF16), pltpu.VMEM((dff, d), BF16)],
        compiler_params=_params(2),
        name="mixmlp",
    )(x, att, hy, mod, w_out, b_out.reshape(1, d), g2.reshape(1, d), w1, w2)


def _poolmlp_kernel(h_ref, hp_ref, hn_ref, mod_ref, g1_ref, pw_ref, ps_ref, g2_ref, w1_hbm, w2_hbm, gf_ref, o_ref,
                    w1_s, w2_s, *, fc, seq_len, layer):
    @pl.when((pl.program_id(0) == 0) & (pl.program_id(1) == 0))
    def _():
        _stage_weights_bf16([(w1_hbm.at[layer], w1_s), (w2_hbm.at[layer], w2_s)])

    tm, d = h_ref.shape[1:]
    halo = POOL_HALO
    rows = tm + 2 * halo
    mod = mod_ref[0]
    h = h_ref[0]
    hx = jnp.concatenate([hp_ref[0], h, hn_ref[0]], axis=0)
    a = (_rms(hx, NORM_EPS) * g1_ref[...]) * (1.0 + mod[:, d:2 * d]) + mod[:, 0:d]
    t = lax.broadcasted_iota(jnp.int32, (rows, 1), 0) + (pl.program_id(1) * tm - halo)
    a = jnp.where((t >= 0) & (t < seq_len), a, 0.0)
    tc = t[halo:halo + tm]
    gd = d // len(POOL_WINDOWS)
    ys = []
    for g, win in enumerate(POOL_WINDOWS):
        ag = a[:, g * gd:(g + 1) * gd]
        f, m = ag, 1
        while 2 * m <= win // 2:
            f = f + pltpu.roll(f, rows - m, 0)
            m *= 2
        s = pltpu.roll(f, win // 2, 0) + f
        cnt = (jnp.minimum(tc + (win - win // 2), seq_len) - jnp.maximum(tc - win // 2, 0)).astype(F32)
        dlt = s[halo:halo + tm] / cnt - ag[halo:halo + tm]
        ys.append(_dot(dlt.astype(BF16), pw_ref[g]))
    y = jnp.concatenate(ys, axis=1) * ps_ref[...]
    h1 = h + mod[:, 2 * d:3 * d] * y
    h2 = _mlp_tail(h1, g2_ref[...], mod[:, 3 * d:4 * d], mod[:, 4 * d:5 * d], mod[:, 5 * d:6 * d], w1_s, w2_s, fc)
    o_ref[0] = _rms(h2, NORM_EPS) * gf_ref[...]


def _poolmlp_call(h, mod, g1, pool_w, pool_scale, g2, w1, w2, gf, layer, tm=512, fc=1024):
    bsz, l, d = h.shape
    dff = w1.shape[2]
    halo = POOL_HALO
    nb = tm // halo
    last = l // halo - 1
    row = pl.BlockSpec((1, tm, d), lambda b, i: (b, i, 0))
    hbm = pl.BlockSpec(memory_space=pl.ANY)
    return pl.pallas_call(
        functools.partial(_poolmlp_kernel, fc=fc, seq_len=l, layer=layer),
        grid=(bsz, l // tm),
        in_specs=[row,
                  pl.BlockSpec((1, halo, d), lambda b, i: (b, jnp.maximum(i * nb - 1, 0), 0)),
                  pl.BlockSpec((1, halo, d), lambda b, i: (b, jnp.minimum((i + 1) * nb, last), 0)),
                  pl.BlockSpec((1, 1, N_MOD * d), lambda b, i: (b, 0, 0)),
                  _const_spec((1, d)), _const_spec(pool_w.shape), _const_spec((1, d)), _const_spec((1, d)),
                  hbm, hbm, _const_spec((1, d))],
        out_specs=row,
        out_shape=jax.ShapeDtypeStruct((bsz, l, d), F32),
        scratch_shapes=[pltpu.VMEM((d, dff), BF16), pltpu.VMEM((dff, d), BF16)],
        compiler_params=_params(2),
        name="poolmlp",
    )(h, h, h, mod, g1.reshape(1, d), pool_w, pool_scale.reshape(1, d), g2.reshape(1, d), w1, w2, gf.reshape(1, d))


def kernel(x, c, ctx, c_ctx, ada_w, ada_b, norm1_g, norm2_g, mix_w_in, mix_b_in, mix_w_out, mix_b_out, lam_q1, lam_k1, lam_q2, lam_k2, subln_g, hy_conv_w, hy_conv_b, hy_pos_w1, hy_pos_b1, hy_freq1, hy_pos_w2, hy_pos_b2, hy_freq2, hy_pos_w3, hy_bias, pool_w, pool_scale, mlp_w1, mlp_w2, final_g):
    bsz, seq_len, d = x.shape
    depth = ada_w.shape[0]
    assert depth == 2 and bsz % 2 == 0 and bsz < MOD_ROWS and seq_len % GRID_W == 0
    att_w = ATT_HEADS * ATT_V_DIM
    q_cols = k_cols = ATT_HEADS * 2 * ATT_QK_DIM
    kv_start, hy_start = q_cols, q_cols + k_cols + att_w
    in_cols = mix_w_in.shape[2]

    cv = jnp.concatenate([c, c_ctx[None, :], jnp.zeros((MOD_ROWS - bsz - 1, d), F32)], axis=0)
    mod = _mod_call(cv, ada_w, ada_b)
    mod_l = [mod[i, :bsz].reshape(bsz, 1, N_MOD * d) for i in range(depth)]

    lam_init = 0.8 - 0.6 * math.exp(-0.3 * 0)
    assert k_cols == q_cols and in_cols - hy_start == 3 * (d - att_w)
    q, k, v, u, x0 = _latproj_call(
        x, mod_l[0][:, :, 0:d], mod_l[0][:, :, d:2 * d], norm1_g[0], mix_w_in, mix_b_in[0], _rope_tables(seq_len),
        hy_conv_w[0], hy_conv_b[0], n_att=hy_start, qk=q_cols, q_scale=ATT_QK_DIM ** -0.5 * math.log2(math.e))
    mod_c = mod[0, bsz:bsz + 1].reshape(1, 1, N_MOD * d)
    kc, vc = _ctxproj_call(ctx, mod_c[:, :, 0:d], mod_c[:, :, d:2 * d], norm1_g[0],
                           mix_w_in[0, :, kv_start:hy_start].astype(BF16), mix_b_in[0, kv_start:hy_start], k_cols)
    lamv = jnp.stack([lam_q1[0], lam_k1[0], lam_q2[0], lam_k2[0]], axis=0)
    att = _attn_call(q, kc, vc, k, v, lamv, subln_g[0], lam_init)
    filt = (hy_pos_w1[0], hy_pos_b1[0], hy_freq1[0], hy_pos_w2[0], hy_pos_b2[0], hy_freq2[0], hy_pos_w3[0])
    hy = _hyena_call(u, x0, filt, hy_bias[0])
    h = _mixmlp_call(x, att, hy, mod_l[0], mix_w_out, mix_b_out[0], norm2_g[0], mlp_w1, mlp_w2, layer=0)

    return _poolmlp_call(h, mod_l[1], norm1_g[1], pool_w[0].astype(BF16), pool_scale[0], norm2_g[1],
                         mlp_w1, mlp_w2, final_g, layer=1)
```

```python
import functools
import math

import numpy as np
import jax
import jax.numpy as jnp
from jax import lax
from jax.experimental import pallas as pl
from jax.experimental.pallas import tpu as pltpu

F32 = jnp.float32
BF16 = jnp.bfloat16

GRID_W = 64
N_MOD = 6
ATT_HEADS = 4
ATT_V_DIM = 128
ATT_QK_DIM = 64
ROPE_BASE = 10000.0
HY_POS_EMB = 33
HY_DECAY_TARGET = 1e-2
HY_FAST_DECAY_PCT = 0.3
HY_SLOW_DECAY_PCT = 1.5
POOL_WINDOWS = (2, 4, 8, 16)
NORM_EPS = 1e-6
SUBLN_EPS = 1e-5
SUBLANES = 8
LANES = 128
FFT_N2 = 64
POOL_HALO = 8
WEIGHT_STAGE_BYTES = 2 * 1024 * 1024
ROW_BLOCK = 1024
ROW_SUBTILES = 2
MOD_ROWS = 8

VMEM_LIMIT_BYTES = 56 * 1024 * 1024


def _params(n_grid_dims):
    return pltpu.CompilerParams(
        dimension_semantics=("arbitrary",) * n_grid_dims,
        vmem_limit_bytes=VMEM_LIMIT_BYTES,
    )


def _const_spec(shape):
    nd = len(shape)
    return pl.BlockSpec(shape, lambda *_: (0,) * nd, pipeline_mode=pl.Buffered(1))


def _split_bf16(a):
    hi = a.astype(BF16)
    lo = (a - hi.astype(F32)).astype(BF16)
    return hi, lo


def _dot(a, b):
    return jnp.dot(a, b, preferred_element_type=F32)


def _dot3(a_hi, a_lo, b_hi, b_lo):
    return _dot(a_hi, b_hi) + (_dot(a_lo, b_hi) + _dot(a_hi, b_lo))


def _dot3f(a, b):
    a_hi, a_lo = _split_bf16(a)
    b_hi, b_lo = _split_bf16(b)
    return _dot3(a_hi, a_lo, b_hi, b_lo)


def _dft_dot(a_bf16, b):
    return _dot(a_bf16, b.astype(BF16))


def _rms(x, eps):
    return x * lax.rsqrt(jnp.mean(x * x, axis=-1, keepdims=True) + eps)


def _mod_kernel(cv_ref, w_ref, b_ref, o_ref):
    cv = cv_ref[...]
    s = cv / (1.0 + jnp.exp(-cv))
    o_ref[0] = _dot3f(s, w_ref[0]) + b_ref[0]


def _mod_call(cv, ada_w, ada_b, tn=1536):
    depth, d, n = ada_w.shape
    return pl.pallas_call(
        _mod_kernel,
        grid=(depth, n // tn),
        in_specs=[
            pl.BlockSpec((MOD_ROWS, d), lambda i, j: (0, 0)),
            pl.BlockSpec((1, d, tn), lambda i, j: (i, 0, j)),
            pl.BlockSpec((1, 1, tn), lambda i, j: (i, 0, j)),
        ],
        out_specs=pl.BlockSpec((1, MOD_ROWS, tn), lambda i, j: (i, 0, j)),
        out_shape=jax.ShapeDtypeStruct((depth, MOD_ROWS, n), F32),
        compiler_params=_params(2),
        name="mod",
    )(cv, ada_w, ada_b.reshape(depth, 1, n))


def _ctxproj_kernel(x_ref, sh_ref, sc_ref, g_ref, w_ref, b_ref, k_ref, v_ref):
    a = _rms(x_ref[0], NORM_EPS) * g_ref[...]
    a = a * (1.0 + sc_ref[0]) + sh_ref[0]
    p = _dot(a.astype(BF16), w_ref[...]) + b_ref[...]
    nk = k_ref.shape[2]
    k_ref[0] = p[:, :nk].astype(k_ref.dtype)
    v_ref[0] = p[:, nk:].astype(v_ref.dtype)


def _ctxproj_call(x, shift, scale, g, w_bf16, b, nk):
    bsz, s, d = x.shape
    n = w_bf16.shape[1]
    vec = pl.BlockSpec((1, 1, d), lambda bi: (0, 0, 0))
    return pl.pallas_call(
        _ctxproj_kernel,
        grid=(bsz,),
        in_specs=[pl.BlockSpec((1, s, d), lambda bi: (bi, 0, 0)), vec, vec,
                  _const_spec((1, d)), _const_spec((d, n)), _const_spec((1, n))],
        out_specs=[pl.BlockSpec((1, s, nk), lambda bi: (bi, 0, 0)), pl.BlockSpec((1, s, n - nk), lambda bi: (bi, 0, 0))],
        out_shape=[jax.ShapeDtypeStruct((bsz, s, nk), BF16), jax.ShapeDtypeStruct((bsz, s, n - nk), BF16)],
        compiler_params=_params(1),
        name="ctxproj",
    )(x, shift, scale, g.reshape(1, d), w_bf16, b.reshape(1, n))


def _latproj_kernel(x_ref, xp_ref, xn_ref, sh_ref, sc_ref, g_ref, w_hbm, b_ref, cos_ref, sa_ref, sb_ref,
                    cw_ref, cb_ref, q_ref, k_ref, v_ref, u_ref, x0_ref, w_ref, *, n_att, q_scale, seq_len):
    @pl.when((pl.program_id(0) == 0) & (pl.program_id(1) == 0))
    def _():
        _stage_weights_bf16([(w_hbm.at[0], w_ref)])

    tm = x_ref.shape[1] // ROW_SUBTILES
    halo = xp_ref.shape[1]
    rows = tm + 2 * halo
    ncb = u_ref.shape[1]
    c = ncb * LANES
    qk = q_ref.shape[2]
    for sub in range(ROW_SUBTILES):
        r0 = sub * tm
        above = xp_ref[0] if sub == 0 else x_ref[0, r0 - halo:r0, :]
        below = xn_ref[0] if sub == ROW_SUBTILES - 1 else x_ref[0, r0 + tm:r0 + tm + halo, :]
        xx = jnp.concatenate([above, x_ref[0, r0:r0 + tm, :], below], axis=0)
        a = _rms(xx, NORM_EPS) * g_ref[...]
        a = a * (1.0 + sc_ref[0]) + sh_ref[0]

        ph = _dot(a.astype(BF16), w_ref[:, n_att:]) + b_ref[:, n_att:]
        t = lax.broadcasted_iota(jnp.int32, (rows, 1), 0) + (pl.program_id(1) * (tm * ROW_SUBTILES) + r0 - halo)
        ph = jnp.where((t >= 0) & (t < seq_len), ph, 0.0)

        def conv(s):
            blk = ph[:, s * c:(s + 1) * c]
            prev = pltpu.roll(blk, 1, 0)[halo:halo + tm]
            nxt = pltpu.roll(blk, rows - 1, 0)[halo:halo + tm]
            cw = cw_ref[:, s * c:(s + 1) * c]
            return prev * cw[0:1] + blk[halo:halo + tm] * cw[1:2] + nxt * cw[2:3] + cb_ref[:, s * c:(s + 1) * c]

        x0 = conv(0)
        u = conv(2) * conv(1)
        for cb in range(ncb):
            x0_ref[0, cb, r0:r0 + tm, :] = x0[:, cb * LANES:(cb + 1) * LANES]
            u_ref[0, cb, r0:r0 + tm, :] = u[:, cb * LANES:(cb + 1) * LANES]

        p = _dot(a[halo:halo + tm].astype(BF16), w_ref[:, :n_att]) + b_ref[:, :n_att]
        cos, sa, sb = (ref[r0:r0 + tm, :] for ref in (cos_ref, sa_ref, sb_ref))
        for o_ref, c_lo, scale in ((q_ref, 0, q_scale), (k_ref, qk, 1.0)):
            for c0 in range(0, qk, LANES):
                blk = p[:, c_lo + c0:c_lo + c0 + LANES]
                r = blk * cos + pltpu.roll(blk, 16, 1) * sa + pltpu.roll(blk, LANES - 16, 1) * sb
                o_ref[0, r0:r0 + tm, c0:c0 + LANES] = (r * scale).astype(o_ref.dtype)
        v_ref[0, r0:r0 + tm, :] = p[:, 2 * qk:].astype(v_ref.dtype)


def _latproj_call(x, shift, scale, g, w_all, b, rope_tabs, conv_w, conv_b, n_att, qk, q_scale, tm=ROW_BLOCK):
    bsz, s, d = x.shape
    n = w_all.shape[2]
    c = (n - n_att) // 3
    halo = SUBLANES
    nb = tm // halo
    last = s // halo - 1
    vec = pl.BlockSpec((1, 1, d), lambda bi, i: (bi, 0, 0))
    tab = pl.BlockSpec((tm, LANES), lambda bi, i: (i, 0))
    row = lambda w: pl.BlockSpec((1, tm, w), lambda bi, i: (bi, i, 0))
    cblk = pl.BlockSpec((1, c // LANES, tm, LANES), lambda bi, i: (bi, 0, i, 0))
    return pl.pallas_call(
        functools.partial(_latproj_kernel, n_att=n_att, q_scale=q_scale, seq_len=s),
        grid=(bsz, s // tm),
        in_specs=[row(d),
                  pl.BlockSpec((1, halo, d), lambda bi, i: (bi, jnp.maximum(i * nb - 1, 0), 0)),
                  pl.BlockSpec((1, halo, d), lambda bi, i: (bi, jnp.minimum((i + 1) * nb, last), 0)),
                  vec, vec, _const_spec((1, d)), pl.BlockSpec(memory_space=pl.ANY), _const_spec((1, n)), tab, tab, tab,
                  _const_spec(conv_w.shape), _const_spec((1, 3 * c))],
        out_specs=[row(qk), row(qk), row(n_att - 2 * qk), cblk, cblk],
        out_shape=[jax.ShapeDtypeStruct((bsz, s, qk), BF16), jax.ShapeDtypeStruct((bsz, s, qk), BF16),
                   jax.ShapeDtypeStruct((bsz, s, n_att - 2 * qk), BF16),
                   jax.ShapeDtypeStruct((bsz, c // LANES, s, LANES), F32),
                   jax.ShapeDtypeStruct((bsz, c // LANES, s, LANES), F32)],
        scratch_shapes=[pltpu.VMEM((d, n), BF16)],
        compiler_params=_params(2),
        name="latproj",
    )(x, x, x, shift, scale, g.reshape(1, d), w_all, b.reshape(1, n), *rope_tabs, conv_w, conv_b.reshape(1, 3 * c))


def _rope_tables(seq_len):
    axis_dim = ATT_QK_DIM // 2
    n_freq = axis_dim // 2
    inv = (ROPE_BASE ** (-np.arange(n_freq, dtype=np.float32) / n_freq)).astype(np.float32)
    t = np.arange(seq_len)
    row, col = t // GRID_W, t % GRID_W
    jj = np.arange(128) % ATT_QK_DIM
    is_col = (jj // axis_dim) == 1
    second = ((jj % axis_dim) >= n_freq)[None, :]
    pos = np.where(is_col[None, :], col[:, None], row[:, None]).astype(np.float32)
    ang = (pos * inv[jj % n_freq][None, :]).astype(np.float64)
    cos, sin = np.cos(ang), np.sin(ang)
    as32 = lambda m: jnp.asarray(m, dtype=F32)
    return as32(cos), as32(np.where(second, sin, 0.0)), as32(np.where(second, 0.0, -sin))


def _attn_kernel(lam_ref, q_ref, kc_ref, vc_ref, kl_ref, vl_ref, g_ref, o_ref, *, tk, lam_init):
    q = q_ref[0]
    tq = q.shape[0]
    lane = lax.broadcasted_iota(jnp.int32, q.shape, 1)
    zero = jnp.zeros_like(q)
    q2 = jnp.concatenate([jnp.where(lane < ATT_QK_DIM, q, zero), jnp.where(lane >= ATT_QK_DIM, q, zero)], axis=0)

    def chunk(k, v, m, acc):
        s = lax.dot_general(q2, k, (((1,), (1,)), ((), ())), preferred_element_type=F32)
        rowmax = jnp.max(s, axis=1, keepdims=True)
        m_new = rowmax if m is None else jnp.maximum(m, rowmax)
        p = jnp.exp2(s - m_new).astype(BF16)
        pv = _dot(p, jnp.concatenate([v, jnp.ones_like(v)], axis=1))
        return m_new, (pv if acc is None else jnp.exp2(m - m_new) * acc + pv)

    m, acc = chunk(kc_ref[0], vc_ref[0], None, None)
    for j in range(kl_ref.shape[1] // tk):
        m, acc = chunk(kl_ref[0, j * tk:(j + 1) * tk, :], vl_ref[0, j * tk:(j + 1) * tk, :], m, acc)

    lamv = lam_ref[...]
    lam = (jnp.exp(jnp.sum(lamv[0:1] * lamv[1:2], axis=1, keepdims=True))
           - jnp.exp(jnp.sum(lamv[2:3] * lamv[3:4], axis=1, keepdims=True)) + lam_init)
    o_all = acc[:, :ATT_V_DIM] / acc[:, ATT_V_DIM:]
    o = o_all[:tq] - lam * o_all[tq:]
    o = _rms(o, SUBLN_EPS) * g_ref[...] * (1.0 - lam_init)
    o_ref[0] = o.astype(o_ref.dtype)


def _attn_call(q, kc, vc, kl, vl, lamv, g, lam_init, tq=1024, tk=256):
    bsz, l, width = q.shape
    heads = width // ATT_V_DIM
    lc = kc.shape[1]
    hd = ATT_V_DIM
    tq = min(tq, l)
    return pl.pallas_call(
        functools.partial(_attn_kernel, tk=tk, lam_init=lam_init),
        grid=(bsz, heads, l // tq),
        in_specs=[
            _const_spec(lamv.shape),
            pl.BlockSpec((1, tq, hd), lambda b, h, i: (b, i, h)),
            pl.BlockSpec((1, lc, hd), lambda b, h, i: (b, 0, h)),
            pl.BlockSpec((1, lc, hd), lambda b, h, i: (b, 0, h)),
            pl.BlockSpec((1, l, hd), lambda b, h, i: (b, 0, h)),
            pl.BlockSpec((1, l, hd), lambda b, h, i: (b, 0, h)),
            _const_spec((1, hd)),
        ],
        out_specs=pl.BlockSpec((1, tq, hd), lambda b, h, i: (b, i, h)),
        out_shape=jax.ShapeDtypeStruct((bsz, l, width), BF16),
        compiler_params=_params(3),
        name="diffattn",
    )(lamv, q, kc, vc, kl, vl, g.reshape(1, hd))


def _gather_minor(refs, n_rows):
    cols = []
    for j in range(SUBLANES):
        parts = [r[pl.ds(j, n_rows, stride=SUBLANES), :] for r in refs]
        cols.append(parts[0] if len(parts) == 1 else jnp.concatenate(parts, axis=0))
    return jnp.concatenate(cols, axis=1)


def _slab_gather(src_ref, jb, slab_ref):
    rows = src_ref.shape[0]
    slab_ref[...] = src_ref[:, jb].reshape(rows * SUBLANES, LANES)
    return _gather_minor([slab_ref], rows)


def _slab_scatter(val, slab_ref, dst_ref, jb):
    rows = dst_ref.shape[0]
    for j in range(SUBLANES):
        slab_ref[pl.ds(j, rows, stride=SUBLANES), :] = val[:, j * LANES:(j + 1) * LANES]
    dst_ref[:, jb] = slab_ref[...].reshape(rows, SUBLANES, LANES)


def _spectrum_store(val, spec_ref, jb):
    rows = val.shape[0]
    for j in range(SUBLANES):
        spec_ref[jb, pl.ds(j, rows, stride=SUBLANES), :] = val[:, j * LANES:(j + 1) * LANES]


def _spectrum_blocks(spec_ref, k0):
    x = spec_ref[:, pl.ds(pl.multiple_of(k0 * SUBLANES, SUBLANES * SUBLANES), SUBLANES * SUBLANES), :]
    return jnp.stack([x[:, i * SUBLANES:(i + 1) * SUBLANES, :].reshape(FFT_N2, LANES) for i in range(SUBLANES)], axis=0)


def _spectrum_put(spec_ref, k0, val):
    nb = FFT_N2 // SUBLANES
    v = val.reshape(SUBLANES, nb, SUBLANES, LANES)
    x = jnp.stack([v[:, jb].reshape(SUBLANES * SUBLANES, LANES) for jb in range(nb)], axis=0)
    spec_ref[:, pl.ds(pl.multiple_of(k0 * SUBLANES, SUBLANES * SUBLANES), SUBLANES * SUBLANES), :] = x


def _twiddle_block(twc_ref, twf_ref, kb):
    n2 = FFT_N2
    rows = pl.ds(pl.multiple_of(kb * n2, n2), n2)
    cr, ci = twc_ref[0, rows, :][None], twc_ref[1, rows, :][None]
    fr, fi = twf_ref[0].reshape(SUBLANES, n2, LANES), twf_ref[1].reshape(SUBLANES, n2, LANES)
    return cr * fr - ci * fi, cr * fi + ci * fr


def _filter_fft_kernel(z_ref, w1_ref, b1_ref, f1_ref, w2_ref, b2_ref, f2_ref, w3f_ref, w3b_ref, dl_ref,
                       fr_ref, g_ref, twc_ref, twf_ref, kr_ref, ki_ref, h2_ref, hs_ref, a_ref, *, tm):
    seq_len = z_ref.shape[0]
    n1 = a_ref.shape[2] // SUBLANES
    n1h = hs_ref.shape[2] // SUBLANES
    n2 = FFT_N2
    nb = n2 // SUBLANES
    w = SUBLANES * LANES

    @pl.when(pl.program_id(0) == 0)
    def _():
        def features(i, _):
            r = pl.ds(pl.multiple_of(i * tm, tm), tm)
            h1 = jnp.sin(f1_ref[...] * (_dot3f(z_ref[r, :], w1_ref[...]) + b1_ref[...]))
            h2_ref[r, :] = jnp.sin(f2_ref[...] * (_dot3f(h1, w2_ref[...]) + b2_ref[...]))
            return 0

        lax.fori_loop(0, seq_len // tm, features, 0)

    def taps(i, _):
        h2 = h2_ref[pl.ds(pl.multiple_of(i * tm, tm), tm), :]
        row = lax.broadcasted_iota(jnp.int32, (tm, LANES), 0) + i * tm
        decay = jnp.exp(-(row.astype(F32) * (1.0 / (seq_len - 1))) * dl_ref[...])
        for t, w3_ref in enumerate((w3f_ref, w3b_ref)):
            h = _dot3f(h2, w3_ref[...]) * decay
            if t == 1:
                h = jnp.where(row == 0, 0.0, h)
            h4 = h.reshape(tm // n2, nb, SUBLANES, LANES)
            for jb in range(nb):
                hs_ref[t, jb, pl.ds(pl.multiple_of(i * (tm // nb), tm // nb), tm // nb), :] = (
                    h4[:, jb].reshape(tm // nb, LANES))
        return 0

    lax.fori_loop(0, seq_len // tm, taps, 0)

    for jb in range(nb):
        z = jnp.concatenate([_gather_minor([hs_ref.at[t, jb]], n1h) for t in range(2)], axis=1)
        out = _dft_dot(fr_ref[...], z)
        for t in range(4):
            _spectrum_store(out[(t % 2) * n1:(t % 2 + 1) * n1, (t // 2) * w:(t // 2 + 1) * w], a_ref.at[t], jb)

    def body(kb, _):
        k0 = pl.multiple_of(kb * SUBLANES, SUBLANES)
        twr, twi = _twiddle_block(twc_ref, twf_ref, kb)
        a = [_spectrum_blocks(a_ref.at[t], k0) for t in range(4)]
        cols = []
        for t in range(2):
            ar, ai = a[2 * t], a[2 * t + 1]
            sr, si = ar * twr - ai * twi, ar * twi + ai * twr
            cols += [jnp.concatenate([sr[i], si[i]], axis=0) for i in range(SUBLANES)]
        x = _dft_dot(g_ref[...], jnp.concatenate(cols, axis=1))
        for i in range(SUBLANES):
            xf = x[:, i * LANES:(i + 1) * LANES]
            xb = x[:, (SUBLANES + i) * LANES:(SUBLANES + i + 1) * LANES]
            rows = pl.ds(pl.multiple_of((k0 + i) * n2, n2), n2)
            kr_ref[rows, :] = xf[:n2] + xb[:n2]
            ki_ref[rows, :] = xf[n2:] - xb[n2:]
        return 0

    lax.fori_loop(0, n1 // SUBLANES, body, 0, unroll=2)


def _filter_fft_call(dft, z, w1, b1, f1, w2, b2, f2, w3, absdelta, tm=512):
    seq_len, hid = z.shape
    c = w3.shape[1] // 2
    n, n1, n1h = dft["n"], dft["n1"], dft["n1h"]
    nb = FFT_N2 // SUBLANES
    ncb = c // LANES
    fr, g = dft["f_real"].astype(BF16), dft["g_fwd"].astype(BF16)
    twc, twf = dft["tw_coarse"], dft["tw_fine"]
    vec = _const_spec((1, hid))
    w3_blk = lambda off: pl.BlockSpec((hid, LANES), lambda i, off=off: (0, i + off))
    out = pl.BlockSpec((None, n, LANES), lambda i: (i, 0, 0))
    return pl.pallas_call(
        functools.partial(_filter_fft_kernel, tm=tm),
        grid=(ncb,),
        in_specs=[_const_spec(z.shape), _const_spec((hid, hid)), vec, vec, _const_spec((hid, hid)), vec, vec,
                  w3_blk(0), w3_blk(ncb), pl.BlockSpec((1, LANES), lambda i: (0, i)),
                  _const_spec(fr.shape), _const_spec(g.shape), _const_spec(twc.shape), _const_spec(twf.shape)],
        out_specs=[out, out],
        out_shape=[jax.ShapeDtypeStruct((ncb, n, LANES), F32)] * 2,
        scratch_shapes=[pltpu.VMEM((seq_len, hid), F32), pltpu.VMEM((2, nb, n1h * SUBLANES, LANES), F32),
                        pltpu.VMEM((4, nb, n1 * SUBLANES, LANES), F32)],
        compiler_params=_params(1),
        name="hyfilter_fft",
    )(z, w1, b1.reshape(1, hid), f1.reshape(1, hid), w2, b2.reshape(1, hid), f2.reshape(1, hid), w3, w3, absdelta,
      fr, g, twc, twf)


def _hyena_conv_kernel(fc_ref, g_ref, gc_ref, e_ref, twc_ref, twf_ref, kr_ref, ki_ref, u_ref, x0_ref, bias_ref, o_ref,
                       slab_in, a_ref, *, inv_n):
    n1 = a_ref.shape[2] // SUBLANES
    n1h = u_ref.shape[1]
    n2 = FFT_N2
    nb = n2 // SUBLANES

    for jb in range(nb):
        z = jnp.concatenate([_slab_gather(u_ref.at[c], jb, slab_in.at[c]) for c in range(2)], axis=0)
        out = _dft_dot(fc_ref[...], z)
        for c in range(2):
            _spectrum_store(out[c * n1:(c + 1) * n1], a_ref.at[c], jb)

    def body(kb, _):
        k0 = pl.multiple_of(kb * SUBLANES, SUBLANES)
        rows = pl.ds(pl.multiple_of(kb * (SUBLANES * n2), SUBLANES * n2), SUBLANES * n2)
        blocks = lambda ref: ref[rows, :].reshape(SUBLANES, n2, LANES)
        (twr, twi), kr, ki = _twiddle_block(twc_ref, twf_ref, kb), blocks(kr_ref), blocks(ki_ref)
        ar, ai = _spectrum_blocks(a_ref.at[0], k0), _spectrum_blocks(a_ref.at[1], k0)
        sr, si = ar * twr - ai * twi, ar * twi + ai * twr
        lanes = lambda re, im: jnp.concatenate([jnp.concatenate([re[i], im[i]], axis=0) for i in range(SUBLANES)], axis=1)
        x = _dft_dot(g_ref[...], lanes(sr, si))
        unl = lambda v, lo: jnp.stack([v[lo:lo + n2, i * LANES:(i + 1) * LANES] for i in range(SUBLANES)], axis=0)
        xr, xi = unl(x, 0), unl(x, n2)
        t = _dft_dot(gc_ref[...], lanes(xr * kr - xi * ki, xr * ki + xi * kr))
        tr, ti = unl(t, 0), unl(t, n2)
        _spectrum_put(a_ref.at[0], k0, tr * twr + ti * twi)
        _spectrum_put(a_ref.at[1], k0, ti * twr - tr * twi)
        return 0

    lax.fori_loop(0, n1 // SUBLANES, body, 0, unroll=2)

    for jb in range(nb):
        t = _gather_minor([a_ref.at[0, jb], a_ref.at[1, jb]], n1)
        y = _dft_dot(e_ref[...], t) * inv_n
        for c in range(2):
            _slab_scatter(y[c * n1h:(c + 1) * n1h], slab_in.at[c], o_ref.at[c], jb)
    o_ref[...] = x0_ref[...] * (o_ref[...] + u_ref[...] * bias_ref[...])


def _dft_blocks(seq_len):
    n = 2 * seq_len
    n2 = FFT_N2
    n1 = n // n2
    n1h = n1 // 2
    k1 = np.arange(n1)[:, None].astype(np.float64)
    a = 2.0 * np.pi * k1 * np.arange(n1h)[None, :] / n1
    fr, fi = np.cos(a), -np.sin(a)
    f_cplx = np.block([[fr, -fi], [fi, fr]])
    f_real = np.concatenate([fr, fi], axis=0)
    e_cplx = np.block([[fr.T, fi.T], [-fi.T, fr.T]])
    b = 2.0 * np.pi * np.arange(n2)[:, None] * np.arange(n2)[None, :] / n2
    gr, gi = np.cos(b), -np.sin(b)
    g_fwd = np.block([[gr, -gi], [gi, gr]])
    g_inv = np.block([[gr, gi], [-gi, gr]])
    def table(k1s):
        ang = (2.0 * np.pi / n) * (k1s[:, None] * np.arange(n2)[None, :]).reshape(-1, 1)
        return np.broadcast_to(np.stack([np.cos(ang), -np.sin(ang)]), (2, k1s.size * n2, LANES))
    as32 = lambda m: jnp.asarray(np.ascontiguousarray(m), dtype=F32)
    return dict(n=n, n1=n1, n1h=n1h, f_cplx=as32(f_cplx), f_real=as32(f_real), e_cplx=as32(e_cplx),
                g_fwd=as32(g_fwd), g_inv=as32(g_inv), tw_coarse=as32(table(np.arange(0, n1, SUBLANES))),
                tw_fine=as32(table(np.arange(SUBLANES))))


def _hyena_call(u, x0, filt, hy_bias):
    bsz, ncb, seq_len, _ = u.shape
    c = ncb * LANES
    n2 = FFT_N2
    dft = _dft_blocks(seq_len)
    n, n1, n1h = dft["n"], dft["n1"], dft["n1h"]
    pairs = bsz // 2

    w1, b1, f1, w2, b2, f2, w3 = filt
    hid = w2.shape[0]
    bands = (HY_POS_EMB - 1) // 2
    t = np.linspace(0.0, 1.0, seq_len)[:, None]
    freqs = np.linspace(1e-4, bands - 1, bands)
    ang = (2.0 * math.pi / seq_len) * np.arange(seq_len)[:, None] * freqs[None, :]
    z = jnp.asarray(np.concatenate([t, np.cos(ang), -np.sin(ang), np.zeros((seq_len, hid - HY_POS_EMB))], axis=-1),
                    dtype=F32)
    w1p = jnp.concatenate([w1, jnp.zeros((hid - HY_POS_EMB, hid), F32)], axis=0)
    max_decay = math.log(HY_DECAY_TARGET) / HY_FAST_DECAY_PCT
    min_decay = math.log(HY_DECAY_TARGET) / HY_SLOW_DECAY_PCT
    absdelta = jnp.asarray(np.abs(np.linspace(min_decay, max_decay, c))[None, :], dtype=F32)
    nb = n2 // SUBLANES
    kr, ki = _filter_fft_call(dft, z, w1p, b1, f1, w2, b2, f2, w3, absdelta)

    view = lambda a: a.reshape(pairs, 2, ncb, n1h, nb, SUBLANES, LANES)
    pair_spec = pl.BlockSpec((None, 2, None, n1h, nb, SUBLANES, LANES), lambda cb, p: (p, 0, cb, 0, 0, 0, 0))
    k_spec = pl.BlockSpec((None, n, LANES), lambda cb, p: (cb, 0, 0))
    twc, twf = dft["tw_coarse"], dft["tw_fine"]
    fc, g, gc, e = (dft[k].astype(BF16) for k in ("f_cplx", "g_fwd", "g_inv", "e_cplx"))
    hy = pl.pallas_call(
        functools.partial(_hyena_conv_kernel, inv_n=1.0 / n),
        grid=(ncb, pairs),
        in_specs=[_const_spec(fc.shape), _const_spec(g.shape), _const_spec(gc.shape), _const_spec(e.shape),
                  _const_spec(twc.shape), _const_spec(twf.shape), k_spec, k_spec, pair_spec, pair_spec,
                  pl.BlockSpec((1, LANES), lambda cb, p: (0, cb))],
        out_specs=pair_spec,
        out_shape=jax.ShapeDtypeStruct((pairs, 2, ncb, n1h, nb, SUBLANES, LANES), F32),
        scratch_shapes=[pltpu.VMEM((2, n1h * SUBLANES, LANES), F32), pltpu.VMEM((2, nb, n1 * SUBLANES, LANES), F32)],
        compiler_params=_params(2),
        name="hyconv",
    )(fc, g, gc, e, twc, twf, kr, ki, view(u), view(x0), hy_bias.reshape(1, c))
    return hy.reshape(bsz, ncb, seq_len, LANES)


def _stage_weights_bf16(pairs):
    for src, dst in pairs:
        n_rows, n_cols = src.shape
        rows = 1 << ((WEIGHT_STAGE_BYTES // (4 * n_cols)).bit_length() - 1)
        n_chunks = n_rows // rows
        assert rows % (2 * SUBLANES) == 0 and n_chunks * rows == n_rows

        def run(stage, sem, src=src, dst=dst, rows=rows, n_chunks=n_chunks):
            copy = lambda k: pltpu.make_async_copy(src.at[pl.ds(k * rows, rows), :], stage.at[k % 2], sem.at[k % 2])
            copy(0).start()
            for k in range(n_chunks):
                if k + 1 < n_chunks:
                    copy(k + 1).start()
                copy(k).wait()
                dst[pl.ds(k * rows, rows), :] = stage[k % 2].astype(BF16)

        pl.run_scoped(run, pltpu.VMEM((2, rows, n_cols), F32), pltpu.SemaphoreType.DMA((2,)))


def _mlp_tail(h, g2, shift, scale, gate, w1_ref, w2_ref, fc):
    a = (_rms(h, NORM_EPS) * g2) * (1.0 + scale) + shift
    a = a.astype(BF16)
    acc = None
    for c0 in range(0, w1_ref.shape[1], fc):
        hid = jnp.maximum(_dot(a, w1_ref[:, c0:c0 + fc]), 0.0)
        part = _dot((hid * hid).astype(BF16), w2_ref[c0:c0 + fc, :])
        acc = part if acc is None else acc + part
    return h + gate * acc


def _mixmlp_kernel(x_ref, att_ref, hy_ref, mod_ref, wo_hbm, bo_ref, g2_ref, w1_hbm, w2_hbm, o_ref,
                   wo_s, w1_s, w2_s, *, fc, layer):
    @pl.when((pl.program_id(0) == 0) & (pl.program_id(1) == 0))
    def _():
        _stage_weights_bf16([(wo_hbm.at[layer // 2], wo_s), (w1_hbm.at[layer], w1_s), (w2_hbm.at[layer], w2_s)])

    blk, d = x_ref.shape[1:]
    tm = blk // ROW_SUBTILES
    wa = att_ref.shape[2]
    mod = mod_ref[0]
    for r in range(ROW_SUBTILES):
        rows = slice(r * tm, (r + 1) * tm)
        hy = jnp.concatenate([hy_ref[0, cb, rows, :] for cb in range(hy_ref.shape[1])], axis=1).astype(BF16)
        y = _dot(att_ref[0, rows, :], wo_s[:wa, :]) + _dot(hy, wo_s[wa:, :]) + bo_ref[...]
        h = x_ref[0, rows, :] + mod[:, 2 * d:3 * d] * y
        o_ref[0, rows, :] = _mlp_tail(h, g2_ref[...], mod[:, 3 * d:4 * d], mod[:, 4 * d:5 * d], mod[:, 5 * d:6 * d],
                                      w1_s, w2_s, fc)


def _mixmlp_call(x, att, hy, mod, w_out, b_out, g2, w1, w2, layer, tm=ROW_BLOCK, fc=1024):
    bsz, l, d = x.shape
    wa = att.shape[2]
    dff = w1.shape[2]
    row = lambda w: pl.BlockSpec((1, tm, w), lambda b, i: (b, i, 0))
    hbm = pl.BlockSpec(memory_space=pl.ANY)
    return pl.pallas_call(
        functools.partial(_mixmlp_kernel, fc=fc, layer=layer),
        grid=(bsz, l // tm),
        in_specs=[row(d), row(wa), pl.BlockSpec((1, hy.shape[1], tm, LANES), lambda b, i: (b, 0, i, 0)),
                  pl.BlockSpec((1, 1, N_MOD * d), lambda b, i: (b, 0, 0)),
                  hbm, _const_spec((1, d)), _const_spec((1, d)), hbm, hbm],
        out_specs=row(d),
        out_shape=jax.ShapeDtypeStruct((bsz, l, d), F32),
        scratch_shapes=[pltpu.VMEM((d, d), BF16), pltpu.VMEM((d, dff), BF16), pltpu.VMEM((dff, d), BF16)],
        compiler_params=_params(2),
        name="mixmlp",
    )(x, att, hy, mod, w_out, b_out.reshape(1, d), g2.reshape(1, d), w1, w2)


def _poolmlp_kernel(h_ref, hp_ref, hn_ref, mod_ref, g1_ref, pw_ref, ps_ref, g2_ref, w1_hbm, w2_hbm, gf_ref, o_ref,
                    w1_s, w2_s, *, fc, seq_len, layer):
    @pl.when((pl.program_id(0) == 0) & (pl.program_id(1) == 0))
    def _():
        _stage_weights_bf16([(w1_hbm.at[layer], w1_s), (w2_hbm.at[layer], w2_s)])

    blk, d = h_ref.shape[1:]
    tm = blk // ROW_SUBTILES
    halo = POOL_HALO
    rows = tm + 2 * halo
    mod = mod_ref[0]
    gd = d // len(POOL_WINDOWS)
    for r in range(ROW_SUBTILES):
        r0 = r * tm
        h = h_ref[0, r0:r0 + tm, :]
        above = hp_ref[0] if r == 0 else h_ref[0, r0 - halo:r0, :]
        below = hn_ref[0] if r == ROW_SUBTILES - 1 else h_ref[0, r0 + tm:r0 + tm + halo, :]
        hx = jnp.concatenate([above, h, below], axis=0)
        a = (_rms(hx, NORM_EPS) * g1_ref[...]) * (1.0 + mod[:, d:2 * d]) + mod[:, 0:d]
        t = lax.broadcasted_iota(jnp.int32, (rows, 1), 0) + (pl.program_id(1) * blk + r0 - halo)
        a = jnp.where((t >= 0) & (t < seq_len), a, 0.0)
        tc = t[halo:halo + tm]
        ys = []
        for g, win in enumerate(POOL_WINDOWS):
            ag = a[:, g * gd:(g + 1) * gd]
            f, m = ag, 1
            while 2 * m <= win // 2:
                f = f + pltpu.roll(f, rows - m, 0)
                m *= 2
            s = pltpu.roll(f, win // 2, 0) + f
            cnt = (jnp.minimum(tc + (win - win // 2), seq_len) - jnp.maximum(tc - win // 2, 0)).astype(F32)
            dlt = s[halo:halo + tm] / cnt - ag[halo:halo + tm]
            ys.append(_dot(dlt.astype(BF16), pw_ref[g]))
        y = jnp.concatenate(ys, axis=1) * ps_ref[...]
        h1 = h + mod[:, 2 * d:3 * d] * y
        h2 = _mlp_tail(h1, g2_ref[...], mod[:, 3 * d:4 * d], mod[:, 4 * d:5 * d], mod[:, 5 * d:6 * d], w1_s, w2_s, fc)
        o_ref[0, r0:r0 + tm, :] = _rms(h2, NORM_EPS) * gf_ref[...]


def _poolmlp_call(h, mod, g1, pool_w, pool_scale, g2, w1, w2, gf, layer, tm=ROW_BLOCK, fc=1024):
    bsz, l, d = h.shape
    dff = w1.shape[2]
    halo = POOL_HALO
    nb = tm // halo
    last = l // halo - 1
    row = pl.BlockSpec((1, tm, d), lambda b, i: (b, i, 0))
    hbm = pl.BlockSpec(memory_space=pl.ANY)
    return pl.pallas_call(
        functools.partial(_poolmlp_kernel, fc=fc, seq_len=l, layer=layer),
        grid=(bsz, l // tm),
        in_specs=[row,
                  pl.BlockSpec((1, halo, d), lambda b, i: (b, jnp.maximum(i * nb - 1, 0), 0)),
                  pl.BlockSpec((1, halo, d), lambda b, i: (b, jnp.minimum((i + 1) * nb, last), 0)),
                  pl.BlockSpec((1, 1, N_MOD * d), lambda b, i: (b, 0, 0)),
                  _const_spec((1, d)), _const_spec(pool_w.shape), _const_spec((1, d)), _const_spec((1, d)),
                  hbm, hbm, _const_spec((1, d))],
        out_specs=row,
        out_shape=jax.ShapeDtypeStruct((bsz, l, d), F32),
        scratch_shapes=[pltpu.VMEM((d, dff), BF16), pltpu.VMEM((dff, d), BF16)],
        compiler_params=_params(2),
        name="poolmlp",
    )(h, h, h, mod, g1.reshape(1, d), pool_w, pool_scale.reshape(1, d), g2.reshape(1, d), w1, w2, gf.reshape(1, d))


def kernel(x, c, ctx, c_ctx, ada_w, ada_b, norm1_g, norm2_g, mix_w_in, mix_b_in, mix_w_out, mix_b_out, lam_q1, lam_k1, lam_q2, lam_k2, subln_g, hy_conv_w, hy_conv_b, hy_pos_w1, hy_pos_b1, hy_freq1, hy_pos_w2, hy_pos_b2, hy_freq2, hy_pos_w3, hy_bias, pool_w, pool_scale, mlp_w1, mlp_w2, final_g):
    bsz, seq_len, d = x.shape
    depth = ada_w.shape[0]
    assert depth == 2 and bsz % 2 == 0 and bsz < MOD_ROWS and seq_len % GRID_W == 0
    att_w = ATT_HEADS * ATT_V_DIM
    q_cols = k_cols = ATT_HEADS * 2 * ATT_QK_DIM
    kv_start, hy_start = q_cols, q_cols + k_cols + att_w
    in_cols = mix_w_in.shape[2]

    cv = jnp.concatenate([c, c_ctx[None, :], jnp.zeros((MOD_ROWS - bsz - 1, d), F32)], axis=0)
    mod = _mod_call(cv, ada_w, ada_b)
    mod_l = [mod[i, :bsz].reshape(bsz, 1, N_MOD * d) for i in range(depth)]

    lam_init = 0.8 - 0.6 * math.exp(-0.3 * 0)
    assert k_cols == q_cols and in_cols - hy_start == 3 * (d - att_w)
    q, k, v, u, x0 = _latproj_call(
        x, mod_l[0][:, :, 0:d], mod_l[0][:, :, d:2 * d], norm1_g[0], mix_w_in, mix_b_in[0], _rope_tables(seq_len),
        hy_conv_w[0], hy_conv_b[0], n_att=hy_start, qk=q_cols, q_scale=ATT_QK_DIM ** -0.5 * math.log2(math.e))
    mod_c = mod[0, bsz:bsz + 1].reshape(1, 1, N_MOD * d)
    kc, vc = _ctxproj_call(ctx, mod_c[:, :, 0:d], mod_c[:, :, d:2 * d], norm1_g[0],
                           mix_w_in[0, :, kv_start:hy_start].astype(BF16), mix_b_in[0, kv_start:hy_start], k_cols)
    lamv = jnp.stack([lam_q1[0], lam_k1[0], lam_q2[0], lam_k2[0]], axis=0)
    att = _attn_call(q, kc, vc, k, v, lamv, subln_g[0], lam_init)
    filt = (hy_pos_w1[0], hy_pos_b1[0], hy_freq1[0], hy_pos_w2[0], hy_pos_b2[0], hy_freq2[0], hy_pos_w3[0])
    hy = _hyena_call(u, x0, filt, hy_bias[0])
    h = _mixmlp_call(x, att, hy, mod_l[0], mix_w_out, mix_b_out[0], norm2_g[0], mlp_w1, mlp_w2, layer=0)

    return _poolmlp_call(h, mod_l[1], norm1_g[1], pool_w[0].astype(BF16), pool_scale[0], norm2_g[1],
                         mlp_w1, mlp_w2, final_g, layer=1)
```

```python
import functools
import math

import numpy as np
import jax
import jax.numpy as jnp
from jax import lax
from jax.experimental import pallas as pl
from jax.experimental.pallas import tpu as pltpu

F32 = jnp.float32
BF16 = jnp.bfloat16

GRID_W = 64
N_MOD = 6
ATT_HEADS = 4
ATT_V_DIM = 128
ATT_QK_DIM = 64
ROPE_BASE = 10000.0
HY_POS_EMB = 33
HY_DECAY_TARGET = 1e-2
HY_FAST_DECAY_PCT = 0.3
HY_SLOW_DECAY_PCT = 1.5
POOL_WINDOWS = (2, 4, 8, 16)
NORM_EPS = 1e-6
SUBLN_EPS = 1e-5
SUBLANES = 8
LANES = 128
FFT_N2 = 64
POOL_HALO = 8
WEIGHT_STAGE_BYTES = 2 * 1024 * 1024
ROW_BLOCK = 1024
ROW_SUBTILES = 2
MOD_ROWS = 8

VMEM_LIMIT_BYTES = 56 * 1024 * 1024


def _params(n_grid_dims):
    return pltpu.CompilerParams(
        dimension_semantics=("arbitrary",) * n_grid_dims,
        vmem_limit_bytes=VMEM_LIMIT_BYTES,
    )


def _const_spec(shape):
    nd = len(shape)
    return pl.BlockSpec(shape, lambda *_: (0,) * nd, pipeline_mode=pl.Buffered(1))


def _split_bf16(a):
    hi = a.astype(BF16)
    lo = (a - hi.astype(F32)).astype(BF16)
    return hi, lo


def _dot(a, b):
    return jnp.dot(a, b, preferred_element_type=F32)


def _dot3(a_hi, a_lo, b_hi, b_lo):
    return _dot(a_hi, b_hi) + (_dot(a_lo, b_hi) + _dot(a_hi, b_lo))


def _dot3f(a, b):
    a_hi, a_lo = _split_bf16(a)
    b_hi, b_lo = _split_bf16(b)
    return _dot3(a_hi, a_lo, b_hi, b_lo)


def _dft_dot(a_bf16, b):
    return _dot(a_bf16, b.astype(BF16))


def _rms(x, eps):
    return x * lax.rsqrt(jnp.mean(x * x, axis=-1, keepdims=True) + eps)


def _mod_kernel(cv_ref, w_ref, b_ref, o_ref):
    cv = cv_ref[...]
    s = cv / (1.0 + jnp.exp(-cv))
    o_ref[0] = _dot3f(s, w_ref[0]) + b_ref[0]


def _mod_call(cv, ada_w, ada_b, tn=1536):
    depth, d, n = ada_w.shape
    return pl.pallas_call(
        _mod_kernel,
        grid=(depth, n // tn),
        in_specs=[
            pl.BlockSpec((MOD_ROWS, d), lambda i, j: (0, 0)),
            pl.BlockSpec((1, d, tn), lambda i, j: (i, 0, j)),
            pl.BlockSpec((1, 1, tn), lambda i, j: (i, 0, j)),
        ],
        out_specs=pl.BlockSpec((1, MOD_ROWS, tn), lambda i, j: (i, 0, j)),
        out_shape=jax.ShapeDtypeStruct((depth, MOD_ROWS, n), F32),
        compiler_params=_params(2),
        name="mod",
    )(cv, ada_w, ada_b.reshape(depth, 1, n))


def _ctxproj_kernel(x_ref, sh_ref, sc_ref, g_ref, w_ref, b_ref, k_ref, v_ref):
    a = _rms(x_ref[0], NORM_EPS) * g_ref[...]
    a = a * (1.0 + sc_ref[0]) + sh_ref[0]
    p = _dot(a.astype(BF16), w_ref[...]) + b_ref[...]
    nk = k_ref.shape[2]
    k_ref[0] = p[:, :nk].astype(k_ref.dtype)
    v_ref[0] = p[:, nk:].astype(v_ref.dtype)


def _ctxproj_call(x, shift, scale, g, w_bf16, b, nk):
    bsz, s, d = x.shape
    n = w_bf16.shape[1]
    vec = pl.BlockSpec((1, 1, d), lambda bi: (0, 0, 0))
    return pl.pallas_call(
        _ctxproj_kernel,
        grid=(bsz,),
        in_specs=[pl.BlockSpec((1, s, d), lambda bi: (bi, 0, 0)), vec, vec,
                  _const_spec((1, d)), _const_spec((d, n)), _const_spec((1, n))],
        out_specs=[pl.BlockSpec((1, s, nk), lambda bi: (bi, 0, 0)), pl.BlockSpec((1, s, n - nk), lambda bi: (bi, 0, 0))],
        out_shape=[jax.ShapeDtypeStruct((bsz, s, nk), BF16), jax.ShapeDtypeStruct((bsz, s, n - nk), BF16)],
        compiler_params=_params(1),
        name="ctxproj",
    )(x, shift, scale, g.reshape(1, d), w_bf16, b.reshape(1, n))


def _latproj_kernel(x_ref, xp_ref, xn_ref, sh_ref, sc_ref, g_ref, w_hbm, b_ref, cos_ref, sa_ref, sb_ref,
                    cw_ref, cb_ref, q_ref, k_ref, v_ref, u_ref, x0_ref, w_ref, *, n_att, q_scale, seq_len):
    @pl.when((pl.program_id(0) == 0) & (pl.program_id(1) == 0))
    def _():
        _stage_weights_bf16([(w_hbm.at[0], w_ref)])

    tm = x_ref.shape[1] // ROW_SUBTILES
    halo = xp_ref.shape[1]
    rows = tm + 2 * halo
    ncb = u_ref.shape[1]
    c = ncb * LANES
    qk = q_ref.shape[2]
    for sub in range(ROW_SUBTILES):
        r0 = sub * tm
        above = xp_ref[0] if sub == 0 else x_ref[0, r0 - halo:r0, :]
        below = xn_ref[0] if sub == ROW_SUBTILES - 1 else x_ref[0, r0 + tm:r0 + tm + halo, :]
        xx = jnp.concatenate([above, x_ref[0, r0:r0 + tm, :], below], axis=0)
        a = _rms(xx, NORM_EPS) * g_ref[...]
        a = a * (1.0 + sc_ref[0]) + sh_ref[0]

        ph = _dot(a.astype(BF16), w_ref[:, n_att:]) + b_ref[:, n_att:]
        t = lax.broadcasted_iota(jnp.int32, (rows, 1), 0) + (pl.program_id(1) * (tm * ROW_SUBTILES) + r0 - halo)
        ph = jnp.where((t >= 0) & (t < seq_len), ph, 0.0)

        def conv(s):
            blk = ph[:, s * c:(s + 1) * c]
            prev = pltpu.roll(blk, 1, 0)[halo:halo + tm]
            nxt = pltpu.roll(blk, rows - 1, 0)[halo:halo + tm]
            cw = cw_ref[:, s * c:(s + 1) * c]
            return prev * cw[0:1] + blk[halo:halo + tm] * cw[1:2] + nxt * cw[2:3] + cb_ref[:, s * c:(s + 1) * c]

        x0 = conv(0)
        u = conv(2) * conv(1)
        nb = FFT_N2 // SUBLANES
        srows = slice(r0 // SUBLANES, (r0 + tm) // SUBLANES)
        for val, o_ref in ((x0, x0_ref), (u, u_ref)):
            for cb in range(ncb):
                v4 = val[:, cb * LANES:(cb + 1) * LANES].reshape(tm // FFT_N2, nb, SUBLANES, LANES)
                for jb in range(nb):
                    o_ref[0, cb, jb, srows, :] = v4[:, jb].reshape(tm // SUBLANES, LANES)

        p = _dot(a[halo:halo + tm].astype(BF16), w_ref[:, :n_att]) + b_ref[:, :n_att]
        cos, sa, sb = (ref[r0:r0 + tm, :] for ref in (cos_ref, sa_ref, sb_ref))
        for o_ref, c_lo, scale in ((q_ref, 0, q_scale), (k_ref, qk, 1.0)):
            for c0 in range(0, qk, LANES):
                blk = p[:, c_lo + c0:c_lo + c0 + LANES]
                r = blk * cos + pltpu.roll(blk, 16, 1) * sa + pltpu.roll(blk, LANES - 16, 1) * sb
                o_ref[0, r0:r0 + tm, c0:c0 + LANES] = (r * scale).astype(o_ref.dtype)
        v_ref[0, r0:r0 + tm, :] = p[:, 2 * qk:].astype(v_ref.dtype)


def _latproj_call(x, shift, scale, g, w_all, b, rope_tabs, conv_w, conv_b, n_att, qk, q_scale, tm=ROW_BLOCK):
    bsz, s, d = x.shape
    n = w_all.shape[2]
    c = (n - n_att) // 3
    halo = SUBLANES
    nb = tm // halo
    last = s // halo - 1
    vec = pl.BlockSpec((1, 1, d), lambda bi, i: (bi, 0, 0))
    tab = pl.BlockSpec((tm, LANES), lambda bi, i: (i, 0))
    row = lambda w: pl.BlockSpec((1, tm, w), lambda bi, i: (bi, i, 0))
    nslab = FFT_N2 // SUBLANES
    cblk = pl.BlockSpec((1, c // LANES, nslab, tm // nslab, LANES), lambda bi, i: (bi, 0, 0, i, 0))
    return pl.pallas_call(
        functools.partial(_latproj_kernel, n_att=n_att, q_scale=q_scale, seq_len=s),
        grid=(bsz, s // tm),
        in_specs=[row(d),
                  pl.BlockSpec((1, halo, d), lambda bi, i: (bi, jnp.maximum(i * nb - 1, 0), 0)),
                  pl.BlockSpec((1, halo, d), lambda bi, i: (bi, jnp.minimum((i + 1) * nb, last), 0)),
                  vec, vec, _const_spec((1, d)), pl.BlockSpec(memory_space=pl.ANY), _const_spec((1, n)), tab, tab, tab,
                  _const_spec(conv_w.shape), _const_spec((1, 3 * c))],
        out_specs=[row(qk), row(qk), row(n_att - 2 * qk), cblk, cblk],
        out_shape=[jax.ShapeDtypeStruct((bsz, s, qk), BF16), jax.ShapeDtypeStruct((bsz, s, qk), BF16),
                   jax.ShapeDtypeStruct((bsz, s, n_att - 2 * qk), BF16),
                   jax.ShapeDtypeStruct((bsz, c // LANES, nslab, s // nslab, LANES), F32),
                   jax.ShapeDtypeStruct((bsz, c // LANES, nslab, s // nslab, LANES), F32)],
        scratch_shapes=[pltpu.VMEM((d, n), BF16)],
        compiler_params=_params(2),
        name="latproj",
    )(x, x, x, shift, scale, g.reshape(1, d), w_all, b.reshape(1, n), *rope_tabs, conv_w, conv_b.reshape(1, 3 * c))


def _rope_tables(seq_len):
    axis_dim = ATT_QK_DIM // 2
    n_freq = axis_dim // 2
    inv = (ROPE_BASE ** (-np.arange(n_freq, dtype=np.float32) / n_freq)).astype(np.float32)
    t = np.arange(seq_len)
    row, col = t // GRID_W, t % GRID_W
    jj = np.arange(128) % ATT_QK_DIM
    is_col = (jj // axis_dim) == 1
    second = ((jj % axis_dim) >= n_freq)[None, :]
    pos = np.where(is_col[None, :], col[:, None], row[:, None]).astype(np.float32)
    ang = (pos * inv[jj % n_freq][None, :]).astype(np.float64)
    cos, sin = np.cos(ang), np.sin(ang)
    as32 = lambda m: jnp.asarray(m, dtype=F32)
    return as32(cos), as32(np.where(second, sin, 0.0)), as32(np.where(second, 0.0, -sin))


def _attn_kernel(lam_ref, q_ref, kc_ref, vc_ref, kl_ref, vl_ref, g_ref, o_ref, *, tk, lam_init):
    q = q_ref[0]
    tq = q.shape[0]
    lane = lax.broadcasted_iota(jnp.int32, q.shape, 1)
    zero = jnp.zeros_like(q)
    q2 = jnp.concatenate([jnp.where(lane < ATT_QK_DIM, q, zero), jnp.where(lane >= ATT_QK_DIM, q, zero)], axis=0)

    def chunk(k, v, m, acc):
        s = lax.dot_general(q2, k, (((1,), (1,)), ((), ())), preferred_element_type=F32)
        rowmax = jnp.max(s, axis=1, keepdims=True)
        m_new = rowmax if m is None else jnp.maximum(m, rowmax)
        p = jnp.exp2(s - m_new).astype(BF16)
        pv = _dot(p, jnp.concatenate([v, jnp.ones_like(v)], axis=1))
        return m_new, (pv if acc is None else jnp.exp2(m - m_new) * acc + pv)

    m, acc = chunk(kc_ref[0], vc_ref[0], None, None)
    for j in range(kl_ref.shape[1] // tk):
        m, acc = chunk(kl_ref[0, j * tk:(j + 1) * tk, :], vl_ref[0, j * tk:(j + 1) * tk, :], m, acc)

    lamv = lam_ref[...]
    lam = (jnp.exp(jnp.sum(lamv[0:1] * lamv[1:2], axis=1, keepdims=True))
           - jnp.exp(jnp.sum(lamv[2:3] * lamv[3:4], axis=1, keepdims=True)) + lam_init)
    o_all = acc[:, :ATT_V_DIM] / acc[:, ATT_V_DIM:]
    o = o_all[:tq] - lam * o_all[tq:]
    o = _rms(o, SUBLN_EPS) * g_ref[...] * (1.0 - lam_init)
    o_ref[0] = o.astype(o_ref.dtype)


def _attn_call(q, kc, vc, kl, vl, lamv, g, lam_init, tq=1024, tk=256):
    bsz, l, width = q.shape
    heads = width // ATT_V_DIM
    lc = kc.shape[1]
    hd = ATT_V_DIM
    tq = min(tq, l)
    return pl.pallas_call(
        functools.partial(_attn_kernel, tk=tk, lam_init=lam_init),
        grid=(bsz, heads, l // tq),
        in_specs=[
            _const_spec(lamv.shape),
            pl.BlockSpec((1, tq, hd), lambda b, h, i: (b, i, h)),
            pl.BlockSpec((1, lc, hd), lambda b, h, i: (b, 0, h)),
            pl.BlockSpec((1, lc, hd), lambda b, h, i: (b, 0, h)),
            pl.BlockSpec((1, l, hd), lambda b, h, i: (b, 0, h)),
            pl.BlockSpec((1, l, hd), lambda b, h, i: (b, 0, h)),
            _const_spec((1, hd)),
        ],
        out_specs=pl.BlockSpec((1, tq, hd), lambda b, h, i: (b, i, h)),
        out_shape=jax.ShapeDtypeStruct((bsz, l, width), BF16),
        compiler_params=_params(3),
        name="diffattn",
    )(lamv, q, kc, vc, kl, vl, g.reshape(1, hd))


def _gather_minor(refs, n_rows):
    cols = []
    for j in range(SUBLANES):
        parts = [r[pl.ds(j, n_rows, stride=SUBLANES), :] for r in refs]
        cols.append(parts[0] if len(parts) == 1 else jnp.concatenate(parts, axis=0))
    return jnp.concatenate(cols, axis=1)


def _spectrum_store(val, spec_ref, jb):
    rows = val.shape[0]
    for j in range(SUBLANES):
        spec_ref[jb, pl.ds(j, rows, stride=SUBLANES), :] = val[:, j * LANES:(j + 1) * LANES]


def _spectrum_blocks(spec_ref, k0):
    start = k0 * SUBLANES if isinstance(k0, int) else pl.multiple_of(k0 * SUBLANES, SUBLANES * SUBLANES)
    x = spec_ref[:, pl.ds(start, SUBLANES * SUBLANES), :]
    return jnp.stack([x[:, i * SUBLANES:(i + 1) * SUBLANES, :].reshape(FFT_N2, LANES) for i in range(SUBLANES)], axis=0)


def _spectrum_put(spec_ref, k0, val):
    nb = FFT_N2 // SUBLANES
    v = val.reshape(SUBLANES, nb, SUBLANES, LANES)
    x = jnp.stack([v[:, jb].reshape(SUBLANES * SUBLANES, LANES) for jb in range(nb)], axis=0)
    spec_ref[:, pl.ds(pl.multiple_of(k0 * SUBLANES, SUBLANES * SUBLANES), SUBLANES * SUBLANES), :] = x


def _twiddle_block(twc_ref, twf_ref, kb):
    n2 = FFT_N2
    rows = pl.ds(pl.multiple_of(kb * n2, n2), n2)
    cr, ci = twc_ref[0, rows, :][None], twc_ref[1, rows, :][None]
    fr, fi = twf_ref[0].reshape(SUBLANES, n2, LANES), twf_ref[1].reshape(SUBLANES, n2, LANES)
    return cr * fr - ci * fi, cr * fi + ci * fr


def _filter_fft_kernel(z_ref, w1_ref, b1_ref, f1_ref, w2_ref, b2_ref, f2_ref, w3f_ref, w3b_ref, dl_ref,
                       fr_ref, g_ref, twc_ref, twf_ref, kr_ref, ki_ref, h2_ref, hs_ref, a_ref, *, tm):
    seq_len = z_ref.shape[0]
    n1 = a_ref.shape[2] // SUBLANES
    n1h = hs_ref.shape[2] // SUBLANES
    n2 = FFT_N2
    nb = n2 // SUBLANES
    w = SUBLANES * LANES

    @pl.when(pl.program_id(0) == 0)
    def _():
        def features(i, _):
            r = pl.ds(pl.multiple_of(i * tm, tm), tm)
            h1 = jnp.sin(f1_ref[...] * (_dot3f(z_ref[r, :], w1_ref[...]) + b1_ref[...]))
            h2_ref[r, :] = jnp.sin(f2_ref[...] * (_dot3f(h1, w2_ref[...]) + b2_ref[...]))
            return 0

        lax.fori_loop(0, seq_len // tm, features, 0)

    def taps(i, _):
        h2 = h2_ref[pl.ds(pl.multiple_of(i * tm, tm), tm), :]
        row = lax.broadcasted_iota(jnp.int32, (tm, LANES), 0) + i * tm
        decay = jnp.exp(-(row.astype(F32) * (1.0 / (seq_len - 1))) * dl_ref[...])
        for t, w3_ref in enumerate((w3f_ref, w3b_ref)):
            h = _dot3f(h2, w3_ref[...]) * decay
            if t == 1:
                h = jnp.where(row == 0, 0.0, h)
            h4 = h.reshape(tm // n2, nb, SUBLANES, LANES)
            for jb in range(nb):
                hs_ref[t, jb, pl.ds(pl.multiple_of(i * (tm // nb), tm // nb), tm // nb), :] = (
                    h4[:, jb].reshape(tm // nb, LANES))
        return 0

    lax.fori_loop(0, seq_len // tm, taps, 0)

    for jb in range(nb):
        z = jnp.concatenate([_gather_minor([hs_ref.at[t, jb]], n1h) for t in range(2)], axis=1)
        out = _dft_dot(fr_ref[...], z)
        for t in range(4):
            _spectrum_store(out[(t % 2) * n1:(t % 2 + 1) * n1, (t // 2) * w:(t // 2 + 1) * w], a_ref.at[t], jb)

    def body(kb, _):
        k0 = pl.multiple_of(kb * SUBLANES, SUBLANES)
        twr, twi = _twiddle_block(twc_ref, twf_ref, kb)
        a = [_spectrum_blocks(a_ref.at[t], k0) for t in range(4)]
        cols = []
        for t in range(2):
            ar, ai = a[2 * t], a[2 * t + 1]
            sr, si = ar * twr - ai * twi, ar * twi + ai * twr
            cols += [jnp.concatenate([sr[i], si[i]], axis=0) for i in range(SUBLANES)]
        x = _dft_dot(g_ref[...], jnp.concatenate(cols, axis=1))
        for i in range(SUBLANES):
            xf = x[:, i * LANES:(i + 1) * LANES]
            xb = x[:, (SUBLANES + i) * LANES:(SUBLANES + i + 1) * LANES]
            rows = pl.ds(pl.multiple_of((k0 + i) * n2, n2), n2)
            kr_ref[rows, :] = xf[:n2] + xb[:n2]
            ki_ref[rows, :] = xf[n2:] - xb[n2:]
        return 0

    lax.fori_loop(0, n1 // SUBLANES, body, 0, unroll=4)


def _filter_fft_call(dft, z, w1, b1, f1, w2, b2, f2, w3, absdelta, tm=512):
    seq_len, hid = z.shape
    c = w3.shape[1] // 2
    n, n1, n1h = dft["n"], dft["n1"], dft["n1h"]
    nb = FFT_N2 // SUBLANES
    ncb = c // LANES
    fr, g = dft["f_real"].astype(BF16), dft["g_fwd"].astype(BF16)
    twc, twf = dft["tw_coarse"], dft["tw_fine"]
    vec = _const_spec((1, hid))
    w3_blk = lambda off: pl.BlockSpec((hid, LANES), lambda i, off=off: (0, i + off))
    out = pl.BlockSpec((None, n, LANES), lambda i: (i, 0, 0))
    return pl.pallas_call(
        functools.partial(_filter_fft_kernel, tm=tm),
        grid=(ncb,),
        in_specs=[_const_spec(z.shape), _const_spec((hid, hid)), vec, vec, _const_spec((hid, hid)), vec, vec,
                  w3_blk(0), w3_blk(ncb), pl.BlockSpec((1, LANES), lambda i: (0, i)),
                  _const_spec(fr.shape), _const_spec(g.shape), _const_spec(twc.shape), _const_spec(twf.shape)],
        out_specs=[out, out],
        out_shape=[jax.ShapeDtypeStruct((ncb, n, LANES), F32)] * 2,
        scratch_shapes=[pltpu.VMEM((seq_len, hid), F32), pltpu.VMEM((2, nb, n1h * SUBLANES, LANES), F32),
                        pltpu.VMEM((4, nb, n1 * SUBLANES, LANES), F32)],
        compiler_params=_params(1),
        name="hyfilter_fft",
    )(z, w1, b1.reshape(1, hid), f1.reshape(1, hid), w2, b2.reshape(1, hid), f2.reshape(1, hid), w3, w3, absdelta,
      fr, g, twc, twf)


def _hyena_conv_kernel(fc_ref, g_ref, gc_ref, e_ref, twc_ref, twf_ref, kr_ref, ki_ref, u_ref, x0_ref, bias_ref, o_ref,
                       a_ref, *, inv_n):
    n1 = a_ref.shape[2] // SUBLANES
    n1h = u_ref.shape[2] // SUBLANES
    n2 = FFT_N2
    nb = n2 // SUBLANES

    for jb in range(nb):
        z = _gather_minor([u_ref.at[0, jb], u_ref.at[1, jb]], n1h)
        out = _dft_dot(fc_ref[...], z)
        for c in range(2):
            _spectrum_store(out[c * n1:(c + 1) * n1], a_ref.at[c], jb)

    def body(kb, _):
        k0 = pl.multiple_of(kb * SUBLANES, SUBLANES)
        rows = pl.ds(pl.multiple_of(kb * (SUBLANES * n2), SUBLANES * n2), SUBLANES * n2)
        blocks = lambda ref: ref[rows, :].reshape(SUBLANES, n2, LANES)
        (twr, twi), kr, ki = _twiddle_block(twc_ref, twf_ref, kb), blocks(kr_ref), blocks(ki_ref)
        ar, ai = _spectrum_blocks(a_ref.at[0], k0), _spectrum_blocks(a_ref.at[1], k0)
        sr, si = ar * twr - ai * twi, ar * twi + ai * twr
        lanes = lambda re, im: jnp.concatenate([jnp.concatenate([re[i], im[i]], axis=0) for i in range(SUBLANES)], axis=1)
        x = _dft_dot(g_ref[...], lanes(sr, si))
        unl = lambda v, lo: jnp.stack([v[lo:lo + n2, i * LANES:(i + 1) * LANES] for i in range(SUBLANES)], axis=0)
        xr, xi = unl(x, 0), unl(x, n2)
        t = _dft_dot(gc_ref[...], lanes(xr * kr - xi * ki, xr * ki + xi * kr))
        tr, ti = unl(t, 0), unl(t, n2)
        _spectrum_put(a_ref.at[0], k0, tr * twr + ti * twi)
        _spectrum_put(a_ref.at[1], k0, ti * twr - tr * twi)
        return 0

    lax.fori_loop(0, n1 // SUBLANES, body, 0, unroll=4)

    for jb in range(nb):
        t = _gather_minor([a_ref.at[0, jb], a_ref.at[1, jb]], n1)
        y = _dft_dot(e_ref[...], t) * inv_n
        for c in range(2):
            _spectrum_store(y[c * n1h:(c + 1) * n1h], o_ref.at[c], jb)
    o_ref[...] = x0_ref[...] * (o_ref[...] + u_ref[...] * bias_ref[...])


def _dft_blocks(seq_len):
    n = 2 * seq_len
    n2 = FFT_N2
    n1 = n // n2
    n1h = n1 // 2
    k1 = np.arange(n1)[:, None].astype(np.float64)
    a = 2.0 * np.pi * k1 * np.arange(n1h)[None, :] / n1
    fr, fi = np.cos(a), -np.sin(a)
    f_cplx = np.block([[fr, -fi], [fi, fr]])
    f_real = np.concatenate([fr, fi], axis=0)
    e_cplx = np.block([[fr.T, fi.T], [-fi.T, fr.T]])
    b = 2.0 * np.pi * np.arange(n2)[:, None] * np.arange(n2)[None, :] / n2
    gr, gi = np.cos(b), -np.sin(b)
    g_fwd = np.block([[gr, -gi], [gi, gr]])
    g_inv = np.block([[gr, gi], [-gi, gr]])
    def table(k1s):
        ang = (2.0 * np.pi / n) * (k1s[:, None] * np.arange(n2)[None, :]).reshape(-1, 1)
        return np.broadcast_to(np.stack([np.cos(ang), -np.sin(ang)]), (2, k1s.size * n2, LANES))
    as32 = lambda m: jnp.asarray(np.ascontiguousarray(m), dtype=F32)
    return dict(n=n, n1=n1, n1h=n1h, f_cplx=as32(f_cplx), f_real=as32(f_real), e_cplx=as32(e_cplx),
                g_fwd=as32(g_fwd), g_inv=as32(g_inv), tw_coarse=as32(table(np.arange(0, n1, SUBLANES))),
                tw_fine=as32(table(np.arange(SUBLANES))))


def _hyena_call(u, x0, filt, hy_bias):
    bsz, ncb, nslab, srows, _ = u.shape
    seq_len = nslab * srows
    c = ncb * LANES
    n2 = FFT_N2
    dft = _dft_blocks(seq_len)
    n, n1, n1h = dft["n"], dft["n1"], dft["n1h"]
    pairs = bsz // 2

    w1, b1, f1, w2, b2, f2, w3 = filt
    hid = w2.shape[0]
    bands = (HY_POS_EMB - 1) // 2
    t = np.linspace(0.0, 1.0, seq_len)[:, None]
    freqs = np.linspace(1e-4, bands - 1, bands)
    ang = (2.0 * math.pi / seq_len) * np.arange(seq_len)[:, None] * freqs[None, :]
    z = jnp.asarray(np.concatenate([t, np.cos(ang), -np.sin(ang), np.zeros((seq_len, hid - HY_POS_EMB))], axis=-1),
                    dtype=F32)
    w1p = jnp.concatenate([w1, jnp.zeros((hid - HY_POS_EMB, hid), F32)], axis=0)
    max_decay = math.log(HY_DECAY_TARGET) / HY_FAST_DECAY_PCT
    min_decay = math.log(HY_DECAY_TARGET) / HY_SLOW_DECAY_PCT
    absdelta = jnp.asarray(np.abs(np.linspace(min_decay, max_decay, c))[None, :], dtype=F32)
    nb = n2 // SUBLANES
    kr, ki = _filter_fft_call(dft, z, w1p, b1, f1, w2, b2, f2, w3, absdelta)

    view = lambda a: a.reshape(pairs, 2, ncb, nb, n1h * SUBLANES, LANES)
    pair_spec = pl.BlockSpec((None, 2, None, nb, n1h * SUBLANES, LANES), lambda cb, p: (p, 0, cb, 0, 0, 0))
    k_spec = pl.BlockSpec((None, n, LANES), lambda cb, p: (cb, 0, 0))
    twc, twf = dft["tw_coarse"], dft["tw_fine"]
    fc, g, gc, e = (dft[k].astype(BF16) for k in ("f_cplx", "g_fwd", "g_inv", "e_cplx"))
    hy = pl.pallas_call(
        functools.partial(_hyena_conv_kernel, inv_n=1.0 / n),
        grid=(ncb, pairs),
        in_specs=[_const_spec(fc.shape), _const_spec(g.shape), _const_spec(gc.shape), _const_spec(e.shape),
                  _const_spec(twc.shape), _const_spec(twf.shape), k_spec, k_spec, pair_spec, pair_spec,
                  pl.BlockSpec((1, LANES), lambda cb, p: (0, cb))],
        out_specs=pair_spec,
        out_shape=jax.ShapeDtypeStruct((pairs, 2, ncb, nb, n1h * SUBLANES, LANES), F32),
        scratch_shapes=[pltpu.VMEM((2, nb, n1 * SUBLANES, LANES), F32)],
        compiler_params=_params(2),
        name="hyconv",
    )(fc, g, gc, e, twc, twf, kr, ki, view(u), view(x0), hy_bias.reshape(1, c))
    return hy.reshape(bsz, ncb, nb, n1h * SUBLANES, LANES)


def _stage_weights_bf16(pairs):
    for src, dst in pairs:
        n_rows, n_cols = src.shape
        rows = 1 << ((WEIGHT_STAGE_BYTES // (4 * n_cols)).bit_length() - 1)
        n_chunks = n_rows // rows
        assert rows % (2 * SUBLANES) == 0 and n_chunks * rows == n_rows

        def run(stage, sem, src=src, dst=dst, rows=rows, n_chunks=n_chunks):
            copy = lambda k: pltpu.make_async_copy(src.at[pl.ds(k * rows, rows), :], stage.at[k % 2], sem.at[k % 2])
            copy(0).start()
            for k in range(n_chunks):
                if k + 1 < n_chunks:
                    copy(k + 1).start()
                copy(k).wait()
                dst[pl.ds(k * rows, rows), :] = stage[k % 2].astype(BF16)

        pl.run_scoped(run, pltpu.VMEM((2, rows, n_cols), F32), pltpu.SemaphoreType.DMA((2,)))


def _mlp_tail(h, g2, shift, scale, gate, w1_ref, w2_ref, fc):
    a = (_rms(h, NORM_EPS) * g2) * (1.0 + scale) + shift
    a = a.astype(BF16)
    acc = None
    for c0 in range(0, w1_ref.shape[1], fc):
        hid = jnp.maximum(_dot(a, w1_ref[:, c0:c0 + fc]), 0.0)
        part = _dot((hid * hid).astype(BF16), w2_ref[c0:c0 + fc, :])
        acc = part if acc is None else acc + part
    return h + gate * acc


def _mixmlp_kernel(x_ref, att_ref, hy_ref, mod_ref, wo_hbm, bo_ref, g2_ref, w1_hbm, w2_hbm, o_ref,
                   wo_s, w1_s, w2_s, *, fc, layer):
    @pl.when((pl.program_id(0) == 0) & (pl.program_id(1) == 0))
    def _():
        _stage_weights_bf16([(wo_hbm.at[layer // 2], wo_s), (w1_hbm.at[layer], w1_s), (w2_hbm.at[layer], w2_s)])

    blk, d = x_ref.shape[1:]
    tm = blk // ROW_SUBTILES
    wa = att_ref.shape[2]
    mod = mod_ref[0]
    for r in range(ROW_SUBTILES):
        rows = slice(r * tm, (r + 1) * tm)
        hy = jnp.concatenate([_spectrum_blocks(hy_ref.at[0, cb], r * tm // FFT_N2).reshape(tm, LANES)
                              for cb in range(hy_ref.shape[1])], axis=1).astype(BF16)
        y = _dot(att_ref[0, rows, :], wo_s[:wa, :]) + _dot(hy, wo_s[wa:, :]) + bo_ref[...]
        h = x_ref[0, rows, :] + mod[:, 2 * d:3 * d] * y
        o_ref[0, rows, :] = _mlp_tail(h, g2_ref[...], mod[:, 3 * d:4 * d], mod[:, 4 * d:5 * d], mod[:, 5 * d:6 * d],
                                      w1_s, w2_s, fc)


def _mixmlp_call(x, att, hy, mod, w_out, b_out, g2, w1, w2, layer, tm=ROW_BLOCK, fc=1024):
    bsz, l, d = x.shape
    wa = att.shape[2]
    dff = w1.shape[2]
    row = lambda w: pl.BlockSpec((1, tm, w), lambda b, i: (b, i, 0))
    hbm = pl.BlockSpec(memory_space=pl.ANY)
    return pl.pallas_call(
        functools.partial(_mixmlp_kernel, fc=fc, layer=layer),
        grid=(bsz, l // tm),
        in_specs=[row(d), row(wa),
                  pl.BlockSpec((1, hy.shape[1], hy.shape[2], tm // hy.shape[2], LANES), lambda b, i: (b, 0, 0, i, 0)),
                  pl.BlockSpec((1, 1, N_MOD * d), lambda b, i: (b, 0, 0)),
                  hbm, _const_spec((1, d)), _const_spec((1, d)), hbm, hbm],
        out_specs=row(d),
        out_shape=jax.ShapeDtypeStruct((bsz, l, d), F32),
        scratch_shapes=[pltpu.VMEM((d, d), BF16), pltpu.VMEM((d, dff), BF16), pltpu.VMEM((dff, d), BF16)],
        compiler_params=_params(2),
        name="mixmlp",
    )(x, att, hy, mod, w_out, b_out.reshape(1, d), g2.reshape(1, d), w1, w2)


def _poolmlp_kernel(h_ref, hp_ref, hn_ref, mod_ref, g1_ref, pw_ref, ps_ref, g2_ref, w1_hbm, w2_hbm, gf_ref, o_ref,
                    w1_s, w2_s, *, fc, seq_len, layer):
    @pl.when((pl.program_id(0) == 0) & (pl.program_id(1) == 0))
    def _():
        _stage_weights_bf16([(w1_hbm.at[layer], w1_s), (w2_hbm.at[layer], w2_s)])

    blk, d = h_ref.shape[1:]
    tm = blk // ROW_SUBTILES
    halo = POOL_HALO
    rows = tm + 2 * halo
    mod = mod_ref[0]
    gd = d // len(POOL_WINDOWS)
    for r in range(ROW_SUBTILES):
        r0 = r * tm
        h = h_ref[0, r0:r0 + tm, :]
        above = hp_ref[0] if r == 0 else h_ref[0, r0 - halo:r0, :]
        below = hn_ref[0] if r == ROW_SUBTILES - 1 else h_ref[0, r0 + tm:r0 + tm + halo, :]
        hx = jnp.concatenate([above, h, below], axis=0)
        a = (_rms(hx, NORM_EPS) * g1_ref[...]) * (1.0 + mod[:, d:2 * d]) + mod[:, 0:d]
        t = lax.broadcasted_iota(jnp.int32, (rows, 1), 0) + (pl.program_id(1) * blk + r0 - halo)
        a = jnp.where((t >= 0) & (t < seq_len), a, 0.0)
        tc = t[halo:halo + tm]
        ys = []
        for g, win in enumerate(POOL_WINDOWS):
            ag = a[:, g * gd:(g + 1) * gd]
            f, m = ag, 1
            while 2 * m <= win // 2:
                f = f + pltpu.roll(f, rows - m, 0)
                m *= 2
            s = pltpu.roll(f, win // 2, 0) + f
            cnt = (jnp.minimum(tc + (win - win // 2), seq_len) - jnp.maximum(tc - win // 2, 0)).astype(F32)
            dlt = s[halo:halo + tm] / cnt - ag[halo:halo + tm]
            ys.append(_dot(dlt.astype(BF16), pw_ref[g]))
        y = jnp.concatenate(ys, axis=1) * ps_ref[...]
        h1 = h + mod[:, 2 * d:3 * d] * y
        h2 = _mlp_tail(h1, g2_ref[...], mod[:, 3 * d:4 * d], mod[:, 4 * d:5 * d], mod[:, 5 * d:6 * d], w1_s, w2_s, fc)
        o_ref[0, r0:r0 + tm, :] = _rms(h2, NORM_EPS) * gf_ref[...]


def _poolmlp_call(h, mod, g1, pool_w, pool_scale, g2, w1, w2, gf, layer, tm=ROW_BLOCK, fc=1024):
    bsz, l, d = h.shape
    dff = w1.shape[2]
    halo = POOL_HALO
    nb = tm // halo
    last = l // halo - 1
    row = pl.BlockSpec((1, tm, d), lambda b, i: (b, i, 0))
    hbm = pl.BlockSpec(memory_space=pl.ANY)
    return pl.pallas_call(
        functools.partial(_poolmlp_kernel, fc=fc, seq_len=l, layer=layer),
        grid=(bsz, l // tm),
        in_specs=[row,
                  pl.BlockSpec((1, halo, d), lambda b, i: (b, jnp.maximum(i * nb - 1, 0), 0)),
                  pl.BlockSpec((1, halo, d), lambda b, i: (b, jnp.minimum((i + 1) * nb, last), 0)),
                  pl.BlockSpec((1, 1, N_MOD * d), lambda b, i: (b, 0, 0)),
                  _const_spec((1, d)), _const_spec(pool_w.shape), _const_spec((1, d)), _const_spec((1, d)),
                  hbm, hbm, _const_spec((1, d))],
        out_specs=row,
        out_shape=jax.ShapeDtypeStruct((bsz, l, d), F32),
        scratch_shapes=[pltpu.VMEM((d, dff), BF16), pltpu.VMEM((dff, d), BF16)],
        compiler_params=_params(2),
        name="poolmlp",
    )(h, h, h, mod, g1.reshape(1, d), pool_w, pool_scale.reshape(1, d), g2.reshape(1, d), w1, w2, gf.reshape(1, d))


def kernel(x, c, ctx, c_ctx, ada_w, ada_b, norm1_g, norm2_g, mix_w_in, mix_b_in, mix_w_out, mix_b_out, lam_q1, lam_k1, lam_q2, lam_k2, subln_g, hy_conv_w, hy_conv_b, hy_pos_w1, hy_pos_b1, hy_freq1, hy_pos_w2, hy_pos_b2, hy_freq2, hy_pos_w3, hy_bias, pool_w, pool_scale, mlp_w1, mlp_w2, final_g):
    bsz, seq_len, d = x.shape
    depth = ada_w.shape[0]
    assert depth == 2 and bsz % 2 == 0 and bsz < MOD_ROWS and seq_len % GRID_W == 0
    att_w = ATT_HEADS * ATT_V_DIM
    q_cols = k_cols = ATT_HEADS * 2 * ATT_QK_DIM
    kv_start, hy_start = q_cols, q_cols + k_cols + att_w
    in_cols = mix_w_in.shape[2]

    cv = jnp.concatenate([c, c_ctx[None, :], jnp.zeros((MOD_ROWS - bsz - 1, d), F32)], axis=0)
    mod = _mod_call(cv, ada_w, ada_b)
    mod_l = [mod[i, :bsz].reshape(bsz, 1, N_MOD * d) for i in range(depth)]

    lam_init = 0.8 - 0.6 * math.exp(-0.3 * 0)
    assert k_cols == q_cols and in_cols - hy_start == 3 * (d - att_w)
    q, k, v, u, x0 = _latproj_call(
        x, mod_l[0][:, :, 0:d], mod_l[0][:, :, d:2 * d], norm1_g[0], mix_w_in, mix_b_in[0], _rope_tables(seq_len),
        hy_conv_w[0], hy_conv_b[0], n_att=hy_start, qk=q_cols, q_scale=ATT_QK_DIM ** -0.5 * math.log2(math.e))
    mod_c = mod[0, bsz:bsz + 1].reshape(1, 1, N_MOD * d)
    kc, vc = _ctxproj_call(ctx, mod_c[:, :, 0:d], mod_c[:, :, d:2 * d], norm1_g[0],
                           mix_w_in[0, :, kv_start:hy_start].astype(BF16), mix_b_in[0, kv_start:hy_start], k_cols)
    lamv = jnp.stack([lam_q1[0], lam_k1[0], lam_q2[0], lam_k2[0]], axis=0)
    att = _attn_call(q, kc, vc, k, v, lamv, subln_g[0], lam_init)
    filt = (hy_pos_w1[0], hy_pos_b1[0], hy_freq1[0], hy_pos_w2[0], hy_pos_b2[0], hy_freq2[0], hy_pos_w3[0])
    hy = _hyena_call(u, x0, filt, hy_bias[0])
    h = _mixmlp_call(x, att, hy, mod_l[0], mix_w_out, mix_b_out[0], norm2_g[0], mlp_w1, mlp_w2, layer=0)

    return _poolmlp_call(h, mod_l[1], norm1_g[1], pool_w[0].astype(BF16), pool_scale[0], norm2_g[1],
                         mlp_w1, mlp_w2, final_g, layer=1)
```

```python
import functools
import math

import numpy as np
import jax
import jax.numpy as jnp
from jax import lax
from jax.experimental import pallas as pl
from jax.experimental.pallas import tpu as pltpu

F32 = jnp.float32
BF16 = jnp.bfloat16

GRID_W = 64
N_MOD = 6
ATT_HEADS = 4
ATT_V_DIM = 128
ATT_QK_DIM = 64
ROPE_BASE = 10000.0
HY_POS_EMB = 33
HY_DECAY_TARGET = 1e-2
HY_FAST_DECAY_PCT = 0.3
HY_SLOW_DECAY_PCT = 1.5
POOL_WINDOWS = (2, 4, 8, 16)
NORM_EPS = 1e-6
SUBLN_EPS = 1e-5
SUBLANES = 8
LANES = 128
FFT_N2 = 64
POOL_HALO = 8
WEIGHT_STAGE_BYTES = 2 * 1024 * 1024
ROW_BLOCK = 1024
ROW_SUBTILES = 2
MOD_ROWS = 8

VMEM_LIMIT_BYTES = 56 * 1024 * 1024


def _params(n_grid_dims):
    return pltpu.CompilerParams(
        dimension_semantics=("arbitrary",) * n_grid_dims,
        vmem_limit_bytes=VMEM_LIMIT_BYTES,
    )


def _const_spec(shape):
    nd = len(shape)
    return pl.BlockSpec(shape, lambda *_: (0,) * nd, pipeline_mode=pl.Buffered(1))


def _split_bf16(a):
    hi = a.astype(BF16)
    lo = (a - hi.astype(F32)).astype(BF16)
    return hi, lo


def _dot(a, b):
    return jnp.dot(a, b, preferred_element_type=F32)


def _dot3(a_hi, a_lo, b_hi, b_lo):
    return _dot(a_hi, b_hi) + (_dot(a_lo, b_hi) + _dot(a_hi, b_lo))


def _dot3f(a, b):
    a_hi, a_lo = _split_bf16(a)
    b_hi, b_lo = _split_bf16(b)
    return _dot3(a_hi, a_lo, b_hi, b_lo)


def _dft_dot(a_bf16, b):
    return _dot(a_bf16, b.astype(BF16))


def _rms(x, eps):
    return x * lax.rsqrt(jnp.mean(x * x, axis=-1, keepdims=True) + eps)


def _mod_kernel(cv_ref, w_ref, b_ref, o_ref):
    cv = cv_ref[...]
    s = cv / (1.0 + jnp.exp(-cv))
    o_ref[0] = _dot3f(s, w_ref[0]) + b_ref[0]


def _mod_call(cv, ada_w, ada_b, tn=1536):
    depth, d, n = ada_w.shape
    return pl.pallas_call(
        _mod_kernel,
        grid=(depth, n // tn),
        in_specs=[
            pl.BlockSpec((MOD_ROWS, d), lambda i, j: (0, 0)),
            pl.BlockSpec((1, d, tn), lambda i, j: (i, 0, j)),
            pl.BlockSpec((1, 1, tn), lambda i, j: (i, 0, j)),
        ],
        out_specs=pl.BlockSpec((1, MOD_ROWS, tn), lambda i, j: (i, 0, j)),
        out_shape=jax.ShapeDtypeStruct((depth, MOD_ROWS, n), F32),
        compiler_params=_params(2),
        name="mod",
    )(cv, ada_w, ada_b.reshape(depth, 1, n))


def _mod_spec(mod, layer):
    return pl.BlockSpec((1,) + mod.shape[1:], lambda *_: (layer, 0, 0), pipeline_mode=pl.Buffered(1))


def _ctxproj_kernel(x_ref, mod_ref, g_ref, w_hbm, b_ref, k_ref, v_ref, w_s, *, ctx_row, col0):
    d = x_ref.shape[2]
    nkv = w_s.shape[1]

    @pl.when(pl.program_id(0) == 0)
    def _():
        _stage_weights_bf16([(w_hbm.at[0, :, pl.ds(col0, nkv)], w_s)])

    m = mod_ref[0, ctx_row:ctx_row + 1, :]
    a = _rms(x_ref[0], NORM_EPS) * g_ref[...]
    a = a * (1.0 + m[:, d:2 * d]) + m[:, 0:d]
    p = _dot(a.astype(BF16), w_s[...]) + b_ref[0:1, col0:col0 + nkv]
    nk = k_ref.shape[2]
    k_ref[0] = p[:, :nk].astype(k_ref.dtype)
    v_ref[0] = p[:, nk:].astype(v_ref.dtype)


def _ctxproj_call(x, mod, ctx_row, g, w_all, b_all, col0, nk, nkv):
    bsz, s, d = x.shape
    return pl.pallas_call(
        functools.partial(_ctxproj_kernel, ctx_row=ctx_row, col0=col0),
        grid=(bsz,),
        in_specs=[pl.BlockSpec((1, s, d), lambda bi: (bi, 0, 0)), _mod_spec(mod, 0), _const_spec((1, d)),
                  pl.BlockSpec(memory_space=pl.ANY), _const_spec(b_all.shape)],
        out_specs=[pl.BlockSpec((1, s, nk), lambda bi: (bi, 0, 0)), pl.BlockSpec((1, s, nkv - nk), lambda bi: (bi, 0, 0))],
        out_shape=[jax.ShapeDtypeStruct((bsz, s, nk), BF16), jax.ShapeDtypeStruct((bsz, s, nkv - nk), BF16)],
        scratch_shapes=[pltpu.VMEM((d, nkv), BF16)],
        compiler_params=_params(1),
        name="ctxproj",
    )(x, mod, g.reshape(1, d), w_all, b_all)


def _latproj_kernel(x_ref, xp_ref, xn_ref, mod_ref, g_ref, w_hbm, b_ref, cos_ref, sa_ref, sb_ref,
                    cw_ref, cb_ref, q_ref, k_ref, v_ref, u_ref, x0_ref, w_ref, *, n_att, q_scale, seq_len):
    @pl.when((pl.program_id(0) == 0) & (pl.program_id(1) == 0))
    def _():
        _stage_weights_bf16([(w_hbm.at[0], w_ref)])

    tm = x_ref.shape[1] // ROW_SUBTILES
    d = x_ref.shape[2]
    mod = mod_ref[0, pl.ds(pl.program_id(0), 1), :]
    halo = xp_ref.shape[1]
    rows = tm + 2 * halo
    ncb = u_ref.shape[1]
    c = ncb * LANES
    qk = q_ref.shape[2]
    for sub in range(ROW_SUBTILES):
        r0 = sub * tm
        above = xp_ref[0] if sub == 0 else x_ref[0, r0 - halo:r0, :]
        below = xn_ref[0] if sub == ROW_SUBTILES - 1 else x_ref[0, r0 + tm:r0 + tm + halo, :]
        xx = jnp.concatenate([above, x_ref[0, r0:r0 + tm, :], below], axis=0)
        a = _rms(xx, NORM_EPS) * g_ref[...]
        a = a * (1.0 + mod[:, d:2 * d]) + mod[:, 0:d]

        ph = _dot(a.astype(BF16), w_ref[:, n_att:]) + b_ref[0:1, n_att:]
        t = lax.broadcasted_iota(jnp.int32, (rows, 1), 0) + (pl.program_id(1) * (tm * ROW_SUBTILES) + r0 - halo)
        ph = jnp.where((t >= 0) & (t < seq_len), ph, 0.0)

        def conv(s):
            blk = ph[:, s * c:(s + 1) * c]
            prev = pltpu.roll(blk, 1, 0)[halo:halo + tm]
            nxt = pltpu.roll(blk, rows - 1, 0)[halo:halo + tm]
            cw = cw_ref[:, s * c:(s + 1) * c]
            return prev * cw[0:1] + blk[halo:halo + tm] * cw[1:2] + nxt * cw[2:3] + cb_ref[:, s * c:(s + 1) * c]

        x0 = conv(0)
        u = conv(2) * conv(1)
        nb = FFT_N2 // SUBLANES
        srows = slice(r0 // SUBLANES, (r0 + tm) // SUBLANES)
        for val, o_ref in ((x0, x0_ref), (u, u_ref)):
            for cb in range(ncb):
                v4 = val[:, cb * LANES:(cb + 1) * LANES].reshape(tm // FFT_N2, nb, SUBLANES, LANES)
                for jb in range(nb):
                    o_ref[0, cb, jb, srows, :] = v4[:, jb].reshape(tm // SUBLANES, LANES)

        p = _dot(a[halo:halo + tm].astype(BF16), w_ref[:, :n_att]) + b_ref[0:1, :n_att]
        cos, sa, sb = (ref[r0:r0 + tm, :] for ref in (cos_ref, sa_ref, sb_ref))
        for o_ref, c_lo, scale in ((q_ref, 0, q_scale), (k_ref, qk, 1.0)):
            for c0 in range(0, qk, LANES):
                blk = p[:, c_lo + c0:c_lo + c0 + LANES]
                r = blk * cos + pltpu.roll(blk, 16, 1) * sa + pltpu.roll(blk, LANES - 16, 1) * sb
                o_ref[0, r0:r0 + tm, c0:c0 + LANES] = (r * scale).astype(o_ref.dtype)
        v_ref[0, r0:r0 + tm, :] = p[:, 2 * qk:].astype(v_ref.dtype)


def _latproj_call(x, mod, g, w_all, b_all, rope_tabs, conv_w, conv_b, n_att, qk, q_scale, tm=ROW_BLOCK):
    bsz, s, d = x.shape
    n = w_all.shape[2]
    c = (n - n_att) // 3
    halo = SUBLANES
    nb = tm // halo
    last = s // halo - 1
    tab = pl.BlockSpec((tm, LANES), lambda bi, i: (i, 0))
    row = lambda w: pl.BlockSpec((1, tm, w), lambda bi, i: (bi, i, 0))
    nslab = FFT_N2 // SUBLANES
    cblk = pl.BlockSpec((1, c // LANES, nslab, tm // nslab, LANES), lambda bi, i: (bi, 0, 0, i, 0))
    return pl.pallas_call(
        functools.partial(_latproj_kernel, n_att=n_att, q_scale=q_scale, seq_len=s),
        grid=(bsz, s // tm),
        in_specs=[row(d),
                  pl.BlockSpec((1, halo, d), lambda bi, i: (bi, jnp.maximum(i * nb - 1, 0), 0)),
                  pl.BlockSpec((1, halo, d), lambda bi, i: (bi, jnp.minimum((i + 1) * nb, last), 0)),
                  _mod_spec(mod, 0), _const_spec((1, d)), pl.BlockSpec(memory_space=pl.ANY), _const_spec(b_all.shape),
                  tab, tab, tab,
                  _const_spec(conv_w.shape), _const_spec((1, 3 * c))],
        out_specs=[row(qk), row(qk), row(n_att - 2 * qk), cblk, cblk],
        out_shape=[jax.ShapeDtypeStruct((bsz, s, qk), BF16), jax.ShapeDtypeStruct((bsz, s, qk), BF16),
                   jax.ShapeDtypeStruct((bsz, s, n_att - 2 * qk), BF16),
                   jax.ShapeDtypeStruct((bsz, c // LANES, nslab, s // nslab, LANES), F32),
                   jax.ShapeDtypeStruct((bsz, c // LANES, nslab, s // nslab, LANES), F32)],
        scratch_shapes=[pltpu.VMEM((d, n), BF16)],
        compiler_params=_params(2),
        name="latproj",
    )(x, x, x, mod, g.reshape(1, d), w_all, b_all, *rope_tabs, conv_w, conv_b.reshape(1, 3 * c))


def _rope_tables(seq_len):
    axis_dim = ATT_QK_DIM // 2
    n_freq = axis_dim // 2
    inv = (ROPE_BASE ** (-np.arange(n_freq, dtype=np.float32) / n_freq)).astype(np.float32)
    t = np.arange(seq_len)
    row, col = t // GRID_W, t % GRID_W
    jj = np.arange(128) % ATT_QK_DIM
    is_col = (jj // axis_dim) == 1
    second = ((jj % axis_dim) >= n_freq)[None, :]
    pos = np.where(is_col[None, :], col[:, None], row[:, None]).astype(np.float32)
    ang = (pos * inv[jj % n_freq][None, :]).astype(np.float64)
    cos, sin = np.cos(ang), np.sin(ang)
    as32 = lambda m: jnp.asarray(m, dtype=F32)
    return as32(cos), as32(np.where(second, sin, 0.0)), as32(np.where(second, 0.0, -sin))


def _attn_kernel(lam_ref, q_ref, kc_ref, vc_ref, kl_ref, vl_ref, g_ref, o_ref, *, tk, lam_init):
    q = q_ref[0]
    tq = q.shape[0]
    lane = lax.broadcasted_iota(jnp.int32, q.shape, 1)
    zero = jnp.zeros_like(q)
    q2 = jnp.concatenate([jnp.where(lane < ATT_QK_DIM, q, zero), jnp.where(lane >= ATT_QK_DIM, q, zero)], axis=0)

    def chunk(k, v, m, acc):
        s = lax.dot_general(q2, k, (((1,), (1,)), ((), ())), preferred_element_type=F32)
        rowmax = jnp.max(s, axis=1, keepdims=True)
        m_new = rowmax if m is None else jnp.maximum(m, rowmax)
        p = jnp.exp2(s - m_new).astype(BF16)
        pv = _dot(p, jnp.concatenate([v, jnp.ones_like(v)], axis=1))
        return m_new, (pv if acc is None else jnp.exp2(m - m_new) * acc + pv)

    m, acc = chunk(kc_ref[0], vc_ref[0], None, None)
    for j in range(kl_ref.shape[1] // tk):
        m, acc = chunk(kl_ref[0, j * tk:(j + 1) * tk, :], vl_ref[0, j * tk:(j + 1) * tk, :], m, acc)

    lamv = lam_ref[...]
    lam = (jnp.exp(jnp.sum(lamv[0:1] * lamv[1:2], axis=1, keepdims=True))
           - jnp.exp(jnp.sum(lamv[2:3] * lamv[3:4], axis=1, keepdims=True)) + lam_init)
    o_all = acc[:, :ATT_V_DIM] / acc[:, ATT_V_DIM:]
    o = o_all[:tq] - lam * o_all[tq:]
    o = _rms(o, SUBLN_EPS) * g_ref[...] * (1.0 - lam_init)
    o_ref[0] = o.astype(o_ref.dtype)


def _attn_call(q, kc, vc, kl, vl, lamv, g, lam_init, tq=1024, tk=256):
    bsz, l, width = q.shape
    heads = width // ATT_V_DIM
    lc = kc.shape[1]
    hd = ATT_V_DIM
    tq = min(tq, l)
    return pl.pallas_call(
        functools.partial(_attn_kernel, tk=tk, lam_init=lam_init),
        grid=(bsz, heads, l // tq),
        in_specs=[
            _const_spec(lamv.shape),
            pl.BlockSpec((1, tq, hd), lambda b, h, i: (b, i, h)),
            pl.BlockSpec((1, lc, hd), lambda b, h, i: (b, 0, h)),
            pl.BlockSpec((1, lc, hd), lambda b, h, i: (b, 0, h)),
            pl.BlockSpec((1, l, hd), lambda b, h, i: (b, 0, h)),
            pl.BlockSpec((1, l, hd), lambda b, h, i: (b, 0, h)),
            _const_spec((1, hd)),
        ],
        out_specs=pl.BlockSpec((1, tq, hd), lambda b, h, i: (b, i, h)),
        out_shape=jax.ShapeDtypeStruct((bsz, l, width), BF16),
        compiler_params=_params(3),
        name="diffattn",
    )(lamv, q, kc, vc, kl, vl, g.reshape(1, hd))


def _gather_minor(refs, n_rows):
    cols = []
    for j in range(SUBLANES):
        parts = [r[pl.ds(j, n_rows, stride=SUBLANES), :] for r in refs]
        cols.append(parts[0] if len(parts) == 1 else jnp.concatenate(parts, axis=0))
    return jnp.concatenate(cols, axis=1)


def _spectrum_store(val, spec_ref, jb):
    rows = val.shape[0]
    for j in range(SUBLANES):
        spec_ref[jb, pl.ds(j, rows, stride=SUBLANES), :] = val[:, j * LANES:(j + 1) * LANES]


def _spectrum_blocks(spec_ref, k0):
    start = k0 * SUBLANES if isinstance(k0, int) else pl.multiple_of(k0 * SUBLANES, SUBLANES * SUBLANES)
    x = spec_ref[:, pl.ds(start, SUBLANES * SUBLANES), :]
    return jnp.stack([x[:, i * SUBLANES:(i + 1) * SUBLANES, :].reshape(FFT_N2, LANES) for i in range(SUBLANES)], axis=0)


def _spectrum_put(spec_ref, k0, val):
    nb = FFT_N2 // SUBLANES
    v = val.reshape(SUBLANES, nb, SUBLANES, LANES)
    x = jnp.stack([v[:, jb].reshape(SUBLANES * SUBLANES, LANES) for jb in range(nb)], axis=0)
    spec_ref[:, pl.ds(pl.multiple_of(k0 * SUBLANES, SUBLANES * SUBLANES), SUBLANES * SUBLANES), :] = x


def _twiddle_block(twc_ref, twf_ref, kb):
    n2 = FFT_N2
    rows = pl.ds(pl.multiple_of(kb * n2, n2), n2)
    cr, ci = twc_ref[0, rows, :][None], twc_ref[1, rows, :][None]
    fr, fi = twf_ref[0].reshape(SUBLANES, n2, LANES), twf_ref[1].reshape(SUBLANES, n2, LANES)
    return cr * fr - ci * fi, cr * fi + ci * fr


def _filter_fft_kernel(z_ref, w1_ref, b1_ref, f1_ref, w2_ref, b2_ref, f2_ref, w3f_ref, w3b_ref, dl_ref,
                       fr_ref, g_ref, twc_ref, twf_ref, kr_ref, ki_ref, h2_ref, hs_ref, a_ref, *, tm):
    half = z_ref.shape[0]
    seq_len = 2 * half
    hid = w1_ref.shape[0]
    n1 = a_ref.shape[2] // SUBLANES
    n1h = hs_ref.shape[2] // SUBLANES
    n2 = FFT_N2
    nb = n2 // SUBLANES
    w = SUBLANES * LANES
    zeros = jnp.zeros((hid, hid), F32)

    @pl.when(pl.program_id(0) == 0)
    def _():
        blockdiag = lambda m: jnp.concatenate([jnp.concatenate([m, zeros], axis=1),
                                               jnp.concatenate([zeros, m], axis=1)], axis=0)
        twice = lambda ref: jnp.concatenate([ref[...], ref[...]], axis=1)
        w1, w2 = blockdiag(w1_ref[...]), blockdiag(w2_ref[...])
        b1, f1, b2, f2 = twice(b1_ref), twice(f1_ref), twice(b2_ref), twice(f2_ref)

        def features(i, _):
            r = pl.ds(pl.multiple_of(i * tm, tm), tm)
            h1 = jnp.sin(f1 * (_dot3f(z_ref[r, :], w1) + b1))
            h2_ref[r, :] = jnp.sin(f2 * (_dot3f(h1, w2) + b2))
            return 0

        lax.fori_loop(0, half // tm, features, 0)

    for upper in range(2):
        pad = lambda m: jnp.concatenate([jnp.zeros_like(m), m] if upper else [m, jnp.zeros_like(m)], axis=0)
        w3s = [pad(w3f_ref[...]), pad(w3b_ref[...])]

        def taps(i, _, upper=upper, w3s=w3s):
            h2 = h2_ref[pl.ds(pl.multiple_of(i * tm, tm), tm), :]
            t0 = upper * half + i * tm
            row = lax.broadcasted_iota(jnp.int32, (tm, LANES), 0) + t0
            decay = jnp.exp(-(row.astype(F32) * (1.0 / (seq_len - 1))) * dl_ref[...])
            for t in range(2):
                h = _dot3f(h2, w3s[t]) * decay
                if t == 1:
                    h = jnp.where(row == 0, 0.0, h)
                h4 = h.reshape(tm // n2, nb, SUBLANES, LANES)
                for jb in range(nb):
                    srow = pl.multiple_of(upper * (half // nb) + i * (tm // nb), tm // nb)
                    hs_ref[t, jb, pl.ds(srow, tm // nb), :] = h4[:, jb].reshape(tm // nb, LANES)
            return 0

        lax.fori_loop(0, half // tm, taps, 0)

    for jb in range(nb):
        z = jnp.concatenate([_gather_minor([hs_ref.at[t, jb]], n1h) for t in range(2)], axis=1)
        out = _dft_dot(fr_ref[...], z)
        for t in range(4):
            _spectrum_store(out[(t % 2) * n1:(t % 2 + 1) * n1, (t // 2) * w:(t // 2 + 1) * w], a_ref.at[t], jb)

    def body(kb, _):
        k0 = pl.multiple_of(kb * SUBLANES, SUBLANES)
        twr, twi = _twiddle_block(twc_ref, twf_ref, kb)
        a = [_spectrum_blocks(a_ref.at[t], k0) for t in range(4)]
        cols = []
        for t in range(2):
            ar, ai = a[2 * t], a[2 * t + 1]
            sr, si = ar * twr - ai * twi, ar * twi + ai * twr
            cols += [jnp.concatenate([sr[i], si[i]], axis=0) for i in range(SUBLANES)]
        x = _dft_dot(g_ref[...], jnp.concatenate(cols, axis=1))
        for i in range(SUBLANES):
            xf = x[:, i * LANES:(i + 1) * LANES]
            xb = x[:, (SUBLANES + i) * LANES:(SUBLANES + i + 1) * LANES]
            rows = pl.ds(pl.multiple_of((k0 + i) * n2, n2), n2)
            kr_ref[rows, :] = xf[:n2] + xb[:n2]
            ki_ref[rows, :] = xf[n2:] - xb[n2:]
        return 0

    lax.fori_loop(0, n1 // SUBLANES, body, 0, unroll=4)


def _filter_fft_call(dft, z, w1, b1, f1, w2, b2, f2, w3, absdelta, tm=512):
    hid = z.shape[1] // 2
    seq_len = 2 * z.shape[0]
    c = w3.shape[1] // 2
    n, n1, n1h = dft["n"], dft["n1"], dft["n1h"]
    nb = FFT_N2 // SUBLANES
    ncb = c // LANES
    fr, g = dft["f_real"].astype(BF16), dft["g_fwd"].astype(BF16)
    twc, twf = dft["tw_coarse"], dft["tw_fine"]
    vec = _const_spec((1, hid))
    w3_blk = lambda off: pl.BlockSpec((hid, LANES), lambda i, off=off: (0, i + off))
    out = pl.BlockSpec((None, n, LANES), lambda i: (i, 0, 0))
    return pl.pallas_call(
        functools.partial(_filter_fft_kernel, tm=tm),
        grid=(ncb,),
        in_specs=[_const_spec(z.shape), _const_spec((hid, hid)), vec, vec, _const_spec((hid, hid)), vec, vec,
                  w3_blk(0), w3_blk(ncb), pl.BlockSpec((1, LANES), lambda i: (0, i)),
                  _const_spec(fr.shape), _const_spec(g.shape), _const_spec(twc.shape), _const_spec(twf.shape)],
        out_specs=[out, out],
        out_shape=[jax.ShapeDtypeStruct((ncb, n, LANES), F32)] * 2,
        scratch_shapes=[pltpu.VMEM(z.shape, F32), pltpu.VMEM((2, nb, n1h * SUBLANES, LANES), F32),
                        pltpu.VMEM((4, nb, n1 * SUBLANES, LANES), F32)],
        compiler_params=_params(1),
        name="hyfilter_fft",
    )(z, w1, b1.reshape(1, hid), f1.reshape(1, hid), w2, b2.reshape(1, hid), f2.reshape(1, hid), w3, w3, absdelta,
      fr, g, twc, twf)


def _hyena_conv_kernel(fc_ref, g_ref, gc_ref, e_ref, twc_ref, twf_ref, kr_ref, ki_ref, u_ref, x0_ref, bias_ref, o_ref,
                       a_ref, *, inv_n):
    n1 = a_ref.shape[2] // SUBLANES
    n1h = u_ref.shape[2] // SUBLANES
    n2 = FFT_N2
    nb = n2 // SUBLANES

    for jb in range(nb):
        z = _gather_minor([u_ref.at[0, jb], u_ref.at[1, jb]], n1h)
        out = _dft_dot(fc_ref[...], z)
        for c in range(2):
            _spectrum_store(out[c * n1:(c + 1) * n1], a_ref.at[c], jb)

    def body(kb, _):
        k0 = pl.multiple_of(kb * SUBLANES, SUBLANES)
        rows = pl.ds(pl.multiple_of(kb * (SUBLANES * n2), SUBLANES * n2), SUBLANES * n2)
        blocks = lambda ref: ref[rows, :].reshape(SUBLANES, n2, LANES)
        (twr, twi), kr, ki = _twiddle_block(twc_ref, twf_ref, kb), blocks(kr_ref), blocks(ki_ref)
        ar, ai = _spectrum_blocks(a_ref.at[0], k0), _spectrum_blocks(a_ref.at[1], k0)
        sr, si = ar * twr - ai * twi, ar * twi + ai * twr
        lanes = lambda re, im: jnp.concatenate([jnp.concatenate([re[i], im[i]], axis=0) for i in range(SUBLANES)], axis=1)
        x = _dft_dot(g_ref[...], lanes(sr, si))
        unl = lambda v, lo: jnp.stack([v[lo:lo + n2, i * LANES:(i + 1) * LANES] for i in range(SUBLANES)], axis=0)
        xr, xi = unl(x, 0), unl(x, n2)
        t = _dft_dot(gc_ref[...], lanes(xr * kr - xi * ki, xr * ki + xi * kr))
        tr, ti = unl(t, 0), unl(t, n2)
        _spectrum_put(a_ref.at[0], k0, tr * twr + ti * twi)
        _spectrum_put(a_ref.at[1], k0, ti * twr - tr * twi)
        return 0

    lax.fori_loop(0, n1 // SUBLANES, body, 0, unroll=4)

    for jb in range(nb):
        t = _gather_minor([a_ref.at[0, jb], a_ref.at[1, jb]], n1)
        y = _dft_dot(e_ref[...], t) * inv_n
        for c in range(2):
            _spectrum_store(y[c * n1h:(c + 1) * n1h], o_ref.at[c], jb)
    o_ref[...] = x0_ref[...] * (o_ref[...] + u_ref[...] * bias_ref[...])


def _dft_blocks(seq_len):
    n = 2 * seq_len
    n2 = FFT_N2
    n1 = n // n2
    n1h = n1 // 2
    k1 = np.arange(n1)[:, None].astype(np.float64)
    a = 2.0 * np.pi * k1 * np.arange(n1h)[None, :] / n1
    fr, fi = np.cos(a), -np.sin(a)
    f_cplx = np.block([[fr, -fi], [fi, fr]])
    f_real = np.concatenate([fr, fi], axis=0)
    e_cplx = np.block([[fr.T, fi.T], [-fi.T, fr.T]])
    b = 2.0 * np.pi * np.arange(n2)[:, None] * np.arange(n2)[None, :] / n2
    gr, gi = np.cos(b), -np.sin(b)
    g_fwd = np.block([[gr, -gi], [gi, gr]])
    g_inv = np.block([[gr, gi], [-gi, gr]])
    def table(k1s):
        ang = (2.0 * np.pi / n) * (k1s[:, None] * np.arange(n2)[None, :]).reshape(-1, 1)
        return np.broadcast_to(np.stack([np.cos(ang), -np.sin(ang)]), (2, k1s.size * n2, LANES))
    as32 = lambda m: jnp.asarray(np.ascontiguousarray(m), dtype=F32)
    return dict(n=n, n1=n1, n1h=n1h, f_cplx=as32(f_cplx), f_real=as32(f_real), e_cplx=as32(e_cplx),
                g_fwd=as32(g_fwd), g_inv=as32(g_inv), tw_coarse=as32(table(np.arange(0, n1, SUBLANES))),
                tw_fine=as32(table(np.arange(SUBLANES))))


def _hyena_call(u, x0, filt, hy_bias):
    bsz, ncb, nslab, srows, _ = u.shape
    seq_len = nslab * srows
    c = ncb * LANES
    n2 = FFT_N2
    dft = _dft_blocks(seq_len)
    n, n1, n1h = dft["n"], dft["n1"], dft["n1h"]
    pairs = bsz // 2

    w1, b1, f1, w2, b2, f2, w3 = filt
    hid = w2.shape[0]
    bands = (HY_POS_EMB - 1) // 2
    t = np.linspace(0.0, 1.0, seq_len)[:, None]
    freqs = np.linspace(1e-4, bands - 1, bands)
    ang = (2.0 * math.pi / seq_len) * np.arange(seq_len)[:, None] * freqs[None, :]
    z = np.concatenate([t, np.cos(ang), -np.sin(ang), np.zeros((seq_len, hid - HY_POS_EMB))], axis=-1)
    z = jnp.asarray(np.concatenate([z[:seq_len // 2], z[seq_len // 2:]], axis=1), dtype=F32)
    w1p = jnp.concatenate([w1, jnp.zeros((hid - HY_POS_EMB, hid), F32)], axis=0)
    max_decay = math.log(HY_DECAY_TARGET) / HY_FAST_DECAY_PCT
    min_decay = math.log(HY_DECAY_TARGET) / HY_SLOW_DECAY_PCT
    absdelta = jnp.asarray(np.abs(np.linspace(min_decay, max_decay, c))[None, :], dtype=F32)
    nb = n2 // SUBLANES
    kr, ki = _filter_fft_call(dft, z, w1p, b1, f1, w2, b2, f2, w3, absdelta)

    view = lambda a: a.reshape(pairs, 2, ncb, nb, n1h * SUBLANES, LANES)
    pair_spec = pl.BlockSpec((None, 2, None, nb, n1h * SUBLANES, LANES), lambda cb, p: (p, 0, cb, 0, 0, 0))
    k_spec = pl.BlockSpec((None, n, LANES), lambda cb, p: (cb, 0, 0))
    twc, twf = dft["tw_coarse"], dft["tw_fine"]
    fc, g, gc, e = (dft[k].astype(BF16) for k in ("f_cplx", "g_fwd", "g_inv", "e_cplx"))
    hy = pl.pallas_call(
        functools.partial(_hyena_conv_kernel, inv_n=1.0 / n),
        grid=(ncb, pairs),
        in_specs=[_const_spec(fc.shape), _const_spec(g.shape), _const_spec(gc.shape), _const_spec(e.shape),
                  _const_spec(twc.shape), _const_spec(twf.shape), k_spec, k_spec, pair_spec, pair_spec,
                  pl.BlockSpec((1, LANES), lambda cb, p: (0, cb))],
        out_specs=pair_spec,
        out_shape=jax.ShapeDtypeStruct((pairs, 2, ncb, nb, n1h * SUBLANES, LANES), F32),
        scratch_shapes=[pltpu.VMEM((2, nb, n1 * SUBLANES, LANES), F32)],
        compiler_params=_params(2),
        name="hyconv",
    )(fc, g, gc, e, twc, twf, kr, ki, view(u), view(x0), hy_bias.reshape(1, c))
    return hy.reshape(bsz, ncb, nb, n1h * SUBLANES, LANES)


def _stage_weights_bf16(pairs):
    for src, dst in pairs:
        n_rows, n_cols = src.shape
        rows = min(n_rows, 1 << ((WEIGHT_STAGE_BYTES // (4 * n_cols)).bit_length() - 1))
        n_chunks = n_rows // rows
        assert rows % (2 * SUBLANES) == 0 and n_chunks * rows == n_rows

        def run(stage, sem, src=src, dst=dst, rows=rows, n_chunks=n_chunks):
            copy = lambda k: pltpu.make_async_copy(src.at[pl.ds(k * rows, rows), :], stage.at[k % 2], sem.at[k % 2])
            copy(0).start()
            for k in range(n_chunks):
                if k + 1 < n_chunks:
                    copy(k + 1).start()
                copy(k).wait()
                dst[pl.ds(k * rows, rows), :] = stage[k % 2].astype(BF16)

        pl.run_scoped(run, pltpu.VMEM((2, rows, n_cols), F32), pltpu.SemaphoreType.DMA((2,)))


def _mlp_tail(h, g2, shift, scale, gate, w1_ref, w2_ref, fc):
    a = (_rms(h, NORM_EPS) * g2) * (1.0 + scale) + shift
    a = a.astype(BF16)
    acc = None
    for c0 in range(0, w1_ref.shape[1], fc):
        hid = jnp.maximum(_dot(a, w1_ref[:, c0:c0 + fc]), 0.0)
        part = _dot((hid * hid).astype(BF16), w2_ref[c0:c0 + fc, :])
        acc = part if acc is None else acc + part
    return h + gate * acc


def _mixmlp_kernel(x_ref, att_ref, hy_ref, mod_ref, wo_hbm, bo_ref, g2_ref, w1_hbm, w2_hbm, o_ref,
                   wo_s, w1_s, w2_s, *, fc, layer):
    @pl.when((pl.program_id(0) == 0) & (pl.program_id(1) == 0))
    def _():
        _stage_weights_bf16([(wo_hbm.at[layer // 2], wo_s), (w1_hbm.at[layer], w1_s), (w2_hbm.at[layer], w2_s)])

    blk, d = x_ref.shape[1:]
    tm = blk // ROW_SUBTILES
    wa = att_ref.shape[2]
    mod = mod_ref[0, pl.ds(pl.program_id(0), 1), :]
    for r in range(ROW_SUBTILES):
        rows = slice(r * tm, (r + 1) * tm)
        hy = jnp.concatenate([_spectrum_blocks(hy_ref.at[0, cb], r * tm // FFT_N2).reshape(tm, LANES)
                              for cb in range(hy_ref.shape[1])], axis=1).astype(BF16)
        y = _dot(att_ref[0, rows, :], wo_s[:wa, :]) + _dot(hy, wo_s[wa:, :]) + bo_ref[...]
        h = x_ref[0, rows, :] + mod[:, 2 * d:3 * d] * y
        o_ref[0, rows, :] = _mlp_tail(h, g2_ref[...], mod[:, 3 * d:4 * d], mod[:, 4 * d:5 * d], mod[:, 5 * d:6 * d],
                                      w1_s, w2_s, fc)


def _mixmlp_call(x, att, hy, mod, w_out, b_out, g2, w1, w2, layer, tm=ROW_BLOCK, fc=1024):
    bsz, l, d = x.shape
    wa = att.shape[2]
    dff = w1.shape[2]
    row = lambda w: pl.BlockSpec((1, tm, w), lambda b, i: (b, i, 0))
    hbm = pl.BlockSpec(memory_space=pl.ANY)
    return pl.pallas_call(
        functools.partial(_mixmlp_kernel, fc=fc, layer=layer),
        grid=(bsz, l // tm),
        in_specs=[row(d), row(wa),
                  pl.BlockSpec((1, hy.shape[1], hy.shape[2], tm // hy.shape[2], LANES), lambda b, i: (b, 0, 0, i, 0)),
                  _mod_spec(mod, layer),
                  hbm, _const_spec((1, d)), _const_spec((1, d)), hbm, hbm],
        out_specs=row(d),
        out_shape=jax.ShapeDtypeStruct((bsz, l, d), F32),
        scratch_shapes=[pltpu.VMEM((d, d), BF16), pltpu.VMEM((d, dff), BF16), pltpu.VMEM((dff, d), BF16)],
        compiler_params=_params(2),
        name="mixmlp",
    )(x, att, hy, mod, w_out, b_out.reshape(1, d), g2.reshape(1, d), w1, w2)


def _poolmlp_kernel(h_ref, hp_ref, hn_ref, mod_ref, g1_ref, pw_hbm, ps_ref, g2_ref, w1_hbm, w2_hbm, gf_ref, o_ref,
                    pw_s, w1_s, w2_s, *, fc, seq_len, layer):
    @pl.when((pl.program_id(0) == 0) & (pl.program_id(1) == 0))
    def _():
        _stage_weights_bf16([(pw_hbm.at[layer // 2], pw_s), (w1_hbm.at[layer], w1_s), (w2_hbm.at[layer], w2_s)])

    blk, d = h_ref.shape[1:]
    tm = blk // ROW_SUBTILES
    halo = POOL_HALO
    rows = tm + 2 * halo
    mod = mod_ref[0, pl.ds(pl.program_id(0), 1), :]
    gd = d // len(POOL_WINDOWS)
    for r in range(ROW_SUBTILES):
        r0 = r * tm
        h = h_ref[0, r0:r0 + tm, :]
        above = hp_ref[0] if r == 0 else h_ref[0, r0 - halo:r0, :]
        below = hn_ref[0] if r == ROW_SUBTILES - 1 else h_ref[0, r0 + tm:r0 + tm + halo, :]
        hx = jnp.concatenate([above, h, below], axis=0)
        a = (_rms(hx, NORM_EPS) * g1_ref[...]) * (1.0 + mod[:, d:2 * d]) + mod[:, 0:d]
        t = lax.broadcasted_iota(jnp.int32, (rows, 1), 0) + (pl.program_id(1) * blk + r0 - halo)
        a = jnp.where((t >= 0) & (t < seq_len), a, 0.0)
        tc = t[halo:halo + tm]
        ys = []
        for g, win in enumerate(POOL_WINDOWS):
            ag = a[:, g * gd:(g + 1) * gd]
            f, m = ag, 1
            while 2 * m <= win // 2:
                f = f + pltpu.roll(f, rows - m, 0)
                m *= 2
            s = pltpu.roll(f, win // 2, 0) + f
            cnt = (jnp.minimum(tc + (win - win // 2), seq_len) - jnp.maximum(tc - win // 2, 0)).astype(F32)
            dlt = s[halo:halo + tm] / cnt - ag[halo:halo + tm]
            ys.append(_dot(dlt.astype(BF16), pw_s[g * gd:(g + 1) * gd, :]))
        y = jnp.concatenate(ys, axis=1) * ps_ref[...]
        h1 = h + mod[:, 2 * d:3 * d] * y
        h2 = _mlp_tail(h1, g2_ref[...], mod[:, 3 * d:4 * d], mod[:, 4 * d:5 * d], mod[:, 5 * d:6 * d], w1_s, w2_s, fc)
        o_ref[0, r0:r0 + tm, :] = _rms(h2, NORM_EPS) * gf_ref[...]


def _poolmlp_call(h, mod, g1, pool_w, pool_scale, g2, w1, w2, gf, layer, tm=ROW_BLOCK, fc=1024):
    bsz, l, d = h.shape
    dff = w1.shape[2]
    gd = d // len(POOL_WINDOWS)
    halo = POOL_HALO
    nb = tm // halo
    last = l // halo - 1
    row = pl.BlockSpec((1, tm, d), lambda b, i: (b, i, 0))
    hbm = pl.BlockSpec(memory_space=pl.ANY)
    return pl.pallas_call(
        functools.partial(_poolmlp_kernel, fc=fc, seq_len=l, layer=layer),
        grid=(bsz, l // tm),
        in_specs=[row,
                  pl.BlockSpec((1, halo, d), lambda b, i: (b, jnp.maximum(i * nb - 1, 0), 0)),
                  pl.BlockSpec((1, halo, d), lambda b, i: (b, jnp.minimum((i + 1) * nb, last), 0)),
                  _mod_spec(mod, layer),
                  _const_spec((1, d)), hbm, _const_spec((1, d)), _const_spec((1, d)),
                  hbm, hbm, _const_spec((1, d))],
        out_specs=row,
        out_shape=jax.ShapeDtypeStruct((bsz, l, d), F32),
        scratch_shapes=[pltpu.VMEM((d, gd), BF16), pltpu.VMEM((d, dff), BF16), pltpu.VMEM((dff, d), BF16)],
        compiler_params=_params(2),
        name="poolmlp",
    )(h, h, h, mod, g1.reshape(1, d), pool_w.reshape(pool_w.shape[0], d, gd), pool_scale.reshape(1, d),
      g2.reshape(1, d), w1, w2, gf.reshape(1, d))


def kernel(x, c, ctx, c_ctx, ada_w, ada_b, norm1_g, norm2_g, mix_w_in, mix_b_in, mix_w_out, mix_b_out, lam_q1, lam_k1, lam_q2, lam_k2, subln_g, hy_conv_w, hy_conv_b, hy_pos_w1, hy_pos_b1, hy_freq1, hy_pos_w2, hy_pos_b2, hy_freq2, hy_pos_w3, hy_bias, pool_w, pool_scale, mlp_w1, mlp_w2, final_g):
    bsz, seq_len, d = x.shape
    depth = ada_w.shape[0]
    assert depth == 2 and bsz % 2 == 0 and bsz < MOD_ROWS and seq_len % GRID_W == 0
    att_w = ATT_HEADS * ATT_V_DIM
    q_cols = k_cols = ATT_HEADS * 2 * ATT_QK_DIM
    kv_start, hy_start = q_cols, q_cols + k_cols + att_w
    in_cols = mix_w_in.shape[2]

    cv = jnp.concatenate([c, c_ctx[None, :], jnp.zeros((MOD_ROWS - bsz - 1, d), F32)], axis=0)
    mod = _mod_call(cv, ada_w, ada_b)

    lam_init = 0.8 - 0.6 * math.exp(-0.3 * 0)
    assert k_cols == q_cols and in_cols - hy_start == 3 * (d - att_w)
    q, k, v, u, x0 = _latproj_call(
        x, mod, norm1_g[0], mix_w_in, mix_b_in, _rope_tables(seq_len),
        hy_conv_w[0], hy_conv_b[0], n_att=hy_start, qk=q_cols, q_scale=ATT_QK_DIM ** -0.5 * math.log2(math.e))
    kc, vc = _ctxproj_call(ctx, mod, bsz, norm1_g[0], mix_w_in, mix_b_in, kv_start, k_cols, hy_start - kv_start)
    lamv = jnp.stack([lam_q1[0], lam_k1[0], lam_q2[0], lam_k2[0]], axis=0)
    att = _attn_call(q, kc, vc, k, v, lamv, subln_g[0], lam_init)
    filt = (hy_pos_w1[0], hy_pos_b1[0], hy_freq1[0], hy_pos_w2[0], hy_pos_b2[0], hy_freq2[0], hy_pos_w3[0])
    hy = _hyena_call(u, x0, filt, hy_bias[0])
    h = _mixmlp_call(x, att, hy, mod, mix_w_out, mix_b_out[0], norm2_g[0], mlp_w1, mlp_w2, layer=0)

    return _poolmlp_call(h, mod, norm1_g[1], pool_w, pool_scale[0], norm2_g[1], mlp_w1, mlp_w2, final_g, layer=1)
```

```python
import functools
import math

import numpy as np
import jax
import jax.numpy as jnp
from jax import lax
from jax.experimental import pallas as pl
from jax.experimental.pallas import tpu as pltpu

F32 = jnp.float32
BF16 = jnp.bfloat16

GRID_W = 64
ATT_HEADS = 4
ATT_V_DIM = 128
ATT_QK_DIM = 64
ROPE_BASE = 10000.0
HY_POS_EMB = 33
HY_DECAY_TARGET = 1e-2
HY_FAST_DECAY_PCT = 0.3
HY_SLOW_DECAY_PCT = 1.5
POOL_WINDOWS = (2, 4, 8, 16)
NORM_EPS = 1e-6
SUBLN_EPS = 1e-5
SUBLANES = 8
LANES = 128
FFT_N2 = 64
POOL_HALO = 8
WEIGHT_STAGE_BYTES = 2 * 1024 * 1024
ROW_BLOCK = 1024
ROW_SUBTILES = 2
MOD_ROWS = 8

VMEM_LIMIT_BYTES = 56 * 1024 * 1024


def _params(n_grid_dims):
    return pltpu.CompilerParams(
        dimension_semantics=("arbitrary",) * n_grid_dims,
        vmem_limit_bytes=VMEM_LIMIT_BYTES,
    )


def _const_spec(shape):
    nd = len(shape)
    return pl.BlockSpec(shape, lambda *_: (0,) * nd, pipeline_mode=pl.Buffered(1))


def _split_bf16(a):
    hi = a.astype(BF16)
    lo = (a - hi.astype(F32)).astype(BF16)
    return hi, lo


def _dot(a, b):
    return jnp.dot(a, b, preferred_element_type=F32)


def _dot3(a_hi, a_lo, b_hi, b_lo):
    return _dot(a_hi, b_hi) + (_dot(a_lo, b_hi) + _dot(a_hi, b_lo))


def _dot3f(a, b):
    a_hi, a_lo = _split_bf16(a)
    b_hi, b_lo = _split_bf16(b)
    return _dot3(a_hi, a_lo, b_hi, b_lo)


def _dft_dot(a_bf16, b):
    return _dot(a_bf16, b.astype(BF16))


def _rms(x, eps):
    return x * lax.rsqrt(jnp.mean(x * x, axis=-1, keepdims=True) + eps)


def _mod_kernel(cv_ref, w_ref, b_ref, o_ref):
    cv = cv_ref[...]
    s_hi, s_lo = _split_bf16(cv / (1.0 + jnp.exp(-cv)))
    w_hi, w_lo = _split_bf16(w_ref[0])
    r = _dot(jnp.concatenate([s_hi, s_lo], axis=0), w_hi)
    o_ref[0] = r[:MOD_ROWS] + (r[MOD_ROWS:] + _dot(s_hi, w_lo)) + b_ref[0]


def _mod_call(cv, ada_w, ada_b, tn=1536):
    depth, d, n = ada_w.shape
    return pl.pallas_call(
        _mod_kernel,
        grid=(depth, n // tn),
        in_specs=[
            pl.BlockSpec((MOD_ROWS, d), lambda i, j: (0, 0)),
            pl.BlockSpec((1, d, tn), lambda i, j: (i, 0, j)),
            pl.BlockSpec((1, 1, tn), lambda i, j: (i, 0, j)),
        ],
        out_specs=pl.BlockSpec((1, MOD_ROWS, tn), lambda i, j: (i, 0, j)),
        out_shape=jax.ShapeDtypeStruct((depth, MOD_ROWS, n), F32),
        compiler_params=_params(2),
        name="mod",
    )(cv, ada_w, ada_b.reshape(depth, 1, n))


def _mod_spec(mod, layer):
    return pl.BlockSpec((1,) + mod.shape[1:], lambda *_: (layer, 0, 0), pipeline_mode=pl.Buffered(1))


def _ctxproj_kernel(x_ref, mod_ref, g_ref, w_hbm, b_ref, k_ref, v_ref, w_s, *, ctx_row, col0):
    d = x_ref.shape[2]
    nkv = w_s.shape[1]

    @pl.when(pl.program_id(0) == 0)
    def _():
        _stage_weights_bf16([(w_hbm.at[0, :, pl.ds(col0, nkv)], w_s)])

    m = mod_ref[0, ctx_row:ctx_row + 1, :]
    a = _rms(x_ref[0], NORM_EPS) * g_ref[...]
    a = a * (1.0 + m[:, d:2 * d]) + m[:, 0:d]
    p = _dot(a.astype(BF16), w_s[...]) + b_ref[0:1, col0:col0 + nkv]
    nk = k_ref.shape[2]
    k_ref[0] = p[:, :nk].astype(k_ref.dtype)
    v_ref[0] = p[:, nk:].astype(v_ref.dtype)


def _ctxproj_call(x, mod, ctx_row, g, w_all, b_all, col0, nk, nkv):
    bsz, s, d = x.shape
    return pl.pallas_call(
        functools.partial(_ctxproj_kernel, ctx_row=ctx_row, col0=col0),
        grid=(bsz,),
        in_specs=[pl.BlockSpec((1, s, d), lambda bi: (bi, 0, 0)), _mod_spec(mod, 0), _const_spec((1, d)),
                  pl.BlockSpec(memory_space=pl.ANY), _const_spec(b_all.shape)],
        out_specs=[pl.BlockSpec((1, s, nk), lambda bi: (bi, 0, 0)), pl.BlockSpec((1, s, nkv - nk), lambda bi: (bi, 0, 0))],
        out_shape=[jax.ShapeDtypeStruct((bsz, s, nk), BF16), jax.ShapeDtypeStruct((bsz, s, nkv - nk), BF16)],
        scratch_shapes=[pltpu.VMEM((d, nkv), BF16)],
        compiler_params=_params(1),
        name="ctxproj",
    )(x, mod, g.reshape(1, d), w_all, b_all)


def _latproj_kernel(x_ref, xp_ref, xn_ref, mod_ref, g_ref, w_hbm, b_ref, cos_ref, sa_ref, sb_ref,
                    cw_ref, cb_ref, q_ref, k_ref, v_ref, u_ref, x0_ref, w_ref, *, n_att, q_scale, seq_len):
    @pl.when((pl.program_id(0) == 0) & (pl.program_id(1) == 0))
    def _():
        _stage_weights_bf16([(w_hbm.at[0], w_ref)])

    tm = x_ref.shape[1] // ROW_SUBTILES
    d = x_ref.shape[2]
    mod = mod_ref[0, pl.ds(pl.program_id(0), 1), :]
    halo = xp_ref.shape[1]
    rows = tm + 2 * halo
    ncb = u_ref.shape[1]
    c = ncb * LANES
    qk = q_ref.shape[2]
    for sub in range(ROW_SUBTILES):
        r0 = sub * tm
        above = xp_ref[0] if sub == 0 else x_ref[0, r0 - halo:r0, :]
        below = xn_ref[0] if sub == ROW_SUBTILES - 1 else x_ref[0, r0 + tm:r0 + tm + halo, :]
        xx = jnp.concatenate([above, x_ref[0, r0:r0 + tm, :], below], axis=0)
        a = _rms(xx, NORM_EPS) * g_ref[...]
        a = a * (1.0 + mod[:, d:2 * d]) + mod[:, 0:d]

        ph = _dot(a.astype(BF16), w_ref[:, n_att:]) + b_ref[0:1, n_att:]
        t = lax.broadcasted_iota(jnp.int32, (rows, 1), 0) + (pl.program_id(1) * (tm * ROW_SUBTILES) + r0 - halo)
        ph = jnp.where((t >= 0) & (t < seq_len), ph, 0.0)

        def conv(s):
            blk = ph[:, s * c:(s + 1) * c]
            prev = pltpu.roll(blk, 1, 0)[halo:halo + tm]
            nxt = pltpu.roll(blk, rows - 1, 0)[halo:halo + tm]
            cw = cw_ref[:, s * c:(s + 1) * c]
            return prev * cw[0:1] + blk[halo:halo + tm] * cw[1:2] + nxt * cw[2:3] + cb_ref[:, s * c:(s + 1) * c]

        x0 = conv(0)
        u = conv(2) * conv(1)
        nb = FFT_N2 // SUBLANES
        srows = slice(r0 // SUBLANES, (r0 + tm) // SUBLANES)
        for val, o_ref in ((x0, x0_ref), (u, u_ref)):
            for cb in range(ncb):
                v4 = val[:, cb * LANES:(cb + 1) * LANES].reshape(tm // FFT_N2, nb, SUBLANES, LANES)
                for jb in range(nb):
                    o_ref[0, cb, jb, srows, :] = v4[:, jb].reshape(tm // SUBLANES, LANES)

        p = _dot(a[halo:halo + tm].astype(BF16), w_ref[:, :n_att]) + b_ref[0:1, :n_att]
        cos, sa, sb = (ref[r0:r0 + tm, :] for ref in (cos_ref, sa_ref, sb_ref))
        for o_ref, c_lo, scale in ((q_ref, 0, q_scale), (k_ref, qk, 1.0)):
            for c0 in range(0, qk, LANES):
                blk = p[:, c_lo + c0:c_lo + c0 + LANES]
                r = blk * cos + pltpu.roll(blk, 16, 1) * sa + pltpu.roll(blk, LANES - 16, 1) * sb
                o_ref[0, r0:r0 + tm, c0:c0 + LANES] = (r * scale).astype(o_ref.dtype)
        v_ref[0, r0:r0 + tm, :] = p[:, 2 * qk:].astype(v_ref.dtype)


def _latproj_call(x, mod, g, w_all, b_all, rope_tabs, conv_w, conv_b, n_att, qk, q_scale, tm=ROW_BLOCK):
    bsz, s, d = x.shape
    n = w_all.shape[2]
    c = (n - n_att) // 3
    halo = SUBLANES
    nb = tm // halo
    last = s // halo - 1
    tab = pl.BlockSpec((tm, LANES), lambda bi, i: (i, 0))
    row = lambda w: pl.BlockSpec((1, tm, w), lambda bi, i: (bi, i, 0))
    nslab = FFT_N2 // SUBLANES
    cblk = pl.BlockSpec((1, c // LANES, nslab, tm // nslab, LANES), lambda bi, i: (bi, 0, 0, i, 0))
    return pl.pallas_call(
        functools.partial(_latproj_kernel, n_att=n_att, q_scale=q_scale, seq_len=s),
        grid=(bsz, s // tm),
        in_specs=[row(d),
                  pl.BlockSpec((1, halo, d), lambda bi, i: (bi, jnp.maximum(i * nb - 1, 0), 0)),
                  pl.BlockSpec((1, halo, d), lambda bi, i: (bi, jnp.minimum((i + 1) * nb, last), 0)),
                  _mod_spec(mod, 0), _const_spec((1, d)), pl.BlockSpec(memory_space=pl.ANY), _const_spec(b_all.shape),
                  tab, tab, tab,
                  _const_spec(conv_w.shape), _const_spec((1, 3 * c))],
        out_specs=[row(qk), row(qk), row(n_att - 2 * qk), cblk, cblk],
        out_shape=[jax.ShapeDtypeStruct((bsz, s, qk), BF16), jax.ShapeDtypeStruct((bsz, s, qk), BF16),
                   jax.ShapeDtypeStruct((bsz, s, n_att - 2 * qk), BF16),
                   jax.ShapeDtypeStruct((bsz, c // LANES, nslab, s // nslab, LANES), F32),
                   jax.ShapeDtypeStruct((bsz, c // LANES, nslab, s // nslab, LANES), F32)],
        scratch_shapes=[pltpu.VMEM((d, n), BF16)],
        compiler_params=_params(2),
        name="latproj",
    )(x, x, x, mod, g.reshape(1, d), w_all, b_all, *rope_tabs, conv_w, conv_b.reshape(1, 3 * c))


def _rope_tables(seq_len):
    axis_dim = ATT_QK_DIM // 2
    n_freq = axis_dim // 2
    inv = (ROPE_BASE ** (-np.arange(n_freq, dtype=np.float32) / n_freq)).astype(np.float32)
    t = np.arange(seq_len)
    row, col = t // GRID_W, t % GRID_W
    jj = np.arange(128) % ATT_QK_DIM
    is_col = (jj // axis_dim) == 1
    second = ((jj % axis_dim) >= n_freq)[None, :]
    pos = np.where(is_col[None, :], col[:, None], row[:, None]).astype(np.float32)
    ang = (pos * inv[jj % n_freq][None, :]).astype(np.float64)
    cos, sin = np.cos(ang), np.sin(ang)
    as32 = lambda m: jnp.asarray(m, dtype=F32)
    return as32(cos), as32(np.where(second, sin, 0.0)), as32(np.where(second, 0.0, -sin))


def _attn_kernel(lam_ref, q_ref, kc_ref, vc_ref, kl_ref, vl_ref, g_ref, o_ref, *, tk, lam_init):
    q = q_ref[0]
    tq = q.shape[0]
    lane = lax.broadcasted_iota(jnp.int32, q.shape, 1)
    zero = jnp.zeros_like(q)
    q2 = jnp.concatenate([jnp.where(lane < ATT_QK_DIM, q, zero), jnp.where(lane >= ATT_QK_DIM, q, zero)], axis=0)

    def chunk(k, v, m, acc):
        s = lax.dot_general(q2, k, (((1,), (1,)), ((), ())), preferred_element_type=F32)
        rowmax = jnp.max(s, axis=1, keepdims=True)
        m_new = rowmax if m is None else jnp.maximum(m, rowmax)
        p = jnp.exp2(s - m_new).astype(BF16)
        pv = _dot(p, jnp.concatenate([v, jnp.ones_like(v)], axis=1))
        return m_new, (pv if acc is None else jnp.exp2(m - m_new) * acc + pv)

    m, acc = chunk(kc_ref[0], vc_ref[0], None, None)
    for j in range(kl_ref.shape[1] // tk):
        m, acc = chunk(kl_ref[0, j * tk:(j + 1) * tk, :], vl_ref[0, j * tk:(j + 1) * tk, :], m, acc)

    lamv = lam_ref[...]
    lam = (jnp.exp(jnp.sum(lamv[0:1] * lamv[1:2], axis=1, keepdims=True))
           - jnp.exp(jnp.sum(lamv[2:3] * lamv[3:4], axis=1, keepdims=True)) + lam_init)
    o_all = acc[:, :ATT_V_DIM] / acc[:, ATT_V_DIM:]
    o = o_all[:tq] - lam * o_all[tq:]
    o = _rms(o, SUBLN_EPS) * g_ref[...] * (1.0 - lam_init)
    o_ref[0] = o.astype(o_ref.dtype)


def _attn_call(q, kc, vc, kl, vl, lamv, g, lam_init, tq=1024, tk=256):
    bsz, l, width = q.shape
    heads = width // ATT_V_DIM
    lc = kc.shape[1]
    hd = ATT_V_DIM
    tq = min(tq, l)
    return pl.pallas_call(
        functools.partial(_attn_kernel, tk=tk, lam_init=lam_init),
        grid=(bsz, heads, l // tq),
        in_specs=[
            _const_spec(lamv.shape),
            pl.BlockSpec((1, tq, hd), lambda b, h, i: (b, i, h)),
            pl.BlockSpec((1, lc, hd), lambda b, h, i: (b, 0, h)),
            pl.BlockSpec((1, lc, hd), lambda b, h, i: (b, 0, h)),
            pl.BlockSpec((1, l, hd), lambda b, h, i: (b, 0, h)),
            pl.BlockSpec((1, l, hd), lambda b, h, i: (b, 0, h)),
            _const_spec((1, hd)),
        ],
        out_specs=pl.BlockSpec((1, tq, hd), lambda b, h, i: (b, i, h)),
        out_shape=jax.ShapeDtypeStruct((bsz, l, width), BF16),
        compiler_params=_params(3),
        name="diffattn",
    )(lamv, q, kc, vc, kl, vl, g.reshape(1, hd))


def _gather_minor(refs, n_rows):
    cols = []
    for j in range(SUBLANES):
        parts = [r[pl.ds(j, n_rows, stride=SUBLANES), :] for r in refs]
        cols.append(parts[0] if len(parts) == 1 else jnp.concatenate(parts, axis=0))
    return jnp.concatenate(cols, axis=1)


def _spectrum_store(val, spec_ref, jb):
    rows = val.shape[0]
    for j in range(SUBLANES):
        spec_ref[jb, pl.ds(j, rows, stride=SUBLANES), :] = val[:, j * LANES:(j + 1) * LANES]


def _spectrum_blocks(spec_ref, k0):
    start = k0 * SUBLANES if isinstance(k0, int) else pl.multiple_of(k0 * SUBLANES, SUBLANES * SUBLANES)
    x = spec_ref[:, pl.ds(start, SUBLANES * SUBLANES), :]
    return jnp.stack([x[:, i * SUBLANES:(i + 1) * SUBLANES, :].reshape(FFT_N2, LANES) for i in range(SUBLANES)], axis=0)


def _spectrum_put(spec_ref, k0, val):
    nb = FFT_N2 // SUBLANES
    v = val.reshape(SUBLANES, nb, SUBLANES, LANES)
    x = jnp.stack([v[:, jb].reshape(SUBLANES * SUBLANES, LANES) for jb in range(nb)], axis=0)
    spec_ref[:, pl.ds(pl.multiple_of(k0 * SUBLANES, SUBLANES * SUBLANES), SUBLANES * SUBLANES), :] = x


def _twiddle_block(twc_ref, twf_ref, kb):
    n2 = FFT_N2
    rows = pl.ds(pl.multiple_of(kb * n2, n2), n2)
    cr, ci = twc_ref[0, rows, :][None], twc_ref[1, rows, :][None]
    fr, fi = twf_ref[0].reshape(SUBLANES, n2, LANES), twf_ref[1].reshape(SUBLANES, n2, LANES)
    return cr * fr - ci * fi, cr * fi + ci * fr


def _filter_fft_kernel(z_ref, w1_ref, b1_ref, f1_ref, w2_ref, b2_ref, f2_ref, w3f_ref, w3b_ref, dl_ref,
                       fr_ref, g_ref, twc_ref, twf_ref, kr_ref, ki_ref, h2_ref, hs_ref, a_ref, *, tm):
    half = z_ref.shape[0]
    seq_len = 2 * half
    hid = w1_ref.shape[0]
    n1 = a_ref.shape[2] // SUBLANES
    n1h = hs_ref.shape[2] // SUBLANES
    n2 = FFT_N2
    nb = n2 // SUBLANES
    w = SUBLANES * LANES
    zeros = jnp.zeros((hid, hid), F32)

    @pl.when(pl.program_id(0) == 0)
    def _():
        blockdiag = lambda m: jnp.concatenate([jnp.concatenate([m, zeros], axis=1),
                                               jnp.concatenate([zeros, m], axis=1)], axis=0)
        twice = lambda ref: jnp.concatenate([ref[...], ref[...]], axis=1)
        w1, w2 = blockdiag(w1_ref[...]), blockdiag(w2_ref[...])
        b1, f1, b2, f2 = twice(b1_ref), twice(f1_ref), twice(b2_ref), twice(f2_ref)

        def features(i, _):
            r = pl.ds(pl.multiple_of(i * tm, tm), tm)
            h1 = jnp.sin(f1 * (_dot3f(z_ref[r, :], w1) + b1))
            h2_ref[r, :] = jnp.sin(f2 * (_dot3f(h1, w2) + b2))
            return 0

        lax.fori_loop(0, half // tm, features, 0)

    for upper in range(2):
        pad = lambda m: jnp.concatenate([jnp.zeros_like(m), m] if upper else [m, jnp.zeros_like(m)], axis=0)
        w3s = [pad(w3f_ref[...]), pad(w3b_ref[...])]

        def taps(i, _, upper=upper, w3s=w3s):
            h2 = h2_ref[pl.ds(pl.multiple_of(i * tm, tm), tm), :]
            t0 = upper * half + i * tm
            row = lax.broadcasted_iota(jnp.int32, (tm, LANES), 0) + t0
            decay = jnp.exp(-(row.astype(F32) * (1.0 / (seq_len - 1))) * dl_ref[...])
            for t in range(2):
                h = _dot3f(h2, w3s[t]) * decay
                if t == 1:
                    h = jnp.where(row == 0, 0.0, h)
                h4 = h.reshape(tm // n2, nb, SUBLANES, LANES)
                for jb in range(nb):
                    srow = pl.multiple_of(upper * (half // nb) + i * (tm // nb), tm // nb)
                    hs_ref[t, jb, pl.ds(srow, tm // nb), :] = h4[:, jb].reshape(tm // nb, LANES)
            return 0

        lax.fori_loop(0, half // tm, taps, 0)

    for jb in range(nb):
        z = jnp.concatenate([_gather_minor([hs_ref.at[t, jb]], n1h) for t in range(2)], axis=1)
        out = _dft_dot(fr_ref[...], z)
        for t in range(4):
            _spectrum_store(out[(t % 2) * n1:(t % 2 + 1) * n1, (t // 2) * w:(t // 2 + 1) * w], a_ref.at[t], jb)

    def body(kb, _):
        k0 = pl.multiple_of(kb * SUBLANES, SUBLANES)
        twr, twi = _twiddle_block(twc_ref, twf_ref, kb)
        a = [_spectrum_blocks(a_ref.at[t], k0) for t in range(4)]
        cols = []
        for t in range(2):
            ar, ai = a[2 * t], a[2 * t + 1]
            sr, si = ar * twr - ai * twi, ar * twi + ai * twr
            cols += [jnp.concatenate([sr[i], si[i]], axis=0) for i in range(SUBLANES)]
        x = _dft_dot(g_ref[...], jnp.concatenate(cols, axis=1))
        for i in range(SUBLANES):
            xf = x[:, i * LANES:(i + 1) * LANES]
            xb = x[:, (SUBLANES + i) * LANES:(SUBLANES + i + 1) * LANES]
            rows = pl.ds(pl.multiple_of((k0 + i) * n2, n2), n2)
            kr_ref[rows, :] = xf[:n2] + xb[:n2]
            ki_ref[rows, :] = xf[n2:] - xb[n2:]
        return 0

    lax.fori_loop(0, n1 // SUBLANES, body, 0, unroll=4)


def _filter_fft_call(dft, z, w1, b1, f1, w2, b2, f2, w3, absdelta, tm=512):
    hid = z.shape[1] // 2
    seq_len = 2 * z.shape[0]
    c = w3.shape[1] // 2
    n, n1, n1h = dft["n"], dft["n1"], dft["n1h"]
    nb = FFT_N2 // SUBLANES
    ncb = c // LANES
    fr, g = dft["f_real"].astype(BF16), dft["g_fwd"].astype(BF16)
    twc, twf = dft["tw_coarse"], dft["tw_fine"]
    vec = _const_spec((1, hid))
    w3_blk = lambda off: pl.BlockSpec((hid, LANES), lambda i, off=off: (0, i + off))
    out = pl.BlockSpec((None, n, LANES), lambda i: (i, 0, 0))
    return pl.pallas_call(
        functools.partial(_filter_fft_kernel, tm=tm),
        grid=(ncb,),
        in_specs=[_const_spec(z.shape), _const_spec((hid, hid)), vec, vec, _const_spec((hid, hid)), vec, vec,
                  w3_blk(0), w3_blk(ncb), pl.BlockSpec((1, LANES), lambda i: (0, i)),
                  _const_spec(fr.shape), _const_spec(g.shape), _const_spec(twc.shape), _const_spec(twf.shape)],
        out_specs=[out, out],
        out_shape=[jax.ShapeDtypeStruct((ncb, n, LANES), F32)] * 2,
        scratch_shapes=[pltpu.VMEM(z.shape, F32), pltpu.VMEM((2, nb, n1h * SUBLANES, LANES), F32),
                        pltpu.VMEM((4, nb, n1 * SUBLANES, LANES), F32)],
        compiler_params=_params(1),
        name="hyfilter_fft",
    )(z, w1, b1.reshape(1, hid), f1.reshape(1, hid), w2, b2.reshape(1, hid), f2.reshape(1, hid), w3, w3, absdelta,
      fr, g, twc, twf)


def _hyena_conv_kernel(fc_ref, g_ref, gc_ref, e_ref, twc_ref, twf_ref, kr_ref, ki_ref, u_ref, x0_ref, bias_ref, o_ref,
                       a_ref, *, inv_n):
    n1 = a_ref.shape[2] // SUBLANES
    n1h = u_ref.shape[2] // SUBLANES
    n2 = FFT_N2
    nb = n2 // SUBLANES

    for jb in range(nb):
        z = _gather_minor([u_ref.at[0, jb], u_ref.at[1, jb]], n1h)
        out = _dft_dot(fc_ref[...], z)
        for c in range(2):
            _spectrum_store(out[c * n1:(c + 1) * n1], a_ref.at[c], jb)

    def body(kb, _):
        k0 = pl.multiple_of(kb * SUBLANES, SUBLANES)
        rows = pl.ds(pl.multiple_of(kb * (SUBLANES * n2), SUBLANES * n2), SUBLANES * n2)
        blocks = lambda ref: ref[rows, :].reshape(SUBLANES, n2, LANES)
        (twr, twi), kr, ki = _twiddle_block(twc_ref, twf_ref, kb), blocks(kr_ref), blocks(ki_ref)
        ar, ai = _spectrum_blocks(a_ref.at[0], k0), _spectrum_blocks(a_ref.at[1], k0)
        sr, si = ar * twr - ai * twi, ar * twi + ai * twr
        lanes = lambda re, im: jnp.concatenate([jnp.concatenate([re[i], im[i]], axis=0) for i in range(SUBLANES)], axis=1)
        x = _dft_dot(g_ref[...], lanes(sr, si))
        unl = lambda v, lo: jnp.stack([v[lo:lo + n2, i * LANES:(i + 1) * LANES] for i in range(SUBLANES)], axis=0)
        xr, xi = unl(x, 0), unl(x, n2)
        t = _dft_dot(gc_ref[...], lanes(xr * kr - xi * ki, xr * ki + xi * kr))
        tr, ti = unl(t, 0), unl(t, n2)
        _spectrum_put(a_ref.at[0], k0, tr * twr + ti * twi)
        _spectrum_put(a_ref.at[1], k0, ti * twr - tr * twi)
        return 0

    lax.fori_loop(0, n1 // SUBLANES, body, 0, unroll=4)

    for jb in range(nb):
        t = _gather_minor([a_ref.at[0, jb], a_ref.at[1, jb]], n1)
        y = _dft_dot(e_ref[...], t) * inv_n
        for c in range(2):
            _spectrum_store(y[c * n1h:(c + 1) * n1h], o_ref.at[c], jb)
    o_ref[...] = x0_ref[...] * (o_ref[...] + u_ref[...] * bias_ref[...])


def _dft_blocks(seq_len):
    n = 2 * seq_len
    n2 = FFT_N2
    n1 = n // n2
    n1h = n1 // 2
    k1 = np.arange(n1)[:, None].astype(np.float64)
    a = 2.0 * np.pi * k1 * np.arange(n1h)[None, :] / n1
    fr, fi = np.cos(a), -np.sin(a)
    f_cplx = np.block([[fr, -fi], [fi, fr]])
    f_real = np.concatenate([fr, fi], axis=0)
    e_cplx = np.block([[fr.T, fi.T], [-fi.T, fr.T]])
    b = 2.0 * np.pi * np.arange(n2)[:, None] * np.arange(n2)[None, :] / n2
    gr, gi = np.cos(b), -np.sin(b)
    g_fwd = np.block([[gr, -gi], [gi, gr]])
    g_inv = np.block([[gr, gi], [-gi, gr]])
    def table(k1s):
        ang = (2.0 * np.pi / n) * (k1s[:, None] * np.arange(n2)[None, :]).reshape(-1, 1)
        return np.broadcast_to(np.stack([np.cos(ang), -np.sin(ang)]), (2, k1s.size * n2, LANES))
    as32 = lambda m: jnp.asarray(np.ascontiguousarray(m), dtype=F32)
    return dict(n=n, n1=n1, n1h=n1h, f_cplx=as32(f_cplx), f_real=as32(f_real), e_cplx=as32(e_cplx),
                g_fwd=as32(g_fwd), g_inv=as32(g_inv), tw_coarse=as32(table(np.arange(0, n1, SUBLANES))),
                tw_fine=as32(table(np.arange(SUBLANES))))


def _hyena_call(u, x0, filt, hy_bias):
    bsz, ncb, nslab, srows, _ = u.shape
    seq_len = nslab * srows
    c = ncb * LANES
    n2 = FFT_N2
    dft = _dft_blocks(seq_len)
    n, n1, n1h = dft["n"], dft["n1"], dft["n1h"]
    pairs = bsz // 2

    w1, b1, f1, w2, b2, f2, w3 = filt
    hid = w2.shape[0]
    bands = (HY_POS_EMB - 1) // 2
    t = np.linspace(0.0, 1.0, seq_len)[:, None]
    freqs = np.linspace(1e-4, bands - 1, bands)
    ang = (2.0 * math.pi / seq_len) * np.arange(seq_len)[:, None] * freqs[None, :]
    z = np.concatenate([t, np.cos(ang), -np.sin(ang), np.zeros((seq_len, hid - HY_POS_EMB))], axis=-1)
    z = jnp.asarray(np.concatenate([z[:seq_len // 2], z[seq_len // 2:]], axis=1), dtype=F32)
    w1p = jnp.concatenate([w1, jnp.zeros((hid - HY_POS_EMB, hid), F32)], axis=0)
    max_decay = math.log(HY_DECAY_TARGET) / HY_FAST_DECAY_PCT
    min_decay = math.log(HY_DECAY_TARGET) / HY_SLOW_DECAY_PCT
    absdelta = jnp.asarray(np.abs(np.linspace(min_decay, max_decay, c))[None, :], dtype=F32)
    nb = n2 // SUBLANES
    kr, ki = _filter_fft_call(dft, z, w1p, b1, f1, w2, b2, f2, w3, absdelta)

    view = lambda a: a.reshape(pairs, 2, ncb, nb, n1h * SUBLANES, LANES)
    pair_spec = pl.BlockSpec((None, 2, None, nb, n1h * SUBLANES, LANES), lambda cb, p: (p, 0, cb, 0, 0, 0))
    k_spec = pl.BlockSpec((None, n, LANES), lambda cb, p: (cb, 0, 0))
    twc, twf = dft["tw_coarse"], dft["tw_fine"]
    fc, g, gc, e = (dft[k].astype(BF16) for k in ("f_cplx", "g_fwd", "g_inv", "e_cplx"))
    hy = pl.pallas_call(
        functools.partial(_hyena_conv_kernel, inv_n=1.0 / n),
        grid=(ncb, pairs),
        in_specs=[_const_spec(fc.shape), _const_spec(g.shape), _const_spec(gc.shape), _const_spec(e.shape),
                  _const_spec(twc.shape), _const_spec(twf.shape), k_spec, k_spec, pair_spec, pair_spec,
                  pl.BlockSpec((1, LANES), lambda cb, p: (0, cb))],
        out_specs=pair_spec,
        out_shape=jax.ShapeDtypeStruct((pairs, 2, ncb, nb, n1h * SUBLANES, LANES), F32),
        scratch_shapes=[pltpu.VMEM((2, nb, n1 * SUBLANES, LANES), F32)],
        compiler_params=_params(2),
        name="hyconv",
    )(fc, g, gc, e, twc, twf, kr, ki, view(u), view(x0), hy_bias.reshape(1, c))
    return hy.reshape(bsz, ncb, nb, n1h * SUBLANES, LANES)


def _stage_weights_bf16(pairs):
    for src, dst in pairs:
        n_rows, n_cols = src.shape
        rows = min(n_rows, 1 << ((WEIGHT_STAGE_BYTES // (4 * n_cols)).bit_length() - 1))
        n_chunks = n_rows // rows
        assert rows % (2 * SUBLANES) == 0 and n_chunks * rows == n_rows

        def run(stage, sem, src=src, dst=dst, rows=rows, n_chunks=n_chunks):
            copy = lambda k: pltpu.make_async_copy(src.at[pl.ds(k * rows, rows), :], stage.at[k % 2], sem.at[k % 2])
            copy(0).start()
            for k in range(n_chunks):
                if k + 1 < n_chunks:
                    copy(k + 1).start()
                copy(k).wait()
                dst[pl.ds(k * rows, rows), :] = stage[k % 2].astype(BF16)

        pl.run_scoped(run, pltpu.VMEM((2, rows, n_cols), F32), pltpu.SemaphoreType.DMA((2,)))


def _mlp_tail(h, g2, shift, scale, gate, w1_ref, w2_ref, fc):
    a = (_rms(h, NORM_EPS) * g2) * (1.0 + scale) + shift
    a = a.astype(BF16)
    acc = None
    for c0 in range(0, w1_ref.shape[1], fc):
        hid = jnp.maximum(_dot(a, w1_ref[:, c0:c0 + fc]), 0.0)
        part = _dot((hid * hid).astype(BF16), w2_ref[c0:c0 + fc, :])
        acc = part if acc is None else acc + part
    return h + gate * acc


def _mixmlp_kernel(x_ref, att_ref, hy_ref, mod_ref, wo_hbm, bo_ref, g2_ref, w1_hbm, w2_hbm, o_ref,
                   wo_s, w1_s, w2_s, *, fc, layer):
    @pl.when((pl.program_id(0) == 0) & (pl.program_id(1) == 0))
    def _():
        _stage_weights_bf16([(wo_hbm.at[layer // 2], wo_s), (w1_hbm.at[layer], w1_s), (w2_hbm.at[layer], w2_s)])

    blk, d = x_ref.shape[1:]
    tm = blk // ROW_SUBTILES
    wa = att_ref.shape[2]
    mod = mod_ref[0, pl.ds(pl.program_id(0), 1), :]
    for r in range(ROW_SUBTILES):
        rows = slice(r * tm, (r + 1) * tm)
        hy = jnp.concatenate([_spectrum_blocks(hy_ref.at[0, cb], r * tm // FFT_N2).reshape(tm, LANES)
                              for cb in range(hy_ref.shape[1])], axis=1).astype(BF16)
        y = _dot(att_ref[0, rows, :], wo_s[:wa, :]) + _dot(hy, wo_s[wa:, :]) + bo_ref[...]
        h = x_ref[0, rows, :] + mod[:, 2 * d:3 * d] * y
        o_ref[0, rows, :] = _mlp_tail(h, g2_ref[...], mod[:, 3 * d:4 * d], mod[:, 4 * d:5 * d], mod[:, 5 * d:6 * d],
                                      w1_s, w2_s, fc)


def _mixmlp_call(x, att, hy, mod, w_out, b_out, g2, w1, w2, layer, tm=ROW_BLOCK, fc=1024):
    bsz, l, d = x.shape
    wa = att.shape[2]
    dff = w1.shape[2]
    row = lambda w: pl.BlockSpec((1, tm, w), lambda b, i: (b, i, 0))
    hbm = pl.BlockSpec(memory_space=pl.ANY)
    return pl.pallas_call(
        functools.partial(_mixmlp_kernel, fc=fc, layer=layer),
        grid=(bsz, l // tm),
        in_specs=[row(d), row(wa),
                  pl.BlockSpec((1, hy.shape[1], hy.shape[2], tm // hy.shape[2], LANES), lambda b, i: (b, 0, 0, i, 0)),
                  _mod_spec(mod, layer),
                  hbm, _const_spec((1, d)), _const_spec((1, d)), hbm, hbm],
        out_specs=row(d),
        out_shape=jax.ShapeDtypeStruct((bsz, l, d), F32),
        scratch_shapes=[pltpu.VMEM((d, d), BF16), pltpu.VMEM((d, dff), BF16), pltpu.VMEM((dff, d), BF16)],
        compiler_params=_params(2),
        name="mixmlp",
    )(x, att, hy, mod, w_out, b_out.reshape(1, d), g2.reshape(1, d), w1, w2)


def _poolmlp_kernel(h_ref, hp_ref, hn_ref, mod_ref, g1_ref, pw_hbm, ps_ref, g2_ref, w1_hbm, w2_hbm, gf_ref, o_ref,
                    pw_s, w1_s, w2_s, *, fc, seq_len, layer):
    @pl.when((pl.program_id(0) == 0) & (pl.program_id(1) == 0))
    def _():
        _stage_weights_bf16([(pw_hbm.at[layer // 2], pw_s), (w1_hbm.at[layer], w1_s), (w2_hbm.at[layer], w2_s)])

    blk, d = h_ref.shape[1:]
    tm = blk // ROW_SUBTILES
    halo = POOL_HALO
    rows = tm + 2 * halo
    mod = mod_ref[0, pl.ds(pl.program_id(0), 1), :]
    gd = d // len(POOL_WINDOWS)
    for r in range(ROW_SUBTILES):
        r0 = r * tm
        h = h_ref[0, r0:r0 + tm, :]
        above = hp_ref[0] if r == 0 else h_ref[0, r0 - halo:r0, :]
        below = hn_ref[0] if r == ROW_SUBTILES - 1 else h_ref[0, r0 + tm:r0 + tm + halo, :]
        hx = jnp.concatenate([above, h, below], axis=0)
        a = (_rms(hx, NORM_EPS) * g1_ref[...]) * (1.0 + mod[:, d:2 * d]) + mod[:, 0:d]
        t = lax.broadcasted_iota(jnp.int32, (rows, 1), 0) + (pl.program_id(1) * blk + r0 - halo)
        a = jnp.where((t >= 0) & (t < seq_len), a, 0.0)
        tc = t[halo:halo + tm]
        ys = []
        for g, win in enumerate(POOL_WINDOWS):
            ag = a[:, g * gd:(g + 1) * gd]
            f, m = ag, 1
            while 2 * m <= win // 2:
                f = f + pltpu.roll(f, rows - m, 0)
                m *= 2
            s = pltpu.roll(f, win // 2, 0) + f
            cnt = (jnp.minimum(tc + (win - win // 2), seq_len) - jnp.maximum(tc - win // 2, 0)).astype(F32)
            dlt = s[halo:halo + tm] / cnt - ag[halo:halo + tm]
            ys.append(_dot(dlt.astype(BF16), pw_s[g * gd:(g + 1) * gd, :]))
        y = jnp.concatenate(ys, axis=1) * ps_ref[...]
        h1 = h + mod[:, 2 * d:3 * d] * y
        h2 = _mlp_tail(h1, g2_ref[...], mod[:, 3 * d:4 * d], mod[:, 4 * d:5 * d], mod[:, 5 * d:6 * d], w1_s, w2_s, fc)
        o_ref[0, r0:r0 + tm, :] = _rms(h2, NORM_EPS) * gf_ref[...]


def _poolmlp_call(h, mod, g1, pool_w, pool_scale, g2, w1, w2, gf, layer, tm=ROW_BLOCK, fc=1024):
    bsz, l, d = h.shape
    dff = w1.shape[2]
    gd = d // len(POOL_WINDOWS)
    halo = POOL_HALO
    nb = tm // halo
    last = l // halo - 1
    row = pl.BlockSpec((1, tm, d), lambda b, i: (b, i, 0))
    hbm = pl.BlockSpec(memory_space=pl.ANY)
    return pl.pallas_call(
        functools.partial(_poolmlp_kernel, fc=fc, seq_len=l, layer=layer),
        grid=(bsz, l // tm),
        in_specs=[row,
                  pl.BlockSpec((1, halo, d), lambda b, i: (b, jnp.maximum(i * nb - 1, 0), 0)),
                  pl.BlockSpec((1, halo, d), lambda b, i: (b, jnp.minimum((i + 1) * nb, last), 0)),
                  _mod_spec(mod, layer),
                  _const_spec((1, d)), hbm, _const_spec((1, d)), _const_spec((1, d)),
                  hbm, hbm, _const_spec((1, d))],
        out_specs=row,
        out_shape=jax.ShapeDtypeStruct((bsz, l, d), F32),
        scratch_shapes=[pltpu.VMEM((d, gd), BF16), pltpu.VMEM((d, dff), BF16), pltpu.VMEM((dff, d), BF16)],
        compiler_params=_params(2),
        name="poolmlp",
    )(h, h, h, mod, g1.reshape(1, d), pool_w.reshape(pool_w.shape[0], d, gd), pool_scale.reshape(1, d),
      g2.reshape(1, d), w1, w2, gf.reshape(1, d))


def kernel(x, c, ctx, c_ctx, ada_w, ada_b, norm1_g, norm2_g, mix_w_in, mix_b_in, mix_w_out, mix_b_out, lam_q1, lam_k1, lam_q2, lam_k2, subln_g, hy_conv_w, hy_conv_b, hy_pos_w1, hy_pos_b1, hy_freq1, hy_pos_w2, hy_pos_b2, hy_freq2, hy_pos_w3, hy_bias, pool_w, pool_scale, mlp_w1, mlp_w2, final_g):
    bsz, seq_len, d = x.shape
    depth = ada_w.shape[0]
    assert depth == 2 and bsz % 2 == 0 and bsz < MOD_ROWS and seq_len % GRID_W == 0
    att_w = ATT_HEADS * ATT_V_DIM
    q_cols = k_cols = ATT_HEADS * 2 * ATT_QK_DIM
    kv_start, hy_start = q_cols, q_cols + k_cols + att_w
    in_cols = mix_w_in.shape[2]

    cv = jnp.concatenate([c, c_ctx[None, :], jnp.zeros((MOD_ROWS - bsz - 1, d), F32)], axis=0)
    mod = _mod_call(cv, ada_w, ada_b)

    lam_init = 0.8 - 0.6 * math.exp(-0.3 * 0)
    assert k_cols == q_cols and in_cols - hy_start == 3 * (d - att_w)
    q, k, v, u, x0 = _latproj_call(
        x, mod, norm1_g[0], mix_w_in, mix_b_in, _rope_tables(seq_len),
        hy_conv_w[0], hy_conv_b[0], n_att=hy_start, qk=q_cols, q_scale=ATT_QK_DIM ** -0.5 * math.log2(math.e))
    kc, vc = _ctxproj_call(ctx, mod, bsz, norm1_g[0], mix_w_in, mix_b_in, kv_start, k_cols, hy_start - kv_start)
    lamv = jnp.stack([lam_q1[0], lam_k1[0], lam_q2[0], lam_k2[0]], axis=0)
    att = _attn_call(q, kc, vc, k, v, lamv, subln_g[0], lam_init)
    filt = (hy_pos_w1[0], hy_pos_b1[0], hy_freq1[0], hy_pos_w2[0], hy_pos_b2[0], hy_freq2[0], hy_pos_w3[0])
    hy = _hyena_call(u, x0, filt, hy_bias[0])
    h = _mixmlp_call(x, att, hy, mod, mix_w_out, mix_b_out[0], norm2_g[0], mlp_w1, mlp_w2, layer=0)

    return _poolmlp_call(h, mod, norm1_g[1], pool_w, pool_scale[0], norm2_g[1], mlp_w1, mlp_w2, final_g, layer=1)
```

```python
import functools
import math

import numpy as np
import jax
import jax.numpy as jnp
from jax import lax
from jax.experimental import pallas as pl
from jax.experimental.pallas import tpu as pltpu

F32 = jnp.float32
BF16 = jnp.bfloat16

GRID_W = 64
ATT_HEADS = 4
ATT_V_DIM = 128
ATT_QK_DIM = 64
ROPE_BASE = 10000.0
HY_POS_EMB = 33
HY_DECAY_TARGET = 1e-2
HY_FAST_DECAY_PCT = 0.3
HY_SLOW_DECAY_PCT = 1.5
POOL_WINDOWS = (2, 4, 8, 16)
NORM_EPS = 1e-6
SUBLN_EPS = 1e-5
SUBLANES = 8
LANES = 128
FFT_N2 = 64
POOL_HALO = 8
WEIGHT_STAGE_BYTES = 2 * 1024 * 1024
ROW_BLOCK = 1024
ROW_SUBTILES = 2
MOD_ROWS = 8

VMEM_LIMIT_BYTES = 56 * 1024 * 1024


def _params(n_grid_dims):
    return pltpu.CompilerParams(
        dimension_semantics=("arbitrary",) * n_grid_dims,
        vmem_limit_bytes=VMEM_LIMIT_BYTES,
    )


def _const_spec(shape):
    nd = len(shape)
    return pl.BlockSpec(shape, lambda *_: (0,) * nd, pipeline_mode=pl.Buffered(1))


def _split_bf16(a):
    hi = a.astype(BF16)
    lo = (a - hi.astype(F32)).astype(BF16)
    return hi, lo


def _dot(a, b):
    return jnp.dot(a, b, preferred_element_type=F32)


def _dot3(a_hi, a_lo, b_hi, b_lo):
    return _dot(a_hi, b_hi) + (_dot(a_lo, b_hi) + _dot(a_hi, b_lo))


def _dot3f(a, b):
    a_hi, a_lo = _split_bf16(a)
    b_hi, b_lo = _split_bf16(b)
    return _dot3(a_hi, a_lo, b_hi, b_lo)


def _dft_dot(a_bf16, b):
    return _dot(a_bf16, b.astype(BF16))


def _rms(x, eps):
    return x * lax.rsqrt(jnp.mean(x * x, axis=-1, keepdims=True) + eps)


def _mod_kernel(cv_ref, w_ref, b_ref, o_ref):
    cv = cv_ref[...]
    s_hi, s_lo = _split_bf16(cv / (1.0 + jnp.exp(-cv)))
    w_hi, w_lo = _split_bf16(w_ref[0])
    r = _dot(jnp.concatenate([s_hi, s_lo], axis=0), w_hi)
    o_ref[0] = r[:MOD_ROWS] + (r[MOD_ROWS:] + _dot(s_hi, w_lo)) + b_ref[0]


def _mod_call(cv, ada_w, ada_b, tn=1536):
    depth, d, n = ada_w.shape
    return pl.pallas_call(
        _mod_kernel,
        grid=(depth, n // tn),
        in_specs=[
            pl.BlockSpec((MOD_ROWS, d), lambda i, j: (0, 0)),
            pl.BlockSpec((1, d, tn), lambda i, j: (i, 0, j)),
            pl.BlockSpec((1, 1, tn), lambda i, j: (i, 0, j)),
        ],
        out_specs=pl.BlockSpec((1, MOD_ROWS, tn), lambda i, j: (i, 0, j)),
        out_shape=jax.ShapeDtypeStruct((depth, MOD_ROWS, n), F32),
        compiler_params=_params(2),
        name="mod",
    )(cv, ada_w, ada_b.reshape(depth, 1, n))


def _mod_spec(mod, layer):
    return pl.BlockSpec((1,) + mod.shape[1:], lambda *_: (layer, 0, 0), pipeline_mode=pl.Buffered(1))


def _layer_spec(stack, layer):
    return pl.BlockSpec((None,) + stack.shape[1:], lambda *_: (layer, 0, 0), pipeline_mode=pl.Buffered(1))


def _ctxproj_kernel(x_ref, mod_ref, g_ref, w_hbm, b_ref, k_ref, v_ref, w_s, *, ctx_row, col0):
    d = x_ref.shape[2]
    nkv = w_s.shape[1]

    @pl.when(pl.program_id(0) == 0)
    def _():
        _stage_weights_bf16([(w_hbm.at[0, :, pl.ds(col0, nkv)], w_s)])

    m = mod_ref[0, ctx_row:ctx_row + 1, :]
    a = _rms(x_ref[0], NORM_EPS) * g_ref[...]
    a = a * (1.0 + m[:, d:2 * d]) + m[:, 0:d]
    p = _dot(a.astype(BF16), w_s[...]) + b_ref[0:1, col0:col0 + nkv]
    nk = k_ref.shape[2]
    k_ref[0] = p[:, :nk].astype(k_ref.dtype)
    v_ref[0] = p[:, nk:].astype(v_ref.dtype)


def _ctxproj_call(x, mod, ctx_row, g, w_all, b_all, col0, nk, nkv):
    bsz, s, d = x.shape
    return pl.pallas_call(
        functools.partial(_ctxproj_kernel, ctx_row=ctx_row, col0=col0),
        grid=(bsz,),
        in_specs=[pl.BlockSpec((1, s, d), lambda bi: (bi, 0, 0)), _mod_spec(mod, 0), _const_spec((1, d)),
                  pl.BlockSpec(memory_space=pl.ANY), _const_spec(b_all.shape)],
        out_specs=[pl.BlockSpec((1, s, nk), lambda bi: (bi, 0, 0)), pl.BlockSpec((1, s, nkv - nk), lambda bi: (bi, 0, 0))],
        out_shape=[jax.ShapeDtypeStruct((bsz, s, nk), BF16), jax.ShapeDtypeStruct((bsz, s, nkv - nk), BF16)],
        scratch_shapes=[pltpu.VMEM((d, nkv), BF16)],
        compiler_params=_params(1),
        name="ctxproj",
    )(x, mod, g.reshape(1, d), w_all, b_all)


def _latproj_kernel(x_ref, xp_ref, xn_ref, mod_ref, g_ref, w_hbm, b_ref, cos_ref, sa_ref, sb_ref,
                    cw_ref, cb_ref, q_ref, k_ref, v_ref, u_ref, x0_ref, w_ref, *, n_att, q_scale, seq_len):
    @pl.when((pl.program_id(0) == 0) & (pl.program_id(1) == 0))
    def _():
        _stage_weights_bf16([(w_hbm.at[0], w_ref)])

    tm = x_ref.shape[1] // ROW_SUBTILES
    d = x_ref.shape[2]
    mod = mod_ref[0, pl.ds(pl.program_id(0), 1), :]
    halo = xp_ref.shape[1]
    rows = tm + 2 * halo
    ncb = u_ref.shape[1]
    c = ncb * LANES
    qk = q_ref.shape[2]
    for sub in range(ROW_SUBTILES):
        r0 = sub * tm
        above = xp_ref[0] if sub == 0 else x_ref[0, r0 - halo:r0, :]
        below = xn_ref[0] if sub == ROW_SUBTILES - 1 else x_ref[0, r0 + tm:r0 + tm + halo, :]
        xx = jnp.concatenate([above, x_ref[0, r0:r0 + tm, :], below], axis=0)
        a = _rms(xx, NORM_EPS) * g_ref[...]
        a = a * (1.0 + mod[:, d:2 * d]) + mod[:, 0:d]

        ph = _dot(a.astype(BF16), w_ref[:, n_att:]) + b_ref[0:1, n_att:]
        t = lax.broadcasted_iota(jnp.int32, (rows, 1), 0) + (pl.program_id(1) * (tm * ROW_SUBTILES) + r0 - halo)
        ph = jnp.where((t >= 0) & (t < seq_len), ph, 0.0)

        def conv(s):
            blk = ph[:, s * c:(s + 1) * c]
            prev = pltpu.roll(blk, 1, 0)[halo:halo + tm]
            nxt = pltpu.roll(blk, rows - 1, 0)[halo:halo + tm]
            cw = cw_ref[:, s * c:(s + 1) * c]
            return prev * cw[0:1] + blk[halo:halo + tm] * cw[1:2] + nxt * cw[2:3] + cb_ref[:, s * c:(s + 1) * c]

        x0 = conv(0)
        u = conv(2) * conv(1)
        nb = FFT_N2 // SUBLANES
        srows = slice(r0 // SUBLANES, (r0 + tm) // SUBLANES)
        for val, o_ref in ((x0, x0_ref), (u, u_ref)):
            for cb in range(ncb):
                v4 = val[:, cb * LANES:(cb + 1) * LANES].reshape(tm // FFT_N2, nb, SUBLANES, LANES)
                for jb in range(nb):
                    o_ref[0, cb, jb, srows, :] = v4[:, jb].reshape(tm // SUBLANES, LANES)

        p = _dot(a[halo:halo + tm].astype(BF16), w_ref[:, :n_att]) + b_ref[0:1, :n_att]
        cos, sa, sb = (ref[r0:r0 + tm, :] for ref in (cos_ref, sa_ref, sb_ref))
        for o_ref, c_lo, scale in ((q_ref, 0, q_scale), (k_ref, qk, 1.0)):
            for c0 in range(0, qk, LANES):
                blk = p[:, c_lo + c0:c_lo + c0 + LANES]
                r = blk * cos + pltpu.roll(blk, 16, 1) * sa + pltpu.roll(blk, LANES - 16, 1) * sb
                o_ref[0, r0:r0 + tm, c0:c0 + LANES] = (r * scale).astype(o_ref.dtype)
        v_ref[0, r0:r0 + tm, :] = p[:, 2 * qk:].astype(v_ref.dtype)


def _latproj_call(x, mod, g, w_all, b_all, rope_tabs, conv_w, conv_b, n_att, qk, q_scale, tm=ROW_BLOCK):
    bsz, s, d = x.shape
    n = w_all.shape[2]
    c = (n - n_att) // 3
    halo = SUBLANES
    nb = tm // halo
    last = s // halo - 1
    tab = pl.BlockSpec((tm, LANES), lambda bi, i: (i, 0))
    row = lambda w: pl.BlockSpec((1, tm, w), lambda bi, i: (bi, i, 0))
    nslab = FFT_N2 // SUBLANES
    cblk = pl.BlockSpec((1, c // LANES, nslab, tm // nslab, LANES), lambda bi, i: (bi, 0, 0, i, 0))
    return pl.pallas_call(
        functools.partial(_latproj_kernel, n_att=n_att, q_scale=q_scale, seq_len=s),
        grid=(bsz, s // tm),
        in_specs=[row(d),
                  pl.BlockSpec((1, halo, d), lambda bi, i: (bi, jnp.maximum(i * nb - 1, 0), 0)),
                  pl.BlockSpec((1, halo, d), lambda bi, i: (bi, jnp.minimum((i + 1) * nb, last), 0)),
                  _mod_spec(mod, 0), _const_spec((1, d)), pl.BlockSpec(memory_space=pl.ANY), _const_spec(b_all.shape),
                  tab, tab, tab,
                  _const_spec(conv_w.shape), _const_spec((1, 3 * c))],
        out_specs=[row(qk), row(qk), row(n_att - 2 * qk), cblk, cblk],
        out_shape=[jax.ShapeDtypeStruct((bsz, s, qk), BF16), jax.ShapeDtypeStruct((bsz, s, qk), BF16),
                   jax.ShapeDtypeStruct((bsz, s, n_att - 2 * qk), BF16),
                   jax.ShapeDtypeStruct((bsz, c // LANES, nslab, s // nslab, LANES), F32),
                   jax.ShapeDtypeStruct((bsz, c // LANES, nslab, s // nslab, LANES), F32)],
        scratch_shapes=[pltpu.VMEM((d, n), BF16)],
        compiler_params=_params(2),
        name="latproj",
    )(x, x, x, mod, g.reshape(1, d), w_all, b_all, *rope_tabs, conv_w, conv_b.reshape(1, 3 * c))


def _rope_tables(seq_len):
    axis_dim = ATT_QK_DIM // 2
    n_freq = axis_dim // 2
    inv = (ROPE_BASE ** (-np.arange(n_freq, dtype=np.float32) / n_freq)).astype(np.float32)
    t = np.arange(seq_len)
    row, col = t // GRID_W, t % GRID_W
    jj = np.arange(128) % ATT_QK_DIM
    is_col = (jj // axis_dim) == 1
    second = ((jj % axis_dim) >= n_freq)[None, :]
    pos = np.where(is_col[None, :], col[:, None], row[:, None]).astype(np.float32)
    ang = (pos * inv[jj % n_freq][None, :]).astype(np.float64)
    cos, sin = np.cos(ang), np.sin(ang)
    as32 = lambda m: jnp.asarray(m, dtype=F32)
    return as32(cos), as32(np.where(second, sin, 0.0)), as32(np.where(second, 0.0, -sin))


def _attn_kernel(lam_ref, q_ref, kc_ref, vc_ref, kl_ref, vl_ref, g_ref, *rest, tk, lam_init):
    n_side = (len(rest) - 1) // 2
    o_ref = rest[n_side]
    q = q_ref[0]
    tq = q.shape[0]
    lane = lax.broadcasted_iota(jnp.int32, q.shape, 1)
    zero = jnp.zeros_like(q)
    q2 = jnp.concatenate([jnp.where(lane < ATT_QK_DIM, q, zero), jnp.where(lane >= ATT_QK_DIM, q, zero)], axis=0)

    def chunk(k, v, m, acc):
        s = lax.dot_general(q2, k, (((1,), (1,)), ((), ())), preferred_element_type=F32)
        rowmax = jnp.max(s, axis=1, keepdims=True)
        m_new = rowmax if m is None else jnp.maximum(m, rowmax)
        p = jnp.exp2(s - m_new).astype(BF16)
        pv = _dot(p, jnp.concatenate([v, jnp.ones_like(v)], axis=1))
        return m_new, (pv if acc is None else jnp.exp2(m - m_new) * acc + pv)

    m, acc = chunk(kc_ref[0], vc_ref[0], None, None)
    for j in range(kl_ref.shape[1] // tk):
        m, acc = chunk(kl_ref[0, j * tk:(j + 1) * tk, :], vl_ref[0, j * tk:(j + 1) * tk, :], m, acc)

    for w_ref, wb_ref in zip(rest[:n_side], rest[n_side + 1:]):
        wb_ref[...] = w_ref[...].astype(BF16)

    lamv = lam_ref[...]
    lam = (jnp.exp(jnp.sum(lamv[0:1] * lamv[1:2], axis=1, keepdims=True))
           - jnp.exp(jnp.sum(lamv[2:3] * lamv[3:4], axis=1, keepdims=True)) + lam_init)
    o_all = acc[:, :ATT_V_DIM] / acc[:, ATT_V_DIM:]
    o = o_all[:tq] - lam * o_all[tq:]
    o = _rms(o, SUBLN_EPS) * g_ref[...] * (1.0 - lam_init)
    o_ref[0] = o.astype(o_ref.dtype)


def _attn_call(q, kc, vc, kl, vl, lamv, g, lam_init, side_weights, tq=1024, tk=256):
    bsz, l, width = q.shape
    heads = width // ATT_V_DIM
    lc = kc.shape[1]
    hd = ATT_V_DIM
    tq = min(tq, l)
    nq = l // tq
    n_steps = bsz * heads * nq
    flat = lambda b, h, i: (b * heads + h) * nq + i
    side_specs = []
    for w in side_weights:
        rows = w.shape[1] // n_steps
        assert rows * n_steps == w.shape[1] and rows % (2 * SUBLANES) == 0
        side_specs.append(pl.BlockSpec((w.shape[0], rows, w.shape[2]), lambda b, h, i: (0, flat(b, h, i), 0)))
    outs = pl.pallas_call(
        functools.partial(_attn_kernel, tk=tk, lam_init=lam_init),
        grid=(bsz, heads, nq),
        in_specs=[
            _const_spec(lamv.shape),
            pl.BlockSpec((1, tq, hd), lambda b, h, i: (b, i, h)),
            pl.BlockSpec((1, lc, hd), lambda b, h, i: (b, 0, h)),
            pl.BlockSpec((1, lc, hd), lambda b, h, i: (b, 0, h)),
            pl.BlockSpec((1, l, hd), lambda b, h, i: (b, 0, h)),
            pl.BlockSpec((1, l, hd), lambda b, h, i: (b, 0, h)),
            _const_spec((1, hd)),
        ] + side_specs,
        out_specs=[pl.BlockSpec((1, tq, hd), lambda b, h, i: (b, i, h))] + side_specs,
        out_shape=[jax.ShapeDtypeStruct((bsz, l, width), BF16)]
                  + [jax.ShapeDtypeStruct(w.shape, BF16) for w in side_weights],
        compiler_params=_params(3),
        name="diffattn",
    )(lamv, q, kc, vc, kl, vl, g.reshape(1, hd), *side_weights)
    return outs[0], outs[1:]


def _gather_minor(refs, n_rows):
    cols = []
    for j in range(SUBLANES):
        parts = [r[pl.ds(j, n_rows, stride=SUBLANES), :] for r in refs]
        cols.append(parts[0] if len(parts) == 1 else jnp.concatenate(parts, axis=0))
    return jnp.concatenate(cols, axis=1)


def _spectrum_store(val, spec_ref, jb):
    rows = val.shape[0]
    for j in range(SUBLANES):
        spec_ref[jb, pl.ds(j, rows, stride=SUBLANES), :] = val[:, j * LANES:(j + 1) * LANES]


def _spectrum_blocks(spec_ref, k0):
    start = k0 * SUBLANES if isinstance(k0, int) else pl.multiple_of(k0 * SUBLANES, SUBLANES * SUBLANES)
    x = spec_ref[:, pl.ds(start, SUBLANES * SUBLANES), :]
    return jnp.stack([x[:, i * SUBLANES:(i + 1) * SUBLANES, :].reshape(FFT_N2, LANES) for i in range(SUBLANES)], axis=0)


def _spectrum_put(spec_ref, k0, val):
    nb = FFT_N2 // SUBLANES
    v = val.reshape(SUBLANES, nb, SUBLANES, LANES)
    x = jnp.stack([v[:, jb].reshape(SUBLANES * SUBLANES, LANES) for jb in range(nb)], axis=0)
    spec_ref[:, pl.ds(pl.multiple_of(k0 * SUBLANES, SUBLANES * SUBLANES), SUBLANES * SUBLANES), :] = x


def _twiddle_block(twc_ref, twf_ref, kb):
    n2 = FFT_N2
    rows = pl.ds(pl.multiple_of(kb * n2, n2), n2)
    cr, ci = twc_ref[0, rows, :][None], twc_ref[1, rows, :][None]
    fr, fi = twf_ref[0].reshape(SUBLANES, n2, LANES), twf_ref[1].reshape(SUBLANES, n2, LANES)
    return cr * fr - ci * fi, cr * fi + ci * fr


def _filter_fft_kernel(z_ref, w1_ref, b1_ref, f1_ref, w2_ref, b2_ref, f2_ref, w3f_ref, w3b_ref, dl_ref,
                       fr_ref, g_ref, twc_ref, twf_ref, kr_ref, ki_ref, h2_ref, hs_ref, a_ref, *, tm):
    half = z_ref.shape[0]
    seq_len = 2 * half
    hid = w1_ref.shape[0]
    n1 = a_ref.shape[2] // SUBLANES
    n1h = hs_ref.shape[2] // SUBLANES
    n2 = FFT_N2
    nb = n2 // SUBLANES
    w = SUBLANES * LANES
    zeros = jnp.zeros((hid, hid), F32)

    @pl.when(pl.program_id(0) == 0)
    def _():
        blockdiag = lambda m: jnp.concatenate([jnp.concatenate([m, zeros], axis=1),
                                               jnp.concatenate([zeros, m], axis=1)], axis=0)
        twice = lambda ref: jnp.concatenate([ref[...], ref[...]], axis=1)
        w1, w2 = blockdiag(w1_ref[...]), blockdiag(w2_ref[...])
        b1, f1, b2, f2 = twice(b1_ref), twice(f1_ref), twice(b2_ref), twice(f2_ref)

        def features(i, _):
            r = pl.ds(pl.multiple_of(i * tm, tm), tm)
            h1 = jnp.sin(f1 * (_dot3f(z_ref[r, :], w1) + b1))
            h2_ref[r, :] = jnp.sin(f2 * (_dot3f(h1, w2) + b2))
            return 0

        lax.fori_loop(0, half // tm, features, 0)

    for upper in range(2):
        pad = lambda m: jnp.concatenate([jnp.zeros_like(m), m] if upper else [m, jnp.zeros_like(m)], axis=0)
        w3s = [pad(w3f_ref[...]), pad(w3b_ref[...])]

        def taps(i, _, upper=upper, w3s=w3s):
            h2 = h2_ref[pl.ds(pl.multiple_of(i * tm, tm), tm), :]
            t0 = upper * half + i * tm
            row = lax.broadcasted_iota(jnp.int32, (tm, LANES), 0) + t0
            decay = jnp.exp(-(row.astype(F32) * (1.0 / (seq_len - 1))) * dl_ref[...])
            for t in range(2):
                h = _dot3f(h2, w3s[t]) * decay
                if t == 1:
                    h = jnp.where(row == 0, 0.0, h)
                h4 = h.reshape(tm // n2, nb, SUBLANES, LANES)
                for jb in range(nb):
                    srow = pl.multiple_of(upper * (half // nb) + i * (tm // nb), tm // nb)
                    hs_ref[t, jb, pl.ds(srow, tm // nb), :] = h4[:, jb].reshape(tm // nb, LANES)
            return 0

        lax.fori_loop(0, half // tm, taps, 0)

    for jb in range(nb):
        z = jnp.concatenate([_gather_minor([hs_ref.at[t, jb]], n1h) for t in range(2)], axis=1)
        out = _dft_dot(fr_ref[...], z)
        for t in range(4):
            _spectrum_store(out[(t % 2) * n1:(t % 2 + 1) * n1, (t // 2) * w:(t // 2 + 1) * w], a_ref.at[t], jb)

    def body(kb, _):
        k0 = pl.multiple_of(kb * SUBLANES, SUBLANES)
        twr, twi = _twiddle_block(twc_ref, twf_ref, kb)
        a = [_spectrum_blocks(a_ref.at[t], k0) for t in range(4)]
        cols = []
        for t in range(2):
            ar, ai = a[2 * t], a[2 * t + 1]
            sr, si = ar * twr - ai * twi, ar * twi + ai * twr
            cols += [jnp.concatenate([sr[i], si[i]], axis=0) for i in range(SUBLANES)]
        x = _dft_dot(g_ref[...], jnp.concatenate(cols, axis=1))
        for i in range(SUBLANES):
            xf = x[:, i * LANES:(i + 1) * LANES]
            xb = x[:, (SUBLANES + i) * LANES:(SUBLANES + i + 1) * LANES]
            rows = pl.ds(pl.multiple_of((k0 + i) * n2, n2), n2)
            kr_ref[rows, :] = xf[:n2] + xb[:n2]
            ki_ref[rows, :] = xf[n2:] - xb[n2:]
        return 0

    lax.fori_loop(0, n1 // SUBLANES, body, 0, unroll=4)


def _filter_fft_call(dft, z, w1, b1, f1, w2, b2, f2, w3, absdelta, tm=512):
    hid = z.shape[1] // 2
    seq_len = 2 * z.shape[0]
    c = w3.shape[1] // 2
    n, n1, n1h = dft["n"], dft["n1"], dft["n1h"]
    nb = FFT_N2 // SUBLANES
    ncb = c // LANES
    fr, g = dft["f_real"].astype(BF16), dft["g_fwd"].astype(BF16)
    twc, twf = dft["tw_coarse"], dft["tw_fine"]
    vec = _const_spec((1, hid))
    w3_blk = lambda off: pl.BlockSpec((hid, LANES), lambda i, off=off: (0, i + off))
    out = pl.BlockSpec((None, n, LANES), lambda i: (i, 0, 0))
    return pl.pallas_call(
        functools.partial(_filter_fft_kernel, tm=tm),
        grid=(ncb,),
        in_specs=[_const_spec(z.shape), _const_spec((hid, hid)), vec, vec, _const_spec((hid, hid)), vec, vec,
                  w3_blk(0), w3_blk(ncb), pl.BlockSpec((1, LANES), lambda i: (0, i)),
                  _const_spec(fr.shape), _const_spec(g.shape), _const_spec(twc.shape), _const_spec(twf.shape)],
        out_specs=[out, out],
        out_shape=[jax.ShapeDtypeStruct((ncb, n, LANES), F32)] * 2,
        scratch_shapes=[pltpu.VMEM(z.shape, F32), pltpu.VMEM((2, nb, n1h * SUBLANES, LANES), F32),
                        pltpu.VMEM((4, nb, n1 * SUBLANES, LANES), F32)],
        compiler_params=_params(1),
        name="hyfilter_fft",
    )(z, w1, b1.reshape(1, hid), f1.reshape(1, hid), w2, b2.reshape(1, hid), f2.reshape(1, hid), w3, w3, absdelta,
      fr, g, twc, twf)


def _hyena_conv_kernel(fc_ref, g_ref, gc_ref, e_ref, twc_ref, twf_ref, kr_ref, ki_ref, u_ref, x0_ref, bias_ref, o_ref,
                       a_ref, *, inv_n):
    n1 = a_ref.shape[2] // SUBLANES
    n1h = u_ref.shape[2] // SUBLANES
    n2 = FFT_N2
    nb = n2 // SUBLANES

    for jb in range(nb):
        z = _gather_minor([u_ref.at[0, jb], u_ref.at[1, jb]], n1h)
        out = _dft_dot(fc_ref[...], z)
        for c in range(2):
            _spectrum_store(out[c * n1:(c + 1) * n1], a_ref.at[c], jb)

    def body(kb, _):
        k0 = pl.multiple_of(kb * SUBLANES, SUBLANES)
        rows = pl.ds(pl.multiple_of(kb * (SUBLANES * n2), SUBLANES * n2), SUBLANES * n2)
        blocks = lambda ref: ref[rows, :].reshape(SUBLANES, n2, LANES)
        (twr, twi), kr, ki = _twiddle_block(twc_ref, twf_ref, kb), blocks(kr_ref), blocks(ki_ref)
        ar, ai = _spectrum_blocks(a_ref.at[0], k0), _spectrum_blocks(a_ref.at[1], k0)
        sr, si = ar * twr - ai * twi, ar * twi + ai * twr
        lanes = lambda re, im: jnp.concatenate([jnp.concatenate([re[i], im[i]], axis=0) for i in range(SUBLANES)], axis=1)
        x = _dft_dot(g_ref[...], lanes(sr, si))
        unl = lambda v, lo: jnp.stack([v[lo:lo + n2, i * LANES:(i + 1) * LANES] for i in range(SUBLANES)], axis=0)
        xr, xi = unl(x, 0), unl(x, n2)
        t = _dft_dot(gc_ref[...], lanes(xr * kr - xi * ki, xr * ki + xi * kr))
        tr, ti = unl(t, 0), unl(t, n2)
        _spectrum_put(a_ref.at[0], k0, tr * twr + ti * twi)
        _spectrum_put(a_ref.at[1], k0, ti * twr - tr * twi)
        return 0

    lax.fori_loop(0, n1 // SUBLANES, body, 0, unroll=4)

    for jb in range(nb):
        t = _gather_minor([a_ref.at[0, jb], a_ref.at[1, jb]], n1)
        y = _dft_dot(e_ref[...], t) * inv_n
        for c in range(2):
            _spectrum_store(y[c * n1h:(c + 1) * n1h], o_ref.at[c], jb)
    o_ref[...] = x0_ref[...] * (o_ref[...] + u_ref[...] * bias_ref[...])


def _dft_blocks(seq_len):
    n = 2 * seq_len
    n2 = FFT_N2
    n1 = n // n2
    n1h = n1 // 2
    k1 = np.arange(n1)[:, None].astype(np.float64)
    a = 2.0 * np.pi * k1 * np.arange(n1h)[None, :] / n1
    fr, fi = np.cos(a), -np.sin(a)
    f_cplx = np.block([[fr, -fi], [fi, fr]])
    f_real = np.concatenate([fr, fi], axis=0)
    e_cplx = np.block([[fr.T, fi.T], [-fi.T, fr.T]])
    b = 2.0 * np.pi * np.arange(n2)[:, None] * np.arange(n2)[None, :] / n2
    gr, gi = np.cos(b), -np.sin(b)
    g_fwd = np.block([[gr, -gi], [gi, gr]])
    g_inv = np.block([[gr, gi], [-gi, gr]])
    def table(k1s):
        ang = (2.0 * np.pi / n) * (k1s[:, None] * np.arange(n2)[None, :]).reshape(-1, 1)
        return np.broadcast_to(np.stack([np.cos(ang), -np.sin(ang)]), (2, k1s.size * n2, LANES))
    as32 = lambda m: jnp.asarray(np.ascontiguousarray(m), dtype=F32)
    return dict(n=n, n1=n1, n1h=n1h, f_cplx=as32(f_cplx), f_real=as32(f_real), e_cplx=as32(e_cplx),
                g_fwd=as32(g_fwd), g_inv=as32(g_inv), tw_coarse=as32(table(np.arange(0, n1, SUBLANES))),
                tw_fine=as32(table(np.arange(SUBLANES))))


def _hyena_call(u, x0, filt, hy_bias):
    bsz, ncb, nslab, srows, _ = u.shape
    seq_len = nslab * srows
    c = ncb * LANES
    n2 = FFT_N2
    dft = _dft_blocks(seq_len)
    n, n1, n1h = dft["n"], dft["n1"], dft["n1h"]
    pairs = bsz // 2

    w1, b1, f1, w2, b2, f2, w3 = filt
    hid = w2.shape[0]
    bands = (HY_POS_EMB - 1) // 2
    t = np.linspace(0.0, 1.0, seq_len)[:, None]
    freqs = np.linspace(1e-4, bands - 1, bands)
    ang = (2.0 * math.pi / seq_len) * np.arange(seq_len)[:, None] * freqs[None, :]
    z = np.concatenate([t, np.cos(ang), -np.sin(ang), np.zeros((seq_len, hid - HY_POS_EMB))], axis=-1)
    z = jnp.asarray(np.concatenate([z[:seq_len // 2], z[seq_len // 2:]], axis=1), dtype=F32)
    w1p = jnp.concatenate([w1, jnp.zeros((hid - HY_POS_EMB, hid), F32)], axis=0)
    max_decay = math.log(HY_DECAY_TARGET) / HY_FAST_DECAY_PCT
    min_decay = math.log(HY_DECAY_TARGET) / HY_SLOW_DECAY_PCT
    absdelta = jnp.asarray(np.abs(np.linspace(min_decay, max_decay, c))[None, :], dtype=F32)
    nb = n2 // SUBLANES
    kr, ki = _filter_fft_call(dft, z, w1p, b1, f1, w2, b2, f2, w3, absdelta)

    view = lambda a: a.reshape(pairs, 2, ncb, nb, n1h * SUBLANES, LANES)
    pair_spec = pl.BlockSpec((None, 2, None, nb, n1h * SUBLANES, LANES), lambda cb, p: (p, 0, cb, 0, 0, 0))
    k_spec = pl.BlockSpec((None, n, LANES), lambda cb, p: (cb, 0, 0))
    twc, twf = dft["tw_coarse"], dft["tw_fine"]
    fc, g, gc, e = (dft[k].astype(BF16) for k in ("f_cplx", "g_fwd", "g_inv", "e_cplx"))
    hy = pl.pallas_call(
        functools.partial(_hyena_conv_kernel, inv_n=1.0 / n),
        grid=(ncb, pairs),
        in_specs=[_const_spec(fc.shape), _const_spec(g.shape), _const_spec(gc.shape), _const_spec(e.shape),
                  _const_spec(twc.shape), _const_spec(twf.shape), k_spec, k_spec, pair_spec, pair_spec,
                  pl.BlockSpec((1, LANES), lambda cb, p: (0, cb))],
        out_specs=pair_spec,
        out_shape=jax.ShapeDtypeStruct((pairs, 2, ncb, nb, n1h * SUBLANES, LANES), F32),
        scratch_shapes=[pltpu.VMEM((2, nb, n1 * SUBLANES, LANES), F32)],
        compiler_params=_params(2),
        name="hyconv",
    )(fc, g, gc, e, twc, twf, kr, ki, view(u), view(x0), hy_bias.reshape(1, c))
    return hy.reshape(bsz, ncb, nb, n1h * SUBLANES, LANES)


def _stage_weights_bf16(pairs):
    for src, dst in pairs:
        n_rows, n_cols = src.shape
        rows = min(n_rows, 1 << ((WEIGHT_STAGE_BYTES // (4 * n_cols)).bit_length() - 1))
        n_chunks = n_rows // rows
        assert rows % (2 * SUBLANES) == 0 and n_chunks * rows == n_rows

        def run(stage, sem, src=src, dst=dst, rows=rows, n_chunks=n_chunks):
            copy = lambda k: pltpu.make_async_copy(src.at[pl.ds(k * rows, rows), :], stage.at[k % 2], sem.at[k % 2])
            copy(0).start()
            for k in range(n_chunks):
                if k + 1 < n_chunks:
                    copy(k + 1).start()
                copy(k).wait()
                dst[pl.ds(k * rows, rows), :] = stage[k % 2].astype(BF16)

        pl.run_scoped(run, pltpu.VMEM((2, rows, n_cols), F32), pltpu.SemaphoreType.DMA((2,)))


def _mlp_tail(h, g2, shift, scale, gate, w1_ref, w2_ref, fc):
    a = (_rms(h, NORM_EPS) * g2) * (1.0 + scale) + shift
    a = a.astype(BF16)
    acc = None
    for c0 in range(0, w1_ref.shape[1], fc):
        hid = jnp.maximum(_dot(a, w1_ref[:, c0:c0 + fc]), 0.0)
        part = _dot((hid * hid).astype(BF16), w2_ref[c0:c0 + fc, :])
        acc = part if acc is None else acc + part
    return h + gate * acc


def _mixmlp_kernel(x_ref, att_ref, hy_ref, mod_ref, wo_s, bo_ref, g2_ref, w1_s, w2_s, o_ref, *, fc):
    blk, d = x_ref.shape[1:]
    tm = blk // ROW_SUBTILES
    wa = att_ref.shape[2]
    mod = mod_ref[0, pl.ds(pl.program_id(0), 1), :]
    for r in range(ROW_SUBTILES):
        rows = slice(r * tm, (r + 1) * tm)
        hy = jnp.concatenate([_spectrum_blocks(hy_ref.at[0, cb], r * tm // FFT_N2).reshape(tm, LANES)
                              for cb in range(hy_ref.shape[1])], axis=1).astype(BF16)
        y = _dot(att_ref[0, rows, :], wo_s[:wa, :]) + _dot(hy, wo_s[wa:, :]) + bo_ref[...]
        h = x_ref[0, rows, :] + mod[:, 2 * d:3 * d] * y
        o_ref[0, rows, :] = _mlp_tail(h, g2_ref[...], mod[:, 3 * d:4 * d], mod[:, 4 * d:5 * d], mod[:, 5 * d:6 * d],
                                      w1_s, w2_s, fc)


def _mixmlp_call(x, att, hy, mod, w_out, b_out, g2, w1, w2, layer, tm=ROW_BLOCK, fc=1024):
    bsz, l, d = x.shape
    wa = att.shape[2]
    row = lambda w: pl.BlockSpec((1, tm, w), lambda b, i: (b, i, 0))
    return pl.pallas_call(
        functools.partial(_mixmlp_kernel, fc=fc),
        grid=(bsz, l // tm),
        in_specs=[row(d), row(wa),
                  pl.BlockSpec((1, hy.shape[1], hy.shape[2], tm // hy.shape[2], LANES), lambda b, i: (b, 0, 0, i, 0)),
                  _mod_spec(mod, layer),
                  _layer_spec(w_out, layer // 2), _const_spec((1, d)), _const_spec((1, d)),
                  _layer_spec(w1, layer), _layer_spec(w2, layer)],
        out_specs=row(d),
        out_shape=jax.ShapeDtypeStruct((bsz, l, d), F32),
        compiler_params=_params(2),
        name="mixmlp",
    )(x, att, hy, mod, w_out, b_out.reshape(1, d), g2.reshape(1, d), w1, w2)


def _poolmlp_kernel(h_ref, hp_ref, hn_ref, mod_ref, g1_ref, pw_s, ps_ref, g2_ref, w1_s, w2_s, gf_ref, o_ref,
                    *, fc, seq_len):
    blk, d = h_ref.shape[1:]
    tm = blk // ROW_SUBTILES
    halo = POOL_HALO
    rows = tm + 2 * halo
    mod = mod_ref[0, pl.ds(pl.program_id(0), 1), :]
    gd = d // len(POOL_WINDOWS)
    for r in range(ROW_SUBTILES):
        r0 = r * tm
        h = h_ref[0, r0:r0 + tm, :]
        above = hp_ref[0] if r == 0 else h_ref[0, r0 - halo:r0, :]
        below = hn_ref[0] if r == ROW_SUBTILES - 1 else h_ref[0, r0 + tm:r0 + tm + halo, :]
        hx = jnp.concatenate([above, h, below], axis=0)
        a = (_rms(hx, NORM_EPS) * g1_ref[...]) * (1.0 + mod[:, d:2 * d]) + mod[:, 0:d]
        t = lax.broadcasted_iota(jnp.int32, (rows, 1), 0) + (pl.program_id(1) * blk + r0 - halo)
        a = jnp.where((t >= 0) & (t < seq_len), a, 0.0)
        tc = t[halo:halo + tm]
        ys = []
        for g, win in enumerate(POOL_WINDOWS):
            ag = a[:, g * gd:(g + 1) * gd]
            f, m = ag, 1
            while 2 * m <= win // 2:
                f = f + pltpu.roll(f, rows - m, 0)
                m *= 2
            s = pltpu.roll(f, win // 2, 0) + f
            cnt = (jnp.minimum(tc + (win - win // 2), seq_len) - jnp.maximum(tc - win // 2, 0)).astype(F32)
            dlt = s[halo:halo + tm] / cnt - ag[halo:halo + tm]
            ys.append(_dot(dlt.astype(BF16), pw_s[g * gd:(g + 1) * gd, :]))
        y = jnp.concatenate(ys, axis=1) * ps_ref[...]
        h1 = h + mod[:, 2 * d:3 * d] * y
        h2 = _mlp_tail(h1, g2_ref[...], mod[:, 3 * d:4 * d], mod[:, 4 * d:5 * d], mod[:, 5 * d:6 * d], w1_s, w2_s, fc)
        o_ref[0, r0:r0 + tm, :] = _rms(h2, NORM_EPS) * gf_ref[...]


def _poolmlp_call(h, mod, g1, pool_w, pool_scale, g2, w1, w2, gf, layer, tm=ROW_BLOCK, fc=1024):
    bsz, l, d = h.shape
    halo = POOL_HALO
    nb = tm // halo
    last = l // halo - 1
    row = pl.BlockSpec((1, tm, d), lambda b, i: (b, i, 0))
    return pl.pallas_call(
        functools.partial(_poolmlp_kernel, fc=fc, seq_len=l),
        grid=(bsz, l // tm),
        in_specs=[row,
                  pl.BlockSpec((1, halo, d), lambda b, i: (b, jnp.maximum(i * nb - 1, 0), 0)),
                  pl.BlockSpec((1, halo, d), lambda b, i: (b, jnp.minimum((i + 1) * nb, last), 0)),
                  _mod_spec(mod, layer),
                  _const_spec((1, d)), _layer_spec(pool_w, layer // 2), _const_spec((1, d)), _const_spec((1, d)),
                  _layer_spec(w1, layer), _layer_spec(w2, layer), _const_spec((1, d))],
        out_specs=row,
        out_shape=jax.ShapeDtypeStruct((bsz, l, d), F32),
        compiler_params=_params(2),
        name="poolmlp",
    )(h, h, h, mod, g1.reshape(1, d), pool_w, pool_scale.reshape(1, d), g2.reshape(1, d), w1, w2, gf.reshape(1, d))


def kernel(x, c, ctx, c_ctx, ada_w, ada_b, norm1_g, norm2_g, mix_w_in, mix_b_in, mix_w_out, mix_b_out, lam_q1, lam_k1, lam_q2, lam_k2, subln_g, hy_conv_w, hy_conv_b, hy_pos_w1, hy_pos_b1, hy_freq1, hy_pos_w2, hy_pos_b2, hy_freq2, hy_pos_w3, hy_bias, pool_w, pool_scale, mlp_w1, mlp_w2, final_g):
    bsz, seq_len, d = x.shape
    depth = ada_w.shape[0]
    assert depth == 2 and bsz % 2 == 0 and bsz < MOD_ROWS and seq_len % GRID_W == 0
    att_w = ATT_HEADS * ATT_V_DIM
    q_cols = k_cols = ATT_HEADS * 2 * ATT_QK_DIM
    kv_start, hy_start = q_cols, q_cols + k_cols + att_w
    in_cols = mix_w_in.shape[2]

    cv = jnp.concatenate([c, c_ctx[None, :], jnp.zeros((MOD_ROWS - bsz - 1, d), F32)], axis=0)
    mod = _mod_call(cv, ada_w, ada_b)

    lam_init = 0.8 - 0.6 * math.exp(-0.3 * 0)
    assert k_cols == q_cols and in_cols - hy_start == 3 * (d - att_w)
    q, k, v, u, x0 = _latproj_call(
        x, mod, norm1_g[0], mix_w_in, mix_b_in, _rope_tables(seq_len),
        hy_conv_w[0], hy_conv_b[0], n_att=hy_start, qk=q_cols, q_scale=ATT_QK_DIM ** -0.5 * math.log2(math.e))
    kc, vc = _ctxproj_call(ctx, mod, bsz, norm1_g[0], mix_w_in, mix_b_in, kv_start, k_cols, hy_start - kv_start)
    lamv = jnp.stack([lam_q1[0], lam_k1[0], lam_q2[0], lam_k2[0]], axis=0)
    pool_w3 = pool_w.reshape(pool_w.shape[0], d, d // len(POOL_WINDOWS))
    att, (w_out_b, w1_b, w2_b, pool_wb) = _attn_call(q, kc, vc, k, v, lamv, subln_g[0], lam_init,
                                                     side_weights=[mix_w_out, mlp_w1, mlp_w2, pool_w3])
    filt = (hy_pos_w1[0], hy_pos_b1[0], hy_freq1[0], hy_pos_w2[0], hy_pos_b2[0], hy_freq2[0], hy_pos_w3[0])
    hy = _hyena_call(u, x0, filt, hy_bias[0])
    h = _mixmlp_call(x, att, hy, mod, w_out_b, mix_b_out[0], norm2_g[0], w1_b, w2_b, layer=0)

    return _poolmlp_call(h, mod, norm1_g[1], pool_wb, pool_scale[0], norm2_g[1], w1_b, w2_b, final_g, layer=1)
```

```python
import functools
import math

import numpy as np
import jax
import jax.numpy as jnp
from jax import lax
from jax.experimental import pallas as pl
from jax.experimental.pallas import tpu as pltpu

F32 = jnp.float32
BF16 = jnp.bfloat16

GRID_W = 64
ATT_HEADS = 4
ATT_V_DIM = 128
ATT_QK_DIM = 64
ROPE_BASE = 10000.0
HY_POS_EMB = 33
HY_DECAY_TARGET = 1e-2
HY_FAST_DECAY_PCT = 0.3
HY_SLOW_DECAY_PCT = 1.5
POOL_WINDOWS = (2, 4, 8, 16)
NORM_EPS = 1e-6
SUBLN_EPS = 1e-5
SUBLANES = 8
LANES = 128
FFT_N2 = 64
POOL_HALO = 8
WEIGHT_STAGE_BYTES = 2 * 1024 * 1024
ROW_BLOCK = 1024
ROW_SUBTILES = 2
MOD_ROWS = 8

VMEM_LIMIT_BYTES = 56 * 1024 * 1024


def _params(n_grid_dims):
    return pltpu.CompilerParams(
        dimension_semantics=("arbitrary",) * n_grid_dims,
        vmem_limit_bytes=VMEM_LIMIT_BYTES,
    )


def _const_spec(shape):
    nd = len(shape)
    return pl.BlockSpec(shape, lambda *_: (0,) * nd, pipeline_mode=pl.Buffered(1))


def _split_bf16(a):
    hi = a.astype(BF16)
    lo = (a - hi.astype(F32)).astype(BF16)
    return hi, lo


def _dot(a, b):
    return jnp.dot(a, b, preferred_element_type=F32)


def _dot3(a_hi, a_lo, b_hi, b_lo):
    return _dot(a_hi, b_hi) + (_dot(a_lo, b_hi) + _dot(a_hi, b_lo))


def _dot3f(a, b):
    a_hi, a_lo = _split_bf16(a)
    b_hi, b_lo = _split_bf16(b)
    return _dot3(a_hi, a_lo, b_hi, b_lo)


def _dft_dot(a_bf16, b):
    return _dot(a_bf16, b.astype(BF16))


def _rms(x, eps):
    return x * lax.rsqrt(jnp.mean(x * x, axis=-1, keepdims=True) + eps)


def _mod_kernel(cv_ref, w_ref, b_ref, o_ref):
    cv = cv_ref[...]
    s_hi, s_lo = _split_bf16(cv / (1.0 + jnp.exp(-cv)))
    w_hi, w_lo = _split_bf16(w_ref[0])
    r = _dot(jnp.concatenate([s_hi, s_lo], axis=0), w_hi)
    o_ref[0] = r[:MOD_ROWS] + (r[MOD_ROWS:] + _dot(s_hi, w_lo)) + b_ref[0]


def _mod_call(cv, ada_w, ada_b, tn=1536):
    depth, d, n = ada_w.shape
    return pl.pallas_call(
        _mod_kernel,
        grid=(depth, n // tn),
        in_specs=[
            pl.BlockSpec((MOD_ROWS, d), lambda i, j: (0, 0)),
            pl.BlockSpec((1, d, tn), lambda i, j: (i, 0, j)),
            pl.BlockSpec((1, 1, tn), lambda i, j: (i, 0, j)),
        ],
        out_specs=pl.BlockSpec((1, MOD_ROWS, tn), lambda i, j: (i, 0, j)),
        out_shape=jax.ShapeDtypeStruct((depth, MOD_ROWS, n), F32),
        compiler_params=_params(2),
        name="mod",
    )(cv, ada_w, ada_b.reshape(depth, 1, n))


def _mod_spec(mod, layer):
    return pl.BlockSpec((1,) + mod.shape[1:], lambda *_: (layer, 0, 0), pipeline_mode=pl.Buffered(1))


def _layer_spec(stack, layer):
    return pl.BlockSpec((None,) + stack.shape[1:], lambda *_: (layer, 0, 0), pipeline_mode=pl.Buffered(1))


def _ctxproj_kernel(x_ref, mod_ref, g_ref, w_hbm, b_ref, k_ref, v_ref, w_s, *, ctx_row, col0):
    d = x_ref.shape[2]
    nkv = w_s.shape[1]

    @pl.when(pl.program_id(0) == 0)
    def _():
        _stage_weights_bf16([(w_hbm.at[0, :, pl.ds(col0, nkv)], w_s)])

    m = mod_ref[0, ctx_row:ctx_row + 1, :]
    a = _rms(x_ref[0], NORM_EPS) * g_ref[...]
    a = a * (1.0 + m[:, d:2 * d]) + m[:, 0:d]
    p = _dot(a.astype(BF16), w_s[...]) + b_ref[0:1, col0:col0 + nkv]
    nk = k_ref.shape[2]
    k_ref[0] = p[:, :nk].astype(k_ref.dtype)
    v_ref[0] = p[:, nk:].astype(v_ref.dtype)


def _ctxproj_call(x, mod, ctx_row, g, w_all, b_all, col0, nk, nkv):
    bsz, s, d = x.shape
    return pl.pallas_call(
        functools.partial(_ctxproj_kernel, ctx_row=ctx_row, col0=col0),
        grid=(bsz,),
        in_specs=[pl.BlockSpec((1, s, d), lambda bi: (bi, 0, 0)), _mod_spec(mod, 0), _const_spec((1, d)),
                  pl.BlockSpec(memory_space=pl.ANY), _const_spec(b_all.shape)],
        out_specs=[pl.BlockSpec((1, s, nk), lambda bi: (bi, 0, 0)), pl.BlockSpec((1, s, nkv - nk), lambda bi: (bi, 0, 0))],
        out_shape=[jax.ShapeDtypeStruct((bsz, s, nk), BF16), jax.ShapeDtypeStruct((bsz, s, nkv - nk), BF16)],
        scratch_shapes=[pltpu.VMEM((d, nkv), BF16)],
        compiler_params=_params(1),
        name="ctxproj",
    )(x, mod, g.reshape(1, d), w_all, b_all)


def _side_cast_specs(side_weights, n_steps, flat):
    specs = []
    for w in side_weights:
        rows = w.shape[1] // n_steps
        assert rows * n_steps == w.shape[1] and rows % (2 * SUBLANES) == 0
        specs.append(pl.BlockSpec((w.shape[0], rows, w.shape[2]), lambda *ids: (0, flat(*ids), 0)))
    return specs


def _latproj_kernel(*refs, n_att, q_scale, seq_len, n_side):
    x_ref, xp_ref, xn_ref, mod_ref, g_ref, w_hbm, b_ref, cos_ref, sa_ref, sb_ref, cw_ref, cb_ref = refs[:12]
    q_ref, k_ref, v_ref, u_ref, x0_ref = refs[12 + n_side:17 + n_side]
    w_ref = refs[-1]
    for src_ref, dst_ref in zip(refs[12:12 + n_side], refs[17 + n_side:17 + 2 * n_side]):
        dst_ref[...] = src_ref[...].astype(BF16)

    @pl.when((pl.program_id(0) == 0) & (pl.program_id(1) == 0))
    def _():
        _stage_weights_bf16([(w_hbm.at[0], w_ref)])

    tm = x_ref.shape[1] // ROW_SUBTILES
    d = x_ref.shape[2]
    mod = mod_ref[0, pl.ds(pl.program_id(0), 1), :]
    halo = xp_ref.shape[1]
    rows = tm + 2 * halo
    ncb = u_ref.shape[1]
    c = ncb * LANES
    qk = q_ref.shape[2]
    for sub in range(ROW_SUBTILES):
        r0 = sub * tm
        above = xp_ref[0] if sub == 0 else x_ref[0, r0 - halo:r0, :]
        below = xn_ref[0] if sub == ROW_SUBTILES - 1 else x_ref[0, r0 + tm:r0 + tm + halo, :]
        xx = jnp.concatenate([above, x_ref[0, r0:r0 + tm, :], below], axis=0)
        a = _rms(xx, NORM_EPS) * g_ref[...]
        a = a * (1.0 + mod[:, d:2 * d]) + mod[:, 0:d]

        ph = _dot(a.astype(BF16), w_ref[:, n_att:]) + b_ref[0:1, n_att:]
        t = lax.broadcasted_iota(jnp.int32, (rows, 1), 0) + (pl.program_id(1) * (tm * ROW_SUBTILES) + r0 - halo)
        ph = jnp.where((t >= 0) & (t < seq_len), ph, 0.0)

        def conv(s):
            blk = ph[:, s * c:(s + 1) * c]
            prev = pltpu.roll(blk, 1, 0)[halo:halo + tm]
            nxt = pltpu.roll(blk, rows - 1, 0)[halo:halo + tm]
            cw = cw_ref[:, s * c:(s + 1) * c]
            return prev * cw[0:1] + blk[halo:halo + tm] * cw[1:2] + nxt * cw[2:3] + cb_ref[:, s * c:(s + 1) * c]

        x0 = conv(0)
        u = conv(2) * conv(1)
        nb = FFT_N2 // SUBLANES
        srows = slice(r0 // SUBLANES, (r0 + tm) // SUBLANES)
        for val, o_ref in ((x0, x0_ref), (u, u_ref)):
            for cb in range(ncb):
                v4 = val[:, cb * LANES:(cb + 1) * LANES].reshape(tm // FFT_N2, nb, SUBLANES, LANES)
                for jb in range(nb):
                    o_ref[0, cb, jb, srows, :] = v4[:, jb].reshape(tm // SUBLANES, LANES)

        p = _dot(a[halo:halo + tm].astype(BF16), w_ref[:, :n_att]) + b_ref[0:1, :n_att]
        cos, sa, sb = (ref[r0:r0 + tm, :] for ref in (cos_ref, sa_ref, sb_ref))
        for o_ref, c_lo, scale in ((q_ref, 0, q_scale), (k_ref, qk, 1.0)):
            for c0 in range(0, qk, LANES):
                blk = p[:, c_lo + c0:c_lo + c0 + LANES]
                r = blk * cos + pltpu.roll(blk, 16, 1) * sa + pltpu.roll(blk, LANES - 16, 1) * sb
                o_ref[0, r0:r0 + tm, c0:c0 + LANES] = (r * scale).astype(o_ref.dtype)
        v_ref[0, r0:r0 + tm, :] = p[:, 2 * qk:].astype(v_ref.dtype)


def _latproj_call(x, mod, g, w_all, b_all, rope_tabs, conv_w, conv_b, n_att, qk, q_scale, side_weights,
                  tm=ROW_BLOCK):
    bsz, s, d = x.shape
    nt = s // tm
    side_specs = _side_cast_specs(side_weights, bsz * nt, lambda bi, i: bi * nt + i)
    n = w_all.shape[2]
    c = (n - n_att) // 3
    halo = SUBLANES
    nb = tm // halo
    last = s // halo - 1
    tab = pl.BlockSpec((tm, LANES), lambda bi, i: (i, 0))
    row = lambda w: pl.BlockSpec((1, tm, w), lambda bi, i: (bi, i, 0))
    nslab = FFT_N2 // SUBLANES
    cblk = pl.BlockSpec((1, c // LANES, nslab, tm // nslab, LANES), lambda bi, i: (bi, 0, 0, i, 0))
    outs = pl.pallas_call(
        functools.partial(_latproj_kernel, n_att=n_att, q_scale=q_scale, seq_len=s, n_side=len(side_weights)),
        grid=(bsz, nt),
        in_specs=[row(d),
                  pl.BlockSpec((1, halo, d), lambda bi, i: (bi, jnp.maximum(i * nb - 1, 0), 0)),
                  pl.BlockSpec((1, halo, d), lambda bi, i: (bi, jnp.minimum((i + 1) * nb, last), 0)),
                  _mod_spec(mod, 0), _const_spec((1, d)), pl.BlockSpec(memory_space=pl.ANY), _const_spec(b_all.shape),
                  tab, tab, tab,
                  _const_spec(conv_w.shape), _const_spec((1, 3 * c))] + side_specs,
        out_specs=[row(qk), row(qk), row(n_att - 2 * qk), cblk, cblk] + side_specs,
        out_shape=[jax.ShapeDtypeStruct((bsz, s, qk), BF16), jax.ShapeDtypeStruct((bsz, s, qk), BF16),
                   jax.ShapeDtypeStruct((bsz, s, n_att - 2 * qk), BF16),
                   jax.ShapeDtypeStruct((bsz, c // LANES, nslab, s // nslab, LANES), F32),
                   jax.ShapeDtypeStruct((bsz, c // LANES, nslab, s // nslab, LANES), F32)]
                  + [jax.ShapeDtypeStruct(w.shape, BF16) for w in side_weights],
        scratch_shapes=[pltpu.VMEM((d, n), BF16)],
        compiler_params=_params(2),
        name="latproj",
    )(x, x, x, mod, g.reshape(1, d), w_all, b_all, *rope_tabs, conv_w, conv_b.reshape(1, 3 * c), *side_weights)
    return outs[:5], outs[5:]


def _rope_tables(seq_len):
    axis_dim = ATT_QK_DIM // 2
    n_freq = axis_dim // 2
    inv = (ROPE_BASE ** (-np.arange(n_freq, dtype=np.float32) / n_freq)).astype(np.float32)
    t = np.arange(seq_len)
    row, col = t // GRID_W, t % GRID_W
    jj = np.arange(128) % ATT_QK_DIM
    is_col = (jj // axis_dim) == 1
    second = ((jj % axis_dim) >= n_freq)[None, :]
    pos = np.where(is_col[None, :], col[:, None], row[:, None]).astype(np.float32)
    ang = (pos * inv[jj % n_freq][None, :]).astype(np.float64)
    cos, sin = np.cos(ang), np.sin(ang)
    as32 = lambda m: jnp.asarray(m, dtype=F32)
    return as32(cos), as32(np.where(second, sin, 0.0)), as32(np.where(second, 0.0, -sin))


def _attn_kernel(lam_ref, q_ref, kc_ref, vc_ref, kl_ref, vl_ref, g_ref, o_ref, *, tk, lam_init):
    q = q_ref[0]
    tq = q.shape[0]
    lane = lax.broadcasted_iota(jnp.int32, q.shape, 1)
    zero = jnp.zeros_like(q)
    q2 = jnp.concatenate([jnp.where(lane < ATT_QK_DIM, q, zero), jnp.where(lane >= ATT_QK_DIM, q, zero)], axis=0)

    def chunk(k, v, m, acc):
        s = lax.dot_general(q2, k, (((1,), (1,)), ((), ())), preferred_element_type=F32)
        rowmax = jnp.max(s, axis=1, keepdims=True)
        m_new = rowmax if m is None else jnp.maximum(m, rowmax)
        p = jnp.exp2(s - m_new).astype(BF16)
        pv = _dot(p, jnp.concatenate([v, jnp.ones_like(v)], axis=1))
        return m_new, (pv if acc is None else jnp.exp2(m - m_new) * acc + pv)

    m, acc = chunk(kc_ref[0], vc_ref[0], None, None)
    for j in range(kl_ref.shape[1] // tk):
        m, acc = chunk(kl_ref[0, j * tk:(j + 1) * tk, :], vl_ref[0, j * tk:(j + 1) * tk, :], m, acc)

    lamv = lam_ref[...]
    lam = (jnp.exp(jnp.sum(lamv[0:1] * lamv[1:2], axis=1, keepdims=True))
           - jnp.exp(jnp.sum(lamv[2:3] * lamv[3:4], axis=1, keepdims=True)) + lam_init)
    o_all = acc[:, :ATT_V_DIM] / acc[:, ATT_V_DIM:]
    o = o_all[:tq] - lam * o_all[tq:]
    o = _rms(o, SUBLN_EPS) * g_ref[...] * (1.0 - lam_init)
    o_ref[0] = o.astype(o_ref.dtype)


def _attn_call(q, kc, vc, kl, vl, lamv, g, lam_init, tq=1024, tk=256):
    bsz, l, width = q.shape
    heads = width // ATT_V_DIM
    lc = kc.shape[1]
    hd = ATT_V_DIM
    tq = min(tq, l)
    return pl.pallas_call(
        functools.partial(_attn_kernel, tk=tk, lam_init=lam_init),
        grid=(bsz, heads, l // tq),
        in_specs=[
            _const_spec(lamv.shape),
            pl.BlockSpec((1, tq, hd), lambda b, h, i: (b, i, h)),
            pl.BlockSpec((1, lc, hd), lambda b, h, i: (b, 0, h)),
            pl.BlockSpec((1, lc, hd), lambda b, h, i: (b, 0, h)),
            pl.BlockSpec((1, l, hd), lambda b, h, i: (b, 0, h)),
            pl.BlockSpec((1, l, hd), lambda b, h, i: (b, 0, h)),
            _const_spec((1, hd)),
        ],
        out_specs=pl.BlockSpec((1, tq, hd), lambda b, h, i: (b, i, h)),
        out_shape=jax.ShapeDtypeStruct((bsz, l, width), BF16),
        compiler_params=_params(3),
        name="diffattn",
    )(lamv, q, kc, vc, kl, vl, g.reshape(1, hd))


def _gather_minor(refs, n_rows):
    cols = []
    for j in range(SUBLANES):
        parts = [r[pl.ds(j, n_rows, stride=SUBLANES), :] for r in refs]
        cols.append(parts[0] if len(parts) == 1 else jnp.concatenate(parts, axis=0))
    return jnp.concatenate(cols, axis=1)


def _spectrum_store(val, spec_ref, jb):
    rows = val.shape[0]
    for j in range(SUBLANES):
        spec_ref[jb, pl.ds(j, rows, stride=SUBLANES), :] = val[:, j * LANES:(j + 1) * LANES]


def _spectrum_blocks(spec_ref, k0):
    start = k0 * SUBLANES if isinstance(k0, int) else pl.multiple_of(k0 * SUBLANES, SUBLANES * SUBLANES)
    x = spec_ref[:, pl.ds(start, SUBLANES * SUBLANES), :]
    return jnp.stack([x[:, i * SUBLANES:(i + 1) * SUBLANES, :].reshape(FFT_N2, LANES) for i in range(SUBLANES)], axis=0)


def _spectrum_put(spec_ref, k0, val):
    nb = FFT_N2 // SUBLANES
    v = val.reshape(SUBLANES, nb, SUBLANES, LANES)
    x = jnp.stack([v[:, jb].reshape(SUBLANES * SUBLANES, LANES) for jb in range(nb)], axis=0)
    spec_ref[:, pl.ds(pl.multiple_of(k0 * SUBLANES, SUBLANES * SUBLANES), SUBLANES * SUBLANES), :] = x


def _twiddle_block(twc_ref, twf_ref, kb):
    n2 = FFT_N2
    rows = pl.ds(pl.multiple_of(kb * n2, n2), n2)
    cr, ci = twc_ref[0, rows, :][None], twc_ref[1, rows, :][None]
    fr, fi = twf_ref[0].reshape(SUBLANES, n2, LANES), twf_ref[1].reshape(SUBLANES, n2, LANES)
    return cr * fr - ci * fi, cr * fi + ci * fr


def _filter_fft_kernel(z_ref, w1_ref, b1_ref, f1_ref, w2_ref, b2_ref, f2_ref, w3f_ref, w3b_ref, dl_ref,
                       fr_ref, g_ref, twc_ref, twf_ref, kr_ref, ki_ref, h2_ref, hs_ref, a_ref, *, tm):
    half = z_ref.shape[0]
    seq_len = 2 * half
    hid = w1_ref.shape[0]
    n1 = a_ref.shape[2] // SUBLANES
    n1h = hs_ref.shape[2] // SUBLANES
    n2 = FFT_N2
    nb = n2 // SUBLANES
    w = SUBLANES * LANES
    zeros = jnp.zeros((hid, hid), F32)

    @pl.when(pl.program_id(0) == 0)
    def _():
        blockdiag = lambda m: jnp.concatenate([jnp.concatenate([m, zeros], axis=1),
                                               jnp.concatenate([zeros, m], axis=1)], axis=0)
        twice = lambda ref: jnp.concatenate([ref[...], ref[...]], axis=1)
        w1, w2 = blockdiag(w1_ref[...]), blockdiag(w2_ref[...])
        b1, f1, b2, f2 = twice(b1_ref), twice(f1_ref), twice(b2_ref), twice(f2_ref)

        def features(i, _):
            r = pl.ds(pl.multiple_of(i * tm, tm), tm)
            h1 = jnp.sin(f1 * (_dot3f(z_ref[r, :], w1) + b1))
            h2_ref[r, :] = jnp.sin(f2 * (_dot3f(h1, w2) + b2))
            return 0

        lax.fori_loop(0, half // tm, features, 0)

    for upper in range(2):
        pad = lambda m: jnp.concatenate([jnp.zeros_like(m), m] if upper else [m, jnp.zeros_like(m)], axis=0)
        w3s = [pad(w3f_ref[...]), pad(w3b_ref[...])]

        def taps(i, _, upper=upper, w3s=w3s):
            h2 = h2_ref[pl.ds(pl.multiple_of(i * tm, tm), tm), :]
            t0 = upper * half + i * tm
            row = lax.broadcasted_iota(jnp.int32, (tm, LANES), 0) + t0
            decay = jnp.exp(-(row.astype(F32) * (1.0 / (seq_len - 1))) * dl_ref[...])
            for t in range(2):
                h = _dot3f(h2, w3s[t]) * decay
                if t == 1:
                    h = jnp.where(row == 0, 0.0, h)
                h4 = h.reshape(tm // n2, nb, SUBLANES, LANES)
                for jb in range(nb):
                    srow = pl.multiple_of(upper * (half // nb) + i * (tm // nb), tm // nb)
                    hs_ref[t, jb, pl.ds(srow, tm // nb), :] = h4[:, jb].reshape(tm // nb, LANES)
            return 0

        lax.fori_loop(0, half // tm, taps, 0)

    for jb in range(nb):
        z = jnp.concatenate([_gather_minor([hs_ref.at[t, jb]], n1h) for t in range(2)], axis=1)
        out = _dft_dot(fr_ref[...], z)
        for t in range(4):
            _spectrum_store(out[(t % 2) * n1:(t % 2 + 1) * n1, (t // 2) * w:(t // 2 + 1) * w], a_ref.at[t], jb)

    def body(kb, _):
        k0 = pl.multiple_of(kb * SUBLANES, SUBLANES)
        twr, twi = _twiddle_block(twc_ref, twf_ref, kb)
        a = [_spectrum_blocks(a_ref.at[t], k0) for t in range(4)]
        cols = []
        for t in range(2):
            ar, ai = a[2 * t], a[2 * t + 1]
            sr, si = ar * twr - ai * twi, ar * twi + ai * twr
            cols += [jnp.concatenate([sr[i], si[i]], axis=0) for i in range(SUBLANES)]
        x = _dft_dot(g_ref[...], jnp.concatenate(cols, axis=1))
        for i in range(SUBLANES):
            xf = x[:, i * LANES:(i + 1) * LANES]
            xb = x[:, (SUBLANES + i) * LANES:(SUBLANES + i + 1) * LANES]
            rows = pl.ds(pl.multiple_of((k0 + i) * n2, n2), n2)
            kr_ref[rows, :] = xf[:n2] + xb[:n2]
            ki_ref[rows, :] = xf[n2:] - xb[n2:]
        return 0

    lax.fori_loop(0, n1 // SUBLANES, body, 0, unroll=4)


def _filter_fft_call(dft, z, w1, b1, f1, w2, b2, f2, w3, absdelta, tm=512):
    hid = z.shape[1] // 2
    seq_len = 2 * z.shape[0]
    c = w3.shape[1] // 2
    n, n1, n1h = dft["n"], dft["n1"], dft["n1h"]
    nb = FFT_N2 // SUBLANES
    ncb = c // LANES
    fr, g = dft["f_real"].astype(BF16), dft["g_fwd"].astype(BF16)
    twc, twf = dft["tw_coarse"], dft["tw_fine"]
    vec = _const_spec((1, hid))
    w3_blk = lambda off: pl.BlockSpec((hid, LANES), lambda i, off=off: (0, i + off))
    out = pl.BlockSpec((None, n, LANES), lambda i: (i, 0, 0))
    return pl.pallas_call(
        functools.partial(_filter_fft_kernel, tm=tm),
        grid=(ncb,),
        in_specs=[_const_spec(z.shape), _const_spec((hid, hid)), vec, vec, _const_spec((hid, hid)), vec, vec,
                  w3_blk(0), w3_blk(ncb), pl.BlockSpec((1, LANES), lambda i: (0, i)),
                  _const_spec(fr.shape), _const_spec(g.shape), _const_spec(twc.shape), _const_spec(twf.shape)],
        out_specs=[out, out],
        out_shape=[jax.ShapeDtypeStruct((ncb, n, LANES), F32)] * 2,
        scratch_shapes=[pltpu.VMEM(z.shape, F32), pltpu.VMEM((2, nb, n1h * SUBLANES, LANES), F32),
                        pltpu.VMEM((4, nb, n1 * SUBLANES, LANES), F32)],
        compiler_params=_params(1),
        name="hyfilter_fft",
    )(z, w1, b1.reshape(1, hid), f1.reshape(1, hid), w2, b2.reshape(1, hid), f2.reshape(1, hid), w3, w3, absdelta,
      fr, g, twc, twf)


def _hyena_conv_kernel(fc_ref, g_ref, gc_ref, e_ref, twc_ref, twf_ref, kr_ref, ki_ref, u_ref, x0_ref, bias_ref, o_ref,
                       a_ref, *, inv_n):
    n1 = a_ref.shape[2] // SUBLANES
    n1h = u_ref.shape[2] // SUBLANES
    n2 = FFT_N2
    nb = n2 // SUBLANES

    for jb in range(nb):
        z = _gather_minor([u_ref.at[0, jb], u_ref.at[1, jb]], n1h)
        out = _dft_dot(fc_ref[...], z)
        for c in range(2):
            _spectrum_store(out[c * n1:(c + 1) * n1], a_ref.at[c], jb)

    def body(kb, _):
        k0 = pl.multiple_of(kb * SUBLANES, SUBLANES)
        rows = pl.ds(pl.multiple_of(kb * (SUBLANES * n2), SUBLANES * n2), SUBLANES * n2)
        blocks = lambda ref: ref[rows, :].reshape(SUBLANES, n2, LANES)
        (twr, twi), kr, ki = _twiddle_block(twc_ref, twf_ref, kb), blocks(kr_ref), blocks(ki_ref)
        ar, ai = _spectrum_blocks(a_ref.at[0], k0), _spectrum_blocks(a_ref.at[1], k0)
        sr, si = ar * twr - ai * twi, ar * twi + ai * twr
        lanes = lambda re, im: jnp.concatenate([jnp.concatenate([re[i], im[i]], axis=0) for i in range(SUBLANES)], axis=1)
        x = _dft_dot(g_ref[...], lanes(sr, si))
        unl = lambda v, lo: jnp.stack([v[lo:lo + n2, i * LANES:(i + 1) * LANES] for i in range(SUBLANES)], axis=0)
        xr, xi = unl(x, 0), unl(x, n2)
        t = _dft_dot(gc_ref[...], lanes(xr * kr - xi * ki, xr * ki + xi * kr))
        tr, ti = unl(t, 0), unl(t, n2)
        _spectrum_put(a_ref.at[0], k0, tr * twr + ti * twi)
        _spectrum_put(a_ref.at[1], k0, ti * twr - tr * twi)
        return 0

    lax.fori_loop(0, n1 // SUBLANES, body, 0, unroll=4)

    for jb in range(nb):
        t = _gather_minor([a_ref.at[0, jb], a_ref.at[1, jb]], n1)
        y = _dft_dot(e_ref[...], t) * inv_n
        for c in range(2):
            _spectrum_store(y[c * n1h:(c + 1) * n1h], o_ref.at[c], jb)
    o_ref[...] = x0_ref[...] * (o_ref[...] + u_ref[...] * bias_ref[...])


def _dft_blocks(seq_len):
    n = 2 * seq_len
    n2 = FFT_N2
    n1 = n // n2
    n1h = n1 // 2
    k1 = np.arange(n1)[:, None].astype(np.float64)
    a = 2.0 * np.pi * k1 * np.arange(n1h)[None, :] / n1
    fr, fi = np.cos(a), -np.sin(a)
    f_cplx = np.block([[fr, -fi], [fi, fr]])
    f_real = np.concatenate([fr, fi], axis=0)
    e_cplx = np.block([[fr.T, fi.T], [-fi.T, fr.T]])
    b = 2.0 * np.pi * np.arange(n2)[:, None] * np.arange(n2)[None, :] / n2
    gr, gi = np.cos(b), -np.sin(b)
    g_fwd = np.block([[gr, -gi], [gi, gr]])
    g_inv = np.block([[gr, gi], [-gi, gr]])
    def table(k1s):
        ang = (2.0 * np.pi / n) * (k1s[:, None] * np.arange(n2)[None, :]).reshape(-1, 1)
        return np.broadcast_to(np.stack([np.cos(ang), -np.sin(ang)]), (2, k1s.size * n2, LANES))
    as32 = lambda m: jnp.asarray(np.ascontiguousarray(m), dtype=F32)
    return dict(n=n, n1=n1, n1h=n1h, f_cplx=as32(f_cplx), f_real=as32(f_real), e_cplx=as32(e_cplx),
                g_fwd=as32(g_fwd), g_inv=as32(g_inv), tw_coarse=as32(table(np.arange(0, n1, SUBLANES))),
                tw_fine=as32(table(np.arange(SUBLANES))))


def _hyena_call(u, x0, filt, hy_bias):
    bsz, ncb, nslab, srows, _ = u.shape
    seq_len = nslab * srows
    c = ncb * LANES
    n2 = FFT_N2
    dft = _dft_blocks(seq_len)
    n, n1, n1h = dft["n"], dft["n1"], dft["n1h"]
    pairs = bsz // 2

    w1, b1, f1, w2, b2, f2, w3 = filt
    hid = w2.shape[0]
    bands = (HY_POS_EMB - 1) // 2
    t = np.linspace(0.0, 1.0, seq_len)[:, None]
    freqs = np.linspace(1e-4, bands - 1, bands)
    ang = (2.0 * math.pi / seq_len) * np.arange(seq_len)[:, None] * freqs[None, :]
    z = np.concatenate([t, np.cos(ang), -np.sin(ang), np.zeros((seq_len, hid - HY_POS_EMB))], axis=-1)
    z = jnp.asarray(np.concatenate([z[:seq_len // 2], z[seq_len // 2:]], axis=1), dtype=F32)
    w1p = jnp.concatenate([w1, jnp.zeros((hid - HY_POS_EMB, hid), F32)], axis=0)
    max_decay = math.log(HY_DECAY_TARGET) / HY_FAST_DECAY_PCT
    min_decay = math.log(HY_DECAY_TARGET) / HY_SLOW_DECAY_PCT
    absdelta = jnp.asarray(np.abs(np.linspace(min_decay, max_decay, c))[None, :], dtype=F32)
    nb = n2 // SUBLANES
    kr, ki = _filter_fft_call(dft, z, w1p, b1, f1, w2, b2, f2, w3, absdelta)

    view = lambda a: a.reshape(pairs, 2, ncb, nb, n1h * SUBLANES, LANES)
    pair_spec = pl.BlockSpec((None, 2, None, nb, n1h * SUBLANES, LANES), lambda cb, p: (p, 0, cb, 0, 0, 0))
    k_spec = pl.BlockSpec((None, n, LANES), lambda cb, p: (cb, 0, 0))
    twc, twf = dft["tw_coarse"], dft["tw_fine"]
    fc, g, gc, e = (dft[k].astype(BF16) for k in ("f_cplx", "g_fwd", "g_inv", "e_cplx"))
    hy = pl.pallas_call(
        functools.partial(_hyena_conv_kernel, inv_n=1.0 / n),
        grid=(ncb, pairs),
        in_specs=[_const_spec(fc.shape), _const_spec(g.shape), _const_spec(gc.shape), _const_spec(e.shape),
                  _const_spec(twc.shape), _const_spec(twf.shape), k_spec, k_spec, pair_spec, pair_spec,
                  pl.BlockSpec((1, LANES), lambda cb, p: (0, cb))],
        out_specs=pair_spec,
        out_shape=jax.ShapeDtypeStruct((pairs, 2, ncb, nb, n1h * SUBLANES, LANES), F32),
        scratch_shapes=[pltpu.VMEM((2, nb, n1 * SUBLANES, LANES), F32)],
        compiler_params=_params(2),
        name="hyconv",
    )(fc, g, gc, e, twc, twf, kr, ki, view(u), view(x0), hy_bias.reshape(1, c))
    return hy.reshape(bsz, ncb, nb, n1h * SUBLANES, LANES)


def _stage_weights_bf16(pairs):
    for src, dst in pairs:
        n_rows, n_cols = src.shape
        rows = min(n_rows, 1 << ((WEIGHT_STAGE_BYTES // (4 * n_cols)).bit_length() - 1))
        n_chunks = n_rows // rows
        assert rows % (2 * SUBLANES) == 0 and n_chunks * rows == n_rows

        def run(stage, sem, src=src, dst=dst, rows=rows, n_chunks=n_chunks):
            copy = lambda k: pltpu.make_async_copy(src.at[pl.ds(k * rows, rows), :], stage.at[k % 2], sem.at[k % 2])
            copy(0).start()
            for k in range(n_chunks):
                if k + 1 < n_chunks:
                    copy(k + 1).start()
                copy(k).wait()
                dst[pl.ds(k * rows, rows), :] = stage[k % 2].astype(BF16)

        pl.run_scoped(run, pltpu.VMEM((2, rows, n_cols), F32), pltpu.SemaphoreType.DMA((2,)))


def _mlp_tail(h, g2, shift, scale, gate, w1_ref, w2_ref, fc):
    a = (_rms(h, NORM_EPS) * g2) * (1.0 + scale) + shift
    a = a.astype(BF16)
    acc = None
    for c0 in range(0, w1_ref.shape[1], fc):
        hid = jnp.maximum(_dot(a, w1_ref[:, c0:c0 + fc]), 0.0)
        part = _dot((hid * hid).astype(BF16), w2_ref[c0:c0 + fc, :])
        acc = part if acc is None else acc + part
    return h + gate * acc


def _mixmlp_kernel(x_ref, att_ref, hy_ref, mod_ref, wo_s, bo_ref, g2_ref, w1_s, w2_s, o_ref, *, fc):
    blk, d = x_ref.shape[1:]
    tm = blk // ROW_SUBTILES
    wa = att_ref.shape[2]
    mod = mod_ref[0, pl.ds(pl.program_id(0), 1), :]
    for r in range(ROW_SUBTILES):
        rows = slice(r * tm, (r + 1) * tm)
        hy = jnp.concatenate([_spectrum_blocks(hy_ref.at[0, cb], r * tm // FFT_N2).reshape(tm, LANES)
                              for cb in range(hy_ref.shape[1])], axis=1).astype(BF16)
        y = _dot(att_ref[0, rows, :], wo_s[:wa, :]) + _dot(hy, wo_s[wa:, :]) + bo_ref[...]
        h = x_ref[0, rows, :] + mod[:, 2 * d:3 * d] * y
        o_ref[0, rows, :] = _mlp_tail(h, g2_ref[...], mod[:, 3 * d:4 * d], mod[:, 4 * d:5 * d], mod[:, 5 * d:6 * d],
                                      w1_s, w2_s, fc)


def _mixmlp_call(x, att, hy, mod, w_out, b_out, g2, w1, w2, layer, tm=ROW_BLOCK, fc=1024):
    bsz, l, d = x.shape
    wa = att.shape[2]
    row = lambda w: pl.BlockSpec((1, tm, w), lambda b, i: (b, i, 0))
    return pl.pallas_call(
        functools.partial(_mixmlp_kernel, fc=fc),
        grid=(bsz, l // tm),
        in_specs=[row(d), row(wa),
                  pl.BlockSpec((1, hy.shape[1], hy.shape[2], tm // hy.shape[2], LANES), lambda b, i: (b, 0, 0, i, 0)),
                  _mod_spec(mod, layer),
                  _layer_spec(w_out, layer // 2), _const_spec((1, d)), _const_spec((1, d)),
                  _layer_spec(w1, layer), _layer_spec(w2, layer)],
        out_specs=row(d),
        out_shape=jax.ShapeDtypeStruct((bsz, l, d), F32),
        compiler_params=_params(2),
        name="mixmlp",
    )(x, att, hy, mod, w_out, b_out.reshape(1, d), g2.reshape(1, d), w1, w2)


def _poolmlp_kernel(h_ref, hp_ref, hn_ref, mod_ref, g1_ref, pw_s, ps_ref, g2_ref, w1_s, w2_s, gf_ref, o_ref,
                    *, fc, seq_len):
    blk, d = h_ref.shape[1:]
    tm = blk // ROW_SUBTILES
    halo = POOL_HALO
    rows = tm + 2 * halo
    mod = mod_ref[0, pl.ds(pl.program_id(0), 1), :]
    gd = d // len(POOL_WINDOWS)
    for r in range(ROW_SUBTILES):
        r0 = r * tm
        h = h_ref[0, r0:r0 + tm, :]
        above = hp_ref[0] if r == 0 else h_ref[0, r0 - halo:r0, :]
        below = hn_ref[0] if r == ROW_SUBTILES - 1 else h_ref[0, r0 + tm:r0 + tm + halo, :]
        hx = jnp.concatenate([above, h, below], axis=0)
        a = (_rms(hx, NORM_EPS) * g1_ref[...]) * (1.0 + mod[:, d:2 * d]) + mod[:, 0:d]
        t = lax.broadcasted_iota(jnp.int32, (rows, 1), 0) + (pl.program_id(1) * blk + r0 - halo)
        a = jnp.where((t >= 0) & (t < seq_len), a, 0.0)
        tc = t[halo:halo + tm]
        ys = []
        for g, win in enumerate(POOL_WINDOWS):
            ag = a[:, g * gd:(g + 1) * gd]
            f, m = ag, 1
            while 2 * m <= win // 2:
                f = f + pltpu.roll(f, rows - m, 0)
                m *= 2
            s = pltpu.roll(f, win // 2, 0) + f
            cnt = (jnp.minimum(tc + (win - win // 2), seq_len) - jnp.maximum(tc - win // 2, 0)).astype(F32)
            dlt = s[halo:halo + tm] / cnt - ag[halo:halo + tm]
            ys.append(_dot(dlt.astype(BF16), pw_s[g * gd:(g + 1) * gd, :]))
        y = jnp.concatenate(ys, axis=1) * ps_ref[...]
        h1 = h + mod[:, 2 * d:3 * d] * y
        h2 = _mlp_tail(h1, g2_ref[...], mod[:, 3 * d:4 * d], mod[:, 4 * d:5 * d], mod[:, 5 * d:6 * d], w1_s, w2_s, fc)
        o_ref[0, r0:r0 + tm, :] = _rms(h2, NORM_EPS) * gf_ref[...]


def _poolmlp_call(h, mod, g1, pool_w, pool_scale, g2, w1, w2, gf, layer, tm=ROW_BLOCK, fc=1024):
    bsz, l, d = h.shape
    halo = POOL_HALO
    nb = tm // halo
    last = l // halo - 1
    row = pl.BlockSpec((1, tm, d), lambda b, i: (b, i, 0))
    return pl.pallas_call(
        functools.partial(_poolmlp_kernel, fc=fc, seq_len=l),
        grid=(bsz, l // tm),
        in_specs=[row,
                  pl.BlockSpec((1, halo, d), lambda b, i: (b, jnp.maximum(i * nb - 1, 0), 0)),
                  pl.BlockSpec((1, halo, d), lambda b, i: (b, jnp.minimum((i + 1) * nb, last), 0)),
                  _mod_spec(mod, layer),
                  _const_spec((1, d)), _layer_spec(pool_w, layer // 2), _const_spec((1, d)), _const_spec((1, d)),
                  _layer_spec(w1, layer), _layer_spec(w2, layer), _const_spec((1, d))],
        out_specs=row,
        out_shape=jax.ShapeDtypeStruct((bsz, l, d), F32),
        compiler_params=_params(2),
        name="poolmlp",
    )(h, h, h, mod, g1.reshape(1, d), pool_w, pool_scale.reshape(1, d), g2.reshape(1, d), w1, w2, gf.reshape(1, d))


def kernel(x, c, ctx, c_ctx, ada_w, ada_b, norm1_g, norm2_g, mix_w_in, mix_b_in, mix_w_out, mix_b_out, lam_q1, lam_k1, lam_q2, lam_k2, subln_g, hy_conv_w, hy_conv_b, hy_pos_w1, hy_pos_b1, hy_freq1, hy_pos_w2, hy_pos_b2, hy_freq2, hy_pos_w3, hy_bias, pool_w, pool_scale, mlp_w1, mlp_w2, final_g):
    bsz, seq_len, d = x.shape
    depth = ada_w.shape[0]
    assert depth == 2 and bsz % 2 == 0 and bsz < MOD_ROWS and seq_len % GRID_W == 0
    att_w = ATT_HEADS * ATT_V_DIM
    q_cols = k_cols = ATT_HEADS * 2 * ATT_QK_DIM
    kv_start, hy_start = q_cols, q_cols + k_cols + att_w
    in_cols = mix_w_in.shape[2]

    cv = jnp.concatenate([c, c_ctx[None, :], jnp.zeros((MOD_ROWS - bsz - 1, d), F32)], axis=0)
    mod = _mod_call(cv, ada_w, ada_b)

    lam_init = 0.8 - 0.6 * math.exp(-0.3 * 0)
    assert k_cols == q_cols and in_cols - hy_start == 3 * (d - att_w)
    pool_w3 = pool_w.reshape(pool_w.shape[0], d, d // len(POOL_WINDOWS))
    (q, k, v, u, x0), (w_out_b, w1_b, w2_b, pool_wb) = _latproj_call(
        x, mod, norm1_g[0], mix_w_in, mix_b_in, _rope_tables(seq_len),
        hy_conv_w[0], hy_conv_b[0], n_att=hy_start, qk=q_cols, q_scale=ATT_QK_DIM ** -0.5 * math.log2(math.e),
        side_weights=[mix_w_out, mlp_w1, mlp_w2, pool_w3])
    kc, vc = _ctxproj_call(ctx, mod, bsz, norm1_g[0], mix_w_in, mix_b_in, kv_start, k_cols, hy_start - kv_start)
    lamv = jnp.stack([lam_q1[0], lam_k1[0], lam_q2[0], lam_k2[0]], axis=0)
    att = _attn_call(q, kc, vc, k, v, lamv, subln_g[0], lam_init)
    filt = (hy_pos_w1[0], hy_pos_b1[0], hy_freq1[0], hy_pos_w2[0], hy_pos_b2[0], hy_freq2[0], hy_pos_w3[0])
    hy = _hyena_call(u, x0, filt, hy_bias[0])
    h = _mixmlp_call(x, att, hy, mod, w_out_b, mix_b_out[0], norm2_g[0], w1_b, w2_b, layer=0)

    return _poolmlp_call(h, mod, norm1_g[1], pool_wb, pool_scale[0], norm2_g[1], w1_b, w2_b, final_g, layer=1)
```

```python
import functools
import math

import numpy as np
import jax
import jax.numpy as jnp
from jax import lax
from jax.experimental import pallas as pl
from jax.experimental.pallas import tpu as pltpu

F32 = jnp.float32
BF16 = jnp.bfloat16

GRID_W = 64
ATT_HEADS = 4
ATT_V_DIM = 128
ATT_QK_DIM = 64
ROPE_BASE = 10000.0
HY_POS_EMB = 33
HY_DECAY_TARGET = 1e-2
HY_FAST_DECAY_PCT = 0.3
HY_SLOW_DECAY_PCT = 1.5
POOL_WINDOWS = (2, 4, 8, 16)
NORM_EPS = 1e-6
SUBLN_EPS = 1e-5
SUBLANES = 8
LANES = 128
FFT_N2 = 64
POOL_HALO = 8
WEIGHT_STAGE_BYTES = 2 * 1024 * 1024
ROW_BLOCK = 1024
ROW_SUBTILES = 2
MOD_ROWS = 8

VMEM_LIMIT_BYTES = 56 * 1024 * 1024


def _params(n_grid_dims):
    return pltpu.CompilerParams(
        dimension_semantics=("arbitrary",) * n_grid_dims,
        vmem_limit_bytes=VMEM_LIMIT_BYTES,
    )


def _const_spec(shape):
    nd = len(shape)
    return pl.BlockSpec(shape, lambda *_: (0,) * nd, pipeline_mode=pl.Buffered(1))


def _split_bf16(a):
    hi = a.astype(BF16)
    lo = (a - hi.astype(F32)).astype(BF16)
    return hi, lo


def _dot(a, b):
    return jnp.dot(a, b, preferred_element_type=F32)


def _dot3(a_hi, a_lo, b_hi, b_lo):
    return _dot(a_hi, b_hi) + (_dot(a_lo, b_hi) + _dot(a_hi, b_lo))


def _dot3f(a, b):
    a_hi, a_lo = _split_bf16(a)
    b_hi, b_lo = _split_bf16(b)
    return _dot3(a_hi, a_lo, b_hi, b_lo)


def _dft_dot(a_bf16, b):
    return _dot(a_bf16, b.astype(BF16))


def _rms(x, eps):
    return x * lax.rsqrt(jnp.mean(x * x, axis=-1, keepdims=True) + eps)


def _mod_kernel(cv_ref, w_ref, b_ref, o_ref):
    cv = cv_ref[...]
    s_hi, s_lo = _split_bf16(cv / (1.0 + jnp.exp(-cv)))
    w_hi, w_lo = _split_bf16(w_ref[0])
    r = _dot(jnp.concatenate([s_hi, s_lo], axis=0), w_hi)
    o_ref[0] = r[:MOD_ROWS] + (r[MOD_ROWS:] + _dot(s_hi, w_lo)) + b_ref[0]


def _mod_call(cv, ada_w, ada_b, tn=1536):
    depth, d, n = ada_w.shape
    return pl.pallas_call(
        _mod_kernel,
        grid=(depth, n // tn),
        in_specs=[
            pl.BlockSpec((MOD_ROWS, d), lambda i, j: (0, 0)),
            pl.BlockSpec((1, d, tn), lambda i, j: (i, 0, j)),
            pl.BlockSpec((1, 1, tn), lambda i, j: (i, 0, j)),
        ],
        out_specs=pl.BlockSpec((1, MOD_ROWS, tn), lambda i, j: (i, 0, j)),
        out_shape=jax.ShapeDtypeStruct((depth, MOD_ROWS, n), F32),
        compiler_params=_params(2),
        name="mod",
    )(cv, ada_w, ada_b.reshape(depth, 1, n))


def _mod_spec(mod, layer):
    return pl.BlockSpec((1,) + mod.shape[1:], lambda *_: (layer, 0, 0), pipeline_mode=pl.Buffered(1))


def _layer_spec(stack, layer):
    return pl.BlockSpec((None,) + stack.shape[1:], lambda *_: (layer, 0, 0), pipeline_mode=pl.Buffered(1))


def _ctxproj_kernel(x_ref, mod_ref, g_ref, w_hbm, b_ref, k_ref, v_ref, w_s, *, ctx_row, col0):
    d = x_ref.shape[2]
    nkv = w_s.shape[1]

    @pl.when(pl.program_id(0) == 0)
    def _():
        _stage_weights_bf16([(w_hbm.at[0, :, pl.ds(col0, nkv)], w_s)])

    m = mod_ref[0, ctx_row:ctx_row + 1, :]
    a = _rms(x_ref[0], NORM_EPS) * g_ref[...]
    a = a * (1.0 + m[:, d:2 * d]) + m[:, 0:d]
    p = _dot(a.astype(BF16), w_s[...]) + b_ref[0:1, col0:col0 + nkv]
    nk = k_ref.shape[2]
    k_ref[0] = p[:, :nk].astype(k_ref.dtype)
    v_ref[0] = p[:, nk:].astype(v_ref.dtype)


def _ctxproj_call(x, mod, ctx_row, g, w_all, b_all, col0, nk, nkv):
    bsz, s, d = x.shape
    return pl.pallas_call(
        functools.partial(_ctxproj_kernel, ctx_row=ctx_row, col0=col0),
        grid=(bsz,),
        in_specs=[pl.BlockSpec((1, s, d), lambda bi: (bi, 0, 0)), _mod_spec(mod, 0), _const_spec((1, d)),
                  pl.BlockSpec(memory_space=pl.ANY), _const_spec(b_all.shape)],
        out_specs=[pl.BlockSpec((1, s, nk), lambda bi: (bi, 0, 0)), pl.BlockSpec((1, s, nkv - nk), lambda bi: (bi, 0, 0))],
        out_shape=[jax.ShapeDtypeStruct((bsz, s, nk), BF16), jax.ShapeDtypeStruct((bsz, s, nkv - nk), BF16)],
        scratch_shapes=[pltpu.VMEM((d, nkv), BF16)],
        compiler_params=_params(1),
        name="ctxproj",
    )(x, mod, g.reshape(1, d), w_all, b_all)


def _side_cast_specs(side_weights, n_steps, flat):
    specs = []
    for w in side_weights:
        rows = w.shape[1] // n_steps
        assert rows * n_steps == w.shape[1] and rows % (2 * SUBLANES) == 0
        specs.append(pl.BlockSpec((w.shape[0], rows, w.shape[2]), lambda *ids: (0, flat(*ids), 0)))
    return specs


def _latproj_kernel(*refs, n_att, q_scale, seq_len, n_side):
    x_ref, xp_ref, xn_ref, mod_ref, g_ref, w_hbm, b_ref, cos_ref, sa_ref, sb_ref, cw_ref, cb_ref = refs[:12]
    q_ref, k_ref, v_ref, u_ref, x0_ref = refs[12 + n_side:17 + n_side]
    w_ref = refs[-1]
    for src_ref, dst_ref in zip(refs[12:12 + n_side], refs[17 + n_side:17 + 2 * n_side]):
        dst_ref[...] = src_ref[...].astype(BF16)

    @pl.when((pl.program_id(0) == 0) & (pl.program_id(1) == 0))
    def _():
        _stage_weights_bf16([(w_hbm.at[0], w_ref)])

    tm = x_ref.shape[1] // ROW_SUBTILES
    d = x_ref.shape[2]
    mod = mod_ref[0, pl.ds(pl.program_id(0), 1), :]
    halo = xp_ref.shape[1]
    rows = tm + 2 * halo
    ncb = u_ref.shape[1]
    c = ncb * LANES
    qk = q_ref.shape[2]
    for sub in range(ROW_SUBTILES):
        r0 = sub * tm
        above = xp_ref[0] if sub == 0 else x_ref[0, r0 - halo:r0, :]
        below = xn_ref[0] if sub == ROW_SUBTILES - 1 else x_ref[0, r0 + tm:r0 + tm + halo, :]
        xx = jnp.concatenate([above, x_ref[0, r0:r0 + tm, :], below], axis=0)
        a = _rms(xx, NORM_EPS) * g_ref[...]
        a = a * (1.0 + mod[:, d:2 * d]) + mod[:, 0:d]

        ph = _dot(a.astype(BF16), w_ref[:, n_att:]) + b_ref[0:1, n_att:]
        t = lax.broadcasted_iota(jnp.int32, (rows, 1), 0) + (pl.program_id(1) * (tm * ROW_SUBTILES) + r0 - halo)
        ph = jnp.where((t >= 0) & (t < seq_len), ph, 0.0)

        def conv(s):
            blk = ph[:, s * c:(s + 1) * c]
            prev = pltpu.roll(blk, 1, 0)[halo:halo + tm]
            nxt = pltpu.roll(blk, rows - 1, 0)[halo:halo + tm]
            cw = cw_ref[:, s * c:(s + 1) * c]
            return prev * cw[0:1] + blk[halo:halo + tm] * cw[1:2] + nxt * cw[2:3] + cb_ref[:, s * c:(s + 1) * c]

        x0 = conv(0)
        u = conv(2) * conv(1)
        nb = FFT_N2 // SUBLANES
        srows = slice(r0 // SUBLANES, (r0 + tm) // SUBLANES)
        for val, o_ref in ((x0, x0_ref), (u, u_ref)):
            for cb in range(ncb):
                v4 = val[:, cb * LANES:(cb + 1) * LANES].reshape(tm // FFT_N2, nb, SUBLANES, LANES)
                for jb in range(nb):
                    o_ref[0, cb, jb, srows, :] = v4[:, jb].reshape(tm // SUBLANES, LANES)

        p = _dot(a[halo:halo + tm].astype(BF16), w_ref[:, :n_att]) + b_ref[0:1, :n_att]
        cos, sa, sb = (ref[r0:r0 + tm, :] for ref in (cos_ref, sa_ref, sb_ref))
        for o_ref, c_lo, scale in ((q_ref, 0, q_scale), (k_ref, qk, 1.0)):
            for c0 in range(0, qk, LANES):
                blk = p[:, c_lo + c0:c_lo + c0 + LANES]
                r = blk * cos + pltpu.roll(blk, 16, 1) * sa + pltpu.roll(blk, LANES - 16, 1) * sb
                o_ref[0, r0:r0 + tm, c0:c0 + LANES] = (r * scale).astype(o_ref.dtype)
        v_ref[0, r0:r0 + tm, :] = p[:, 2 * qk:].astype(v_ref.dtype)


def _latproj_call(x, mod, g, w_all, b_all, rope_tabs, conv_w, conv_b, n_att, qk, q_scale, side_weights,
                  tm=ROW_BLOCK):
    bsz, s, d = x.shape
    nt = s // tm
    side_specs = _side_cast_specs(side_weights, bsz * nt, lambda bi, i: bi * nt + i)
    n = w_all.shape[2]
    c = (n - n_att) // 3
    halo = SUBLANES
    nb = tm // halo
    last = s // halo - 1
    tab = pl.BlockSpec((tm, LANES), lambda bi, i: (i, 0))
    row = lambda w: pl.BlockSpec((1, tm, w), lambda bi, i: (bi, i, 0))
    nslab = FFT_N2 // SUBLANES
    cblk = pl.BlockSpec((1, c // LANES, nslab, tm // nslab, LANES), lambda bi, i: (bi, 0, 0, i, 0))
    outs = pl.pallas_call(
        functools.partial(_latproj_kernel, n_att=n_att, q_scale=q_scale, seq_len=s, n_side=len(side_weights)),
        grid=(bsz, nt),
        in_specs=[row(d),
                  pl.BlockSpec((1, halo, d), lambda bi, i: (bi, jnp.maximum(i * nb - 1, 0), 0)),
                  pl.BlockSpec((1, halo, d), lambda bi, i: (bi, jnp.minimum((i + 1) * nb, last), 0)),
                  _mod_spec(mod, 0), _const_spec((1, d)), pl.BlockSpec(memory_space=pl.ANY), _const_spec(b_all.shape),
                  tab, tab, tab,
                  _const_spec(conv_w.shape), _const_spec((1, 3 * c))] + side_specs,
        out_specs=[row(qk), row(qk), row(n_att - 2 * qk), cblk, cblk] + side_specs,
        out_shape=[jax.ShapeDtypeStruct((bsz, s, qk), BF16), jax.ShapeDtypeStruct((bsz, s, qk), BF16),
                   jax.ShapeDtypeStruct((bsz, s, n_att - 2 * qk), BF16),
                   jax.ShapeDtypeStruct((bsz, c // LANES, nslab, s // nslab, LANES), F32),
                   jax.ShapeDtypeStruct((bsz, c // LANES, nslab, s // nslab, LANES), F32)]
                  + [jax.ShapeDtypeStruct(w.shape, BF16) for w in side_weights],
        scratch_shapes=[pltpu.VMEM((d, n), BF16)],
        compiler_params=_params(2),
        name="latproj",
    )(x, x, x, mod, g.reshape(1, d), w_all, b_all, *rope_tabs, conv_w, conv_b.reshape(1, 3 * c), *side_weights)
    return outs[:5], outs[5:]


def _rope_tables(seq_len):
    axis_dim = ATT_QK_DIM // 2
    n_freq = axis_dim // 2
    inv = (ROPE_BASE ** (-np.arange(n_freq, dtype=np.float32) / n_freq)).astype(np.float32)
    t = np.arange(seq_len)
    row, col = t // GRID_W, t % GRID_W
    jj = np.arange(128) % ATT_QK_DIM
    is_col = (jj // axis_dim) == 1
    second = ((jj % axis_dim) >= n_freq)[None, :]
    pos = np.where(is_col[None, :], col[:, None], row[:, None]).astype(np.float32)
    ang = (pos * inv[jj % n_freq][None, :]).astype(np.float64)
    cos, sin = np.cos(ang), np.sin(ang)
    as32 = lambda m: jnp.asarray(m, dtype=F32)
    return as32(cos), as32(np.where(second, sin, 0.0)), as32(np.where(second, 0.0, -sin))


def _attn_kernel(lam_ref, q_ref, kc_ref, vc_ref, kl_ref, vl_ref, g_ref, o_ref, *, tk, lam_init):
    q = q_ref[0]
    tq = q.shape[0]
    lane = lax.broadcasted_iota(jnp.int32, q.shape, 1)
    zero = jnp.zeros_like(q)
    q2 = jnp.concatenate([jnp.where(lane < ATT_QK_DIM, q, zero), jnp.where(lane >= ATT_QK_DIM, q, zero)], axis=0)

    def chunk(k, v, m, acc):
        s = lax.dot_general(q2, k, (((1,), (1,)), ((), ())), preferred_element_type=F32)
        rowmax = jnp.max(s, axis=1, keepdims=True)
        m_new = rowmax if m is None else jnp.maximum(m, rowmax)
        p = jnp.exp2(s - m_new).astype(BF16)
        pv = _dot(p, jnp.concatenate([v, jnp.ones_like(v)], axis=1))
        return m_new, (pv if acc is None else jnp.exp2(m - m_new) * acc + pv)

    m, acc = chunk(kc_ref[0], vc_ref[0], None, None)
    for j in range(kl_ref.shape[1] // tk):
        m, acc = chunk(kl_ref[0, j * tk:(j + 1) * tk, :], vl_ref[0, j * tk:(j + 1) * tk, :], m, acc)

    lamv = lam_ref[...]
    lam = (jnp.exp(jnp.sum(lamv[0:1] * lamv[1:2], axis=1, keepdims=True))
           - jnp.exp(jnp.sum(lamv[2:3] * lamv[3:4], axis=1, keepdims=True)) + lam_init)
    o_all = acc[:, :ATT_V_DIM] / acc[:, ATT_V_DIM:]
    o = o_all[:tq] - lam * o_all[tq:]
    o = _rms(o, SUBLN_EPS) * g_ref[...] * (1.0 - lam_init)
    o_ref[0] = o.astype(o_ref.dtype)


def _attn_call(q, kc, vc, kl, vl, lamv, g, lam_init, tq=1024, tk=256):
    bsz, l, width = q.shape
    heads = width // ATT_V_DIM
    lc = kc.shape[1]
    hd = ATT_V_DIM
    tq = min(tq, l)
    return pl.pallas_call(
        functools.partial(_attn_kernel, tk=tk, lam_init=lam_init),
        grid=(bsz, heads, l // tq),
        in_specs=[
            _const_spec(lamv.shape),
            pl.BlockSpec((1, tq, hd), lambda b, h, i: (b, i, h)),
            pl.BlockSpec((1, lc, hd), lambda b, h, i: (b, 0, h)),
            pl.BlockSpec((1, lc, hd), lambda b, h, i: (b, 0, h)),
            pl.BlockSpec((1, l, hd), lambda b, h, i: (b, 0, h)),
            pl.BlockSpec((1, l, hd), lambda b, h, i: (b, 0, h)),
            _const_spec((1, hd)),
        ],
        out_specs=pl.BlockSpec((1, tq, hd), lambda b, h, i: (b, i, h)),
        out_shape=jax.ShapeDtypeStruct((bsz, l, width), BF16),
        compiler_params=_params(3),
        name="diffattn",
    )(lamv, q, kc, vc, kl, vl, g.reshape(1, hd))


def _gather_minor(refs, n_rows):
    cols = []
    for j in range(SUBLANES):
        parts = [r[pl.ds(j, n_rows, stride=SUBLANES), :] for r in refs]
        cols.append(parts[0] if len(parts) == 1 else jnp.concatenate(parts, axis=0))
    return jnp.concatenate(cols, axis=1)


def _spectrum_store(val, spec_ref, jb):
    rows = val.shape[0]
    for j in range(SUBLANES):
        spec_ref[jb, pl.ds(j, rows, stride=SUBLANES), :] = val[:, j * LANES:(j + 1) * LANES]


def _spectrum_blocks(spec_ref, k0):
    start = k0 * SUBLANES if isinstance(k0, int) else pl.multiple_of(k0 * SUBLANES, SUBLANES * SUBLANES)
    x = spec_ref[:, pl.ds(start, SUBLANES * SUBLANES), :]
    return jnp.stack([x[:, i * SUBLANES:(i + 1) * SUBLANES, :].reshape(FFT_N2, LANES) for i in range(SUBLANES)], axis=0)


def _spectrum_put(spec_ref, k0, val):
    nb = FFT_N2 // SUBLANES
    v = val.reshape(SUBLANES, nb, SUBLANES, LANES)
    x = jnp.stack([v[:, jb].reshape(SUBLANES * SUBLANES, LANES) for jb in range(nb)], axis=0)
    spec_ref[:, pl.ds(pl.multiple_of(k0 * SUBLANES, SUBLANES * SUBLANES), SUBLANES * SUBLANES), :] = x


def _twiddle_block(twc_ref, twf_ref, kb):
    n2 = FFT_N2
    rows = pl.ds(pl.multiple_of(kb * n2, n2), n2)
    cr, ci = twc_ref[0, rows, :][None], twc_ref[1, rows, :][None]
    fr, fi = twf_ref[0].reshape(SUBLANES, n2, LANES), twf_ref[1].reshape(SUBLANES, n2, LANES)
    return cr * fr - ci * fi, cr * fi + ci * fr


def _filter_fft_kernel(z_ref, w1_ref, b1_ref, f1_ref, w2_ref, b2_ref, f2_ref, w3f_ref, w3b_ref, dl_ref,
                       fr_ref, g_ref, twc_ref, twf_ref, kr_ref, ki_ref, h2_ref, hs_ref, a_ref, *, tm):
    half = z_ref.shape[0]
    seq_len = 2 * half
    hid = w1_ref.shape[0]
    n1 = a_ref.shape[2] // SUBLANES
    n1h = hs_ref.shape[2] // SUBLANES
    n2 = FFT_N2
    nb = n2 // SUBLANES
    w = SUBLANES * LANES
    zeros = jnp.zeros((hid, hid), F32)

    @pl.when(pl.program_id(0) == 0)
    def _():
        blockdiag = lambda m: jnp.concatenate([jnp.concatenate([m, zeros], axis=1),
                                               jnp.concatenate([zeros, m], axis=1)], axis=0)
        twice = lambda ref: jnp.concatenate([ref[...], ref[...]], axis=1)
        w1, w2 = blockdiag(w1_ref[...]), blockdiag(w2_ref[...])
        b1, f1, b2, f2 = twice(b1_ref), twice(f1_ref), twice(b2_ref), twice(f2_ref)

        def features(i, _):
            r = pl.ds(pl.multiple_of(i * tm, tm), tm)
            h1 = jnp.sin(f1 * (_dot3f(z_ref[r, :], w1) + b1))
            h2_ref[r, :] = jnp.sin(f2 * (_dot3f(h1, w2) + b2))
            return 0

        lax.fori_loop(0, half // tm, features, 0)

    for upper in range(2):
        pad = lambda m: jnp.concatenate([jnp.zeros_like(m), m] if upper else [m, jnp.zeros_like(m)], axis=0)
        w3s = [pad(w3f_ref[...]), pad(w3b_ref[...])]

        def taps(i, _, upper=upper, w3s=w3s):
            h2 = h2_ref[pl.ds(pl.multiple_of(i * tm, tm), tm), :]
            t0 = upper * half + i * tm
            row = lax.broadcasted_iota(jnp.int32, (tm, LANES), 0) + t0
            decay = jnp.exp(-(row.astype(F32) * (1.0 / (seq_len - 1))) * dl_ref[...])
            for t in range(2):
                h = _dot3f(h2, w3s[t]) * decay
                if t == 1:
                    h = jnp.where(row == 0, 0.0, h)
                h4 = h.reshape(tm // n2, nb, SUBLANES, LANES)
                for jb in range(nb):
                    srow = pl.multiple_of(upper * (half // nb) + i * (tm // nb), tm // nb)
                    hs_ref[t, jb, pl.ds(srow, tm // nb), :] = h4[:, jb].reshape(tm // nb, LANES)
            return 0

        lax.fori_loop(0, half // tm, taps, 0)

    for jb in range(nb):
        z = jnp.concatenate([_gather_minor([hs_ref.at[t, jb]], n1h) for t in range(2)], axis=1)
        out = _dft_dot(fr_ref[...], z)
        for t in range(4):
            _spectrum_store(out[(t % 2) * n1:(t % 2 + 1) * n1, (t // 2) * w:(t // 2 + 1) * w], a_ref.at[t], jb)

    def body(kb, _):
        k0 = pl.multiple_of(kb * SUBLANES, SUBLANES)
        twr, twi = _twiddle_block(twc_ref, twf_ref, kb)
        a = [_spectrum_blocks(a_ref.at[t], k0) for t in range(4)]
        cols = []
        for t in range(2):
            ar, ai = a[2 * t], a[2 * t + 1]
            sr, si = ar * twr - ai * twi, ar * twi + ai * twr
            cols += [jnp.concatenate([sr[i], si[i]], axis=0) for i in range(SUBLANES)]
        x = _dft_dot(g_ref[...], jnp.concatenate(cols, axis=1))
        for i in range(SUBLANES):
            xf = x[:, i * LANES:(i + 1) * LANES]
            xb = x[:, (SUBLANES + i) * LANES:(SUBLANES + i + 1) * LANES]
            rows = pl.ds(pl.multiple_of((k0 + i) * n2, n2), n2)
            kr_ref[rows, :] = xf[:n2] + xb[:n2]
            ki_ref[rows, :] = xf[n2:] - xb[n2:]
        return 0

    lax.fori_loop(0, n1 // SUBLANES, body, 0, unroll=4)


def _filter_fft_call(dft, z, w1, b1, f1, w2, b2, f2, w3, absdelta, tm=512):
    hid = z.shape[1] // 2
    seq_len = 2 * z.shape[0]
    c = w3.shape[1] // 2
    n, n1, n1h = dft["n"], dft["n1"], dft["n1h"]
    nb = FFT_N2 // SUBLANES
    ncb = c // LANES
    fr, g = dft["f_real"].astype(BF16), dft["g_fwd"].astype(BF16)
    twc, twf = dft["tw_coarse"], dft["tw_fine"]
    vec = _const_spec((1, hid))
    w3_blk = lambda off: pl.BlockSpec((hid, LANES), lambda i, off=off: (0, i + off))
    out = pl.BlockSpec((None, n, LANES), lambda i: (i, 0, 0))
    return pl.pallas_call(
        functools.partial(_filter_fft_kernel, tm=tm),
        grid=(ncb,),
        in_specs=[_const_spec(z.shape), _const_spec((hid, hid)), vec, vec, _const_spec((hid, hid)), vec, vec,
                  w3_blk(0), w3_blk(ncb), pl.BlockSpec((1, LANES), lambda i: (0, i)),
                  _const_spec(fr.shape), _const_spec(g.shape), _const_spec(twc.shape), _const_spec(twf.shape)],
        out_specs=[out, out],
        out_shape=[jax.ShapeDtypeStruct((ncb, n, LANES), F32)] * 2,
        scratch_shapes=[pltpu.VMEM(z.shape, F32), pltpu.VMEM((2, nb, n1h * SUBLANES, LANES), F32),
                        pltpu.VMEM((4, nb, n1 * SUBLANES, LANES), F32)],
        compiler_params=_params(1),
        name="hyfilter_fft",
    )(z, w1, b1.reshape(1, hid), f1.reshape(1, hid), w2, b2.reshape(1, hid), f2.reshape(1, hid), w3, w3, absdelta,
      fr, g, twc, twf)


def _hyena_conv_kernel(fc_ref, g_ref, gc_ref, e_ref, twc_ref, twf_ref, kr_ref, ki_ref, u_ref, x0_ref, bias_ref, o_ref,
                       a_ref, *, inv_n):
    n1 = a_ref.shape[2] // SUBLANES
    n1h = u_ref.shape[2] // SUBLANES
    n2 = FFT_N2
    nb = n2 // SUBLANES

    for jb in range(nb):
        z = _gather_minor([u_ref.at[0, jb], u_ref.at[1, jb]], n1h)
        out = _dft_dot(fc_ref[...], z)
        for c in range(2):
            _spectrum_store(out[c * n1:(c + 1) * n1], a_ref.at[c], jb)

    def body(kb, _):
        k0 = pl.multiple_of(kb * SUBLANES, SUBLANES)
        rows = pl.ds(pl.multiple_of(kb * (SUBLANES * n2), SUBLANES * n2), SUBLANES * n2)
        blocks = lambda ref: ref[rows, :].reshape(SUBLANES, n2, LANES)
        (twr, twi), kr, ki = _twiddle_block(twc_ref, twf_ref, kb), blocks(kr_ref), blocks(ki_ref)
        ar, ai = _spectrum_blocks(a_ref.at[0], k0), _spectrum_blocks(a_ref.at[1], k0)
        sr, si = ar * twr - ai * twi, ar * twi + ai * twr
        lanes = lambda re, im: jnp.concatenate([jnp.concatenate([re[i], im[i]], axis=0) for i in range(SUBLANES)], axis=1)
        x = _dft_dot(g_ref[...], lanes(sr, si))
        unl = lambda v, lo: jnp.stack([v[lo:lo + n2, i * LANES:(i + 1) * LANES] for i in range(SUBLANES)], axis=0)
        xr, xi = unl(x, 0), unl(x, n2)
        t = _dft_dot(gc_ref[...], lanes(xr * kr - xi * ki, xr * ki + xi * kr))
        tr, ti = unl(t, 0), unl(t, n2)
        _spectrum_put(a_ref.at[0], k0, tr * twr + ti * twi)
        _spectrum_put(a_ref.at[1], k0, ti * twr - tr * twi)
        return 0

    lax.fori_loop(0, n1 // SUBLANES, body, 0, unroll=4)

    for jb in range(nb):
        t = _gather_minor([a_ref.at[0, jb], a_ref.at[1, jb]], n1)
        y = _dft_dot(e_ref[...], t) * inv_n
        for c in range(2):
            _spectrum_store(y[c * n1h:(c + 1) * n1h], o_ref.at[c], jb)
    o_ref[...] = x0_ref[...] * (o_ref[...] + u_ref[...] * bias_ref[...])


def _dft_blocks(seq_len):
    n = 2 * seq_len
    n2 = FFT_N2
    n1 = n // n2
    n1h = n1 // 2
    k1 = np.arange(n1)[:, None].astype(np.float64)
    a = 2.0 * np.pi * k1 * np.arange(n1h)[None, :] / n1
    fr, fi = np.cos(a), -np.sin(a)
    f_cplx = np.block([[fr, -fi], [fi, fr]])
    f_real = np.concatenate([fr, fi], axis=0)
    e_cplx = np.block([[fr.T, fi.T], [-fi.T, fr.T]])
    b = 2.0 * np.pi * np.arange(n2)[:, None] * np.arange(n2)[None, :] / n2
    gr, gi = np.cos(b), -np.sin(b)
    g_fwd = np.block([[gr, -gi], [gi, gr]])
    g_inv = np.block([[gr, gi], [-gi, gr]])
    def table(k1s):
        ang = (2.0 * np.pi / n) * (k1s[:, None] * np.arange(n2)[None, :]).reshape(-1, 1)
        return np.broadcast_to(np.stack([np.cos(ang), -np.sin(ang)]), (2, k1s.size * n2, LANES))
    as32 = lambda m: jnp.asarray(np.ascontiguousarray(m), dtype=F32)
    return dict(n=n, n1=n1, n1h=n1h, f_cplx=as32(f_cplx), f_real=as32(f_real), e_cplx=as32(e_cplx),
                g_fwd=as32(g_fwd), g_inv=as32(g_inv), tw_coarse=as32(table(np.arange(0, n1, SUBLANES))),
                tw_fine=as32(table(np.arange(SUBLANES))))


def _hyena_call(u, x0, filt, hy_bias):
    bsz, ncb, nslab, srows, _ = u.shape
    seq_len = nslab * srows
    c = ncb * LANES
    n2 = FFT_N2
    dft = _dft_blocks(seq_len)
    n, n1, n1h = dft["n"], dft["n1"], dft["n1h"]
    pairs = bsz // 2

    w1, b1, f1, w2, b2, f2, w3 = filt
    hid = w2.shape[0]
    bands = (HY_POS_EMB - 1) // 2
    t = np.linspace(0.0, 1.0, seq_len)[:, None]
    freqs = np.linspace(1e-4, bands - 1, bands)
    ang = (2.0 * math.pi / seq_len) * np.arange(seq_len)[:, None] * freqs[None, :]
    z = np.concatenate([t, np.cos(ang), -np.sin(ang), np.zeros((seq_len, hid - HY_POS_EMB))], axis=-1)
    z = jnp.asarray(np.concatenate([z[:seq_len // 2], z[seq_len // 2:]], axis=1), dtype=F32)
    w1p = jnp.concatenate([w1, jnp.zeros((hid - HY_POS_EMB, hid), F32)], axis=0)
    max_decay = math.log(HY_DECAY_TARGET) / HY_FAST_DECAY_PCT
    min_decay = math.log(HY_DECAY_TARGET) / HY_SLOW_DECAY_PCT
    absdelta = jnp.asarray(np.abs(np.linspace(min_decay, max_decay, c))[None, :], dtype=F32)
    nb = n2 // SUBLANES
    kr, ki = _filter_fft_call(dft, z, w1p, b1, f1, w2, b2, f2, w3, absdelta)

    view = lambda a: a.reshape(pairs, 2, ncb, nb, n1h * SUBLANES, LANES)
    pair_spec = pl.BlockSpec((None, 2, None, nb, n1h * SUBLANES, LANES), lambda cb, p: (p, 0, cb, 0, 0, 0))
    k_spec = pl.BlockSpec((None, n, LANES), lambda cb, p: (cb, 0, 0))
    twc, twf = dft["tw_coarse"], dft["tw_fine"]
    fc, g, gc, e = (dft[k].astype(BF16) for k in ("f_cplx", "g_fwd", "g_inv", "e_cplx"))
    hy = pl.pallas_call(
        functools.partial(_hyena_conv_kernel, inv_n=1.0 / n),
        grid=(ncb, pairs),
        in_specs=[_const_spec(fc.shape), _const_spec(g.shape), _const_spec(gc.shape), _const_spec(e.shape),
                  _const_spec(twc.shape), _const_spec(twf.shape), k_spec, k_spec, pair_spec, pair_spec,
                  pl.BlockSpec((1, LANES), lambda cb, p: (0, cb))],
        out_specs=pair_spec,
        out_shape=jax.ShapeDtypeStruct((pairs, 2, ncb, nb, n1h * SUBLANES, LANES), F32),
        scratch_shapes=[pltpu.VMEM((2, nb, n1 * SUBLANES, LANES), F32)],
        compiler_params=_params(2),
        name="hyconv",
    )(fc, g, gc, e, twc, twf, kr, ki, view(u), view(x0), hy_bias.reshape(1, c))
    return hy.reshape(bsz, ncb, nb, n1h * SUBLANES, LANES)


def _stage_weights_bf16(pairs):
    for src, dst in pairs:
        n_rows, n_cols = src.shape
        rows = min(n_rows, 1 << ((WEIGHT_STAGE_BYTES // (4 * n_cols)).bit_length() - 1))
        n_chunks = n_rows // rows
        assert rows % (2 * SUBLANES) == 0 and n_chunks * rows == n_rows

        def run(stage, sem, src=src, dst=dst, rows=rows, n_chunks=n_chunks):
            copy = lambda k: pltpu.make_async_copy(src.at[pl.ds(k * rows, rows), :], stage.at[k % 2], sem.at[k % 2])
            copy(0).start()
            for k in range(n_chunks):
                if k + 1 < n_chunks:
                    copy(k + 1).start()
                copy(k).wait()
                dst[pl.ds(k * rows, rows), :] = stage[k % 2].astype(BF16)

        pl.run_scoped(run, pltpu.VMEM((2, rows, n_cols), F32), pltpu.SemaphoreType.DMA((2,)))


def _mlp_tail(h, g2, shift, scale, gate, w1_ref, w2_ref, fc):
    a = (_rms(h, NORM_EPS) * g2) * (1.0 + scale) + shift
    a = a.astype(BF16)
    acc = None
    for c0 in range(0, w1_ref.shape[1], fc):
        hid = jnp.maximum(_dot(a, w1_ref[:, c0:c0 + fc]), 0.0)
        part = _dot((hid * hid).astype(BF16), w2_ref[c0:c0 + fc, :])
        acc = part if acc is None else acc + part
    return h + gate * acc


def _mixmlp_kernel(x_ref, att_ref, hy_ref, mod_ref, wo_s, bo_ref, g2_ref, w1_s, w2_s, modn_ref, gn_ref, o_ref, an_ref,
                   *, fc):
    blk, d = x_ref.shape[1:]
    tm = blk // ROW_SUBTILES
    wa = att_ref.shape[2]
    mod = mod_ref[0, pl.ds(pl.program_id(0), 1), :]
    modn = modn_ref[0, pl.ds(pl.program_id(0), 1), :]
    for r in range(ROW_SUBTILES):
        rows = slice(r * tm, (r + 1) * tm)
        hy = jnp.concatenate([_spectrum_blocks(hy_ref.at[0, cb], r * tm // FFT_N2).reshape(tm, LANES)
                              for cb in range(hy_ref.shape[1])], axis=1).astype(BF16)
        y = _dot(att_ref[0, rows, :], wo_s[:wa, :]) + _dot(hy, wo_s[wa:, :]) + bo_ref[...]
        h = x_ref[0, rows, :] + mod[:, 2 * d:3 * d] * y
        h2 = _mlp_tail(h, g2_ref[...], mod[:, 3 * d:4 * d], mod[:, 4 * d:5 * d], mod[:, 5 * d:6 * d], w1_s, w2_s, fc)
        o_ref[0, rows, :] = h2
        an_ref[0, rows, :] = (_rms(h2, NORM_EPS) * gn_ref[...]) * (1.0 + modn[:, d:2 * d]) + modn[:, 0:d]


def _mixmlp_call(x, att, hy, mod, w_out, b_out, g2, w1, w2, g1_next, layer, tm=ROW_BLOCK, fc=1024):
    bsz, l, d = x.shape
    wa = att.shape[2]
    row = lambda w: pl.BlockSpec((1, tm, w), lambda b, i: (b, i, 0))
    return pl.pallas_call(
        functools.partial(_mixmlp_kernel, fc=fc),
        grid=(bsz, l // tm),
        in_specs=[row(d), row(wa),
                  pl.BlockSpec((1, hy.shape[1], hy.shape[2], tm // hy.shape[2], LANES), lambda b, i: (b, 0, 0, i, 0)),
                  _mod_spec(mod, layer),
                  _layer_spec(w_out, layer // 2), _const_spec((1, d)), _const_spec((1, d)),
                  _layer_spec(w1, layer), _layer_spec(w2, layer),
                  _mod_spec(mod, layer + 1), _const_spec((1, d))],
        out_specs=[row(d), row(d)],
        out_shape=[jax.ShapeDtypeStruct((bsz, l, d), F32)] * 2,
        compiler_params=_params(2),
        name="mixmlp",
    )(x, att, hy, mod, w_out, b_out.reshape(1, d), g2.reshape(1, d), w1, w2, mod, g1_next.reshape(1, d))


def _poolmlp_kernel(h_ref, a_ref, ap_ref, an_ref, mod_ref, pw_s, ps_ref, g2_ref, w1_s, w2_s, gf_ref, o_ref,
                    *, fc, seq_len):
    blk, d = h_ref.shape[1:]
    tm = blk // ROW_SUBTILES
    halo = POOL_HALO
    rows = tm + 2 * halo
    mod = mod_ref[0, pl.ds(pl.program_id(0), 1), :]
    gd = d // len(POOL_WINDOWS)
    for r in range(ROW_SUBTILES):
        r0 = r * tm
        h = h_ref[0, r0:r0 + tm, :]
        above = ap_ref[0] if r == 0 else a_ref[0, r0 - halo:r0, :]
        below = an_ref[0] if r == ROW_SUBTILES - 1 else a_ref[0, r0 + tm:r0 + tm + halo, :]
        a = jnp.concatenate([above, a_ref[0, r0:r0 + tm, :], below], axis=0)
        t = lax.broadcasted_iota(jnp.int32, (rows, 1), 0) + (pl.program_id(1) * blk + r0 - halo)
        a = jnp.where((t >= 0) & (t < seq_len), a, 0.0)
        tc = t[halo:halo + tm]
        ys = []
        for g, win in enumerate(POOL_WINDOWS):
            ag = a[:, g * gd:(g + 1) * gd]
            f, m = ag, 1
            while 2 * m <= win // 2:
                f = f + pltpu.roll(f, rows - m, 0)
                m *= 2
            s = pltpu.roll(f, win // 2, 0) + f
            cnt = (jnp.minimum(tc + (win - win // 2), seq_len) - jnp.maximum(tc - win // 2, 0)).astype(F32)
            dlt = s[halo:halo + tm] / cnt - ag[halo:halo + tm]
            ys.append(_dot(dlt.astype(BF16), pw_s[g * gd:(g + 1) * gd, :]))
        y = jnp.concatenate(ys, axis=1) * ps_ref[...]
        h1 = h + mod[:, 2 * d:3 * d] * y
        h2 = _mlp_tail(h1, g2_ref[...], mod[:, 3 * d:4 * d], mod[:, 4 * d:5 * d], mod[:, 5 * d:6 * d], w1_s, w2_s, fc)
        o_ref[0, r0:r0 + tm, :] = _rms(h2, NORM_EPS) * gf_ref[...]


def _poolmlp_call(h, a, mod, pool_w, pool_scale, g2, w1, w2, gf, layer, tm=ROW_BLOCK, fc=1024):
    bsz, l, d = h.shape
    halo = POOL_HALO
    nb = tm // halo
    last = l // halo - 1
    row = pl.BlockSpec((1, tm, d), lambda b, i: (b, i, 0))
    return pl.pallas_call(
        functools.partial(_poolmlp_kernel, fc=fc, seq_len=l),
        grid=(bsz, l // tm),
        in_specs=[row, row,
                  pl.BlockSpec((1, halo, d), lambda b, i: (b, jnp.maximum(i * nb - 1, 0), 0)),
                  pl.BlockSpec((1, halo, d), lambda b, i: (b, jnp.minimum((i + 1) * nb, last), 0)),
                  _mod_spec(mod, layer),
                  _layer_spec(pool_w, layer // 2), _const_spec((1, d)), _const_spec((1, d)),
                  _layer_spec(w1, layer), _layer_spec(w2, layer), _const_spec((1, d))],
        out_specs=row,
        out_shape=jax.ShapeDtypeStruct((bsz, l, d), F32),
        compiler_params=_params(2),
        name="poolmlp",
    )(h, a, a, a, mod, pool_w, pool_scale.reshape(1, d), g2.reshape(1, d), w1, w2, gf.reshape(1, d))


def kernel(x, c, ctx, c_ctx, ada_w, ada_b, norm1_g, norm2_g, mix_w_in, mix_b_in, mix_w_out, mix_b_out, lam_q1, lam_k1, lam_q2, lam_k2, subln_g, hy_conv_w, hy_conv_b, hy_pos_w1, hy_pos_b1, hy_freq1, hy_pos_w2, hy_pos_b2, hy_freq2, hy_pos_w3, hy_bias, pool_w, pool_scale, mlp_w1, mlp_w2, final_g):
    bsz, seq_len, d = x.shape
    depth = ada_w.shape[0]
    assert depth == 2 and bsz % 2 == 0 and bsz < MOD_ROWS and seq_len % GRID_W == 0
    att_w = ATT_HEADS * ATT_V_DIM
    q_cols = k_cols = ATT_HEADS * 2 * ATT_QK_DIM
    kv_start, hy_start = q_cols, q_cols + k_cols + att_w
    in_cols = mix_w_in.shape[2]

    cv = jnp.concatenate([c, c_ctx[None, :], jnp.zeros((MOD_ROWS - bsz - 1, d), F32)], axis=0)
    mod = _mod_call(cv, ada_w, ada_b)

    lam_init = 0.8 - 0.6 * math.exp(-0.3 * 0)
    assert k_cols == q_cols and in_cols - hy_start == 3 * (d - att_w)
    pool_w3 = pool_w.reshape(pool_w.shape[0], d, d // len(POOL_WINDOWS))
    (q, k, v, u, x0), (w_out_b, w1_b, w2_b, pool_wb) = _latproj_call(
        x, mod, norm1_g[0], mix_w_in, mix_b_in, _rope_tables(seq_len),
        hy_conv_w[0], hy_conv_b[0], n_att=hy_start, qk=q_cols, q_scale=ATT_QK_DIM ** -0.5 * math.log2(math.e),
        side_weights=[mix_w_out, mlp_w1, mlp_w2, pool_w3])
    kc, vc = _ctxproj_call(ctx, mod, bsz, norm1_g[0], mix_w_in, mix_b_in, kv_start, k_cols, hy_start - kv_start)
    lamv = jnp.stack([lam_q1[0], lam_k1[0], lam_q2[0], lam_k2[0]], axis=0)
    att = _attn_call(q, kc, vc, k, v, lamv, subln_g[0], lam_init)
    filt = (hy_pos_w1[0], hy_pos_b1[0], hy_freq1[0], hy_pos_w2[0], hy_pos_b2[0], hy_freq2[0], hy_pos_w3[0])
    hy = _hyena_call(u, x0, filt, hy_bias[0])
    h, a1 = _mixmlp_call(x, att, hy, mod, w_out_b, mix_b_out[0], norm2_g[0], w1_b, w2_b, norm1_g[1], layer=0)

    return _poolmlp_call(h, a1, mod, pool_wb, pool_scale[0], norm2_g[1], w1_b, w2_b, final_g, layer=1)
```
